```python
import math
import jax, jax.numpy as jnp
from jax import lax
import numpy as np

D_MODEL = 1024
BATCH = 32
SEQ = 256
DEPTH = 2
DEC_BATCH = 8
DEC_SEQ = 2048
PAST_LEN = 512

GRID_W = 64
QUERY_BLOCK = 128
ROPE_BASE = 10000.0
DIFF_HEADS = 4
DIFF_DH = 64
DIFF_W = DIFF_HEADS * 2 * DIFF_DH
NA_HEADS = 8
NA_DH = 64
NA_W = NA_HEADS * NA_DH
NA_WIN_ROWS = 8
NA_WIN_COLS = 16
MLA_HEADS = 8
MLA_Q_LORA = 384
MLA_KV_LORA = 256
MLA_NOPE = 64
MLA_ROPE = 32
MLA_V = 64
MLA_W = MLA_HEADS * MLA_V
N_EXPERTS = 64
N_GROUPS = 8
TOPK_GROUPS = 4
TOP_K = 8
EXPERT_FF = 256
SHARED_FF = 256
ROUTED_SCALE = 2.5
DN_ALPHA = (2 * DEPTH) ** 0.25
DN_BETA = (8 * DEPTH) ** -0.25
LN_EPS = 1e-5
RMS_EPS = 1e-6
IN_SPLITS = (DIFF_W, DIFF_W, DIFF_W, NA_W, NA_W, NA_W, MLA_Q_LORA, MLA_KV_LORA, MLA_ROPE, D_MODEL, D_MODEL, D_MODEL)
IN_WIDTH = sum(IN_SPLITS)

kernel_name = 'hybrid_diffusion_diffattn_natten_mla_moe_step'


def layer_norm(x, g, b):
    xf = x.astype(jnp.float32)
    mu = xf.mean(-1, keepdims=True)
    var = jnp.square(xf - mu).mean(-1, keepdims=True)
    return ((xf - mu) * lax.rsqrt(var + LN_EPS) * g.astype(jnp.float32) + b.astype(jnp.float32)).astype(x.dtype)


def rms_norm(x, g):
    xf = x.astype(jnp.float32)
    return (xf * lax.rsqrt(jnp.mean(xf * xf, -1, keepdims=True) + RMS_EPS) * g.astype(jnp.float32)).astype(x.dtype)


def rope_1d(x, pos):
    half = x.shape[-1] // 2
    inv = ROPE_BASE ** (-jnp.arange(half, dtype=jnp.float32) / half)
    ang = pos.astype(jnp.float32)[:, None] * inv[None, :]
    cos = jnp.cos(ang)[:, None, :]
    sin = jnp.sin(ang)[:, None, :]
    xf = x.astype(jnp.float32)
    x1, x2 = xf[..., :half], xf[..., half:]
    return jnp.concatenate([x1 * cos - x2 * sin, x1 * sin + x2 * cos], -1).astype(x.dtype)


def rope_2d(x):
    n = x.shape[1]
    t = jnp.arange(n)
    d2 = x.shape[-1] // 2
    return jnp.concatenate([rope_1d(x[..., :d2], t // GRID_W), rope_1d(x[..., d2:], t % GRID_W)], -1)


def _query_blocks(q):
    b, n = q.shape[:2]
    qb = min(QUERY_BLOCK, n)
    return jnp.moveaxis(q.reshape((b, n // qb, qb) + q.shape[2:]), 1, 0)


def _unblock(o):
    o = jnp.moveaxis(o, 0, 1)
    return o.reshape((o.shape[0], o.shape[1] * o.shape[2]) + o.shape[3:])


def dense_attention(q, k, v):
    def block(qb):
        s = jnp.einsum('bqhd,bkhd->bhqk', qb, k).astype(jnp.float32)
        p = jax.nn.softmax(s, axis=-1).astype(v.dtype)
        return jnp.einsum('bhqk,bkhe->bqhe', p, v)
    return _unblock(lax.map(block, _query_blocks(q)))


def diff_attention(q, k, v, lam):
    def block(qb):
        s = jnp.einsum('bqhmd,bkhmd->bhmqk', qb, k).astype(jnp.float32)
        p = jax.nn.softmax(s, axis=-1)
        a = (p[:, :, 0] - lam * p[:, :, 1]).astype(v.dtype)
        return jnp.einsum('bhqk,bkhe->bqhe', a, v)
    return _unblock(lax.map(block, _query_blocks(q)))


def neighbourhood_attention(q, k, v, kc, vc, rpb):
    b, n, h, d = q.shape
    L = kc.shape[1]
    rows = n // GRID_W
    wr = min(NA_WIN_ROWS, rows)
    wc = min(NA_WIN_COLS, GRID_W)
    kg = k.reshape(b, rows, GRID_W, h, d)
    vg = v.reshape(b, rows, GRID_W, h, d)
    qg = jnp.moveaxis(q.reshape(b, rows, GRID_W, h, d), 1, 0)
    col = jnp.arange(GRID_W)
    col_start = jnp.clip(col - wc // 2, 0, GRID_W - wc)
    col_idx = col_start[:, None] + jnp.arange(wc)[None, :]
    col_bias = col_idx - col[:, None] + NA_WIN_COLS - 1

    def row_block(args):
        qr, r = args
        r0 = jnp.clip(r - wr // 2, 0, rows - wr)
        kwin = lax.dynamic_slice_in_dim(kg, r0, wr, axis=1)[:, :, col_idx]
        vwin = lax.dynamic_slice_in_dim(vg, r0, wr, axis=1)[:, :, col_idx]
        row_bias = r0 + jnp.arange(wr) - r + NA_WIN_ROWS - 1
        bias = jnp.transpose(rpb[:, row_bias][:, :, col_bias], (0, 2, 1, 3)).astype(jnp.float32)
        s_nb = jnp.einsum('bchd,bicjhd->bhcij', qr, kwin).astype(jnp.float32) + bias[None]
        s_ctx = jnp.einsum('bchd,blhd->bhcl', qr, kc).astype(jnp.float32)
        s = jnp.concatenate([s_ctx, s_nb.reshape(b, h, GRID_W, wr * wc)], axis=-1)
        p = jax.nn.softmax(s, axis=-1).astype(v.dtype)
        p_nb = p[..., L:].reshape(b, h, GRID_W, wr, wc)
        return (jnp.einsum('bhcl,blhd->bchd', p[..., :L], vc)
                + jnp.einsum('bhcij,bicjhd->bchd', p_nb, vwin))

    o = lax.map(row_block, (qg, jnp.arange(rows)))
    return jnp.moveaxis(o, 0, 1).reshape(b, n, h, d)


def diff_lambda_value(lam_params, layer):
    lp = lam_params.astype(jnp.float32)
    lam_init = 0.8 - 0.6 * math.exp(-0.3 * layer)
    lam = jnp.exp(jnp.sum(lp[0] * lp[1])) - jnp.exp(jnp.sum(lp[2] * lp[3])) + lam_init
    return lam, lam_init


def diff_head_norm(o, P, lam_init):
    b, n = o.shape[:2]
    return (rms_norm(o, P['diff_subln_g']) * (1.0 - lam_init)).reshape(b, n, DIFF_W)


def mixer_inputs(h, P):
    b, n, _ = h.shape
    offsets = np.cumsum(IN_SPLITS)[:-1].tolist()
    dq, dk, dv, nq, nk, nv, qa, kva, kr, ga, gb, gc = jnp.split(h @ P['w_in'], offsets, axis=-1)
    dq = dq.reshape(b, n, DIFF_HEADS, 2, DIFF_DH) * DIFF_DH ** -0.5
    dk = dk.reshape(b, n, DIFF_HEADS, 2, DIFF_DH)
    dv = dv.reshape(b, n, DIFF_HEADS, 2 * DIFF_DH)
    nq = nq.reshape(b, n, NA_HEADS, NA_DH) * NA_DH ** -0.5
    nk = nk.reshape(b, n, NA_HEADS, NA_DH)
    nv = nv.reshape(b, n, NA_HEADS, NA_DH)
    mq = (rms_norm(qa, P['mla_qa_g']) @ P['mla_wq_b']).reshape(b, n, MLA_HEADS, MLA_NOPE + MLA_ROPE)
    mq = mq * (MLA_NOPE + MLA_ROPE) ** -0.5
    ckv = rms_norm(kva, P['mla_kva_g'])
    return dq, dk, dv, nq, nk, nv, mq, ckv, kr, (ga, gb, gc)


def mla_keys(ckv, kr, wkv_b):
    b, n, _ = ckv.shape
    kv = (ckv @ wkv_b).reshape(b, n, MLA_HEADS, MLA_NOPE + MLA_V)
    k = jnp.concatenate([kv[..., :MLA_NOPE], jnp.broadcast_to(kr[:, :, None, :], (b, n, MLA_HEADS, MLA_ROPE))], -1)
    return k, kv[..., MLA_NOPE:]


def merge_branches(o_diff, o_na, o_mla, gates, P):
    ga, gb, gc = gates
    m = (jax.nn.sigmoid(ga) * (o_diff @ P['w_branch_diff'])
         + jax.nn.sigmoid(gb) * (o_na @ P['w_branch_na'])
         + jax.nn.sigmoid(gc) * (o_mla @ P['w_branch_mla']))
    return m @ P['w_out']


def moe(h, P):
    b, n, d = h.shape
    t = h.reshape(b * n, d)
    scores = jax.nn.sigmoid((t @ P['w_router']).astype(jnp.float32))
    biased = scores + P['router_bias'].astype(jnp.float32)
    grp = biased.reshape(-1, N_GROUPS, N_EXPERTS // N_GROUPS)
    grp_score = lax.top_k(grp, 2)[0].sum(-1)
    _, grp_idx = lax.top_k(grp_score, TOPK_GROUPS)
    grp_mask = jax.nn.one_hot(grp_idx, N_GROUPS, dtype=jnp.float32).sum(1) > 0
    exp_mask = jnp.repeat(grp_mask, N_EXPERTS // N_GROUPS, axis=1)
    _, idx = lax.top_k(jnp.where(exp_mask, biased, -jnp.inf), TOP_K)
    w = jnp.take_along_axis(scores, idx, axis=-1)
    w = w / jnp.sum(w, -1, keepdims=True) * ROUTED_SCALE
    gate = jnp.einsum('tk,tke->te', w, jax.nn.one_hot(idx, N_EXPERTS, dtype=jnp.float32))

    def expert(acc, inp):
        wg, wu, wd, g = inp
        y = (jax.nn.silu(t @ wg) * (t @ wu)) @ wd
        return acc + g[:, None].astype(y.dtype) * y, None

    routed, _ = lax.scan(expert, jnp.zeros_like(t), (P['w_exp_gate'], P['w_exp_up'], P['w_exp_down'], gate.T))
    shared = (jax.nn.silu(t @ P['w_sh_gate']) * (t @ P['w_sh_up'])) @ P['w_sh_down']
    return (routed + shared).reshape(b, n, d)


def modulation(cvec, w_mod, b_mod):
    m = jax.nn.silu(cvec) @ w_mod + b_mod
    if m.ndim == 2:
        m = m[:, None, :]
    return jnp.split(m, 6, axis=-1)


def post_norm_residual(x, gate, out, g, bias):
    return layer_norm(DN_ALPHA * x + gate * out, g, bias)


def channel_step(x, shift, scale, gate, P):
    h = x * (1 + scale) + shift
    return post_norm_residual(x, gate, moe(h, P), P['ln2_g'], P['ln2_b'])


def context_layer(x, mods, P, layer):
    sh1, sc1, g1, sh2, sc2, g2 = mods
    b, n, _ = x.shape
    h = x * (1 + sc1) + sh1
    dq, dk, dv, nq, nk, nv, mq, ckv, kr, gates = mixer_inputs(h, P)
    lam, lam_init = diff_lambda_value(P['diff_lambda'], layer)
    o_d = diff_head_norm(diff_attention(dq, dk, dv, lam), P, lam_init)
    o_n = dense_attention(nq, nk, nv).reshape(b, n, NA_W)
    mk, mv = mla_keys(ckv, kr, P['mla_wkv_b'])
    o_m = dense_attention(mq, mk, mv).reshape(b, n, MLA_W)
    x = post_norm_residual(x, g1, merge_branches(o_d, o_n, o_m, gates, P), P['ln1_g'], P['ln1_b'])
    x = channel_step(x, sh2, sc2, g2, P)
    return x, (dk.reshape(b, n, DIFF_HEADS, 2 * DIFF_DH), dv, nk, nv, ckv, kr)


def latent_layer(x, mods, cache, P, layer):
    sh1, sc1, g1, sh2, sc2, g2 = mods
    ck_d, cv_d, ck_n, cv_n, c_ckv, c_kr = cache
    b, n, _ = x.shape
    L = ck_d.shape[1]
    h = x * (1 + sc1) + sh1
    dq, dk, dv, nq, nk, nv, mq, ckv, kr, gates = mixer_inputs(h, P)
    lam, lam_init = diff_lambda_value(P['diff_lambda'], layer)
    dq = rope_2d(dq.reshape(b, n, 2 * DIFF_HEADS, DIFF_DH)).reshape(b, n, DIFF_HEADS, 2, DIFF_DH)
    dk = rope_2d(dk.reshape(b, n, 2 * DIFF_HEADS, DIFF_DH)).reshape(b, n, DIFF_HEADS, 2, DIFF_DH)
    k_d = jnp.concatenate([ck_d.reshape(b, L, DIFF_HEADS, 2, DIFF_DH), dk], axis=1)
    v_d = jnp.concatenate([cv_d, dv], axis=1)
    o_d = diff_head_norm(diff_attention(dq, k_d, v_d, lam), P, lam_init)
    o_n = neighbourhood_attention(nq, nk, nv, ck_n, cv_n, P['na_rpb']).reshape(b, n, NA_W)
    mq = jnp.concatenate([mq[..., :MLA_NOPE], rope_2d(mq[..., MLA_NOPE:])], axis=-1)
    kr = rope_2d(kr[:, :, None, :])[:, :, 0]
    mk, mv = mla_keys(jnp.concatenate([c_ckv, ckv], axis=1), jnp.concatenate([c_kr, kr], axis=1), P['mla_wkv_b'])
    o_m = dense_attention(mq, mk, mv).reshape(b, n, MLA_W)
    x = post_norm_residual(x, g1, merge_branches(o_d, o_n, o_m, gates, P), P['ln1_g'], P['ln1_b'])
    return channel_step(x, sh2, sc2, g2, P)


def setup_inputs(seed: int = 0) -> dict:
    key = jax.random.key(seed)
    keys = iter(jax.random.split(key, 48))

    def nrm(shape, scale=1.0):
        return jax.random.normal(next(keys), shape, jnp.float32) * scale

    d = D_MODEL
    return {
        'x_prompt': nrm((BATCH, SEQ, d)),
        'x_sample': nrm((DEC_BATCH, DEC_SEQ, d)),
        'cache_diff_k': nrm((DEC_BATCH, DEPTH, PAST_LEN, DIFF_HEADS, 2 * DIFF_DH)),
        'cache_diff_v': nrm((DEC_BATCH, DEPTH, PAST_LEN, DIFF_HEADS, 2 * DIFF_DH)),
        'cache_na_k': nrm((DEC_BATCH, DEPTH, PAST_LEN, NA_HEADS, NA_DH)),
        'cache_na_v': nrm((DEC_BATCH, DEPTH, PAST_LEN, NA_HEADS, NA_DH)),
        'cache_mla_ckv': nrm((DEC_BATCH, DEPTH, PAST_LEN, MLA_KV_LORA)),
        'cache_mla_krope': nrm((DEC_BATCH, DEPTH, PAST_LEN, MLA_ROPE)),
        'c': nrm((DEC_BATCH, d)),
        'c_ctx': nrm((d,)),
        'w_mod': nrm((DEPTH, d, 6 * d), 0.5 * d ** -0.5),
        'b_mod': nrm((DEPTH, 6 * d), 0.02),
        'w_in': nrm((DEPTH, d, IN_WIDTH), d ** -0.5),
        'diff_lambda': nrm((DEPTH, 4, DIFF_DH), 0.1),
        'diff_subln_g': 1.0 + nrm((DEPTH, 2 * DIFF_DH), 0.02),
        'na_rpb': nrm((DEPTH, NA_HEADS, 2 * NA_WIN_ROWS - 1, 2 * NA_WIN_COLS - 1), 0.1),
        'mla_qa_g': 1.0 + nrm((DEPTH, MLA_Q_LORA), 0.02),
        'mla_wq_b': nrm((DEPTH, MLA_Q_LORA, MLA_HEADS * (MLA_NOPE + MLA_ROPE)), MLA_Q_LORA ** -0.5),
        'mla_kva_g': 1.0 + nrm((DEPTH, MLA_KV_LORA), 0.02),
        'mla_wkv_b': nrm((DEPTH, MLA_KV_LORA, MLA_HEADS * (MLA_NOPE + MLA_V)), MLA_KV_LORA ** -0.5),
        'w_branch_diff': nrm((DEPTH, DIFF_W, d), DN_BETA * DIFF_W ** -0.5),
        'w_branch_na': nrm((DEPTH, NA_W, d), DN_BETA * NA_W ** -0.5),
        'w_branch_mla': nrm((DEPTH, MLA_W, d), DN_BETA * MLA_W ** -0.5),
        'w_out': nrm((DEPTH, d, d), DN_BETA * d ** -0.5),
        'ln1_g': 1.0 + nrm((DEPTH, d), 0.02),
        'ln1_b': nrm((DEPTH, d), 0.02),
        'ln2_g': 1.0 + nrm((DEPTH, d), 0.02),
        'ln2_b': nrm((DEPTH, d), 0.02),
        'w_router': nrm((DEPTH, d, N_EXPERTS), d ** -0.5),
        'router_bias': nrm((DEPTH, N_EXPERTS), 0.01),
        'w_exp_gate': nrm((DEPTH, N_EXPERTS, d, EXPERT_FF), d ** -0.5),
        'w_exp_up': nrm((DEPTH, N_EXPERTS, d, EXPERT_FF), d ** -0.5),
        'w_exp_down': nrm((DEPTH, N_EXPERTS, EXPERT_FF, d), DN_BETA * EXPERT_FF ** -0.5),
        'w_sh_gate': nrm((DEPTH, d, SHARED_FF), d ** -0.5),
        'w_sh_up': nrm((DEPTH, d, SHARED_FF), d ** -0.5),
        'w_sh_down': nrm((DEPTH, SHARED_FF, d), DN_BETA * SHARED_FF ** -0.5),
    }


def reference(x_prompt, x_sample, cache_diff_k, cache_diff_v, cache_na_k, cache_na_v, cache_mla_ckv, cache_mla_krope,
              c, c_ctx, w_mod, b_mod, w_in, diff_lambda, diff_subln_g, na_rpb, mla_qa_g, mla_wq_b, mla_kva_g, mla_wkv_b,
              w_branch_diff, w_branch_na, w_branch_mla, w_out, ln1_g, ln1_b, ln2_g, ln2_b, w_router, router_bias,
              w_exp_gate, w_exp_up, w_exp_down, w_sh_gate, w_sh_up, w_sh_down):
    y_prompt = x_prompt
    y_sample = x_sample
    ctx_states = []
    for l in range(DEPTH):
        P = {
            'w_in': w_in[l], 'diff_lambda': diff_lambda[l], 'diff_subln_g': diff_subln_g[l], 'na_rpb': na_rpb[l],
            'mla_qa_g': mla_qa_g[l], 'mla_wq_b': mla_wq_b[l], 'mla_kva_g': mla_kva_g[l], 'mla_wkv_b': mla_wkv_b[l],
            'w_branch_diff': w_branch_diff[l], 'w_branch_na': w_branch_na[l], 'w_branch_mla': w_branch_mla[l],
            'w_out': w_out[l], 'ln1_g': ln1_g[l], 'ln1_b': ln1_b[l], 'ln2_g': ln2_g[l], 'ln2_b': ln2_b[l],
            'w_router': w_router[l], 'router_bias': router_bias[l], 'w_exp_gate': w_exp_gate[l],
            'w_exp_up': w_exp_up[l], 'w_exp_down': w_exp_down[l], 'w_sh_gate': w_sh_gate[l],
            'w_sh_up': w_sh_up[l], 'w_sh_down': w_sh_down[l],
        }
        y_prompt, st = context_layer(y_prompt, modulation(c_ctx, w_mod[l], b_mod[l]), P, l)
        ctx_states.append(st)
        cache_l = (cache_diff_k[:, l], cache_diff_v[:, l], cache_na_k[:, l], cache_na_v[:, l],
                   cache_mla_ckv[:, l], cache_mla_krope[:, l])
        y_sample = latent_layer(y_sample, modulation(c, w_mod[l], b_mod[l]), cache_l, P, l)
    new_diff_k = jnp.stack([s[0] for s in ctx_states], axis=1)
    new_diff_v = jnp.stack([s[1] for s in ctx_states], axis=1)
    new_na_k = jnp.stack([s[2] for s in ctx_states], axis=1)
    new_na_v = jnp.stack([s[3] for s in ctx_states], axis=1)
    new_mla_ckv = jnp.stack([s[4] for s in ctx_states], axis=1)
    new_mla_krope = jnp.stack([s[5] for s in ctx_states], axis=1)
    return (y_prompt, y_sample, new_diff_k, new_diff_v, new_na_k, new_na_v, new_mla_ckv, new_mla_krope)
```

```python
import functools

import numpy as np
import jax
import jax.numpy as jnp
from jax import lax
from jax.experimental import pallas as pl
from jax.experimental.pallas import tpu as pltpu

D_MODEL = 1024
BATCH = 32
SEQ = 256
DEPTH = 2
DEC_BATCH = 8
DEC_SEQ = 2048
PAST_LEN = 512
GRID_W = 64
GRID_ROWS = DEC_SEQ // GRID_W
ROPE_BASE = 10000.0
DIFF_HEADS = 4
DIFF_DH = 64
DIFF_W = 512
NA_HEADS = 8
NA_DH = 64
NA_W = 512
NA_WIN_ROWS = 8
NA_WIN_COLS = 16
MLA_HEADS = 8
MLA_Q_LORA = 384
MLA_KV_LORA = 256
MLA_NOPE = 64
MLA_ROPE = 32
MLA_V = 64
MLA_W = 512
N_EXPERTS = 64
N_GROUPS = 8
GROUP_SIZE = N_EXPERTS // N_GROUPS
TOPK_GROUPS = 4
TOP_K = 8
EXPERT_FF = 256
SHARED_FF = 256
ROUTED_SCALE = 2.5
DN_ALPHA = (2 * DEPTH) ** 0.25
LN_EPS = 1e-5
RMS_EPS = 1e-6

F32 = jnp.float32
BF16 = jnp.bfloat16

LANES = 128
VMEM_LIMIT_BYTES = 56 * 1024 * 1024
NEG_BIG = -1e30

DIFF_SCALE = DIFF_DH ** -0.5
NA_SCALE = NA_DH ** -0.5
MLA_SCALE = (MLA_NOPE + MLA_ROPE) ** -0.5

PROJ_TM = 512
ATT_TQ = 256
NA_Q_ROWS = ATT_TQ // GRID_W
NA_KEY_ROWS = NA_Q_ROWS + NA_WIN_ROWS
NA_KEYS = NA_KEY_ROWS * GRID_W
MOE_TM = 1024


def _dot(a, b):
    return jnp.dot(a, b, preferred_element_type=F32)


def _dot_nt(a, b):
    return lax.dot_general(a, b, (((1,), (1,)), ((), ())), preferred_element_type=F32)


def _sigmoid(x):
    return 1.0 / (1.0 + jnp.exp(-x))


def _silu(x):
    return x * _sigmoid(x)


def _params(*sem):
    return pltpu.CompilerParams(dimension_semantics=sem, vmem_limit_bytes=VMEM_LIMIT_BYTES)


def _full(shape):
    n = len(shape)
    return pl.BlockSpec(shape, lambda *_: (0,) * n)


def _layer_norm(y, g, b):
    mu = jnp.mean(y, axis=-1, keepdims=True)
    yc = y - mu
    var = jnp.mean(yc * yc, axis=-1, keepdims=True)
    return yc * lax.rsqrt(var + LN_EPS) * g + b


def _rms(x, g):
    return x * lax.rsqrt(jnp.mean(x * x, axis=-1, keepdims=True) + RMS_EPS) * g


MOD_ROWS = 16
MOD_TN = 1536


def _mod_kernel(c_ref, w_ref, b_ref, o_ref):
    s = _silu(c_ref[...]).astype(BF16)
    o_ref[0] = _dot(s, w_ref[0].astype(BF16)) + b_ref[0]


def _modulation(cvec, w_mod, b_mod):
    n = 6 * D_MODEL
    return pl.pallas_call(
        _mod_kernel,
        out_shape=jax.ShapeDtypeStruct((DEPTH, MOD_ROWS, n), F32),
        grid=(DEPTH, n // MOD_TN),
        in_specs=[
            pl.BlockSpec((MOD_ROWS, D_MODEL), lambda l, j: (0, 0)),
            pl.BlockSpec((1, D_MODEL, MOD_TN), lambda l, j: (l, 0, j)),
            pl.BlockSpec((1, 1, MOD_TN), lambda l, j: (l, 0, j)),
        ],
        out_specs=pl.BlockSpec((1, MOD_ROWS, MOD_TN), lambda l, j: (l, 0, j)),
        compiler_params=_params("parallel", "parallel"),
        name="modulation",
    )(cvec, w_mod, b_mod.reshape(DEPTH, 1, n))


def _proj_common(x_ref, mod_ref, wa_ref, wm_ref, qag_ref, kvag_ref, wqb_ref, wkvb_ref):
    m = mod_ref[0]
    h = (x_ref[...] * (1.0 + m[1:2]) + m[0:1]).astype(BF16)
    a = _dot(h, wa_ref[...])
    mm = _dot(h, wm_ref[...])
    qan = _rms(mm[:, :MLA_Q_LORA], qag_ref[...]).astype(BF16)
    mq = _dot(qan, wqb_ref[...])
    ckv = _rms(mm[:, MLA_Q_LORA:MLA_Q_LORA + MLA_KV_LORA], kvag_ref[...])
    kv = _dot(ckv.astype(BF16), wkvb_ref[...])
    return h, a, mm, qan, mq, ckv, kv


def _proj_ctx_kernel(x_ref, mod_ref, wa_ref, wm_ref, qag_ref, kvag_ref, wqb_ref, wkvb_ref,
                     dq_ref, nq_ref, mqn_ref, mqr_ref, kr4_ref, kn_ref, vn_ref,
                     dk_ref, dv_ref, nk_ref, nv_ref, ckv_ref, kr_ref):
    _, a, mm, _, mq, ckv, kv = _proj_common(x_ref, mod_ref, wa_ref, wm_ref, qag_ref, kvag_ref,
                                            wqb_ref, wkvb_ref)
    dq_ref[...] = (a[:, 0:512] * DIFF_SCALE).astype(BF16)
    dk_ref[...] = a[:, 512:1024]
    dv_ref[...] = a[:, 1024:1536]
    nq_ref[...] = (a[:, 1536:2048] * NA_SCALE).astype(BF16)
    nk_ref[...] = a[:, 2048:2560]
    nv_ref[...] = a[:, 2560:3072]
    mqn_ref[...] = (mq[:, :512] * MLA_SCALE).astype(BF16)
    mqr_ref[...] = (mq[:, 512:768] * MLA_SCALE).astype(BF16)
    kr4 = mm[:, 640:768]
    kr4_ref[...] = kr4.astype(BF16)
    kr_ref[...] = kr4[:, :MLA_ROPE]
    ckv_ref[...] = ckv
    kn_ref[...] = kv[:, :512].astype(BF16)
    vn_ref[...] = kv[:, 512:].astype(BF16)


def _proj_lat_kernel(x_ref, mod_ref, wa_ref, wm_ref, qag_ref, kvag_ref, wqb_ref, wkvb_ref,
                     wp_ref, wqp_ref, cd_ref, sd_ref, cm_ref, sm_ref,
                     dq_ref, dk_ref, dv_ref, nq_ref, nk_ref, nv_ref,
                     mqn_ref, mqr_ref, kr4_ref, kn_ref, vn_ref):
    h, a, mm, qan, mq, _, kv = _proj_common(x_ref, mod_ref, wa_ref, wm_ref, qag_ref, kvag_ref,
                                            wqb_ref, wkvb_ref)
    ap = _dot(h, wp_ref[...])
    mqp = _dot(qan, wqp_ref[...])
    cd = cd_ref[...]
    sd = sd_ref[...]
    cm = cm_ref[...]
    sm = sm_ref[...]
    for j in range(DIFF_W // LANES):
        lo, hi = LANES * j, LANES * (j + 1)
        dq_ref[:, lo:hi] = ((a[:, lo:hi] * cd + ap[:, lo:hi] * sd) * DIFF_SCALE).astype(BF16)
        dk_ref[:, lo:hi] = (a[:, 512 + lo:512 + hi] * cd + ap[:, 512 + lo:512 + hi] * sd).astype(BF16)
    dv_ref[...] = a[:, 1024:1536].astype(BF16)
    nq_ref[...] = (a[:, 1536:2048] * NA_SCALE).astype(BF16)
    nk_ref[...] = a[:, 2048:2560].astype(BF16)
    nv_ref[...] = a[:, 2560:3072].astype(BF16)
    mqn_ref[...] = (mq[:, :512] * MLA_SCALE).astype(BF16)
    for j in range(2):
        lo, hi = LANES * j, LANES * (j + 1)
        mqr_ref[:, lo:hi] = ((mq[:, 512 + lo:512 + hi] * cm + mqp[:, lo:hi] * sm) * MLA_SCALE).astype(BF16)
    kr4_ref[...] = (mm[:, 640:768] * cm + mm[:, 768:896] * sm).astype(BF16)
    kn_ref[...] = kv[:, :512].astype(BF16)
    vn_ref[...] = kv[:, 512:].astype(BF16)


def _project(x, mod, W, rope, *, latent):
    t = x.shape[0]
    tm = PROJ_TM
    tokens_per_batch = DEC_SEQ if latent else t
    steps_per_batch = tokens_per_batch // tm
    row = lambda w: pl.BlockSpec((tm, w), lambda i: (i, 0))
    common_in = [
        row(D_MODEL),
        pl.BlockSpec((1, 6, D_MODEL), lambda i: (i // steps_per_batch, 0, 0)),
        _full(W["wa"].shape), _full(W["wm"].shape), _full((1, MLA_Q_LORA)), _full((1, MLA_KV_LORA)),
        _full(W["wqb"].shape), _full(W["wkvb"].shape),
    ]
    common_args = [x, mod, W["wa"], W["wm"], W["qag"], W["kvag"], W["wqb"], W["wkvb"]]
    bf = lambda w: jax.ShapeDtypeStruct((t, w), BF16)
    f32 = lambda w: jax.ShapeDtypeStruct((t, w), F32)
    if latent:
        tab = pl.BlockSpec((tm, LANES), lambda i: (i % steps_per_batch, 0))
        widths = [512, 512, 512, 512, 512, 512, 512, 256, 128, 512, 512]
        return pl.pallas_call(
            _proj_lat_kernel,
            out_shape=[bf(w) for w in widths],
            grid=(t // tm,),
            in_specs=common_in + [_full(W["wp"].shape), _full(W["wqp"].shape), tab, tab, tab, tab],
            out_specs=[row(w) for w in widths],
            compiler_params=_params("parallel"),
            name="proj_lat",
        )(*common_args, W["wp"], W["wqp"], rope["cd"], rope["sd"], rope["cm"], rope["sm"])
    bf_w = [512, 512, 512, 256, 128, 512, 512]
    f32_w = [512, 512, 512, 512, 256, 32]
    return pl.pallas_call(
        _proj_ctx_kernel,
        out_shape=[bf(w) for w in bf_w] + [f32(w) for w in f32_w],
        grid=(t // tm,),
        in_specs=common_in,
        out_specs=[row(w) for w in bf_w + f32_w],
        compiler_params=_params("parallel"),
        name="proj_ctx",
    )(*common_args)


def _lane_iota():
    return lax.broadcasted_iota(jnp.int32, (1, LANES), 1)


def _softmax_parts(parts):
    m = functools.reduce(jnp.maximum, [jnp.max(s, axis=-1, keepdims=True) for s in parts])
    es = [jnp.exp(s - m) for s in parts]
    l = functools.reduce(lambda u, v: u + v, [jnp.sum(e, axis=-1, keepdims=True) for e in es])
    return [e / l for e in es]


def _diff_lambda(lam_ref, layer):
    lp = lam_ref[...]
    lam_init = 0.8 - 0.6 * float(np.exp(-0.3 * layer))
    s1 = jnp.sum(lp[0:1] * lp[1:2], axis=-1, keepdims=True)
    s2 = jnp.sum(lp[2:3] * lp[3:4], axis=-1, keepdims=True)
    return jnp.exp(s1) - jnp.exp(s2) + lam_init, lam_init


def _diff_heads(q_ref, ks, vs, lam_ref, g_ref, o_ref, layer):
    lam, lam_init = _diff_lambda(lam_ref, layer)
    first_map = _lane_iota() < DIFF_DH
    g = g_ref[...]
    for h in range(DIFF_HEADS):
        hs = slice(LANES * h, LANES * (h + 1))
        q = q_ref[:, hs]
        q1 = jnp.where(first_map, q, jnp.zeros_like(q))
        q2 = jnp.where(first_map, jnp.zeros_like(q), q)
        kk = [k[:, hs].astype(BF16) for k in ks]
        p1 = _softmax_parts([_dot_nt(q1, k) for k in kk])
        p2 = _softmax_parts([_dot_nt(q2, k) for k in kk])
        o = None
        for a1, a2, v in zip(p1, p2, vs):
            part = _dot((a1 - lam * a2).astype(BF16), v[:, hs].astype(BF16))
            o = part if o is None else o + part
        o = _rms(o, g) * (1.0 - lam_init)
        o_ref[:, hs] = o.astype(BF16)


def _pair_heads(q_of, k_of, v_of, bias_of, o_ref, n_pairs):
    first = _lane_iota() < 64
    for j in range(n_pairs):
        ps = slice(LANES * j, LANES * (j + 1))
        ks = k_of(j)
        vs = v_of(j)
        outs = []
        for hh in range(2):
            q = q_of(j, hh)
            ss = [_dot_nt(q, k) for k in ks]
            bs = bias_of(2 * j + hh)
            ss = [s if b is None else s + b for s, b in zip(ss, bs)]
            ps_ = _softmax_parts(ss)
            o = None
            for p, v in zip(ps_, vs):
                part = _dot(p.astype(BF16), v)
                o = part if o is None else o + part
            outs.append(o)
        o_ref[:, ps] = jnp.where(first, outs[0], outs[1]).astype(BF16)


def _na_q(q_ref):
    first = _lane_iota() < NA_DH

    def q_of(j, hh):
        q = q_ref[:, LANES * j:LANES * (j + 1)]
        keep = first if hh == 0 else jnp.logical_not(first)
        return jnp.where(keep, q, jnp.zeros_like(q))
    return q_of


def _mla_q(qn_ref, qr_ref):
    lane = _lane_iota()
    first = lane < MLA_NOPE

    def q_of(j, hh):
        h = 2 * j + hh
        qn = qn_ref[:, LANES * j:LANES * (j + 1)]
        keep = first if hh == 0 else jnp.logical_not(first)
        qn = jnp.where(keep, qn, jnp.zeros_like(qn))
        qr = qr_ref[:, LANES * (h // 4):LANES * (h // 4 + 1)]
        qr = jnp.where((lane // MLA_ROPE) == (h % 4), qr, jnp.zeros_like(qr))
        return jnp.concatenate([qn, qr], axis=1)
    return q_of


def _attn_ctx_kernel(layer, dq_ref, dk_ref, dv_ref, nq_ref, nk_ref, nv_ref,
                     mqn_ref, mqr_ref, kn_ref, kr4_ref, vn_ref, lam_ref, g_ref,
                     od_ref, on_ref, om_ref):
    _diff_heads(dq_ref, [dk_ref], [dv_ref], lam_ref, g_ref, od_ref, layer)
    none = lambda h: [None]
    pair = lambda j: slice(LANES * j, LANES * (j + 1))
    _pair_heads(_na_q(nq_ref),
                lambda j: [nk_ref[:, pair(j)].astype(BF16)],
                lambda j: [nv_ref[:, pair(j)].astype(BF16)],
                none, on_ref, NA_HEADS // 2)
    kr4 = kr4_ref[...]
    _pair_heads(_mla_q(mqn_ref, mqr_ref),
                lambda j: [jnp.concatenate([kn_ref[:, pair(j)], kr4], axis=1)],
                lambda j: [vn_ref[:, pair(j)]],
                none, om_ref, MLA_HEADS // 2)


def _attn_ctx(p, lam, g, layer):
    t = p["dq"].shape[0]
    row = lambda w: pl.BlockSpec((SEQ, w), lambda b: (b, 0))
    names = ["dq", "dk", "dv", "nq", "nk", "nv", "mqn", "mqr", "kn", "kr4", "vn"]
    out = jax.ShapeDtypeStruct((t, 512), BF16)
    return pl.pallas_call(
        functools.partial(_attn_ctx_kernel, layer),
        out_shape=[out, out, out],
        grid=(t // SEQ,),
        in_specs=[row(p[n].shape[1]) for n in names] + [_full((4, DIFF_DH)), _full((1, LANES))],
        out_specs=[row(512)] * 3,
        compiler_params=_params("parallel"),
        name="attn_ctx",
    )(*[p[n] for n in names], lam, g)


def _cache_spec(width, layer):
    return pl.BlockSpec((None, None, PAST_LEN, width), lambda b, q: (b, layer, 0, 0))


def _batch_spec(width):
    return pl.BlockSpec((None, DEC_SEQ, width), lambda b, q: (b, 0, 0))


def _qtile_spec(width):
    steps = DEC_SEQ // ATT_TQ
    return pl.BlockSpec((ATT_TQ, width), lambda b, q: (b * steps + q, 0))


def _attn_diff_lat_kernel(layer, q_ref, kc_ref, vc_ref, kn_ref, vn_ref, lam_ref, g_ref, o_ref):
    _diff_heads(q_ref, [kc_ref, kn_ref], [vc_ref, vn_ref], lam_ref, g_ref, o_ref, layer)


def _attn_diff_lat(p, cache_k, cache_v, lam, g, layer):
    t = p["dq"].shape[0]
    b3 = lambda a: a.reshape(DEC_BATCH, DEC_SEQ, a.shape[1])
    return pl.pallas_call(
        functools.partial(_attn_diff_lat_kernel, layer),
        out_shape=jax.ShapeDtypeStruct((t, DIFF_W), BF16),
        grid=(DEC_BATCH, DEC_SEQ // ATT_TQ),
        in_specs=[_qtile_spec(DIFF_W), _cache_spec(DIFF_W, layer), _cache_spec(DIFF_W, layer),
                  _batch_spec(DIFF_W), _batch_spec(DIFF_W),
                  pl.BlockSpec((4, DIFF_DH), lambda b, q: (0, 0)),
                  pl.BlockSpec((1, LANES), lambda b, q: (0, 0))],
        out_specs=_qtile_spec(DIFF_W),
        compiler_params=_params("parallel", "parallel"),
        name="attn_diff_lat",
    )(p["dq"], cache_k, cache_v, b3(p["dk"]), b3(p["dv"]), lam, g)


def _na_key_start(q):
    return jnp.clip(q * NA_Q_ROWS - NA_WIN_ROWS // 2, 0, GRID_ROWS - NA_KEY_ROWS)


def _attn_na_lat_kernel(q_ref, kc_ref, vc_ref, kn_ref, vn_ref, bias_ref, o_ref):
    start = pl.multiple_of(_na_key_start(pl.program_id(1)) * GRID_W, GRID_W)
    kw = kn_ref[pl.ds(start, NA_KEYS), :]
    vw = vn_ref[pl.ds(start, NA_KEYS), :]
    pair = lambda j: slice(LANES * j, LANES * (j + 1))
    _pair_heads(_na_q(q_ref),
                lambda j: [kc_ref[:, pair(j)].astype(BF16), kw[:, pair(j)]],
                lambda j: [vc_ref[:, pair(j)].astype(BF16), vw[:, pair(j)]],
                lambda h: [None, bias_ref[h]],
                o_ref, NA_HEADS // 2)


def _na_bias_tables():
    blocks = []
    qi = np.arange(ATT_TQ)
    kj = np.arange(NA_KEYS)
    c = qi % GRID_W
    c0 = np.clip(c - NA_WIN_COLS // 2, 0, GRID_W - NA_WIN_COLS)
    kc = kj % GRID_W
    for qb in range(DEC_SEQ // ATT_TQ):
        ks = int(np.clip(qb * NA_Q_ROWS - NA_WIN_ROWS // 2, 0, GRID_ROWS - NA_KEY_ROWS))
        r = qb * NA_Q_ROWS + qi // GRID_W
        r0 = np.clip(r - NA_WIN_ROWS // 2, 0, GRID_ROWS - NA_WIN_ROWS)
        kr = ks + kj // GRID_W
        valid = ((kr[None, :] >= r0[:, None]) & (kr[None, :] < r0[:, None] + NA_WIN_ROWS)
                 & (kc[None, :] >= c0[:, None]) & (kc[None, :] < c0[:, None] + NA_WIN_COLS))
        ridx = np.clip(kr[None, :] - r[:, None] + NA_WIN_ROWS - 1, 0, 2 * NA_WIN_ROWS - 2)
        cidx = np.clip(kc[None, :] - c[:, None] + NA_WIN_COLS - 1, 0, 2 * NA_WIN_COLS - 2)
        blocks.append((valid, ridx, cidx))
    kinds = []
    kind_of_block = []
    for blk in blocks:
        for n, other in enumerate(kinds):
            if all(np.array_equal(u, v) for u, v in zip(blk, other)):
                kind_of_block.append(n)
                break
        else:
            kind_of_block.append(len(kinds))
            kinds.append(blk)
    return kinds, kind_of_block


_NA_KINDS, _NA_KIND_OF_BLOCK = _na_bias_tables()


def _na_bias(rpb):
    out = []
    for valid, ridx, cidx in _NA_KINDS:
        b = rpb[:, ridx, cidx]
        out.append(jnp.where(valid[None], b, NEG_BIG))
    return jnp.stack(out).astype(F32)


def _na_kind(q):
    return (q > 0).astype(jnp.int32) + (q == DEC_SEQ // ATT_TQ - 1).astype(jnp.int32)


def _attn_na_lat(p, cache_k, cache_v, bias, layer):
    assert _NA_KIND_OF_BLOCK == [0] + [1] * (DEC_SEQ // ATT_TQ - 2) + [2]
    t = p["nq"].shape[0]
    b3 = lambda a: a.reshape(DEC_BATCH, DEC_SEQ, a.shape[1])
    return pl.pallas_call(
        _attn_na_lat_kernel,
        out_shape=jax.ShapeDtypeStruct((t, NA_W), BF16),
        grid=(DEC_BATCH, DEC_SEQ // ATT_TQ),
        in_specs=[_qtile_spec(NA_W), _cache_spec(NA_W, layer), _cache_spec(NA_W, layer),
                  _batch_spec(NA_W), _batch_spec(NA_W),
                  pl.BlockSpec((None, NA_HEADS, ATT_TQ, NA_KEYS), lambda b, q: (_na_kind(q), 0, 0, 0))],
        out_specs=_qtile_spec(NA_W),
        compiler_params=_params("parallel", "parallel"),
        name="attn_na_lat",
    )(p["nq"], cache_k, cache_v, b3(p["nk"]), b3(p["nv"]), bias)


def _mla_cache_kernel(ckv_ref, kr_ref, wkvb_ref, rep_ref, kc_ref, vc_ref, krc_ref):
    kv = _dot(ckv_ref[...].astype(BF16), wkvb_ref[...])
    kc_ref[...] = kv[:, :512].astype(BF16)
    vc_ref[...] = kv[:, 512:].astype(BF16)
    krc_ref[...] = _dot(kr_ref[...].astype(BF16), rep_ref[...]).astype(BF16)


def _mla_cache(cache_ckv, cache_kr, wkvb, layer):
    rep = jnp.asarray(np.tile(np.eye(MLA_ROPE, dtype=np.float32), (1, LANES // MLA_ROPE)), BF16)
    spec_in = lambda w: pl.BlockSpec((None, None, PAST_LEN, w), lambda b: (b, layer, 0, 0))
    spec_out = lambda w: pl.BlockSpec((None, PAST_LEN, w), lambda b: (b, 0, 0))
    shp = lambda w: jax.ShapeDtypeStruct((DEC_BATCH, PAST_LEN, w), BF16)
    return pl.pallas_call(
        _mla_cache_kernel,
        out_shape=[shp(512), shp(512), shp(LANES)],
        grid=(DEC_BATCH,),
        in_specs=[spec_in(MLA_KV_LORA), spec_in(MLA_ROPE), _full(wkvb.shape), _full(rep.shape)],
        out_specs=[spec_out(512), spec_out(512), spec_out(LANES)],
        compiler_params=_params("parallel"),
        name="mla_cache",
    )(cache_ckv, cache_kr, wkvb, rep)


def _attn_mla_lat_kernel(qn_ref, qr_ref, kc_ref, krc_ref, vc_ref, kn_ref, krn_ref, vn_ref, o_ref):
    pair = lambda j: slice(LANES * j, LANES * (j + 1))
    krc = krc_ref[...]
    krn = krn_ref[...]
    _pair_heads(_mla_q(qn_ref, qr_ref),
                lambda j: [jnp.concatenate([kc_ref[:, pair(j)], krc], axis=1),
                           jnp.concatenate([kn_ref[:, pair(j)], krn], axis=1)],
                lambda j: [vc_ref[:, pair(j)], vn_ref[:, pair(j)]],
                lambda h: [None, None],
                o_ref, MLA_HEADS // 2)


def _attn_mla_lat(p, kc, vc, krc):
    t = p["mqn"].shape[0]
    b3 = lambda a: a.reshape(DEC_BATCH, DEC_SEQ, a.shape[1])
    cspec = lambda w: pl.BlockSpec((None, PAST_LEN, w), lambda b, q: (b, 0, 0))
    return pl.pallas_call(
        _attn_mla_lat_kernel,
        out_shape=jax.ShapeDtypeStruct((t, MLA_W), BF16),
        grid=(DEC_BATCH, DEC_SEQ // ATT_TQ),
        in_specs=[_qtile_spec(512), _qtile_spec(256), cspec(512), cspec(LANES), cspec(512),
                  _batch_spec(512), _batch_spec(LANES), _batch_spec(512)],
        out_specs=_qtile_spec(MLA_W),
        compiler_params=_params("parallel", "parallel"),
        name="attn_mla_lat",
    )(p["mqn"], p["mqr"], kc, krc, vc, b3(p["kn"]), b3(p["kr4"]), b3(p["vn"]))


def _merge_kernel(x_ref, mod_ref, od_ref, on_ref, om_ref, wg_ref, wbd_ref, wbn_ref, wbm_ref, wo_ref,
                  g_ref, b_ref, x1_ref, h2_ref):
    x = x_ref[...]
    m = mod_ref[0]
    h = (x * (1.0 + m[1:2]) + m[0:1]).astype(BF16)
    gates = _dot(h, wg_ref[...])
    mix = (_sigmoid(gates[:, 0:1024]) * _dot(od_ref[...], wbd_ref[...])
           + _sigmoid(gates[:, 1024:2048]) * _dot(on_ref[...], wbn_ref[...])
           + _sigmoid(gates[:, 2048:3072]) * _dot(om_ref[...], wbm_ref[...]))
    out = _dot(mix.astype(BF16), wo_ref[...])
    x1 = _layer_norm(DN_ALPHA * x + m[2:3] * out, g_ref[...], b_ref[...])
    x1_ref[...] = x1
    h2_ref[...] = (x1 * (1.0 + m[4:5]) + m[3:4]).astype(BF16)


def _merge(x, mod, od, on, om, W, *, latent):
    t = x.shape[0]
    tm = PROJ_TM
    steps_per_batch = (DEC_SEQ if latent else t) // tm
    row = lambda w: pl.BlockSpec((tm, w), lambda i: (i, 0))
    return pl.pallas_call(
        _merge_kernel,
        out_shape=[jax.ShapeDtypeStruct((t, D_MODEL), F32), jax.ShapeDtypeStruct((t, D_MODEL), BF16)],
        grid=(t // tm,),
        in_specs=[row(D_MODEL),
                  pl.BlockSpec((1, 6, D_MODEL), lambda i: (i // steps_per_batch, 0, 0)),
                  row(512), row(512), row(512),
                  _full(W["wg"].shape), _full(W["wbd"].shape), _full(W["wbn"].shape),
                  _full(W["wbm"].shape), _full(W["wo"].shape),
                  _full((1, D_MODEL)), _full((1, D_MODEL))],
        out_specs=[row(D_MODEL), row(D_MODEL)],
        compiler_params=_params("parallel"),
        name="merge",
    )(x, mod, od, on, om, W["wg"], W["wbd"], W["wbn"], W["wbm"], W["wo"], W["ln1_g"], W["ln1_b"])


def _first_index_of_max(vals, idx, sentinel):
    mx = functools.reduce(jnp.maximum, [jnp.max(v, axis=0, keepdims=True) for v in vals])
    cand = [jnp.min(jnp.where(v == mx, i, sentinel), axis=0, keepdims=True) for v, i in zip(vals, idx)]
    return mx, functools.reduce(jnp.minimum, cand)


def _router_kernel(h_ref, wr_ref, bias_ref, gate_ref):
    tm = h_ref.shape[0]
    logits = _dot_nt(wr_ref[...], h_ref[...])
    scores = _sigmoid(logits)
    biased = scores + bias_ref[...]
    member = lax.broadcasted_iota(jnp.int32, (GROUP_SIZE, tm), 0)
    slabs = [biased[GROUP_SIZE * g:GROUP_SIZE * (g + 1)] for g in range(N_GROUPS)]
    gscore = []
    for s in slabs:
        m1, first = _first_index_of_max([s], [member], GROUP_SIZE)
        m2 = jnp.max(jnp.where(member == first, -jnp.inf, s), axis=0, keepdims=True)
        gscore.append(m1 + m2)
    gs = jnp.concatenate(gscore, axis=0)
    gidx = lax.broadcasted_iota(jnp.int32, (N_GROUPS, tm), 0)
    gsel = jnp.zeros((N_GROUPS, tm), F32)
    for _ in range(TOPK_GROUPS):
        _, first = _first_index_of_max([gs], [gidx], N_GROUPS)
        pick = gidx == first
        gsel = jnp.where(pick, 1.0, gsel)
        gs = jnp.where(pick, -jnp.inf, gs)
    cur = [jnp.where(gsel[g:g + 1] > 0.0, slabs[g], -jnp.inf) for g in range(N_GROUPS)]
    eidx = [member + GROUP_SIZE * g for g in range(N_GROUPS)]
    sel = [jnp.zeros((GROUP_SIZE, tm), F32) for _ in range(N_GROUPS)]
    for _ in range(TOP_K):
        _, first = _first_index_of_max(cur, eidx, N_EXPERTS)
        for g in range(N_GROUPS):
            pick = eidx[g] == first
            sel[g] = jnp.where(pick, 1.0, sel[g])
            cur[g] = jnp.where(pick, -jnp.inf, cur[g])
    w = [jnp.where(sel[g] > 0.0, scores[GROUP_SIZE * g:GROUP_SIZE * (g + 1)], 0.0) for g in range(N_GROUPS)]
    total = functools.reduce(lambda u, v: u + v, [jnp.sum(x, axis=0, keepdims=True) for x in w])
    w = [x / total * ROUTED_SCALE for x in w]
    gate_t = jnp.concatenate(w + [jnp.zeros((LANES - N_EXPERTS, tm), F32)], axis=0)
    gate_ref[...] = gate_t.T


def _router(h2, wr_t, bias_col):
    t = h2.shape[0]
    tm = PROJ_TM
    return pl.pallas_call(
        _router_kernel,
        out_shape=jax.ShapeDtypeStruct((t, LANES), F32),
        grid=(t // tm,),
        in_specs=[pl.BlockSpec((tm, D_MODEL), lambda i: (i, 0)),
                  _full((N_EXPERTS, D_MODEL)), _full((N_EXPERTS, 1))],
        out_specs=pl.BlockSpec((tm, LANES), lambda i: (i, 0)),
        compiler_params=_params("parallel"),
        name="router",
    )(h2, wr_t, bias_col)


def _moe_kernel(h_ref, gate_ref, weg_ref, weu_ref, wed_ref, x1_ref, mod_ref, wsg_ref, wsu_ref, wsd_ref,
                g_ref, b_ref, out_ref, acc_ref):
    e = pl.program_id(1)
    h = h_ref[...]

    @pl.when(e == 0)
    def _():
        acc_ref[...] = jnp.zeros_like(acc_ref)

    act = _silu(_dot(h, weg_ref[0])) * _dot(h, weu_ref[0])
    y = _dot(act.astype(BF16), wed_ref[0])
    ge = jnp.sum(jnp.where(_lane_iota() == e, gate_ref[...], 0.0), axis=-1, keepdims=True)
    acc_ref[...] += ge * y

    @pl.when(e == N_EXPERTS - 1)
    def _():
        shared = _dot((_silu(_dot(h, wsg_ref[...])) * _dot(h, wsu_ref[...])).astype(BF16), wsd_ref[...])
        m = mod_ref[0]
        y2 = DN_ALPHA * x1_ref[...] + m[5:6] * (acc_ref[...] + shared)
        out_ref[...] = _layer_norm(y2, g_ref[...], b_ref[...])


def _moe(h2, gate, x1, mod, W, *, latent):
    t = h2.shape[0]
    tm = MOE_TM
    steps_per_batch = (DEC_SEQ if latent else t) // tm
    row = lambda w: pl.BlockSpec((tm, w), lambda i, e: (i, 0))
    full = lambda shape: pl.BlockSpec(shape, lambda i, e: (0,) * len(shape))
    return pl.pallas_call(
        _moe_kernel,
        out_shape=jax.ShapeDtypeStruct((t, D_MODEL), F32),
        grid=(t // tm, N_EXPERTS),
        in_specs=[row(D_MODEL), row(LANES),
                  pl.BlockSpec((1, D_MODEL, EXPERT_FF), lambda i, e: (e, 0, 0)),
                  pl.BlockSpec((1, D_MODEL, EXPERT_FF), lambda i, e: (e, 0, 0)),
                  pl.BlockSpec((1, EXPERT_FF, D_MODEL), lambda i, e: (e, 0, 0)),
                  row(D_MODEL),
                  pl.BlockSpec((1, 6, D_MODEL), lambda i, e: (i // steps_per_batch, 0, 0)),
                  full((D_MODEL, SHARED_FF)), full((D_MODEL, SHARED_FF)), full((SHARED_FF, D_MODEL)),
                  full((1, D_MODEL)), full((1, D_MODEL))],
        out_specs=row(D_MODEL),
        scratch_shapes=[pltpu.VMEM((tm, D_MODEL), F32)],
        compiler_params=_params("parallel", "arbitrary"),
        name="moe",
    )(h2, gate, W["weg"], W["weu"], W["wed"], x1, mod, W["wsg"], W["wsu"], W["wsd"],
      W["ln2_g"], W["ln2_b"])


def _rope_partner(n_blocks, block):
    half = block // 2
    i = np.arange(n_blocks * block)
    return np.where((i % block) < half, i + half, i - half)


_QB_NOPE = np.concatenate([np.arange(MLA_NOPE) + (MLA_NOPE + MLA_ROPE) * h for h in range(MLA_HEADS)])
_QB_ROPE = np.concatenate([np.arange(MLA_ROPE) + (MLA_NOPE + MLA_ROPE) * h + MLA_NOPE for h in range(MLA_HEADS)])
_KVB_NOPE = np.concatenate([np.arange(MLA_NOPE) + (MLA_NOPE + MLA_V) * h for h in range(MLA_HEADS)])
_KVB_V = np.concatenate([np.arange(MLA_V) + (MLA_NOPE + MLA_V) * h + MLA_NOPE for h in range(MLA_HEADS)])
_DIFF_PARTNER = _rope_partner(2 * DIFF_W // 32, 32)
_MLA_PARTNER = _rope_partner(MLA_HEADS * MLA_ROPE // 16, 16)
_KR_PARTNER = _rope_partner(MLA_ROPE // 16, 16)


def _layer_weights(l, w_in, mla_qa_g, mla_wq_b, mla_kva_g, mla_wkv_b, w_branch_diff, w_branch_na,
                   w_branch_mla, w_out, ln1_g, ln1_b, ln2_g, ln2_b, w_router, router_bias,
                   w_exp_gate, w_exp_up, w_exp_down, w_sh_gate, w_sh_up, w_sh_down):
    win = w_in[l].astype(BF16)
    wa = win[:, :3072]
    qa_kva = win[:, 3072:3712]
    kr = win[:, 3712:3744]
    kr4 = jnp.tile(kr, (1, LANES // MLA_ROPE))
    krp4 = jnp.tile(kr[:, _KR_PARTNER], (1, LANES // MLA_ROPE))
    wqb = mla_wq_b[l].astype(BF16)
    wq_rope = wqb[:, _QB_ROPE]
    wkvb = mla_wkv_b[l].astype(BF16)
    row = lambda v: v[l].reshape(1, -1).astype(F32)
    return {
        "wa": wa,
        "wp": wa[:, :2 * DIFF_W][:, _DIFF_PARTNER],
        "wm_ctx": jnp.concatenate([qa_kva, kr4], axis=1),
        "wm_lat": jnp.concatenate([qa_kva, kr4, krp4], axis=1),
        "qag": row(mla_qa_g), "kvag": row(mla_kva_g),
        "wqb": jnp.concatenate([wqb[:, _QB_NOPE], wq_rope], axis=1),
        "wqp": wq_rope[:, _MLA_PARTNER],
        "wkvb": jnp.concatenate([wkvb[:, _KVB_NOPE], wkvb[:, _KVB_V]], axis=1),
        "wg": win[:, 3744:],
        "wbd": w_branch_diff[l].astype(BF16), "wbn": w_branch_na[l].astype(BF16),
        "wbm": w_branch_mla[l].astype(BF16), "wo": w_out[l].astype(BF16),
        "ln1_g": row(ln1_g), "ln1_b": row(ln1_b), "ln2_g": row(ln2_g), "ln2_b": row(ln2_b),
        "wr_t": w_router[l].T.astype(BF16), "rbias": router_bias[l].reshape(N_EXPERTS, 1).astype(F32),
        "weg": w_exp_gate[l].astype(BF16), "weu": w_exp_up[l].astype(BF16), "wed": w_exp_down[l].astype(BF16),
        "wsg": w_sh_gate[l].astype(BF16), "wsu": w_sh_up[l].astype(BF16), "wsd": w_sh_down[l].astype(BF16),
    }


def _rope_tables():
    t = jnp.arange(DEC_SEQ)
    pos = [(t // GRID_W).astype(F32), (t % GRID_W).astype(F32)]

    def table(block):
        half = block // 4
        inv = ROPE_BASE ** (-jnp.arange(half, dtype=F32) / half)
        cos, sin = [], []
        for p in pos:
            ang = p[:, None] * inv[None, :]
            cos += [jnp.cos(ang), jnp.cos(ang)]
            sin += [-jnp.sin(ang), jnp.sin(ang)]
        reps = LANES // block
        return (jnp.tile(jnp.concatenate(cos, axis=1), (1, reps)),
                jnp.tile(jnp.concatenate(sin, axis=1), (1, reps)))

    cd, sd = table(DIFF_DH)
    cm, sm = table(MLA_ROPE)
    return {"cd": cd, "sd": sd, "cm": cm, "sm": sm}


def kernel(x_prompt, x_sample, cache_diff_k, cache_diff_v, cache_na_k, cache_na_v, cache_mla_ckv, cache_mla_krope, c, c_ctx, w_mod, b_mod, w_in, diff_lambda, diff_subln_g, na_rpb, mla_qa_g, mla_wq_b, mla_kva_g, mla_wkv_b, w_branch_diff, w_branch_na, w_branch_mla, w_out, ln1_g, ln1_b, ln2_g, ln2_b, w_router, router_bias, w_exp_gate, w_exp_up, w_exp_down, w_sh_gate, w_sh_up, w_sh_down):
    t_ctx = BATCH * SEQ
    t_lat = DEC_BATCH * DEC_SEQ
    cvec = jnp.concatenate([c, c_ctx[None, :], jnp.zeros((MOD_ROWS - DEC_BATCH - 1, D_MODEL), F32)], axis=0)
    mods = _modulation(cvec, w_mod, b_mod).reshape(DEPTH, MOD_ROWS, 6, D_MODEL)
    rope = _rope_tables()
    ck_d = cache_diff_k.reshape(DEC_BATCH, DEPTH, PAST_LEN, DIFF_W)
    cv_d = cache_diff_v.reshape(DEC_BATCH, DEPTH, PAST_LEN, DIFF_W)
    ck_n = cache_na_k.reshape(DEC_BATCH, DEPTH, PAST_LEN, NA_W)
    cv_n = cache_na_v.reshape(DEC_BATCH, DEPTH, PAST_LEN, NA_W)

    xc = x_prompt.reshape(t_ctx, D_MODEL)
    xl = x_sample.reshape(t_lat, D_MODEL)
    states = []
    for l in range(DEPTH):
        W = _layer_weights(l, w_in, mla_qa_g, mla_wq_b, mla_kva_g, mla_wkv_b, w_branch_diff, w_branch_na,
                           w_branch_mla, w_out, ln1_g, ln1_b, ln2_g, ln2_b, w_router, router_bias,
                           w_exp_gate, w_exp_up, w_exp_down, w_sh_gate, w_sh_up, w_sh_down)
        lam = diff_lambda[l].astype(F32)
        subln = diff_subln_g[l].reshape(1, LANES).astype(F32)
        mod_c = mods[l, DEC_BATCH:DEC_BATCH + 1]
        mod_l = mods[l, :DEC_BATCH]

        Wc = dict(W, wm=W["wm_ctx"])
        names = ["dq", "nq", "mqn", "mqr", "kr4", "kn", "vn", "dk", "dv", "nk", "nv", "ckv", "kr"]
        pc = dict(zip(names, _project(xc, mod_c, Wc, None, latent=False)))
        od, on, om = _attn_ctx(pc, lam, subln, l)
        x1, h2 = _merge(xc, mod_c, od, on, om, W, latent=False)
        gate = _router(h2, W["wr_t"], W["rbias"])
        xc = _moe(h2, gate, x1, mod_c, W, latent=False)
        states.append(pc)

        Wl = dict(W, wm=W["wm_lat"])
        names = ["dq", "dk", "dv", "nq", "nk", "nv", "mqn", "mqr", "kr4", "kn", "vn"]
        pll = dict(zip(names, _project(xl, mod_l, Wl, rope, latent=True)))
        od = _attn_diff_lat(pll, ck_d, cv_d, lam, subln, l)
        on = _attn_na_lat(pll, ck_n, cv_n, _na_bias(na_rpb[l].astype(F32)), l)
        kc, vc, krc = _mla_cache(cache_mla_ckv, cache_mla_krope, W["wkvb"], l)
        om = _attn_mla_lat(pll, kc, vc, krc)
        x1, h2 = _merge(xl, mod_l, od, on, om, W, latent=True)
        gate = _router(h2, W["wr_t"], W["rbias"])
        xl = _moe(h2, gate, x1, mod_l, W, latent=True)

    def stack(name, shape):
        return jnp.stack([s[name].reshape((BATCH, SEQ) + shape) for s in states], axis=1)

    return (xc.reshape(BATCH, SEQ, D_MODEL), xl.reshape(DEC_BATCH, DEC_SEQ, D_MODEL),
            stack("dk", (DIFF_HEADS, 2 * DIFF_DH)), stack("dv", (DIFF_HEADS, 2 * DIFF_DH)),
            stack("nk", (NA_HEADS, NA_DH)), stack("nv", (NA_HEADS, NA_DH)),
            stack("ckv", (MLA_KV_LORA,)), stack("kr", (MLA_ROPE,)))
```

```python
import functools

import numpy as np
import jax
import jax.numpy as jnp
from jax import lax
from jax.experimental import pallas as pl
from jax.experimental.pallas import tpu as pltpu

D_MODEL = 1024
BATCH = 32
SEQ = 256
DEPTH = 2
DEC_BATCH = 8
DEC_SEQ = 2048
PAST_LEN = 512
GRID_W = 64
GRID_ROWS = DEC_SEQ // GRID_W
ROPE_BASE = 10000.0
DIFF_HEADS = 4
DIFF_DH = 64
DIFF_W = 512
NA_HEADS = 8
NA_DH = 64
NA_W = 512
NA_WIN_ROWS = 8
NA_WIN_COLS = 16
MLA_HEADS = 8
MLA_Q_LORA = 384
MLA_KV_LORA = 256
MLA_NOPE = 64
MLA_ROPE = 32
MLA_V = 64
MLA_W = 512
N_EXPERTS = 64
N_GROUPS = 8
GROUP_SIZE = N_EXPERTS // N_GROUPS
TOPK_GROUPS = 4
TOP_K = 8
EXPERT_FF = 256
SHARED_FF = 256
ROUTED_SCALE = 2.5
DN_ALPHA = (2 * DEPTH) ** 0.25
LN_EPS = 1e-5
RMS_EPS = 1e-6

F32 = jnp.float32
BF16 = jnp.bfloat16

LANES = 128
VMEM_LIMIT_BYTES = 56 * 1024 * 1024
NEG_BIG = -1e30

DIFF_SCALE = DIFF_DH ** -0.5
NA_SCALE = NA_DH ** -0.5
MLA_SCALE = (MLA_NOPE + MLA_ROPE) ** -0.5

PROJ_TM = 512
ATT_TQ = 256
NA_Q_ROWS = ATT_TQ // GRID_W
NA_KEY_ROWS = NA_Q_ROWS + NA_WIN_ROWS
NA_KEYS = NA_KEY_ROWS * GRID_W
MOE_TILE = 256
DISPATCH_TM = 512
COMBINE_TM = 128


def _dot(a, b):
    return jnp.dot(a, b, preferred_element_type=F32)


def _dot_nt(a, b):
    return lax.dot_general(a, b, (((1,), (1,)), ((), ())), preferred_element_type=F32)


def _sigmoid(x):
    return 1.0 / (1.0 + jnp.exp(-x))


def _silu(x):
    return x * _sigmoid(x)


def _params(*sem):
    return pltpu.CompilerParams(dimension_semantics=sem, vmem_limit_bytes=VMEM_LIMIT_BYTES)


def _full(shape):
    n = len(shape)
    return pl.BlockSpec(shape, lambda *_: (0,) * n)


def _layer_norm(y, g, b):
    mu = jnp.mean(y, axis=-1, keepdims=True)
    yc = y - mu
    var = jnp.mean(yc * yc, axis=-1, keepdims=True)
    return yc * lax.rsqrt(var + LN_EPS) * g + b


def _rms(x, g):
    return x * lax.rsqrt(jnp.mean(x * x, axis=-1, keepdims=True) + RMS_EPS) * g


MOD_ROWS = 16
MOD_TN = 1536


def _mod_kernel(c_ref, w_ref, b_ref, o_ref):
    s = _silu(c_ref[...]).astype(BF16)
    o_ref[0] = _dot(s, w_ref[0].astype(BF16)) + b_ref[0]


def _modulation(cvec, w_mod, b_mod):
    n = 6 * D_MODEL
    return pl.pallas_call(
        _mod_kernel,
        out_shape=jax.ShapeDtypeStruct((DEPTH, MOD_ROWS, n), F32),
        grid=(DEPTH, n // MOD_TN),
        in_specs=[
            pl.BlockSpec((MOD_ROWS, D_MODEL), lambda l, j: (0, 0)),
            pl.BlockSpec((1, D_MODEL, MOD_TN), lambda l, j: (l, 0, j)),
            pl.BlockSpec((1, 1, MOD_TN), lambda l, j: (l, 0, j)),
        ],
        out_specs=pl.BlockSpec((1, MOD_ROWS, MOD_TN), lambda l, j: (l, 0, j)),
        compiler_params=_params("parallel", "parallel"),
        name="modulation",
    )(cvec, w_mod, b_mod.reshape(DEPTH, 1, n))


def _proj_common(x_ref, mod_ref, wa_ref, wm_ref, qag_ref, kvag_ref, wqb_ref, wkvb_ref):
    m = mod_ref[0]
    h = (x_ref[...] * (1.0 + m[1:2]) + m[0:1]).astype(BF16)
    a = _dot(h, wa_ref[...])
    mm = _dot(h, wm_ref[...])
    qan = _rms(mm[:, :MLA_Q_LORA], qag_ref[...]).astype(BF16)
    mq = _dot(qan, wqb_ref[...])
    ckv = _rms(mm[:, MLA_Q_LORA:MLA_Q_LORA + MLA_KV_LORA], kvag_ref[...])
    kv = _dot(ckv.astype(BF16), wkvb_ref[...])
    return h, a, mm, qan, mq, ckv, kv


def _proj_ctx_kernel(x_ref, mod_ref, wa_ref, wm_ref, qag_ref, kvag_ref, wqb_ref, wkvb_ref,
                     dq_ref, nq_ref, mqn_ref, mqr_ref, kr4_ref, kn_ref, vn_ref,
                     dk_ref, dv_ref, nk_ref, nv_ref, ckv_ref, kr_ref):
    _, a, mm, _, mq, ckv, kv = _proj_common(x_ref, mod_ref, wa_ref, wm_ref, qag_ref, kvag_ref,
                                            wqb_ref, wkvb_ref)
    dq_ref[...] = (a[:, 0:512] * DIFF_SCALE).astype(BF16)
    dk_ref[...] = a[:, 512:1024]
    dv_ref[...] = a[:, 1024:1536]
    nq_ref[...] = (a[:, 1536:2048] * NA_SCALE).astype(BF16)
    nk_ref[...] = a[:, 2048:2560]
    nv_ref[...] = a[:, 2560:3072]
    mqn_ref[...] = (mq[:, :512] * MLA_SCALE).astype(BF16)
    mqr_ref[...] = (mq[:, 512:768] * MLA_SCALE).astype(BF16)
    kr4 = mm[:, 640:768]
    kr4_ref[...] = kr4.astype(BF16)
    kr_ref[...] = kr4[:, :MLA_ROPE]
    ckv_ref[...] = ckv
    kn_ref[...] = kv[:, :512].astype(BF16)
    vn_ref[...] = kv[:, 512:].astype(BF16)


def _proj_lat_kernel(x_ref, mod_ref, wa_ref, wm_ref, qag_ref, kvag_ref, wqb_ref, wkvb_ref,
                     wp_ref, wqp_ref, cd_ref, sd_ref, cm_ref, sm_ref,
                     dq_ref, dk_ref, dv_ref, nq_ref, nk_ref, nv_ref,
                     mqn_ref, mqr_ref, kr4_ref, kn_ref, vn_ref):
    h, a, mm, qan, mq, _, kv = _proj_common(x_ref, mod_ref, wa_ref, wm_ref, qag_ref, kvag_ref,
                                            wqb_ref, wkvb_ref)
    ap = _dot(h, wp_ref[...])
    mqp = _dot(qan, wqp_ref[...])
    cd = cd_ref[...]
    sd = sd_ref[...]
    cm = cm_ref[...]
    sm = sm_ref[...]
    for j in range(DIFF_W // LANES):
        lo, hi = LANES * j, LANES * (j + 1)
        dq_ref[:, lo:hi] = ((a[:, lo:hi] * cd + ap[:, lo:hi] * sd) * DIFF_SCALE).astype(BF16)
        dk_ref[:, lo:hi] = (a[:, 512 + lo:512 + hi] * cd + ap[:, 512 + lo:512 + hi] * sd).astype(BF16)
    dv_ref[...] = a[:, 1024:1536].astype(BF16)
    nq_ref[...] = (a[:, 1536:2048] * NA_SCALE).astype(BF16)
    nk_ref[...] = a[:, 2048:2560].astype(BF16)
    nv_ref[...] = a[:, 2560:3072].astype(BF16)
    mqn_ref[...] = (mq[:, :512] * MLA_SCALE).astype(BF16)
    for j in range(2):
        lo, hi = LANES * j, LANES * (j + 1)
        mqr_ref[:, lo:hi] = ((mq[:, 512 + lo:512 + hi] * cm + mqp[:, lo:hi] * sm) * MLA_SCALE).astype(BF16)
    kr4_ref[...] = (mm[:, 640:768] * cm + mm[:, 768:896] * sm).astype(BF16)
    kn_ref[...] = kv[:, :512].astype(BF16)
    vn_ref[...] = kv[:, 512:].astype(BF16)


def _project(x, mod, W, rope, *, latent):
    t = x.shape[0]
    tm = PROJ_TM
    tokens_per_batch = DEC_SEQ if latent else t
    steps_per_batch = tokens_per_batch // tm
    row = lambda w: pl.BlockSpec((tm, w), lambda i: (i, 0))
    common_in = [
        row(D_MODEL),
        pl.BlockSpec((1, 6, D_MODEL), lambda i: (i // steps_per_batch, 0, 0)),
        _full(W["wa"].shape), _full(W["wm"].shape), _full((1, MLA_Q_LORA)), _full((1, MLA_KV_LORA)),
        _full(W["wqb"].shape), _full(W["wkvb"].shape),
    ]
    common_args = [x, mod, W["wa"], W["wm"], W["qag"], W["kvag"], W["wqb"], W["wkvb"]]
    bf = lambda w: jax.ShapeDtypeStruct((t, w), BF16)
    f32 = lambda w: jax.ShapeDtypeStruct((t, w), F32)
    if latent:
        tab = pl.BlockSpec((tm, LANES), lambda i: (i % steps_per_batch, 0))
        widths = [512, 512, 512, 512, 512, 512, 512, 256, 128, 512, 512]
        return pl.pallas_call(
            _proj_lat_kernel,
            out_shape=[bf(w) for w in widths],
            grid=(t // tm,),
            in_specs=common_in + [_full(W["wp"].shape), _full(W["wqp"].shape), tab, tab, tab, tab],
            out_specs=[row(w) for w in widths],
            compiler_params=_params("parallel"),
            name="proj_lat",
        )(*common_args, W["wp"], W["wqp"], rope["cd"], rope["sd"], rope["cm"], rope["sm"])
    bf_w = [512, 512, 512, 256, 128, 512, 512]
    f32_w = [512, 512, 512, 512, 256, 32]
    return pl.pallas_call(
        _proj_ctx_kernel,
        out_shape=[bf(w) for w in bf_w] + [f32(w) for w in f32_w],
        grid=(t // tm,),
        in_specs=common_in,
        out_specs=[row(w) for w in bf_w + f32_w],
        compiler_params=_params("parallel"),
        name="proj_ctx",
    )(*common_args)


def _lane_iota():
    return lax.broadcasted_iota(jnp.int32, (1, LANES), 1)


def _softmax_parts(parts):
    m = functools.reduce(jnp.maximum, [jnp.max(s, axis=-1, keepdims=True) for s in parts])
    es = [jnp.exp(s - m) for s in parts]
    l = functools.reduce(lambda u, v: u + v, [jnp.sum(e, axis=-1, keepdims=True) for e in es])
    return [e / l for e in es]


def _diff_lambda(lam_ref, layer):
    lp = lam_ref[...]
    lam_init = 0.8 - 0.6 * float(np.exp(-0.3 * layer))
    s1 = jnp.sum(lp[0:1] * lp[1:2], axis=-1, keepdims=True)
    s2 = jnp.sum(lp[2:3] * lp[3:4], axis=-1, keepdims=True)
    return jnp.exp(s1) - jnp.exp(s2) + lam_init, lam_init


def _diff_heads(q_ref, ks, vs, lam_ref, g_ref, o_ref, layer):
    lam, lam_init = _diff_lambda(lam_ref, layer)
    first_map = _lane_iota() < DIFF_DH
    g = g_ref[...]
    for h in range(DIFF_HEADS):
        hs = slice(LANES * h, LANES * (h + 1))
        q = q_ref[:, hs]
        q1 = jnp.where(first_map, q, jnp.zeros_like(q))
        q2 = jnp.where(first_map, jnp.zeros_like(q), q)
        kk = [k[:, hs].astype(BF16) for k in ks]
        p1 = _softmax_parts([_dot_nt(q1, k) for k in kk])
        p2 = _softmax_parts([_dot_nt(q2, k) for k in kk])
        o = None
        for a1, a2, v in zip(p1, p2, vs):
            part = _dot((a1 - lam * a2).astype(BF16), v[:, hs].astype(BF16))
            o = part if o is None else o + part
        o = _rms(o, g) * (1.0 - lam_init)
        o_ref[:, hs] = o.astype(BF16)


def _pair_heads(q_of, k_of, v_of, bias_of, o_ref, n_pairs):
    first = _lane_iota() < 64
    for j in range(n_pairs):
        ps = slice(LANES * j, LANES * (j + 1))
        ks = k_of(j)
        vs = v_of(j)
        outs = []
        for hh in range(2):
            q = q_of(j, hh)
            ss = [_dot_nt(q, k) for k in ks]
            bs = bias_of(2 * j + hh)
            ss = [s if b is None else s + b for s, b in zip(ss, bs)]
            ps_ = _softmax_parts(ss)
            o = None
            for p, v in zip(ps_, vs):
                part = _dot(p.astype(BF16), v)
                o = part if o is None else o + part
            outs.append(o)
        o_ref[:, ps] = jnp.where(first, outs[0], outs[1]).astype(BF16)


def _na_q(q_ref):
    first = _lane_iota() < NA_DH

    def q_of(j, hh):
        q = q_ref[:, LANES * j:LANES * (j + 1)]
        keep = first if hh == 0 else jnp.logical_not(first)
        return jnp.where(keep, q, jnp.zeros_like(q))
    return q_of


def _mla_q(qn_ref, qr_ref):
    lane = _lane_iota()
    first = lane < MLA_NOPE

    def q_of(j, hh):
        h = 2 * j + hh
        qn = qn_ref[:, LANES * j:LANES * (j + 1)]
        keep = first if hh == 0 else jnp.logical_not(first)
        qn = jnp.where(keep, qn, jnp.zeros_like(qn))
        qr = qr_ref[:, LANES * (h // 4):LANES * (h // 4 + 1)]
        qr = jnp.where((lane // MLA_ROPE) == (h % 4), qr, jnp.zeros_like(qr))
        return jnp.concatenate([qn, qr], axis=1)
    return q_of


def _attn_ctx_kernel(layer, dq_ref, dk_ref, dv_ref, nq_ref, nk_ref, nv_ref,
                     mqn_ref, mqr_ref, kn_ref, kr4_ref, vn_ref, lam_ref, g_ref,
                     od_ref, on_ref, om_ref):
    _diff_heads(dq_ref, [dk_ref], [dv_ref], lam_ref, g_ref, od_ref, layer)
    none = lambda h: [None]
    pair = lambda j: slice(LANES * j, LANES * (j + 1))
    _pair_heads(_na_q(nq_ref),
                lambda j: [nk_ref[:, pair(j)].astype(BF16)],
                lambda j: [nv_ref[:, pair(j)].astype(BF16)],
                none, on_ref, NA_HEADS // 2)
    kr4 = kr4_ref[...]
    _pair_heads(_mla_q(mqn_ref, mqr_ref),
                lambda j: [jnp.concatenate([kn_ref[:, pair(j)], kr4], axis=1)],
                lambda j: [vn_ref[:, pair(j)]],
                none, om_ref, MLA_HEADS // 2)


def _attn_ctx(p, lam, g, layer):
    t = p["dq"].shape[0]
    row = lambda w: pl.BlockSpec((SEQ, w), lambda b: (b, 0))
    names = ["dq", "dk", "dv", "nq", "nk", "nv", "mqn", "mqr", "kn", "kr4", "vn"]
    out = jax.ShapeDtypeStruct((t, 512), BF16)
    return pl.pallas_call(
        functools.partial(_attn_ctx_kernel, layer),
        out_shape=[out, out, out],
        grid=(t // SEQ,),
        in_specs=[row(p[n].shape[1]) for n in names] + [_full((4, DIFF_DH)), _full((1, LANES))],
        out_specs=[row(512)] * 3,
        compiler_params=_params("parallel"),
        name="attn_ctx",
    )(*[p[n] for n in names], lam, g)


def _cache_spec(width, layer):
    return pl.BlockSpec((None, None, PAST_LEN, width), lambda b, q: (b, layer, 0, 0))


def _batch_spec(width):
    return pl.BlockSpec((None, DEC_SEQ, width), lambda b, q: (b, 0, 0))


def _qtile_spec(width):
    steps = DEC_SEQ // ATT_TQ
    return pl.BlockSpec((ATT_TQ, width), lambda b, q: (b * steps + q, 0))


def _attn_diff_lat_kernel(layer, q_ref, kc_ref, vc_ref, kn_ref, vn_ref, lam_ref, g_ref, o_ref):
    _diff_heads(q_ref, [kc_ref, kn_ref], [vc_ref, vn_ref], lam_ref, g_ref, o_ref, layer)


def _attn_diff_lat(p, cache_k, cache_v, lam, g, layer):
    t = p["dq"].shape[0]
    b3 = lambda a: a.reshape(DEC_BATCH, DEC_SEQ, a.shape[1])
    return pl.pallas_call(
        functools.partial(_attn_diff_lat_kernel, layer),
        out_shape=jax.ShapeDtypeStruct((t, DIFF_W), BF16),
        grid=(DEC_BATCH, DEC_SEQ // ATT_TQ),
        in_specs=[_qtile_spec(DIFF_W), _cache_spec(DIFF_W, layer), _cache_spec(DIFF_W, layer),
                  _batch_spec(DIFF_W), _batch_spec(DIFF_W),
                  pl.BlockSpec((4, DIFF_DH), lambda b, q: (0, 0)),
                  pl.BlockSpec((1, LANES), lambda b, q: (0, 0))],
        out_specs=_qtile_spec(DIFF_W),
        compiler_params=_params("parallel", "parallel"),
        name="attn_diff_lat",
    )(p["dq"], cache_k, cache_v, b3(p["dk"]), b3(p["dv"]), lam, g)


def _na_key_start(q):
    return jnp.clip(q * NA_Q_ROWS - NA_WIN_ROWS // 2, 0, GRID_ROWS - NA_KEY_ROWS)


def _attn_na_lat_kernel(q_ref, kc_ref, vc_ref, kn_ref, vn_ref, bias_ref, o_ref):
    start = pl.multiple_of(_na_key_start(pl.program_id(1)) * GRID_W, GRID_W)
    kw = kn_ref[pl.ds(start, NA_KEYS), :]
    vw = vn_ref[pl.ds(start, NA_KEYS), :]
    pair = lambda j: slice(LANES * j, LANES * (j + 1))
    _pair_heads(_na_q(q_ref),
                lambda j: [kc_ref[:, pair(j)].astype(BF16), kw[:, pair(j)]],
                lambda j: [vc_ref[:, pair(j)].astype(BF16), vw[:, pair(j)]],
                lambda h: [None, bias_ref[h]],
                o_ref, NA_HEADS // 2)


def _na_bias_tables():
    n_blocks = DEC_SEQ // ATT_TQ
    qr = np.arange(NA_Q_ROWS)
    kr = np.arange(NA_KEY_ROWS)
    row_sel, row_ok = [], []
    for qb in range(n_blocks):
        ks = int(np.clip(qb * NA_Q_ROWS - NA_WIN_ROWS // 2, 0, GRID_ROWS - NA_KEY_ROWS))
        r = qb * NA_Q_ROWS + qr
        r0 = np.clip(r - NA_WIN_ROWS // 2, 0, GRID_ROWS - NA_WIN_ROWS)
        krow = ks + kr
        ok = (krow[None, :] >= r0[:, None]) & (krow[None, :] < r0[:, None] + NA_WIN_ROWS)
        off = krow[None, :] - r[:, None] + NA_WIN_ROWS - 1
        sel = (off[:, :, None] == np.arange(2 * NA_WIN_ROWS - 1)) & ok[:, :, None]
        row_sel.append(sel.astype(np.float32))
        row_ok.append(ok)
    kinds, kind_of_block = [], []
    for qb in range(n_blocks):
        for n, other in enumerate(kinds):
            if np.array_equal(row_sel[qb], row_sel[other]):
                kind_of_block.append(n)
                break
        else:
            kind_of_block.append(len(kinds))
            kinds.append(qb)
    c = np.arange(GRID_W)
    c0 = np.clip(c - NA_WIN_COLS // 2, 0, GRID_W - NA_WIN_COLS)
    col_ok = (c[None, :] >= c0[:, None]) & (c[None, :] < c0[:, None] + NA_WIN_COLS)
    coff = c[None, :] - c[:, None] + NA_WIN_COLS - 1
    col_sel = ((coff[:, :, None] == np.arange(2 * NA_WIN_COLS - 1)) & col_ok[:, :, None]).astype(np.float32)
    rsel = np.stack([row_sel[qb] for qb in kinds])
    valid = np.stack([row_ok[qb][:, None, :, None] & col_ok[None, :, None, :] for qb in kinds])
    valid = valid.reshape(len(kinds), 1, ATT_TQ, NA_KEYS)
    return rsel, col_sel, valid, kind_of_block


_NA_ROW_SEL, _NA_COL_SEL, _NA_VALID, _NA_KIND_OF_BLOCK = _na_bias_tables()


def _na_bias(rpb):
    hp = lax.Precision.HIGHEST
    cols = jnp.einsum("hij,ckj->hick", rpb, _NA_COL_SEL, precision=hp)
    b = jnp.einsum("nqri,hick->nhqcrk", _NA_ROW_SEL, cols, precision=hp)
    b = b.reshape(_NA_ROW_SEL.shape[0], NA_HEADS, ATT_TQ, NA_KEYS)
    return jnp.where(_NA_VALID, b, NEG_BIG).astype(F32)


def _na_kind(q):
    return (q > 0).astype(jnp.int32) + (q == DEC_SEQ // ATT_TQ - 1).astype(jnp.int32)


def _attn_na_lat(p, cache_k, cache_v, bias, layer):
    assert _NA_KIND_OF_BLOCK == [0] + [1] * (DEC_SEQ // ATT_TQ - 2) + [2]
    t = p["nq"].shape[0]
    b3 = lambda a: a.reshape(DEC_BATCH, DEC_SEQ, a.shape[1])
    return pl.pallas_call(
        _attn_na_lat_kernel,
        out_shape=jax.ShapeDtypeStruct((t, NA_W), BF16),
        grid=(DEC_BATCH, DEC_SEQ // ATT_TQ),
        in_specs=[_qtile_spec(NA_W), _cache_spec(NA_W, layer), _cache_spec(NA_W, layer),
                  _batch_spec(NA_W), _batch_spec(NA_W),
                  pl.BlockSpec((None, NA_HEADS, ATT_TQ, NA_KEYS), lambda b, q: (_na_kind(q), 0, 0, 0))],
        out_specs=_qtile_spec(NA_W),
        compiler_params=_params("parallel", "parallel"),
        name="attn_na_lat",
    )(p["nq"], cache_k, cache_v, b3(p["nk"]), b3(p["nv"]), bias)


def _mla_cache_kernel(ckv_ref, kr_ref, wkvb_ref, rep_ref, kc_ref, vc_ref, krc_ref):
    kv = _dot(ckv_ref[...].astype(BF16), wkvb_ref[...])
    kc_ref[...] = kv[:, :512].astype(BF16)
    vc_ref[...] = kv[:, 512:].astype(BF16)
    krc_ref[...] = _dot(kr_ref[...].astype(BF16), rep_ref[...]).astype(BF16)


def _mla_cache(cache_ckv, cache_kr, wkvb, layer):
    rep = jnp.asarray(np.tile(np.eye(MLA_ROPE, dtype=np.float32), (1, LANES // MLA_ROPE)), BF16)
    spec_in = lambda w: pl.BlockSpec((None, None, PAST_LEN, w), lambda b: (b, layer, 0, 0))
    spec_out = lambda w: pl.BlockSpec((None, PAST_LEN, w), lambda b: (b, 0, 0))
    shp = lambda w: jax.ShapeDtypeStruct((DEC_BATCH, PAST_LEN, w), BF16)
    return pl.pallas_call(
        _mla_cache_kernel,
        out_shape=[shp(512), shp(512), shp(LANES)],
        grid=(DEC_BATCH,),
        in_specs=[spec_in(MLA_KV_LORA), spec_in(MLA_ROPE), _full(wkvb.shape), _full(rep.shape)],
        out_specs=[spec_out(512), spec_out(512), spec_out(LANES)],
        compiler_params=_params("parallel"),
        name="mla_cache",
    )(cache_ckv, cache_kr, wkvb, rep)


def _attn_mla_lat_kernel(qn_ref, qr_ref, kc_ref, krc_ref, vc_ref, kn_ref, krn_ref, vn_ref, o_ref):
    pair = lambda j: slice(LANES * j, LANES * (j + 1))
    krc = krc_ref[...]
    krn = krn_ref[...]
    _pair_heads(_mla_q(qn_ref, qr_ref),
                lambda j: [jnp.concatenate([kc_ref[:, pair(j)], krc], axis=1),
                           jnp.concatenate([kn_ref[:, pair(j)], krn], axis=1)],
                lambda j: [vc_ref[:, pair(j)], vn_ref[:, pair(j)]],
                lambda h: [None, None],
                o_ref, MLA_HEADS // 2)


def _attn_mla_lat(p, kc, vc, krc):
    t = p["mqn"].shape[0]
    b3 = lambda a: a.reshape(DEC_BATCH, DEC_SEQ, a.shape[1])
    cspec = lambda w: pl.BlockSpec((None, PAST_LEN, w), lambda b, q: (b, 0, 0))
    return pl.pallas_call(
        _attn_mla_lat_kernel,
        out_shape=jax.ShapeDtypeStruct((t, MLA_W), BF16),
        grid=(DEC_BATCH, DEC_SEQ // ATT_TQ),
        in_specs=[_qtile_spec(512), _qtile_spec(256), cspec(512), cspec(LANES), cspec(512),
                  _batch_spec(512), _batch_spec(LANES), _batch_spec(512)],
        out_specs=_qtile_spec(MLA_W),
        compiler_params=_params("parallel", "parallel"),
        name="attn_mla_lat",
    )(p["mqn"], p["mqr"], kc, krc, vc, b3(p["kn"]), b3(p["kr4"]), b3(p["vn"]))


def _merge_kernel(x_ref, mod_ref, od_ref, on_ref, om_ref, wg_ref, wbd_ref, wbn_ref, wbm_ref, wo_ref,
                  g_ref, b_ref, x1_ref, h2_ref):
    x = x_ref[...]
    m = mod_ref[0]
    h = (x * (1.0 + m[1:2]) + m[0:1]).astype(BF16)
    gates = _dot(h, wg_ref[...])
    mix = (_sigmoid(gates[:, 0:1024]) * _dot(od_ref[...], wbd_ref[...])
           + _sigmoid(gates[:, 1024:2048]) * _dot(on_ref[...], wbn_ref[...])
           + _sigmoid(gates[:, 2048:3072]) * _dot(om_ref[...], wbm_ref[...]))
    out = _dot(mix.astype(BF16), wo_ref[...])
    x1 = _layer_norm(DN_ALPHA * x + m[2:3] * out, g_ref[...], b_ref[...])
    x1_ref[...] = x1
    h2_ref[...] = x1 * (1.0 + m[4:5]) + m[3:4]


def _merge(x, mod, od, on, om, W, *, latent):
    t = x.shape[0]
    tm = PROJ_TM
    steps_per_batch = (DEC_SEQ if latent else t) // tm
    row = lambda w: pl.BlockSpec((tm, w), lambda i: (i, 0))
    return pl.pallas_call(
        _merge_kernel,
        out_shape=[jax.ShapeDtypeStruct((t, D_MODEL), F32), jax.ShapeDtypeStruct((t, D_MODEL), F32)],
        grid=(t // tm,),
        in_specs=[row(D_MODEL),
                  pl.BlockSpec((1, 6, D_MODEL), lambda i: (i // steps_per_batch, 0, 0)),
                  row(512), row(512), row(512),
                  _full(W["wg"].shape), _full(W["wbd"].shape), _full(W["wbn"].shape),
                  _full(W["wbm"].shape), _full(W["wo"].shape),
                  _full((1, D_MODEL)), _full((1, D_MODEL))],
        out_specs=[row(D_MODEL), row(D_MODEL)],
        compiler_params=_params("parallel"),
        name="merge",
    )(x, mod, od, on, om, W["wg"], W["wbd"], W["wbn"], W["wbm"], W["wo"], W["ln1_g"], W["ln1_b"])


def _first_index_of_max(vals, idx, sentinel):
    mx = functools.reduce(jnp.maximum, [jnp.max(v, axis=0, keepdims=True) for v in vals])
    cand = [jnp.min(jnp.where(v == mx, i, sentinel), axis=0, keepdims=True) for v, i in zip(vals, idx)]
    return mx, functools.reduce(jnp.minimum, cand)


def _router_kernel(h_ref, wr_ref, bias_ref, tri_ref, eid_ref, rank_ref, wtok_ref, cnt_ref, base_ref):
    tm = h_ref.shape[0]

    @pl.when(pl.program_id(0) == 0)
    def _():
        base_ref[...] = jnp.zeros_like(base_ref)

    logits = _dot_nt(wr_ref[...], h_ref[...].astype(BF16))
    scores = _sigmoid(logits)
    biased = scores + bias_ref[...]
    member = lax.broadcasted_iota(jnp.int32, (GROUP_SIZE, tm), 0)
    slabs = [biased[GROUP_SIZE * g:GROUP_SIZE * (g + 1)] for g in range(N_GROUPS)]
    gscore = []
    for s in slabs:
        m1, first = _first_index_of_max([s], [member], GROUP_SIZE)
        m2 = jnp.max(jnp.where(member == first, -jnp.inf, s), axis=0, keepdims=True)
        gscore.append(m1 + m2)
    gs = jnp.concatenate(gscore, axis=0)
    gidx = lax.broadcasted_iota(jnp.int32, (N_GROUPS, tm), 0)
    gsel = jnp.zeros((N_GROUPS, tm), F32)
    for _ in range(TOPK_GROUPS):
        _, first = _first_index_of_max([gs], [gidx], N_GROUPS)
        pick = gidx == first
        gsel = jnp.where(pick, 1.0, gsel)
        gs = jnp.where(pick, -jnp.inf, gs)
    cur = [jnp.where(gsel[g:g + 1] > 0.0, slabs[g], -jnp.inf) for g in range(N_GROUPS)]
    eidx = [member + GROUP_SIZE * g for g in range(N_GROUPS)]
    sel = [jnp.zeros((GROUP_SIZE, tm), F32) for _ in range(N_GROUPS)]
    picks = []
    for _ in range(TOP_K):
        _, first = _first_index_of_max(cur, eidx, N_EXPERTS)
        pick = [eidx[g] == first for g in range(N_GROUPS)]
        picks.append((first, pick))
        for g in range(N_GROUPS):
            sel[g] = jnp.where(pick[g], 1.0, sel[g])
            cur[g] = jnp.where(pick[g], -jnp.inf, cur[g])
    w = [jnp.where(sel[g] > 0.0, scores[GROUP_SIZE * g:GROUP_SIZE * (g + 1)], 0.0) for g in range(N_GROUPS)]
    total = functools.reduce(lambda u, v: u + v, [jnp.sum(x, axis=0, keepdims=True) for x in w])
    w = [x / total * ROUTED_SCALE for x in w]

    sel_all = jnp.concatenate(sel, axis=0)
    incl = _dot(sel_all.astype(BF16), tri_ref[...])
    base = base_ref[:, 0:1]
    rank_all = incl - sel_all + base
    rank = [rank_all[GROUP_SIZE * g:GROUP_SIZE * (g + 1)] for g in range(N_GROUPS)]
    cnt = base + jnp.sum(sel_all, axis=1, keepdims=True)
    base_ref[...] = jnp.broadcast_to(cnt, base_ref.shape)
    cnt_ref[...] = jnp.broadcast_to(cnt, cnt_ref.shape)

    def picked(vals, pick):
        parts = [jnp.sum(jnp.where(p, v, 0.0), axis=0, keepdims=True) for p, v in zip(pick, vals)]
        return functools.reduce(lambda u, v: u + v, parts)

    eid_ref[...] = jnp.concatenate([first for first, _ in picks], axis=0)
    rank_ref[...] = jnp.concatenate([picked(rank, pick) for _, pick in picks], axis=0).astype(jnp.int32)
    w_rows = [picked(w, pick) for _, pick in picks] + [jnp.zeros((LANES - TOP_K, tm), F32)]
    wtok_ref[...] = jnp.concatenate(w_rows, axis=0).T


def _router(h2, wr_t, bias_col):
    t = h2.shape[0]
    tm = PROJ_TM
    tri = jnp.asarray(np.triu(np.ones((tm, tm), np.float32)), BF16)
    slots = lambda dt: jax.ShapeDtypeStruct((TOP_K, t), dt)
    return pl.pallas_call(
        _router_kernel,
        out_shape=[slots(jnp.int32), slots(jnp.int32), jax.ShapeDtypeStruct((t, LANES), F32),
                   jax.ShapeDtypeStruct((N_EXPERTS, LANES), F32)],
        grid=(t // tm,),
        in_specs=[pl.BlockSpec((tm, D_MODEL), lambda i: (i, 0)),
                  _full((N_EXPERTS, D_MODEL)), _full((N_EXPERTS, 1)), _full((tm, tm))],
        out_specs=[pl.BlockSpec((TOP_K, tm), lambda i: (0, i)), pl.BlockSpec((TOP_K, tm), lambda i: (0, i)),
                   pl.BlockSpec((tm, LANES), lambda i: (i, 0)), _full((N_EXPERTS, LANES))],
        scratch_shapes=[pltpu.VMEM((N_EXPERTS, LANES), F32)],
        compiler_params=_params("arbitrary"),
        name="router",
    )(h2, wr_t, bias_col, tri)


def _moe_tiles(t):
    return (t * TOP_K) // MOE_TILE + N_EXPERTS


def _moe_plan(eid, rank, cnt, t):
    counts = cnt[:, 0].astype(jnp.int32)
    tiles = jnp.maximum((counts + MOE_TILE - 1) // MOE_TILE, 1)
    ends = jnp.cumsum(tiles)
    starts = (ends - tiles) * MOE_TILE
    experts = jnp.arange(N_EXPERTS, dtype=jnp.int32)
    pos = rank + jnp.sum(jnp.where(eid[:, :, None] == experts, starts, 0), axis=-1)
    tile_ids = jnp.arange(_moe_tiles(t), dtype=jnp.int32)
    tile_expert = jnp.minimum(jnp.sum(tile_ids[:, None] >= ends[None, :], axis=-1), N_EXPERTS - 1)
    last_tile_row = (ends - 1) * MOE_TILE
    return pos, tile_expert.astype(jnp.int32), last_tile_row.astype(jnp.int32), ends[-1:].astype(jnp.int32)


def _dispatch_kernel(last_ref, nact_ref, pos_ref, h_hbm, xs_hbm, zero_ref, fill_sem, row_sem):
    i = pl.program_id(0)
    dt = pos_ref.shape[1]
    n_tiles = xs_hbm.shape[0] // MOE_TILE

    def fill(row):
        return pltpu.make_async_copy(zero_ref, xs_hbm.at[pl.ds(pl.multiple_of(row, MOE_TILE), MOE_TILE)], fill_sem)

    @pl.when(i == 0)
    def _():
        zero_ref[...] = jnp.zeros_like(zero_ref)
        n_tail = n_tiles - nact_ref[0]
        for e in range(N_EXPERTS):
            fill(last_ref[e]).start()
        lax.fori_loop(0, n_tail, lambda j, c: (fill((nact_ref[0] + j) * MOE_TILE).start(), c)[1], 0)
        for e in range(N_EXPERTS):
            fill(last_ref[e]).wait()
        lax.fori_loop(0, n_tail, lambda j, c: (fill((nact_ref[0] + j) * MOE_TILE).wait(), c)[1], 0)

    def token(tk, c):
        src = h_hbm.at[pl.ds(i * dt + tk, 1)]
        for k in range(TOP_K):
            pltpu.make_async_copy(src, xs_hbm.at[pl.ds(pos_ref[k, tk], 1)], row_sem).start()
        return c

    lax.fori_loop(0, dt, token, 0)
    pltpu.make_async_copy(h_hbm.at[pl.ds(0, dt * TOP_K)], xs_hbm.at[pl.ds(0, dt * TOP_K)], row_sem).wait()


def _dispatch(h2, pos, last_tile_row, n_active):
    t = h2.shape[0]
    dt = DISPATCH_TM
    return pl.pallas_call(
        _dispatch_kernel,
        out_shape=jax.ShapeDtypeStruct((_moe_tiles(t) * MOE_TILE, D_MODEL), F32),
        grid_spec=pltpu.PrefetchScalarGridSpec(
            num_scalar_prefetch=2,
            grid=(t // dt,),
            in_specs=[pl.BlockSpec((TOP_K, dt), lambda i, *_: (0, i), memory_space=pltpu.SMEM),
                      pl.BlockSpec(memory_space=pl.ANY)],
            out_specs=pl.BlockSpec(memory_space=pl.ANY),
            scratch_shapes=[pltpu.VMEM((MOE_TILE, D_MODEL), F32),
                            pltpu.SemaphoreType.DMA, pltpu.SemaphoreType.DMA]),
        compiler_params=_params("arbitrary"),
        name="moe_dispatch",
    )(last_tile_row, n_active, pos, h2)


def _ffn_kernel(te_ref, nact_ref, xs_ref, weg_ref, weu_ref, wed_ref, ys_ref):
    i = pl.program_id(0)

    @pl.when(i < nact_ref[0])
    def _():
        x = xs_ref[...].astype(BF16)
        act = _silu(_dot(x, weg_ref[0])) * _dot(x, weu_ref[0])
        ys_ref[...] = _dot(act.astype(BF16), wed_ref[0])

    @pl.when(i >= nact_ref[0])
    def _():
        ys_ref[...] = jnp.zeros_like(ys_ref)


def _ffn(xs, tile_expert, n_active, W):
    n_tiles = xs.shape[0] // MOE_TILE
    wspec = lambda shape: pl.BlockSpec((1,) + shape, lambda i, te, na: (te[i], 0, 0))
    return pl.pallas_call(
        _ffn_kernel,
        out_shape=jax.ShapeDtypeStruct(xs.shape, F32),
        grid_spec=pltpu.PrefetchScalarGridSpec(
            num_scalar_prefetch=2,
            grid=(n_tiles,),
            in_specs=[pl.BlockSpec((MOE_TILE, D_MODEL), lambda i, te, na: (i, 0)),
                      wspec((D_MODEL, EXPERT_FF)), wspec((D_MODEL, EXPERT_FF)), wspec((EXPERT_FF, D_MODEL))],
            out_specs=pl.BlockSpec((MOE_TILE, D_MODEL), lambda i, te, na: (i, 0))),
        compiler_params=_params("parallel"),
        name="moe_ffn",
    )(tile_expert, n_active, xs, W["weg"], W["weu"], W["wed"])


def _combine_kernel(pos_ref, nxt_ref, h_ref, wtok_ref, x1_ref, mod_ref, wsg_ref, wsu_ref, wsd_ref,
                    g_ref, b_ref, ys_hbm, out_ref, buf_ref, sem_ref):
    i = pl.program_id(0)
    n = pl.num_programs(0)
    ct = h_ref.shape[0]

    def rows(p_ref, slot, tk, k):
        return pltpu.make_async_copy(ys_hbm.at[pl.ds(p_ref[k, tk], 1)], buf_ref.at[slot, k, pl.ds(tk, 1)],
                                     sem_ref.at[slot])

    def issue(p_ref, slot):
        def token(tk, c):
            for k in range(TOP_K):
                rows(p_ref, slot, tk, k).start()
            return c
        lax.fori_loop(0, ct, token, 0)

    slot = i % 2

    @pl.when(i == 0)
    def _():
        issue(pos_ref, 0)

    @pl.when(i + 1 < n)
    def _():
        issue(nxt_ref, 1 - slot)

    for k in range(TOP_K):
        pltpu.make_async_copy(ys_hbm.at[pl.ds(0, ct)], buf_ref.at[slot, k], sem_ref.at[slot]).wait()

    wt = wtok_ref[...]
    routed = None
    for k in range(TOP_K):
        part = wt[:, k:k + 1] * buf_ref[slot, k]
        routed = part if routed is None else routed + part
    h = h_ref[...].astype(BF16)
    shared = _dot((_silu(_dot(h, wsg_ref[...])) * _dot(h, wsu_ref[...])).astype(BF16), wsd_ref[...])
    m = mod_ref[0]
    y2 = DN_ALPHA * x1_ref[...] + m[5:6] * (routed + shared)
    out_ref[...] = _layer_norm(y2, g_ref[...], b_ref[...])


def _combine(ys, pos, h2, wtok, x1, mod, W, *, latent):
    t = h2.shape[0]
    ct = COMBINE_TM
    n = t // ct
    steps_per_batch = (DEC_SEQ if latent else t) // ct
    row = lambda w: pl.BlockSpec((ct, w), lambda i: (i, 0))
    return pl.pallas_call(
        _combine_kernel,
        out_shape=jax.ShapeDtypeStruct((t, D_MODEL), F32),
        grid=(n,),
        in_specs=[pl.BlockSpec((TOP_K, ct), lambda i: (0, i), memory_space=pltpu.SMEM),
                  pl.BlockSpec((TOP_K, ct), lambda i: (0, jnp.minimum(i + 1, n - 1)), memory_space=pltpu.SMEM),
                  row(D_MODEL), row(LANES), row(D_MODEL),
                  pl.BlockSpec((1, 6, D_MODEL), lambda i: (i // steps_per_batch, 0, 0)),
                  _full((D_MODEL, SHARED_FF)), _full((D_MODEL, SHARED_FF)), _full((SHARED_FF, D_MODEL)),
                  _full((1, D_MODEL)), _full((1, D_MODEL)),
                  pl.BlockSpec(memory_space=pl.ANY)],
        out_specs=row(D_MODEL),
        scratch_shapes=[pltpu.VMEM((2, TOP_K, ct, D_MODEL), F32), pltpu.SemaphoreType.DMA((2,))],
        compiler_params=_params("arbitrary"),
        name="moe_combine",
    )(pos, pos, h2, wtok, x1, mod, W["wsg"], W["wsu"], W["wsd"], W["ln2_g"], W["ln2_b"], ys)


def _moe(h2, x1, mod, W, *, latent):
    t = h2.shape[0]
    eid, rank, wtok, cnt = _router(h2, W["wr_t"], W["rbias"])
    pos, tile_expert, last_tile_row, n_active = _moe_plan(eid, rank, cnt, t)
    xs = _dispatch(h2, pos, last_tile_row, n_active)
    ys = _ffn(xs, tile_expert, n_active, W)
    return _combine(ys, pos, h2, wtok, x1, mod, W, latent=latent)


def _rope_partner(n_blocks, block):
    half = block // 2
    i = np.arange(n_blocks * block)
    return np.where((i % block) < half, i + half, i - half)


_QB_NOPE = np.concatenate([np.arange(MLA_NOPE) + (MLA_NOPE + MLA_ROPE) * h for h in range(MLA_HEADS)])
_QB_ROPE = np.concatenate([np.arange(MLA_ROPE) + (MLA_NOPE + MLA_ROPE) * h + MLA_NOPE for h in range(MLA_HEADS)])
_KVB_NOPE = np.concatenate([np.arange(MLA_NOPE) + (MLA_NOPE + MLA_V) * h for h in range(MLA_HEADS)])
_KVB_V = np.concatenate([np.arange(MLA_V) + (MLA_NOPE + MLA_V) * h + MLA_NOPE for h in range(MLA_HEADS)])
_DIFF_PARTNER = _rope_partner(2 * DIFF_W // 32, 32)
_MLA_PARTNER = _rope_partner(MLA_HEADS * MLA_ROPE // 16, 16)
_KR_PARTNER = _rope_partner(MLA_ROPE // 16, 16)


def _layer_weights(l, w_in, mla_qa_g, mla_wq_b, mla_kva_g, mla_wkv_b, w_branch_diff, w_branch_na,
                   w_branch_mla, w_out, ln1_g, ln1_b, ln2_g, ln2_b, w_router, router_bias,
                   w_exp_gate, w_exp_up, w_exp_down, w_sh_gate, w_sh_up, w_sh_down):
    win = w_in[l].astype(BF16)
    wa = win[:, :3072]
    qa_kva = win[:, 3072:3712]
    kr = win[:, 3712:3744]
    kr4 = jnp.tile(kr, (1, LANES // MLA_ROPE))
    krp4 = jnp.tile(kr[:, _KR_PARTNER], (1, LANES // MLA_ROPE))
    wqb = mla_wq_b[l].astype(BF16)
    wq_rope = wqb[:, _QB_ROPE]
    wkvb = mla_wkv_b[l].astype(BF16)
    row = lambda v: v[l].reshape(1, -1).astype(F32)
    return {
        "wa": wa,
        "wp": wa[:, :2 * DIFF_W][:, _DIFF_PARTNER],
        "wm_ctx": jnp.concatenate([qa_kva, kr4], axis=1),
        "wm_lat": jnp.concatenate([qa_kva, kr4, krp4], axis=1),
        "qag": row(mla_qa_g), "kvag": row(mla_kva_g),
        "wqb": jnp.concatenate([wqb[:, _QB_NOPE], wq_rope], axis=1),
        "wqp": wq_rope[:, _MLA_PARTNER],
        "wkvb": jnp.concatenate([wkvb[:, _KVB_NOPE], wkvb[:, _KVB_V]], axis=1),
        "wg": win[:, 3744:],
        "wbd": w_branch_diff[l].astype(BF16), "wbn": w_branch_na[l].astype(BF16),
        "wbm": w_branch_mla[l].astype(BF16), "wo": w_out[l].astype(BF16),
        "ln1_g": row(ln1_g), "ln1_b": row(ln1_b), "ln2_g": row(ln2_g), "ln2_b": row(ln2_b),
        "wr_t": w_router[l].T.astype(BF16), "rbias": router_bias[l].reshape(N_EXPERTS, 1).astype(F32),
        "weg": w_exp_gate[l].astype(BF16), "weu": w_exp_up[l].astype(BF16), "wed": w_exp_down[l].astype(BF16),
        "wsg": w_sh_gate[l].astype(BF16), "wsu": w_sh_up[l].astype(BF16), "wsd": w_sh_down[l].astype(BF16),
    }


def _rope_tables():
    t = jnp.arange(DEC_SEQ)
    pos = [(t // GRID_W).astype(F32), (t % GRID_W).astype(F32)]

    def table(block):
        half = block // 4
        inv = ROPE_BASE ** (-jnp.arange(half, dtype=F32) / half)
        cos, sin = [], []
        for p in pos:
            ang = p[:, None] * inv[None, :]
            cos += [jnp.cos(ang), jnp.cos(ang)]
            sin += [-jnp.sin(ang), jnp.sin(ang)]
        reps = LANES // block
        return (jnp.tile(jnp.concatenate(cos, axis=1), (1, reps)),
                jnp.tile(jnp.concatenate(sin, axis=1), (1, reps)))

    cd, sd = table(DIFF_DH)
    cm, sm = table(MLA_ROPE)
    return {"cd": cd, "sd": sd, "cm": cm, "sm": sm}


def kernel(x_prompt, x_sample, cache_diff_k, cache_diff_v, cache_na_k, cache_na_v, cache_mla_ckv, cache_mla_krope, c, c_ctx, w_mod, b_mod, w_in, diff_lambda, diff_subln_g, na_rpb, mla_qa_g, mla_wq_b, mla_kva_g, mla_wkv_b, w_branch_diff, w_branch_na, w_branch_mla, w_out, ln1_g, ln1_b, ln2_g, ln2_b, w_router, router_bias, w_exp_gate, w_exp_up, w_exp_down, w_sh_gate, w_sh_up, w_sh_down):
    t_ctx = BATCH * SEQ
    t_lat = DEC_BATCH * DEC_SEQ
    cvec = jnp.concatenate([c, c_ctx[None, :], jnp.zeros((MOD_ROWS - DEC_BATCH - 1, D_MODEL), F32)], axis=0)
    mods = _modulation(cvec, w_mod, b_mod).reshape(DEPTH, MOD_ROWS, 6, D_MODEL)
    rope = _rope_tables()
    ck_d = cache_diff_k.reshape(DEC_BATCH, DEPTH, PAST_LEN, DIFF_W)
    cv_d = cache_diff_v.reshape(DEC_BATCH, DEPTH, PAST_LEN, DIFF_W)
    ck_n = cache_na_k.reshape(DEC_BATCH, DEPTH, PAST_LEN, NA_W)
    cv_n = cache_na_v.reshape(DEC_BATCH, DEPTH, PAST_LEN, NA_W)

    xc = x_prompt.reshape(t_ctx, D_MODEL)
    xl = x_sample.reshape(t_lat, D_MODEL)
    states = []
    for l in range(DEPTH):
        W = _layer_weights(l, w_in, mla_qa_g, mla_wq_b, mla_kva_g, mla_wkv_b, w_branch_diff, w_branch_na,
                           w_branch_mla, w_out, ln1_g, ln1_b, ln2_g, ln2_b, w_router, router_bias,
                           w_exp_gate, w_exp_up, w_exp_down, w_sh_gate, w_sh_up, w_sh_down)
        lam = diff_lambda[l].astype(F32)
        subln = diff_subln_g[l].reshape(1, LANES).astype(F32)
        mod_c = mods[l, DEC_BATCH:DEC_BATCH + 1]
        mod_l = mods[l, :DEC_BATCH]

        Wc = dict(W, wm=W["wm_ctx"])
        names = ["dq", "nq", "mqn", "mqr", "kr4", "kn", "vn", "dk", "dv", "nk", "nv", "ckv", "kr"]
        pc = dict(zip(names, _project(xc, mod_c, Wc, None, latent=False)))
        od, on, om = _attn_ctx(pc, lam, subln, l)
        x1, h2 = _merge(xc, mod_c, od, on, om, W, latent=False)
        xc = _moe(h2, x1, mod_c, W, latent=False)
        states.append(pc)

        Wl = dict(W, wm=W["wm_lat"])
        names = ["dq", "dk", "dv", "nq", "nk", "nv", "mqn", "mqr", "kr4", "kn", "vn"]
        pll = dict(zip(names, _project(xl, mod_l, Wl, rope, latent=True)))
        od = _attn_diff_lat(pll, ck_d, cv_d, lam, subln, l)
        on = _attn_na_lat(pll, ck_n, cv_n, _na_bias(na_rpb[l].astype(F32)), l)
        kc, vc, krc = _mla_cache(cache_mla_ckv, cache_mla_krope, W["wkvb"], l)
        om = _attn_mla_lat(pll, kc, vc, krc)
        x1, h2 = _merge(xl, mod_l, od, on, om, W, latent=True)
        xl = _moe(h2, x1, mod_l, W, latent=True)

    def stack(name, shape):
        return jnp.stack([s[name].reshape((BATCH, SEQ) + shape) for s in states], axis=1)

    return (xc.reshape(BATCH, SEQ, D_MODEL), xl.reshape(DEC_BATCH, DEC_SEQ, D_MODEL),
            stack("dk", (DIFF_HEADS, 2 * DIFF_DH)), stack("dv", (DIFF_HEADS, 2 * DIFF_DH)),
            stack("nk", (NA_HEADS, NA_DH)), stack("nv", (NA_HEADS, NA_DH)),
            stack("ckv", (MLA_KV_LORA,)), stack("kr", (MLA_ROPE,)))
```

```python
import functools

import numpy as np
import jax
import jax.numpy as jnp
from jax import lax
from jax.experimental import pallas as pl
from jax.experimental.pallas import tpu as pltpu
from jax.experimental.pallas import tpu_sc as plsc

D_MODEL = 1024
BATCH = 32
SEQ = 256
DEPTH = 2
DEC_BATCH = 8
DEC_SEQ = 2048
PAST_LEN = 512
GRID_W = 64
GRID_ROWS = DEC_SEQ // GRID_W
ROPE_BASE = 10000.0
DIFF_HEADS = 4
DIFF_DH = 64
DIFF_W = 512
NA_HEADS = 8
NA_DH = 64
NA_W = 512
NA_WIN_ROWS = 8
NA_WIN_COLS = 16
MLA_HEADS = 8
MLA_Q_LORA = 384
MLA_KV_LORA = 256
MLA_NOPE = 64
MLA_ROPE = 32
MLA_V = 64
MLA_W = 512
N_EXPERTS = 64
N_GROUPS = 8
GROUP_SIZE = N_EXPERTS // N_GROUPS
TOPK_GROUPS = 4
TOP_K = 8
EXPERT_FF = 256
SHARED_FF = 256
ROUTED_SCALE = 2.5
DN_ALPHA = (2 * DEPTH) ** 0.25
LN_EPS = 1e-5
RMS_EPS = 1e-6

F32 = jnp.float32
BF16 = jnp.bfloat16

LANES = 128
VMEM_LIMIT_BYTES = 56 * 1024 * 1024
NEG_BIG = -1e30

DIFF_SCALE = DIFF_DH ** -0.5
NA_SCALE = NA_DH ** -0.5
MLA_SCALE = (MLA_NOPE + MLA_ROPE) ** -0.5

PROJ_TM = 512
ATT_TQ = 256
NA_Q_ROWS = ATT_TQ // GRID_W
NA_KEY_ROWS = NA_Q_ROWS + NA_WIN_ROWS
NA_KEYS = NA_KEY_ROWS * GRID_W
MOE_TILE = 256
SC_ROWS = 64
COMBINE_TM = 256


def _dot(a, b):
    return jnp.dot(a, b, preferred_element_type=F32)


def _dot_nt(a, b):
    return lax.dot_general(a, b, (((1,), (1,)), ((), ())), preferred_element_type=F32)


def _sigmoid(x):
    return 1.0 / (1.0 + jnp.exp(-x))


def _silu(x):
    return x * _sigmoid(x)


def _params(*sem):
    return pltpu.CompilerParams(dimension_semantics=sem, vmem_limit_bytes=VMEM_LIMIT_BYTES)


def _full(shape):
    n = len(shape)
    return pl.BlockSpec(shape, lambda *_: (0,) * n)


def _layer_norm(y, g, b):
    mu = jnp.mean(y, axis=-1, keepdims=True)
    yc = y - mu
    var = jnp.mean(yc * yc, axis=-1, keepdims=True)
    return yc * lax.rsqrt(var + LN_EPS) * g + b


def _rms(x, g):
    return x * lax.rsqrt(jnp.mean(x * x, axis=-1, keepdims=True) + RMS_EPS) * g


MOD_ROWS = 16
MOD_TN = 1536


def _mod_kernel(c_ref, w_ref, b_ref, o_ref):
    s = _silu(c_ref[...]).astype(BF16)
    o_ref[0] = _dot(s, w_ref[0].astype(BF16)) + b_ref[0]


def _modulation(cvec, w_mod, b_mod):
    n = 6 * D_MODEL
    return pl.pallas_call(
        _mod_kernel,
        out_shape=jax.ShapeDtypeStruct((DEPTH, MOD_ROWS, n), F32),
        grid=(DEPTH, n // MOD_TN),
        in_specs=[
            pl.BlockSpec((MOD_ROWS, D_MODEL), lambda l, j: (0, 0)),
            pl.BlockSpec((1, D_MODEL, MOD_TN), lambda l, j: (l, 0, j)),
            pl.BlockSpec((1, 1, MOD_TN), lambda l, j: (l, 0, j)),
        ],
        out_specs=pl.BlockSpec((1, MOD_ROWS, MOD_TN), lambda l, j: (l, 0, j)),
        compiler_params=_params("parallel", "parallel"),
        name="modulation",
    )(cvec, w_mod, b_mod.reshape(DEPTH, 1, n))


def _proj_common(x_ref, mod_ref, wa_ref, wm_ref, qag_ref, kvag_ref, wqb_ref, wkvb_ref):
    m = mod_ref[0]
    h = (x_ref[...] * (1.0 + m[1:2]) + m[0:1]).astype(BF16)
    a = _dot(h, wa_ref[...])
    mm = _dot(h, wm_ref[...])
    qan = _rms(mm[:, :MLA_Q_LORA], qag_ref[...]).astype(BF16)
    mq = _dot(qan, wqb_ref[...])
    ckv = _rms(mm[:, MLA_Q_LORA:MLA_Q_LORA + MLA_KV_LORA], kvag_ref[...])
    kv = _dot(ckv.astype(BF16), wkvb_ref[...])
    return h, a, mm, qan, mq, ckv, kv


def _proj_ctx_kernel(x_ref, mod_ref, wa_ref, wm_ref, qag_ref, kvag_ref, wqb_ref, wkvb_ref,
                     dq_ref, nq_ref, mqn_ref, mqr_ref, kr4_ref, kn_ref, vn_ref,
                     dk_ref, dv_ref, nk_ref, nv_ref, ckv_ref, kr_ref):
    _, a, mm, _, mq, ckv, kv = _proj_common(x_ref, mod_ref, wa_ref, wm_ref, qag_ref, kvag_ref,
                                            wqb_ref, wkvb_ref)
    dq_ref[...] = (a[:, 0:512] * DIFF_SCALE).astype(BF16)
    dk_ref[...] = a[:, 512:1024]
    dv_ref[...] = a[:, 1024:1536]
    nq_ref[...] = (a[:, 1536:2048] * NA_SCALE).astype(BF16)
    nk_ref[...] = a[:, 2048:2560]
    nv_ref[...] = a[:, 2560:3072]
    mqn_ref[...] = (mq[:, :512] * MLA_SCALE).astype(BF16)
    mqr_ref[...] = (mq[:, 512:768] * MLA_SCALE).astype(BF16)
    kr4 = mm[:, 640:768]
    kr4_ref[...] = kr4.astype(BF16)
    kr_ref[...] = kr4[:, :MLA_ROPE]
    ckv_ref[...] = ckv
    kn_ref[...] = kv[:, :512].astype(BF16)
    vn_ref[...] = kv[:, 512:].astype(BF16)


def _proj_lat_kernel(x_ref, mod_ref, wa_ref, wm_ref, qag_ref, kvag_ref, wqb_ref, wkvb_ref,
                     wp_ref, wqp_ref, cd_ref, sd_ref, cm_ref, sm_ref,
                     dq_ref, dk_ref, dv_ref, nq_ref, nk_ref, nv_ref,
                     mqn_ref, mqr_ref, kr4_ref, kn_ref, vn_ref):
    h, a, mm, qan, mq, _, kv = _proj_common(x_ref, mod_ref, wa_ref, wm_ref, qag_ref, kvag_ref,
                                            wqb_ref, wkvb_ref)
    ap = _dot(h, wp_ref[...])
    mqp = _dot(qan, wqp_ref[...])
    cd = cd_ref[...]
    sd = sd_ref[...]
    cm = cm_ref[...]
    sm = sm_ref[...]
    for j in range(DIFF_W // LANES):
        lo, hi = LANES * j, LANES * (j + 1)
        dq_ref[:, lo:hi] = ((a[:, lo:hi] * cd + ap[:, lo:hi] * sd) * DIFF_SCALE).astype(BF16)
        dk_ref[:, lo:hi] = (a[:, 512 + lo:512 + hi] * cd + ap[:, 512 + lo:512 + hi] * sd).astype(BF16)
    dv_ref[...] = a[:, 1024:1536].astype(BF16)
    nq_ref[...] = (a[:, 1536:2048] * NA_SCALE).astype(BF16)
    nk_ref[...] = a[:, 2048:2560].astype(BF16)
    nv_ref[...] = a[:, 2560:3072].astype(BF16)
    mqn_ref[...] = (mq[:, :512] * MLA_SCALE).astype(BF16)
    for j in range(2):
        lo, hi = LANES * j, LANES * (j + 1)
        mqr_ref[:, lo:hi] = ((mq[:, 512 + lo:512 + hi] * cm + mqp[:, lo:hi] * sm) * MLA_SCALE).astype(BF16)
    kr4_ref[...] = (mm[:, 640:768] * cm + mm[:, 768:896] * sm).astype(BF16)
    kn_ref[...] = kv[:, :512].astype(BF16)
    vn_ref[...] = kv[:, 512:].astype(BF16)


def _project(x, mod, W, rope, *, latent):
    t = x.shape[0]
    tm = PROJ_TM
    tokens_per_batch = DEC_SEQ if latent else t
    steps_per_batch = tokens_per_batch // tm
    row = lambda w: pl.BlockSpec((tm, w), lambda i: (i, 0))
    common_in = [
        row(D_MODEL),
        pl.BlockSpec((1, 6, D_MODEL), lambda i: (i // steps_per_batch, 0, 0)),
        _full(W["wa"].shape), _full(W["wm"].shape), _full((1, MLA_Q_LORA)), _full((1, MLA_KV_LORA)),
        _full(W["wqb"].shape), _full(W["wkvb"].shape),
    ]
    common_args = [x, mod, W["wa"], W["wm"], W["qag"], W["kvag"], W["wqb"], W["wkvb"]]
    bf = lambda w: jax.ShapeDtypeStruct((t, w), BF16)
    f32 = lambda w: jax.ShapeDtypeStruct((t, w), F32)
    if latent:
        tab = pl.BlockSpec((tm, LANES), lambda i: (i % steps_per_batch, 0))
        widths = [512, 512, 512, 512, 512, 512, 512, 256, 128, 512, 512]
        return pl.pallas_call(
            _proj_lat_kernel,
            out_shape=[bf(w) for w in widths],
            grid=(t // tm,),
            in_specs=common_in + [_full(W["wp"].shape), _full(W["wqp"].shape), tab, tab, tab, tab],
            out_specs=[row(w) for w in widths],
            compiler_params=_params("parallel"),
            name="proj_lat",
        )(*common_args, W["wp"], W["wqp"], rope["cd"], rope["sd"], rope["cm"], rope["sm"])
    bf_w = [512, 512, 512, 256, 128, 512, 512]
    f32_w = [512, 512, 512, 512, 256, 32]
    return pl.pallas_call(
        _proj_ctx_kernel,
        out_shape=[bf(w) for w in bf_w] + [f32(w) for w in f32_w],
        grid=(t // tm,),
        in_specs=common_in,
        out_specs=[row(w) for w in bf_w + f32_w],
        compiler_params=_params("parallel"),
        name="proj_ctx",
    )(*common_args)


def _lane_iota():
    return lax.broadcasted_iota(jnp.int32, (1, LANES), 1)


def _softmax_parts(parts):
    m = functools.reduce(jnp.maximum, [jnp.max(s, axis=-1, keepdims=True) for s in parts])
    es = [jnp.exp(s - m) for s in parts]
    l = functools.reduce(lambda u, v: u + v, [jnp.sum(e, axis=-1, keepdims=True) for e in es])
    return [e / l for e in es]


def _diff_lambda(lam_ref, layer):
    lp = lam_ref[...]
    lam_init = 0.8 - 0.6 * float(np.exp(-0.3 * layer))
    s1 = jnp.sum(lp[0:1] * lp[1:2], axis=-1, keepdims=True)
    s2 = jnp.sum(lp[2:3] * lp[3:4], axis=-1, keepdims=True)
    return jnp.exp(s1) - jnp.exp(s2) + lam_init, lam_init


def _diff_heads(q_ref, ks, vs, lam_ref, g_ref, o_ref, layer):
    lam, lam_init = _diff_lambda(lam_ref, layer)
    first_map = _lane_iota() < DIFF_DH
    g = g_ref[...]
    for h in range(DIFF_HEADS):
        hs = slice(LANES * h, LANES * (h + 1))
        q = q_ref[:, hs]
        q1 = jnp.where(first_map, q, jnp.zeros_like(q))
        q2 = jnp.where(first_map, jnp.zeros_like(q), q)
        kk = [k[:, hs].astype(BF16) for k in ks]
        p1 = _softmax_parts([_dot_nt(q1, k) for k in kk])
        p2 = _softmax_parts([_dot_nt(q2, k) for k in kk])
        o = None
        for a1, a2, v in zip(p1, p2, vs):
            part = _dot((a1 - lam * a2).astype(BF16), v[:, hs].astype(BF16))
            o = part if o is None else o + part
        o = _rms(o, g) * (1.0 - lam_init)
        o_ref[:, hs] = o.astype(BF16)


def _pair_heads(q_of, k_of, v_of, bias_of, o_ref, n_pairs):
    first = _lane_iota() < 64
    for j in range(n_pairs):
        ps = slice(LANES * j, LANES * (j + 1))
        ks = k_of(j)
        vs = v_of(j)
        outs = []
        for hh in range(2):
            q = q_of(j, hh)
            ss = [_dot_nt(q, k) for k in ks]
            bs = bias_of(2 * j + hh)
            ss = [s if b is None else s + b for s, b in zip(ss, bs)]
            ps_ = _softmax_parts(ss)
            o = None
            for p, v in zip(ps_, vs):
                part = _dot(p.astype(BF16), v)
                o = part if o is None else o + part
            outs.append(o)
        o_ref[:, ps] = jnp.where(first, outs[0], outs[1]).astype(BF16)


def _na_q(q_ref):
    first = _lane_iota() < NA_DH

    def q_of(j, hh):
        q = q_ref[:, LANES * j:LANES * (j + 1)]
        keep = first if hh == 0 else jnp.logical_not(first)
        return jnp.where(keep, q, jnp.zeros_like(q))
    return q_of


def _mla_q(qn_ref, qr_ref):
    lane = _lane_iota()
    first = lane < MLA_NOPE

    def q_of(j, hh):
        h = 2 * j + hh
        qn = qn_ref[:, LANES * j:LANES * (j + 1)]
        keep = first if hh == 0 else jnp.logical_not(first)
        qn = jnp.where(keep, qn, jnp.zeros_like(qn))
        qr = qr_ref[:, LANES * (h // 4):LANES * (h // 4 + 1)]
        qr = jnp.where((lane // MLA_ROPE) == (h % 4), qr, jnp.zeros_like(qr))
        return jnp.concatenate([qn, qr], axis=1)
    return q_of


def _attn_ctx_kernel(layer, dq_ref, dk_ref, dv_ref, nq_ref, nk_ref, nv_ref,
                     mqn_ref, mqr_ref, kn_ref, kr4_ref, vn_ref, lam_ref, g_ref,
                     od_ref, on_ref, om_ref):
    _diff_heads(dq_ref, [dk_ref], [dv_ref], lam_ref, g_ref, od_ref, layer)
    none = lambda h: [None]
    pair = lambda j: slice(LANES * j, LANES * (j + 1))
    _pair_heads(_na_q(nq_ref),
                lambda j: [nk_ref[:, pair(j)].astype(BF16)],
                lambda j: [nv_ref[:, pair(j)].astype(BF16)],
                none, on_ref, NA_HEADS // 2)
    kr4 = kr4_ref[...]
    _pair_heads(_mla_q(mqn_ref, mqr_ref),
                lambda j: [jnp.concatenate([kn_ref[:, pair(j)], kr4], axis=1)],
                lambda j: [vn_ref[:, pair(j)]],
                none, om_ref, MLA_HEADS // 2)


def _attn_ctx(p, lam, g, layer):
    t = p["dq"].shape[0]
    row = lambda w: pl.BlockSpec((SEQ, w), lambda b: (b, 0))
    names = ["dq", "dk", "dv", "nq", "nk", "nv", "mqn", "mqr", "kn", "kr4", "vn"]
    out = jax.ShapeDtypeStruct((t, 512), BF16)
    return pl.pallas_call(
        functools.partial(_attn_ctx_kernel, layer),
        out_shape=[out, out, out],
        grid=(t // SEQ,),
        in_specs=[row(p[n].shape[1]) for n in names] + [_full((4, DIFF_DH)), _full((1, LANES))],
        out_specs=[row(512)] * 3,
        compiler_params=_params("parallel"),
        name="attn_ctx",
    )(*[p[n] for n in names], lam, g)


def _cache_spec(width, layer):
    return pl.BlockSpec((None, None, PAST_LEN, width), lambda b, q: (b, layer, 0, 0))


def _batch_spec(width):
    return pl.BlockSpec((None, DEC_SEQ, width), lambda b, q: (b, 0, 0))


def _qtile_spec(width):
    steps = DEC_SEQ // ATT_TQ
    return pl.BlockSpec((ATT_TQ, width), lambda b, q: (b * steps + q, 0))


def _attn_diff_lat_kernel(layer, q_ref, kc_ref, vc_ref, kn_ref, vn_ref, lam_ref, g_ref, o_ref):
    _diff_heads(q_ref, [kc_ref, kn_ref], [vc_ref, vn_ref], lam_ref, g_ref, o_ref, layer)


def _attn_diff_lat(p, cache_k, cache_v, lam, g, layer):
    t = p["dq"].shape[0]
    b3 = lambda a: a.reshape(DEC_BATCH, DEC_SEQ, a.shape[1])
    return pl.pallas_call(
        functools.partial(_attn_diff_lat_kernel, layer),
        out_shape=jax.ShapeDtypeStruct((t, DIFF_W), BF16),
        grid=(DEC_BATCH, DEC_SEQ // ATT_TQ),
        in_specs=[_qtile_spec(DIFF_W), _cache_spec(DIFF_W, layer), _cache_spec(DIFF_W, layer),
                  _batch_spec(DIFF_W), _batch_spec(DIFF_W),
                  pl.BlockSpec((4, DIFF_DH), lambda b, q: (0, 0)),
                  pl.BlockSpec((1, LANES), lambda b, q: (0, 0))],
        out_specs=_qtile_spec(DIFF_W),
        compiler_params=_params("parallel", "parallel"),
        name="attn_diff_lat",
    )(p["dq"], cache_k, cache_v, b3(p["dk"]), b3(p["dv"]), lam, g)


def _na_key_start(q):
    return jnp.clip(q * NA_Q_ROWS - NA_WIN_ROWS // 2, 0, GRID_ROWS - NA_KEY_ROWS)


def _attn_na_lat_kernel(q_ref, kc_ref, vc_ref, kn_ref, vn_ref, bias_ref, o_ref):
    start = pl.multiple_of(_na_key_start(pl.program_id(1)) * GRID_W, GRID_W)
    kw = kn_ref[pl.ds(start, NA_KEYS), :]
    vw = vn_ref[pl.ds(start, NA_KEYS), :]
    pair = lambda j: slice(LANES * j, LANES * (j + 1))
    _pair_heads(_na_q(q_ref),
                lambda j: [kc_ref[:, pair(j)].astype(BF16), kw[:, pair(j)]],
                lambda j: [vc_ref[:, pair(j)].astype(BF16), vw[:, pair(j)]],
                lambda h: [None, bias_ref[h]],
                o_ref, NA_HEADS // 2)


def _na_bias_tables():
    n_blocks = DEC_SEQ // ATT_TQ
    qr = np.arange(NA_Q_ROWS)
    kr = np.arange(NA_KEY_ROWS)
    row_sel, row_ok = [], []
    for qb in range(n_blocks):
        ks = int(np.clip(qb * NA_Q_ROWS - NA_WIN_ROWS // 2, 0, GRID_ROWS - NA_KEY_ROWS))
        r = qb * NA_Q_ROWS + qr
        r0 = np.clip(r - NA_WIN_ROWS // 2, 0, GRID_ROWS - NA_WIN_ROWS)
        krow = ks + kr
        ok = (krow[None, :] >= r0[:, None]) & (krow[None, :] < r0[:, None] + NA_WIN_ROWS)
        off = krow[None, :] - r[:, None] + NA_WIN_ROWS - 1
        sel = (off[:, :, None] == np.arange(2 * NA_WIN_ROWS - 1)) & ok[:, :, None]
        row_sel.append(sel.astype(np.float32))
        row_ok.append(ok)
    kinds, kind_of_block = [], []
    for qb in range(n_blocks):
        for n, other in enumerate(kinds):
            if np.array_equal(row_sel[qb], row_sel[other]):
                kind_of_block.append(n)
                break
        else:
            kind_of_block.append(len(kinds))
            kinds.append(qb)
    c = np.arange(GRID_W)
    c0 = np.clip(c - NA_WIN_COLS // 2, 0, GRID_W - NA_WIN_COLS)
    col_ok = (c[None, :] >= c0[:, None]) & (c[None, :] < c0[:, None] + NA_WIN_COLS)
    coff = c[None, :] - c[:, None] + NA_WIN_COLS - 1
    col_sel = ((coff[:, :, None] == np.arange(2 * NA_WIN_COLS - 1)) & col_ok[:, :, None]).astype(np.float32)
    rsel = np.stack([row_sel[qb] for qb in kinds])
    valid = np.stack([row_ok[qb][:, None, :, None] & col_ok[None, :, None, :] for qb in kinds])
    valid = valid.reshape(len(kinds), 1, ATT_TQ, NA_KEYS)
    return rsel, col_sel, valid, kind_of_block


_NA_ROW_SEL, _NA_COL_SEL, _NA_VALID, _NA_KIND_OF_BLOCK = _na_bias_tables()


def _na_bias(rpb):
    hp = lax.Precision.HIGHEST
    cols = jnp.einsum("hij,ckj->hick", rpb, _NA_COL_SEL, precision=hp)
    b = jnp.einsum("nqri,hick->nhqcrk", _NA_ROW_SEL, cols, precision=hp)
    b = b.reshape(_NA_ROW_SEL.shape[0], NA_HEADS, ATT_TQ, NA_KEYS)
    return jnp.where(_NA_VALID, b, NEG_BIG).astype(F32)


def _na_kind(q):
    return (q > 0).astype(jnp.int32) + (q == DEC_SEQ // ATT_TQ - 1).astype(jnp.int32)


def _attn_na_lat(p, cache_k, cache_v, bias, layer):
    assert _NA_KIND_OF_BLOCK == [0] + [1] * (DEC_SEQ // ATT_TQ - 2) + [2]
    t = p["nq"].shape[0]
    b3 = lambda a: a.reshape(DEC_BATCH, DEC_SEQ, a.shape[1])
    return pl.pallas_call(
        _attn_na_lat_kernel,
        out_shape=jax.ShapeDtypeStruct((t, NA_W), BF16),
        grid=(DEC_BATCH, DEC_SEQ // ATT_TQ),
        in_specs=[_qtile_spec(NA_W), _cache_spec(NA_W, layer), _cache_spec(NA_W, layer),
                  _batch_spec(NA_W), _batch_spec(NA_W),
                  pl.BlockSpec((None, NA_HEADS, ATT_TQ, NA_KEYS), lambda b, q: (_na_kind(q), 0, 0, 0))],
        out_specs=_qtile_spec(NA_W),
        compiler_params=_params("parallel", "parallel"),
        name="attn_na_lat",
    )(p["nq"], cache_k, cache_v, b3(p["nk"]), b3(p["nv"]), bias)


def _mla_cache_kernel(ckv_ref, kr_ref, wkvb_ref, rep_ref, kc_ref, vc_ref, krc_ref):
    kv = _dot(ckv_ref[...].astype(BF16), wkvb_ref[...])
    kc_ref[...] = kv[:, :512].astype(BF16)
    vc_ref[...] = kv[:, 512:].astype(BF16)
    krc_ref[...] = _dot(kr_ref[...].astype(BF16), rep_ref[...]).astype(BF16)


def _mla_cache(cache_ckv, cache_kr, wkvb, layer):
    rep = jnp.asarray(np.tile(np.eye(MLA_ROPE, dtype=np.float32), (1, LANES // MLA_ROPE)), BF16)
    spec_in = lambda w: pl.BlockSpec((None, None, PAST_LEN, w), lambda b: (b, layer, 0, 0))
    spec_out = lambda w: pl.BlockSpec((None, PAST_LEN, w), lambda b: (b, 0, 0))
    shp = lambda w: jax.ShapeDtypeStruct((DEC_BATCH, PAST_LEN, w), BF16)
    return pl.pallas_call(
        _mla_cache_kernel,
        out_shape=[shp(512), shp(512), shp(LANES)],
        grid=(DEC_BATCH,),
        in_specs=[spec_in(MLA_KV_LORA), spec_in(MLA_ROPE), _full(wkvb.shape), _full(rep.shape)],
        out_specs=[spec_out(512), spec_out(512), spec_out(LANES)],
        compiler_params=_params("parallel"),
        name="mla_cache",
    )(cache_ckv, cache_kr, wkvb, rep)


def _attn_mla_lat_kernel(qn_ref, qr_ref, kc_ref, krc_ref, vc_ref, kn_ref, krn_ref, vn_ref, o_ref):
    pair = lambda j: slice(LANES * j, LANES * (j + 1))
    krc = krc_ref[...]
    krn = krn_ref[...]
    _pair_heads(_mla_q(qn_ref, qr_ref),
                lambda j: [jnp.concatenate([kc_ref[:, pair(j)], krc], axis=1),
                           jnp.concatenate([kn_ref[:, pair(j)], krn], axis=1)],
                lambda j: [vc_ref[:, pair(j)], vn_ref[:, pair(j)]],
                lambda h: [None, None],
                o_ref, MLA_HEADS // 2)


def _attn_mla_lat(p, kc, vc, krc):
    t = p["mqn"].shape[0]
    b3 = lambda a: a.reshape(DEC_BATCH, DEC_SEQ, a.shape[1])
    cspec = lambda w: pl.BlockSpec((None, PAST_LEN, w), lambda b, q: (b, 0, 0))
    return pl.pallas_call(
        _attn_mla_lat_kernel,
        out_shape=jax.ShapeDtypeStruct((t, MLA_W), BF16),
        grid=(DEC_BATCH, DEC_SEQ // ATT_TQ),
        in_specs=[_qtile_spec(512), _qtile_spec(256), cspec(512), cspec(LANES), cspec(512),
                  _batch_spec(512), _batch_spec(LANES), _batch_spec(512)],
        out_specs=_qtile_spec(MLA_W),
        compiler_params=_params("parallel", "parallel"),
        name="attn_mla_lat",
    )(p["mqn"], p["mqr"], kc, krc, vc, b3(p["kn"]), b3(p["kr4"]), b3(p["vn"]))


def _merge_kernel(x_ref, mod_ref, od_ref, on_ref, om_ref, wg_ref, wbd_ref, wbn_ref, wbm_ref, wo_ref,
                  g_ref, b_ref, x1_ref, h2_ref):
    x = x_ref[...]
    m = mod_ref[0]
    h = (x * (1.0 + m[1:2]) + m[0:1]).astype(BF16)
    gates = _dot(h, wg_ref[...])
    mix = (_sigmoid(gates[:, 0:1024]) * _dot(od_ref[...], wbd_ref[...])
           + _sigmoid(gates[:, 1024:2048]) * _dot(on_ref[...], wbn_ref[...])
           + _sigmoid(gates[:, 2048:3072]) * _dot(om_ref[...], wbm_ref[...]))
    out = _dot(mix.astype(BF16), wo_ref[...])
    x1 = _layer_norm(DN_ALPHA * x + m[2:3] * out, g_ref[...], b_ref[...])
    x1_ref[...] = x1
    h2_ref[...] = x1 * (1.0 + m[4:5]) + m[3:4]


def _merge(x, mod, od, on, om, W, *, latent):
    t = x.shape[0]
    tm = PROJ_TM
    steps_per_batch = (DEC_SEQ if latent else t) // tm
    row = lambda w: pl.BlockSpec((tm, w), lambda i: (i, 0))
    return pl.pallas_call(
        _merge_kernel,
        out_shape=[jax.ShapeDtypeStruct((t, D_MODEL), F32), jax.ShapeDtypeStruct((t, D_MODEL), F32)],
        grid=(t // tm,),
        in_specs=[row(D_MODEL),
                  pl.BlockSpec((1, 6, D_MODEL), lambda i: (i // steps_per_batch, 0, 0)),
                  row(512), row(512), row(512),
                  _full(W["wg"].shape), _full(W["wbd"].shape), _full(W["wbn"].shape),
                  _full(W["wbm"].shape), _full(W["wo"].shape),
                  _full((1, D_MODEL)), _full((1, D_MODEL))],
        out_specs=[row(D_MODEL), row(D_MODEL)],
        compiler_params=_params("parallel"),
        name="merge",
    )(x, mod, od, on, om, W["wg"], W["wbd"], W["wbn"], W["wbm"], W["wo"], W["ln1_g"], W["ln1_b"])


def _first_index_of_max(vals, idx, sentinel):
    mx = functools.reduce(jnp.maximum, [jnp.max(v, axis=0, keepdims=True) for v in vals])
    cand = [jnp.min(jnp.where(v == mx, i, sentinel), axis=0, keepdims=True) for v, i in zip(vals, idx)]
    return mx, functools.reduce(jnp.minimum, cand)


def _router_kernel(h_ref, wr_ref, bias_ref, tri_ref, eid_ref, rank_ref, wtok_ref, cnt_ref, base_ref):
    tm = h_ref.shape[0]

    @pl.when(pl.program_id(0) == 0)
    def _():
        base_ref[...] = jnp.zeros_like(base_ref)

    logits = _dot_nt(wr_ref[...], h_ref[...].astype(BF16))
    scores = _sigmoid(logits)
    biased = scores + bias_ref[...]
    member = lax.broadcasted_iota(jnp.int32, (GROUP_SIZE, tm), 0)
    slabs = [biased[GROUP_SIZE * g:GROUP_SIZE * (g + 1)] for g in range(N_GROUPS)]
    gscore = []
    for s in slabs:
        m1, first = _first_index_of_max([s], [member], GROUP_SIZE)
        m2 = jnp.max(jnp.where(member == first, -jnp.inf, s), axis=0, keepdims=True)
        gscore.append(m1 + m2)
    gs = jnp.concatenate(gscore, axis=0)
    gidx = lax.broadcasted_iota(jnp.int32, (N_GROUPS, tm), 0)
    gsel = jnp.zeros((N_GROUPS, tm), F32)
    for _ in range(TOPK_GROUPS):
        _, first = _first_index_of_max([gs], [gidx], N_GROUPS)
        pick = gidx == first
        gsel = jnp.where(pick, 1.0, gsel)
        gs = jnp.where(pick, -jnp.inf, gs)
    cur = [jnp.where(gsel[g:g + 1] > 0.0, slabs[g], -jnp.inf) for g in range(N_GROUPS)]
    eidx = [member + GROUP_SIZE * g for g in range(N_GROUPS)]
    sel = [jnp.zeros((GROUP_SIZE, tm), F32) for _ in range(N_GROUPS)]
    picks = []
    for _ in range(TOP_K):
        _, first = _first_index_of_max(cur, eidx, N_EXPERTS)
        pick = [eidx[g] == first for g in range(N_GROUPS)]
        picks.append((first, pick))
        for g in range(N_GROUPS):
            sel[g] = jnp.where(pick[g], 1.0, sel[g])
            cur[g] = jnp.where(pick[g], -jnp.inf, cur[g])
    w = [jnp.where(sel[g] > 0.0, scores[GROUP_SIZE * g:GROUP_SIZE * (g + 1)], 0.0) for g in range(N_GROUPS)]
    total = functools.reduce(lambda u, v: u + v, [jnp.sum(x, axis=0, keepdims=True) for x in w])
    w = [x / total * ROUTED_SCALE for x in w]

    sel_all = jnp.concatenate(sel, axis=0)
    incl = _dot(sel_all.astype(BF16), tri_ref[...])
    base = base_ref[:, 0:1]
    rank_all = incl - sel_all + base
    rank = [rank_all[GROUP_SIZE * g:GROUP_SIZE * (g + 1)] for g in range(N_GROUPS)]
    cnt = base + jnp.sum(sel_all, axis=1, keepdims=True)
    base_ref[...] = jnp.broadcast_to(cnt, base_ref.shape)
    cnt_ref[...] = jnp.broadcast_to(cnt, cnt_ref.shape)

    def picked(vals, pick):
        parts = [jnp.sum(jnp.where(p, v, 0.0), axis=0, keepdims=True) for p, v in zip(pick, vals)]
        return functools.reduce(lambda u, v: u + v, parts)

    eid_ref[...] = jnp.concatenate([first for first, _ in picks], axis=0)
    rank_ref[...] = jnp.concatenate([picked(rank, pick) for _, pick in picks], axis=0).astype(jnp.int32)
    w_rows = [picked(w, pick) for _, pick in picks] + [jnp.zeros((LANES - TOP_K, tm), F32)]
    wtok_ref[...] = jnp.concatenate(w_rows, axis=0).T


def _router(h2, wr_t, bias_col):
    t = h2.shape[0]
    tm = PROJ_TM
    tri = jnp.asarray(np.triu(np.ones((tm, tm), np.float32)), BF16)
    slots = lambda dt: jax.ShapeDtypeStruct((TOP_K, t), dt)
    return pl.pallas_call(
        _router_kernel,
        out_shape=[slots(jnp.int32), slots(jnp.int32), jax.ShapeDtypeStruct((t, LANES), F32),
                   jax.ShapeDtypeStruct((N_EXPERTS, LANES), F32)],
        grid=(t // tm,),
        in_specs=[pl.BlockSpec((tm, D_MODEL), lambda i: (i, 0)),
                  _full((N_EXPERTS, D_MODEL)), _full((N_EXPERTS, 1)), _full((tm, tm))],
        out_specs=[pl.BlockSpec((TOP_K, tm), lambda i: (0, i)), pl.BlockSpec((TOP_K, tm), lambda i: (0, i)),
                   pl.BlockSpec((tm, LANES), lambda i: (i, 0)), _full((N_EXPERTS, LANES))],
        scratch_shapes=[pltpu.VMEM((N_EXPERTS, LANES), F32)],
        compiler_params=_params("arbitrary"),
        name="router",
    )(h2, wr_t, bias_col, tri)


def _moe_tiles(t):
    return (t * TOP_K) // MOE_TILE + N_EXPERTS


def _moe_plan(eid, rank, cnt, t):
    counts = cnt[:, 0].astype(jnp.int32)
    tiles = jnp.maximum((counts + MOE_TILE - 1) // MOE_TILE, 1)
    ends = jnp.cumsum(tiles)
    starts = ends - tiles
    experts = jnp.arange(N_EXPERTS, dtype=jnp.int32)
    pos = rank + jnp.sum(jnp.where(eid[:, :, None] == experts, starts * MOE_TILE, 0), axis=-1)
    tile_ids = jnp.arange(_moe_tiles(t), dtype=jnp.int32)
    owner = tile_ids[:, None] >= ends[None, :]
    tile_expert = jnp.minimum(jnp.sum(owner, axis=-1), N_EXPERTS - 1)
    is_owner = tile_expert[:, None] == experts[None, :]
    start_of = jnp.sum(jnp.where(is_owner, starts, 0), axis=-1)
    count_of = jnp.sum(jnp.where(is_owner, counts, 0), axis=-1)
    real = jnp.clip(count_of - (tile_ids - start_of) * MOE_TILE, 0, MOE_TILE)
    real = jnp.where(tile_ids < ends[-1], real, 0)
    return pos.astype(jnp.int32), tile_expert.astype(jnp.int32), real.astype(jnp.int32)


def _sc_workers():
    info = plsc.get_sparse_core_info()
    return info.num_cores, info.num_subcores


def _sc_mesh():
    return plsc.VectorSubcoreMesh(core_axis_name="core", subcore_axis_name="subcore")


def _sc_worker_id(n_cores):
    return lax.axis_index("subcore") * n_cores + lax.axis_index("core")


def _dispatch(h2, pos):
    t = h2.shape[0]
    n_cores, n_sub = _sc_workers()
    ch = SC_ROWS
    per_worker = t // (n_cores * n_sub)
    n_chunks = per_worker // ch
    pos_chunks = pos.reshape(TOP_K, t // ch, ch).transpose(1, 0, 2)

    @functools.partial(
        pl.kernel, mesh=_sc_mesh(),
        out_type=jax.ShapeDtypeStruct((_moe_tiles(t) * MOE_TILE, D_MODEL), F32),
        scratch_types=[pltpu.VMEM((TOP_K, ch), jnp.int32), pltpu.VMEM((ch, D_MODEL), F32),
                       pltpu.SemaphoreType.DMA],
        name="moe_dispatch",
    )
    def run(h_hbm, pos_hbm, xs_hbm, idx_ref, rows_ref, sem):
        first = _sc_worker_id(n_cores) * n_chunks

        @pl.loop(0, n_chunks)
        def _(j):
            c = first + j
            pltpu.sync_copy(pos_hbm.at[c], idx_ref)
            pltpu.sync_copy(h_hbm.at[pl.ds(pl.multiple_of(c * ch, ch), ch)], rows_ref)
            copies = [pltpu.async_copy(rows_ref, xs_hbm.at[idx_ref.at[k]], sem) for k in range(TOP_K)]
            for cp in copies:
                cp.wait()

    return run(h2, pos_chunks)


def _gather_rows(ys, idx):
    n = idx.shape[0]
    n_cores, n_sub = _sc_workers()
    ch = SC_ROWS
    per_worker = n // (n_cores * n_sub)
    n_chunks = per_worker // ch

    @functools.partial(
        pl.kernel, mesh=_sc_mesh(),
        out_type=jax.ShapeDtypeStruct((n, D_MODEL), F32),
        scratch_types=[pltpu.VMEM((ch,), jnp.int32), pltpu.VMEM((ch, D_MODEL), F32),
                       pltpu.SemaphoreType.DMA],
        name="moe_gather",
    )
    def run(ys_hbm, idx_hbm, out_hbm, idx_ref, rows_ref, sem):
        first = _sc_worker_id(n_cores) * per_worker

        @pl.loop(0, n_chunks)
        def _(j):
            off = pl.multiple_of(first + j * ch, ch)
            pltpu.sync_copy(idx_hbm.at[pl.ds(off, ch)], idx_ref)
            pltpu.async_copy(ys_hbm.at[idx_ref], rows_ref, sem).wait()
            pltpu.sync_copy(rows_ref, out_hbm.at[pl.ds(off, ch)])

    return run(ys, idx)


def _ffn_kernel(te_ref, real_ref, xs_ref, weg_ref, weu_ref, wed_ref, ys_ref):
    i = pl.program_id(0)
    real = real_ref[i]

    @pl.when(real > 0)
    def _():
        row = lax.broadcasted_iota(jnp.int32, (MOE_TILE, 1), 0)
        x = jnp.where(row < real, xs_ref[...], 0.0).astype(BF16)
        act = _silu(_dot(x, weg_ref[0])) * _dot(x, weu_ref[0])
        ys_ref[...] = _dot(act.astype(BF16), wed_ref[0])

    @pl.when(real == 0)
    def _():
        ys_ref[...] = jnp.zeros_like(ys_ref)


def _ffn(xs, tile_expert, tile_real, W):
    n_tiles = xs.shape[0] // MOE_TILE
    wspec = lambda shape: pl.BlockSpec((1,) + shape, lambda i, te, tr: (te[i], 0, 0))
    return pl.pallas_call(
        _ffn_kernel,
        out_shape=jax.ShapeDtypeStruct(xs.shape, F32),
        grid_spec=pltpu.PrefetchScalarGridSpec(
            num_scalar_prefetch=2,
            grid=(n_tiles,),
            in_specs=[pl.BlockSpec((MOE_TILE, D_MODEL), lambda i, te, tr: (i, 0)),
                      wspec((D_MODEL, EXPERT_FF)), wspec((D_MODEL, EXPERT_FF)), wspec((EXPERT_FF, D_MODEL))],
            out_specs=pl.BlockSpec((MOE_TILE, D_MODEL), lambda i, te, tr: (i, 0))),
        compiler_params=_params("parallel"),
        name="moe_ffn",
    )(tile_expert, tile_real, xs, W["weg"], W["weu"], W["wed"])


def _combine_kernel(y_ref, h_ref, wtok_ref, x1_ref, mod_ref, wsg_ref, wsu_ref, wsd_ref,
                    g_ref, b_ref, out_ref):
    wt = wtok_ref[...]
    routed = None
    for k in range(TOP_K):
        part = wt[:, k:k + 1] * y_ref[k]
        routed = part if routed is None else routed + part
    h = h_ref[...].astype(BF16)
    shared = _dot((_silu(_dot(h, wsg_ref[...])) * _dot(h, wsu_ref[...])).astype(BF16), wsd_ref[...])
    m = mod_ref[0]
    y2 = DN_ALPHA * x1_ref[...] + m[5:6] * (routed + shared)
    out_ref[...] = _layer_norm(y2, g_ref[...], b_ref[...])


def _combine(ytok, h2, wtok, x1, mod, W, *, latent):
    t = h2.shape[0]
    ct = COMBINE_TM
    steps_per_batch = (DEC_SEQ if latent else t) // ct
    row = lambda w: pl.BlockSpec((ct, w), lambda i: (i, 0))
    return pl.pallas_call(
        _combine_kernel,
        out_shape=jax.ShapeDtypeStruct((t, D_MODEL), F32),
        grid=(t // ct,),
        in_specs=[pl.BlockSpec((TOP_K, ct, D_MODEL), lambda i: (0, i, 0)),
                  row(D_MODEL), row(LANES), row(D_MODEL),
                  pl.BlockSpec((1, 6, D_MODEL), lambda i: (i // steps_per_batch, 0, 0)),
                  _full((D_MODEL, SHARED_FF)), _full((D_MODEL, SHARED_FF)), _full((SHARED_FF, D_MODEL)),
                  _full((1, D_MODEL)), _full((1, D_MODEL))],
        out_specs=row(D_MODEL),
        compiler_params=_params("parallel"),
        name="moe_combine",
    )(ytok, h2, wtok, x1, mod, W["wsg"], W["wsu"], W["wsd"], W["ln2_g"], W["ln2_b"])


def _moe(h2, x1, mod, W, *, latent):
    t = h2.shape[0]
    eid, rank, wtok, cnt = _router(h2, W["wr_t"], W["rbias"])
    pos, tile_expert, tile_real = _moe_plan(eid, rank, cnt, t)
    xs = _dispatch(h2, pos)
    ys = _ffn(xs, tile_expert, tile_real, W)
    ytok = _gather_rows(ys, pos.reshape(TOP_K * t)).reshape(TOP_K, t, D_MODEL)
    return _combine(ytok, h2, wtok, x1, mod, W, latent=latent)


def _rope_partner(n_blocks, block):
    half = block // 2
    i = np.arange(n_blocks * block)
    return np.where((i % block) < half, i + half, i - half)


_QB_NOPE = np.concatenate([np.arange(MLA_NOPE) + (MLA_NOPE + MLA_ROPE) * h for h in range(MLA_HEADS)])
_QB_ROPE = np.concatenate([np.arange(MLA_ROPE) + (MLA_NOPE + MLA_ROPE) * h + MLA_NOPE for h in range(MLA_HEADS)])
_KVB_NOPE = np.concatenate([np.arange(MLA_NOPE) + (MLA_NOPE + MLA_V) * h for h in range(MLA_HEADS)])
_KVB_V = np.concatenate([np.arange(MLA_V) + (MLA_NOPE + MLA_V) * h + MLA_NOPE for h in range(MLA_HEADS)])
_DIFF_PARTNER = _rope_partner(2 * DIFF_W // 32, 32)
_MLA_PARTNER = _rope_partner(MLA_HEADS * MLA_ROPE // 16, 16)
_KR_PARTNER = _rope_partner(MLA_ROPE // 16, 16)


def _layer_weights(l, w_in, mla_qa_g, mla_wq_b, mla_kva_g, mla_wkv_b, w_branch_diff, w_branch_na,
                   w_branch_mla, w_out, ln1_g, ln1_b, ln2_g, ln2_b, w_router, router_bias,
                   w_exp_gate, w_exp_up, w_exp_down, w_sh_gate, w_sh_up, w_sh_down):
    win = w_in[l].astype(BF16)
    wa = win[:, :3072]
    qa_kva = win[:, 3072:3712]
    kr = win[:, 3712:3744]
    kr4 = jnp.tile(kr, (1, LANES // MLA_ROPE))
    krp4 = jnp.tile(kr[:, _KR_PARTNER], (1, LANES // MLA_ROPE))
    wqb = mla_wq_b[l].astype(BF16)
    wq_rope = wqb[:, _QB_ROPE]
    wkvb = mla_wkv_b[l].astype(BF16)
    row = lambda v: v[l].reshape(1, -1).astype(F32)
    return {
        "wa": wa,
        "wp": wa[:, :2 * DIFF_W][:, _DIFF_PARTNER],
        "wm_ctx": jnp.concatenate([qa_kva, kr4], axis=1),
        "wm_lat": jnp.concatenate([qa_kva, kr4, krp4], axis=1),
        "qag": row(mla_qa_g), "kvag": row(mla_kva_g),
        "wqb": jnp.concatenate([wqb[:, _QB_NOPE], wq_rope], axis=1),
        "wqp": wq_rope[:, _MLA_PARTNER],
        "wkvb": jnp.concatenate([wkvb[:, _KVB_NOPE], wkvb[:, _KVB_V]], axis=1),
        "wg": win[:, 3744:],
        "wbd": w_branch_diff[l].astype(BF16), "wbn": w_branch_na[l].astype(BF16),
        "wbm": w_branch_mla[l].astype(BF16), "wo": w_out[l].astype(BF16),
        "ln1_g": row(ln1_g), "ln1_b": row(ln1_b), "ln2_g": row(ln2_g), "ln2_b": row(ln2_b),
        "wr_t": w_router[l].T.astype(BF16), "rbias": router_bias[l].reshape(N_EXPERTS, 1).astype(F32),
        "weg": w_exp_gate[l].astype(BF16), "weu": w_exp_up[l].astype(BF16), "wed": w_exp_down[l].astype(BF16),
        "wsg": w_sh_gate[l].astype(BF16), "wsu": w_sh_up[l].astype(BF16), "wsd": w_sh_down[l].astype(BF16),
    }


def _rope_tables():
    t = jnp.arange(DEC_SEQ)
    pos = [(t // GRID_W).astype(F32), (t % GRID_W).astype(F32)]

    def table(block):
        half = block // 4
        inv = ROPE_BASE ** (-jnp.arange(half, dtype=F32) / half)
        cos, sin = [], []
        for p in pos:
            ang = p[:, None] * inv[None, :]
            cos += [jnp.cos(ang), jnp.cos(ang)]
            sin += [-jnp.sin(ang), jnp.sin(ang)]
        reps = LANES // block
        return (jnp.tile(jnp.concatenate(cos, axis=1), (1, reps)),
                jnp.tile(jnp.concatenate(sin, axis=1), (1, reps)))

    cd, sd = table(DIFF_DH)
    cm, sm = table(MLA_ROPE)
    return {"cd": cd, "sd": sd, "cm": cm, "sm": sm}


def kernel(x_prompt, x_sample, cache_diff_k, cache_diff_v, cache_na_k, cache_na_v, cache_mla_ckv, cache_mla_krope, c, c_ctx, w_mod, b_mod, w_in, diff_lambda, diff_subln_g, na_rpb, mla_qa_g, mla_wq_b, mla_kva_g, mla_wkv_b, w_branch_diff, w_branch_na, w_branch_mla, w_out, ln1_g, ln1_b, ln2_g, ln2_b, w_router, router_bias, w_exp_gate, w_exp_up, w_exp_down, w_sh_gate, w_sh_up, w_sh_down):
    t_ctx = BATCH * SEQ
    t_lat = DEC_BATCH * DEC_SEQ
    cvec = jnp.concatenate([c, c_ctx[None, :], jnp.zeros((MOD_ROWS - DEC_BATCH - 1, D_MODEL), F32)], axis=0)
    mods = _modulation(cvec, w_mod, b_mod).reshape(DEPTH, MOD_ROWS, 6, D_MODEL)
    rope = _rope_tables()
    ck_d = cache_diff_k.reshape(DEC_BATCH, DEPTH, PAST_LEN, DIFF_W)
    cv_d = cache_diff_v.reshape(DEC_BATCH, DEPTH, PAST_LEN, DIFF_W)
    ck_n = cache_na_k.reshape(DEC_BATCH, DEPTH, PAST_LEN, NA_W)
    cv_n = cache_na_v.reshape(DEC_BATCH, DEPTH, PAST_LEN, NA_W)

    xc = x_prompt.reshape(t_ctx, D_MODEL)
    xl = x_sample.reshape(t_lat, D_MODEL)
    states = []
    for l in range(DEPTH):
        W = _layer_weights(l, w_in, mla_qa_g, mla_wq_b, mla_kva_g, mla_wkv_b, w_branch_diff, w_branch_na,
                           w_branch_mla, w_out, ln1_g, ln1_b, ln2_g, ln2_b, w_router, router_bias,
                           w_exp_gate, w_exp_up, w_exp_down, w_sh_gate, w_sh_up, w_sh_down)
        lam = diff_lambda[l].astype(F32)
        subln = diff_subln_g[l].reshape(1, LANES).astype(F32)
        mod_c = mods[l, DEC_BATCH:DEC_BATCH + 1]
        mod_l = mods[l, :DEC_BATCH]

        Wc = dict(W, wm=W["wm_ctx"])
        names = ["dq", "nq", "mqn", "mqr", "kr4", "kn", "vn", "dk", "dv", "nk", "nv", "ckv", "kr"]
        pc = dict(zip(names, _project(xc, mod_c, Wc, None, latent=False)))
        od, on, om = _attn_ctx(pc, lam, subln, l)
        x1, h2 = _merge(xc, mod_c, od, on, om, W, latent=False)
        xc = _moe(h2, x1, mod_c, W, latent=False)
        states.append(pc)

        Wl = dict(W, wm=W["wm_lat"])
        names = ["dq", "dk", "dv", "nq", "nk", "nv", "mqn", "mqr", "kr4", "kn", "vn"]
        pll = dict(zip(names, _project(xl, mod_l, Wl, rope, latent=True)))
        od = _attn_diff_lat(pll, ck_d, cv_d, lam, subln, l)
        on = _attn_na_lat(pll, ck_n, cv_n, _na_bias(na_rpb[l].astype(F32)), l)
        kc, vc, krc = _mla_cache(cache_mla_ckv, cache_mla_krope, W["wkvb"], l)
        om = _attn_mla_lat(pll, kc, vc, krc)
        x1, h2 = _merge(xl, mod_l, od, on, om, W, latent=True)
        xl = _moe(h2, x1, mod_l, W, latent=True)

    def stack(name, shape):
        return jnp.stack([s[name].reshape((BATCH, SEQ) + shape) for s in states], axis=1)

    return (xc.reshape(BATCH, SEQ, D_MODEL), xl.reshape(DEC_BATCH, DEC_SEQ, D_MODEL),
            stack("dk", (DIFF_HEADS, 2 * DIFF_DH)), stack("dv", (DIFF_HEADS, 2 * DIFF_DH)),
            stack("nk", (NA_HEADS, NA_DH)), stack("nv", (NA_HEADS, NA_DH)),
            stack("ckv", (MLA_KV_LORA,)), stack("kr", (MLA_ROPE,)))
```

```python
import functools

import numpy as np
import jax
import jax.numpy as jnp
from jax import lax
from jax.experimental import pallas as pl
from jax.experimental.pallas import tpu as pltpu
from jax.experimental.pallas import tpu_sc as plsc

D_MODEL = 1024
BATCH = 32
SEQ = 256
DEPTH = 2
DEC_BATCH = 8
DEC_SEQ = 2048
PAST_LEN = 512
GRID_W = 64
GRID_ROWS = DEC_SEQ // GRID_W
ROPE_BASE = 10000.0
DIFF_HEADS = 4
DIFF_DH = 64
DIFF_W = 512
NA_HEADS = 8
NA_DH = 64
NA_W = 512
NA_WIN_ROWS = 8
NA_WIN_COLS = 16
MLA_HEADS = 8
MLA_Q_LORA = 384
MLA_KV_LORA = 256
MLA_NOPE = 64
MLA_ROPE = 32
MLA_V = 64
MLA_W = 512
N_EXPERTS = 64
N_GROUPS = 8
GROUP_SIZE = N_EXPERTS // N_GROUPS
TOPK_GROUPS = 4
TOP_K = 8
EXPERT_FF = 256
SHARED_FF = 256
ROUTED_SCALE = 2.5
DN_ALPHA = (2 * DEPTH) ** 0.25
LN_EPS = 1e-5
RMS_EPS = 1e-6

F32 = jnp.float32
BF16 = jnp.bfloat16

LANES = 128
VMEM_LIMIT_BYTES = 56 * 1024 * 1024
NEG_BIG = -1e30

DIFF_SCALE = DIFF_DH ** -0.5
NA_SCALE = NA_DH ** -0.5
MLA_SCALE = (MLA_NOPE + MLA_ROPE) ** -0.5

PROJ_TM = 512
ATT_TQ = 256
NA_Q_ROWS = ATT_TQ // GRID_W
NA_KEY_ROWS = NA_Q_ROWS + NA_WIN_ROWS
NA_KEYS = NA_KEY_ROWS * GRID_W
MOE_TILE_MIN = 128
MOE_TILE_MAX = 512
SC_ROWS = 128
COMBINE_TM = 256


def _dot(a, b):
    return jnp.dot(a, b, preferred_element_type=F32)


def _dot_nt(a, b):
    return lax.dot_general(a, b, (((1,), (1,)), ((), ())), preferred_element_type=F32)


def _sigmoid(x):
    return 1.0 / (1.0 + jnp.exp(-x))


def _silu(x):
    return x * _sigmoid(x)


def _params(*sem):
    return pltpu.CompilerParams(dimension_semantics=sem, vmem_limit_bytes=VMEM_LIMIT_BYTES)


def _full(shape):
    n = len(shape)
    return pl.BlockSpec(shape, lambda *_: (0,) * n)


def _layer_norm(y, g, b):
    mu = jnp.mean(y, axis=-1, keepdims=True)
    yc = y - mu
    var = jnp.mean(yc * yc, axis=-1, keepdims=True)
    return yc * lax.rsqrt(var + LN_EPS) * g + b


def _rms(x, g):
    return x * lax.rsqrt(jnp.mean(x * x, axis=-1, keepdims=True) + RMS_EPS) * g


HIGH_HALF = -65536
PACKED_W = D_MODEL // 2


def _pack_bf16_pairs(x):
    n = x.shape[1] // 2
    bits = lax.bitcast_convert_type(x.astype(BF16).astype(F32), jnp.int32)
    return lax.shift_right_logical(bits[:, :n], 16) | (bits[:, n:] & HIGH_HALF)


def _unpack_bf16_pairs(w):
    lo = lax.bitcast_convert_type(lax.shift_left(w, 16), F32)
    hi = lax.bitcast_convert_type(w & HIGH_HALF, F32)
    return jnp.concatenate([lo, hi], axis=1)


MOD_ROWS = 16
MOD_TN = 1536


def _mod_kernel(c_ref, w_ref, b_ref, o_ref):
    s = _silu(c_ref[...]).astype(BF16)
    o_ref[0] = _dot(s, w_ref[0].astype(BF16)) + b_ref[0]


def _modulation(cvec, w_mod, b_mod):
    n = 6 * D_MODEL
    return pl.pallas_call(
        _mod_kernel,
        out_shape=jax.ShapeDtypeStruct((DEPTH, MOD_ROWS, n), F32),
        grid=(DEPTH, n // MOD_TN),
        in_specs=[
            pl.BlockSpec((MOD_ROWS, D_MODEL), lambda l, j: (0, 0)),
            pl.BlockSpec((1, D_MODEL, MOD_TN), lambda l, j: (l, 0, j)),
            pl.BlockSpec((1, 1, MOD_TN), lambda l, j: (l, 0, j)),
        ],
        out_specs=pl.BlockSpec((1, MOD_ROWS, MOD_TN), lambda l, j: (l, 0, j)),
        compiler_params=_params("parallel", "parallel"),
        name="modulation",
    )(cvec, w_mod, b_mod.reshape(DEPTH, 1, n))


def _proj_common(x_ref, mod_ref, wa_ref, wm_ref, qag_ref, kvag_ref, wqb_ref, wkvb_ref):
    m = mod_ref[0]
    h = (x_ref[...] * (1.0 + m[1:2]) + m[0:1]).astype(BF16)
    a = _dot(h, wa_ref[...])
    mm = _dot(h, wm_ref[...])
    qan = _rms(mm[:, :MLA_Q_LORA], qag_ref[...]).astype(BF16)
    mq = _dot(qan, wqb_ref[...])
    ckv = _rms(mm[:, MLA_Q_LORA:MLA_Q_LORA + MLA_KV_LORA], kvag_ref[...])
    kv = _dot(ckv.astype(BF16), wkvb_ref[...])
    return h, a, mm, qan, mq, ckv, kv


def _proj_ctx_kernel(x_ref, mod_ref, wa_ref, wm_ref, qag_ref, kvag_ref, wqb_ref, wkvb_ref,
                     dq_ref, nq_ref, mqn_ref, mqr_ref, kr4_ref, kn_ref, vn_ref,
                     dk_ref, dv_ref, nk_ref, nv_ref, ckv_ref, kr_ref):
    _, a, mm, _, mq, ckv, kv = _proj_common(x_ref, mod_ref, wa_ref, wm_ref, qag_ref, kvag_ref,
                                            wqb_ref, wkvb_ref)
    dq_ref[...] = (a[:, 0:512] * DIFF_SCALE).astype(BF16)
    dk_ref[...] = a[:, 512:1024]
    dv_ref[...] = a[:, 1024:1536]
    nq_ref[...] = (a[:, 1536:2048] * NA_SCALE).astype(BF16)
    nk_ref[...] = a[:, 2048:2560]
    nv_ref[...] = a[:, 2560:3072]
    mqn_ref[...] = (mq[:, :512] * MLA_SCALE).astype(BF16)
    mqr_ref[...] = (mq[:, 512:768] * MLA_SCALE).astype(BF16)
    kr4 = mm[:, 640:768]
    kr4_ref[...] = kr4.astype(BF16)
    kr_ref[...] = kr4[:, :MLA_ROPE]
    ckv_ref[...] = ckv
    kn_ref[...] = kv[:, :512].astype(BF16)
    vn_ref[...] = kv[:, 512:].astype(BF16)


def _proj_lat_kernel(x_ref, mod_ref, wa_ref, wm_ref, qag_ref, kvag_ref, wqb_ref, wkvb_ref,
                     wp_ref, wqp_ref, cd_ref, sd_ref, cm_ref, sm_ref,
                     dq_ref, dk_ref, dv_ref, nq_ref, nk_ref, nv_ref,
                     mqn_ref, mqr_ref, kr4_ref, kn_ref, vn_ref):
    h, a, mm, qan, mq, _, kv = _proj_common(x_ref, mod_ref, wa_ref, wm_ref, qag_ref, kvag_ref,
                                            wqb_ref, wkvb_ref)
    ap = _dot(h, wp_ref[...])
    mqp = _dot(qan, wqp_ref[...])
    cd = cd_ref[...]
    sd = sd_ref[...]
    cm = cm_ref[...]
    sm = sm_ref[...]
    for j in range(DIFF_W // LANES):
        lo, hi = LANES * j, LANES * (j + 1)
        dq_ref[:, lo:hi] = ((a[:, lo:hi] * cd + ap[:, lo:hi] * sd) * DIFF_SCALE).astype(BF16)
        dk_ref[:, lo:hi] = (a[:, 512 + lo:512 + hi] * cd + ap[:, 512 + lo:512 + hi] * sd).astype(BF16)
    dv_ref[...] = a[:, 1024:1536].astype(BF16)
    nq_ref[...] = (a[:, 1536:2048] * NA_SCALE).astype(BF16)
    nk_ref[...] = a[:, 2048:2560].astype(BF16)
    nv_ref[...] = a[:, 2560:3072].astype(BF16)
    mqn_ref[...] = (mq[:, :512] * MLA_SCALE).astype(BF16)
    for j in range(2):
        lo, hi = LANES * j, LANES * (j + 1)
        mqr_ref[:, lo:hi] = ((mq[:, 512 + lo:512 + hi] * cm + mqp[:, lo:hi] * sm) * MLA_SCALE).astype(BF16)
    kr4_ref[...] = (mm[:, 640:768] * cm + mm[:, 768:896] * sm).astype(BF16)
    kn_ref[...] = kv[:, :512].astype(BF16)
    vn_ref[...] = kv[:, 512:].astype(BF16)


def _project(x, mod, W, rope, *, latent):
    t = x.shape[0]
    tm = PROJ_TM
    tokens_per_batch = DEC_SEQ if latent else t
    steps_per_batch = tokens_per_batch // tm
    row = lambda w: pl.BlockSpec((tm, w), lambda i: (i, 0))
    common_in = [
        row(D_MODEL),
        pl.BlockSpec((1, 6, D_MODEL), lambda i: (i // steps_per_batch, 0, 0)),
        _full(W["wa"].shape), _full(W["wm"].shape), _full((1, MLA_Q_LORA)), _full((1, MLA_KV_LORA)),
        _full(W["wqb"].shape), _full(W["wkvb"].shape),
    ]
    common_args = [x, mod, W["wa"], W["wm"], W["qag"], W["kvag"], W["wqb"], W["wkvb"]]
    bf = lambda w: jax.ShapeDtypeStruct((t, w), BF16)
    f32 = lambda w: jax.ShapeDtypeStruct((t, w), F32)
    if latent:
        tab = pl.BlockSpec((tm, LANES), lambda i: (i % steps_per_batch, 0))
        widths = [512, 512, 512, 512, 512, 512, 512, 256, 128, 512, 512]
        return pl.pallas_call(
            _proj_lat_kernel,
            out_shape=[bf(w) for w in widths],
            grid=(t // tm,),
            in_specs=common_in + [_full(W["wp"].shape), _full(W["wqp"].shape), tab, tab, tab, tab],
            out_specs=[row(w) for w in widths],
            compiler_params=_params("parallel"),
            name="proj_lat",
        )(*common_args, W["wp"], W["wqp"], rope["cd"], rope["sd"], rope["cm"], rope["sm"])
    bf_w = [512, 512, 512, 256, 128, 512, 512]
    f32_w = [512, 512, 512, 512, 256, 32]
    return pl.pallas_call(
        _proj_ctx_kernel,
        out_shape=[bf(w) for w in bf_w] + [f32(w) for w in f32_w],
        grid=(t // tm,),
        in_specs=common_in,
        out_specs=[row(w) for w in bf_w + f32_w],
        compiler_params=_params("parallel"),
        name="proj_ctx",
    )(*common_args)


def _lane_iota():
    return lax.broadcasted_iota(jnp.int32, (1, LANES), 1)


def _softmax_parts(parts):
    m = functools.reduce(jnp.maximum, [jnp.max(s, axis=-1, keepdims=True) for s in parts])
    es = [jnp.exp(s - m) for s in parts]
    l = functools.reduce(lambda u, v: u + v, [jnp.sum(e, axis=-1, keepdims=True) for e in es])
    return [e / l for e in es]


def _diff_lambda(lam_ref, layer):
    lp = lam_ref[...]
    lam_init = 0.8 - 0.6 * float(np.exp(-0.3 * layer))
    s1 = jnp.sum(lp[0:1] * lp[1:2], axis=-1, keepdims=True)
    s2 = jnp.sum(lp[2:3] * lp[3:4], axis=-1, keepdims=True)
    return jnp.exp(s1) - jnp.exp(s2) + lam_init, lam_init


def _diff_heads(q_ref, ks, vs, lam_ref, g_ref, o_ref, layer):
    lam, lam_init = _diff_lambda(lam_ref, layer)
    first_map = _lane_iota() < DIFF_DH
    g = g_ref[...]
    for h in range(DIFF_HEADS):
        hs = slice(LANES * h, LANES * (h + 1))
        q = q_ref[:, hs]
        q1 = jnp.where(first_map, q, jnp.zeros_like(q))
        q2 = jnp.where(first_map, jnp.zeros_like(q), q)
        kk = [k[:, hs].astype(BF16) for k in ks]
        p1 = _softmax_parts([_dot_nt(q1, k) for k in kk])
        p2 = _softmax_parts([_dot_nt(q2, k) for k in kk])
        o = None
        for a1, a2, v in zip(p1, p2, vs):
            part = _dot((a1 - lam * a2).astype(BF16), v[:, hs].astype(BF16))
            o = part if o is None else o + part
        o = _rms(o, g) * (1.0 - lam_init)
        o_ref[:, hs] = o.astype(BF16)


def _pair_heads(q_of, k_of, v_of, bias_of, o_ref, n_pairs):
    first = _lane_iota() < 64
    for j in range(n_pairs):
        ps = slice(LANES * j, LANES * (j + 1))
        ks = k_of(j)
        vs = v_of(j)
        outs = []
        for hh in range(2):
            q = q_of(j, hh)
            ss = [_dot_nt(q, k) for k in ks]
            bs = bias_of(2 * j + hh)
            ss = [s if b is None else s + b for s, b in zip(ss, bs)]
            ps_ = _softmax_parts(ss)
            o = None
            for p, v in zip(ps_, vs):
                part = _dot(p.astype(BF16), v)
                o = part if o is None else o + part
            outs.append(o)
        o_ref[:, ps] = jnp.where(first, outs[0], outs[1]).astype(BF16)


def _na_q(q_ref):
    first = _lane_iota() < NA_DH

    def q_of(j, hh):
        q = q_ref[:, LANES * j:LANES * (j + 1)]
        keep = first if hh == 0 else jnp.logical_not(first)
        return jnp.where(keep, q, jnp.zeros_like(q))
    return q_of


def _mla_q(qn_ref, qr_ref):
    lane = _lane_iota()
    first = lane < MLA_NOPE

    def q_of(j, hh):
        h = 2 * j + hh
        qn = qn_ref[:, LANES * j:LANES * (j + 1)]
        keep = first if hh == 0 else jnp.logical_not(first)
        qn = jnp.where(keep, qn, jnp.zeros_like(qn))
        qr = qr_ref[:, LANES * (h // 4):LANES * (h // 4 + 1)]
        qr = jnp.where((lane // MLA_ROPE) == (h % 4), qr, jnp.zeros_like(qr))
        return jnp.concatenate([qn, qr], axis=1)
    return q_of


def _attn_ctx_kernel(layer, dq_ref, dk_ref, dv_ref, nq_ref, nk_ref, nv_ref,
                     mqn_ref, mqr_ref, kn_ref, kr4_ref, vn_ref, lam_ref, g_ref,
                     od_ref, on_ref, om_ref):
    _diff_heads(dq_ref, [dk_ref], [dv_ref], lam_ref, g_ref, od_ref, layer)
    none = lambda h: [None]
    pair = lambda j: slice(LANES * j, LANES * (j + 1))
    _pair_heads(_na_q(nq_ref),
                lambda j: [nk_ref[:, pair(j)].astype(BF16)],
                lambda j: [nv_ref[:, pair(j)].astype(BF16)],
                none, on_ref, NA_HEADS // 2)
    kr4 = kr4_ref[...]
    _pair_heads(_mla_q(mqn_ref, mqr_ref),
                lambda j: [jnp.concatenate([kn_ref[:, pair(j)], kr4], axis=1)],
                lambda j: [vn_ref[:, pair(j)]],
                none, om_ref, MLA_HEADS // 2)


def _attn_ctx(p, lam, g, layer):
    t = p["dq"].shape[0]
    row = lambda w: pl.BlockSpec((SEQ, w), lambda b: (b, 0))
    names = ["dq", "dk", "dv", "nq", "nk", "nv", "mqn", "mqr", "kn", "kr4", "vn"]
    out = jax.ShapeDtypeStruct((t, 512), BF16)
    return pl.pallas_call(
        functools.partial(_attn_ctx_kernel, layer),
        out_shape=[out, out, out],
        grid=(t // SEQ,),
        in_specs=[row(p[n].shape[1]) for n in names] + [_full((4, DIFF_DH)), _full((1, LANES))],
        out_specs=[row(512)] * 3,
        compiler_params=_params("parallel"),
        name="attn_ctx",
    )(*[p[n] for n in names], lam, g)


def _cache_spec(width, layer):
    return pl.BlockSpec((None, None, PAST_LEN, width), lambda b, q: (b, layer, 0, 0))


def _batch_spec(width):
    return pl.BlockSpec((None, DEC_SEQ, width), lambda b, q: (b, 0, 0))


def _qtile_spec(width):
    steps = DEC_SEQ // ATT_TQ
    return pl.BlockSpec((ATT_TQ, width), lambda b, q: (b * steps + q, 0))


def _attn_diff_lat_kernel(layer, q_ref, kc_ref, vc_ref, kn_ref, vn_ref, lam_ref, g_ref, o_ref):
    _diff_heads(q_ref, [kc_ref, kn_ref], [vc_ref, vn_ref], lam_ref, g_ref, o_ref, layer)


def _attn_diff_lat(p, cache_k, cache_v, lam, g, layer):
    t = p["dq"].shape[0]
    b3 = lambda a: a.reshape(DEC_BATCH, DEC_SEQ, a.shape[1])
    return pl.pallas_call(
        functools.partial(_attn_diff_lat_kernel, layer),
        out_shape=jax.ShapeDtypeStruct((t, DIFF_W), BF16),
        grid=(DEC_BATCH, DEC_SEQ // ATT_TQ),
        in_specs=[_qtile_spec(DIFF_W), _cache_spec(DIFF_W, layer), _cache_spec(DIFF_W, layer),
                  _batch_spec(DIFF_W), _batch_spec(DIFF_W),
                  pl.BlockSpec((4, DIFF_DH), lambda b, q: (0, 0)),
                  pl.BlockSpec((1, LANES), lambda b, q: (0, 0))],
        out_specs=_qtile_spec(DIFF_W),
        compiler_params=_params("parallel", "parallel"),
        name="attn_diff_lat",
    )(p["dq"], cache_k, cache_v, b3(p["dk"]), b3(p["dv"]), lam, g)


def _na_key_start(q):
    return jnp.clip(q * NA_Q_ROWS - NA_WIN_ROWS // 2, 0, GRID_ROWS - NA_KEY_ROWS)


def _attn_na_lat_kernel(q_ref, kc_ref, vc_ref, kn_ref, vn_ref, bias_ref, o_ref):
    start = pl.multiple_of(_na_key_start(pl.program_id(1)) * GRID_W, GRID_W)
    kw = kn_ref[pl.ds(start, NA_KEYS), :]
    vw = vn_ref[pl.ds(start, NA_KEYS), :]
    pair = lambda j: slice(LANES * j, LANES * (j + 1))
    _pair_heads(_na_q(q_ref),
                lambda j: [kc_ref[:, pair(j)].astype(BF16), kw[:, pair(j)]],
                lambda j: [vc_ref[:, pair(j)].astype(BF16), vw[:, pair(j)]],
                lambda h: [None, bias_ref[h]],
                o_ref, NA_HEADS // 2)


def _na_bias_tables():
    n_blocks = DEC_SEQ // ATT_TQ
    qr = np.arange(NA_Q_ROWS)
    kr = np.arange(NA_KEY_ROWS)
    row_sel, row_ok = [], []
    for qb in range(n_blocks):
        ks = int(np.clip(qb * NA_Q_ROWS - NA_WIN_ROWS // 2, 0, GRID_ROWS - NA_KEY_ROWS))
        r = qb * NA_Q_ROWS + qr
        r0 = np.clip(r - NA_WIN_ROWS // 2, 0, GRID_ROWS - NA_WIN_ROWS)
        krow = ks + kr
        ok = (krow[None, :] >= r0[:, None]) & (krow[None, :] < r0[:, None] + NA_WIN_ROWS)
        off = krow[None, :] - r[:, None] + NA_WIN_ROWS - 1
        sel = (off[:, :, None] == np.arange(2 * NA_WIN_ROWS - 1)) & ok[:, :, None]
        row_sel.append(sel.astype(np.float32))
        row_ok.append(ok)
    kinds, kind_of_block = [], []
    for qb in range(n_blocks):
        for n, other in enumerate(kinds):
            if np.array_equal(row_sel[qb], row_sel[other]):
                kind_of_block.append(n)
                break
        else:
            kind_of_block.append(len(kinds))
            kinds.append(qb)
    c = np.arange(GRID_W)
    c0 = np.clip(c - NA_WIN_COLS // 2, 0, GRID_W - NA_WIN_COLS)
    col_ok = (c[None, :] >= c0[:, None]) & (c[None, :] < c0[:, None] + NA_WIN_COLS)
    coff = c[None, :] - c[:, None] + NA_WIN_COLS - 1
    col_sel = ((coff[:, :, None] == np.arange(2 * NA_WIN_COLS - 1)) & col_ok[:, :, None]).astype(np.float32)
    rsel = np.stack([row_sel[qb] for qb in kinds])
    valid = np.stack([row_ok[qb][:, None, :, None] & col_ok[None, :, None, :] for qb in kinds])
    valid = valid.reshape(len(kinds), 1, ATT_TQ, NA_KEYS)
    return rsel, col_sel, valid, kind_of_block


_NA_ROW_SEL, _NA_COL_SEL, _NA_VALID, _NA_KIND_OF_BLOCK = _na_bias_tables()


def _na_bias(rpb):
    hp = lax.Precision.HIGHEST
    cols = jnp.einsum("hij,ckj->hick", rpb, _NA_COL_SEL, precision=hp)
    b = jnp.einsum("nqri,hick->nhqcrk", _NA_ROW_SEL, cols, precision=hp)
    b = b.reshape(_NA_ROW_SEL.shape[0], NA_HEADS, ATT_TQ, NA_KEYS)
    return jnp.where(_NA_VALID, b, NEG_BIG).astype(F32)


def _na_kind(q):
    return (q > 0).astype(jnp.int32) + (q == DEC_SEQ // ATT_TQ - 1).astype(jnp.int32)


def _attn_na_lat(p, cache_k, cache_v, bias, layer):
    assert _NA_KIND_OF_BLOCK == [0] + [1] * (DEC_SEQ // ATT_TQ - 2) + [2]
    t = p["nq"].shape[0]
    b3 = lambda a: a.reshape(DEC_BATCH, DEC_SEQ, a.shape[1])
    return pl.pallas_call(
        _attn_na_lat_kernel,
        out_shape=jax.ShapeDtypeStruct((t, NA_W), BF16),
        grid=(DEC_BATCH, DEC_SEQ // ATT_TQ),
        in_specs=[_qtile_spec(NA_W), _cache_spec(NA_W, layer), _cache_spec(NA_W, layer),
                  _batch_spec(NA_W), _batch_spec(NA_W),
                  pl.BlockSpec((None, NA_HEADS, ATT_TQ, NA_KEYS), lambda b, q: (_na_kind(q), 0, 0, 0))],
        out_specs=_qtile_spec(NA_W),
        compiler_params=_params("parallel", "parallel"),
        name="attn_na_lat",
    )(p["nq"], cache_k, cache_v, b3(p["nk"]), b3(p["nv"]), bias)


def _mla_cache_kernel(ckv_ref, kr_ref, wkvb_ref, rep_ref, kc_ref, vc_ref, krc_ref):
    kv = _dot(ckv_ref[...].astype(BF16), wkvb_ref[...])
    kc_ref[...] = kv[:, :512].astype(BF16)
    vc_ref[...] = kv[:, 512:].astype(BF16)
    krc_ref[...] = _dot(kr_ref[...].astype(BF16), rep_ref[...]).astype(BF16)


def _mla_cache(cache_ckv, cache_kr, wkvb, layer):
    rep = jnp.asarray(np.tile(np.eye(MLA_ROPE, dtype=np.float32), (1, LANES // MLA_ROPE)), BF16)
    spec_in = lambda w: pl.BlockSpec((None, None, PAST_LEN, w), lambda b: (b, layer, 0, 0))
    spec_out = lambda w: pl.BlockSpec((None, PAST_LEN, w), lambda b: (b, 0, 0))
    shp = lambda w: jax.ShapeDtypeStruct((DEC_BATCH, PAST_LEN, w), BF16)
    return pl.pallas_call(
        _mla_cache_kernel,
        out_shape=[shp(512), shp(512), shp(LANES)],
        grid=(DEC_BATCH,),
        in_specs=[spec_in(MLA_KV_LORA), spec_in(MLA_ROPE), _full(wkvb.shape), _full(rep.shape)],
        out_specs=[spec_out(512), spec_out(512), spec_out(LANES)],
        compiler_params=_params("parallel"),
        name="mla_cache",
    )(cache_ckv, cache_kr, wkvb, rep)


def _attn_mla_lat_kernel(qn_ref, qr_ref, kc_ref, krc_ref, vc_ref, kn_ref, krn_ref, vn_ref, o_ref):
    pair = lambda j: slice(LANES * j, LANES * (j + 1))
    krc = krc_ref[...]
    krn = krn_ref[...]
    _pair_heads(_mla_q(qn_ref, qr_ref),
                lambda j: [jnp.concatenate([kc_ref[:, pair(j)], krc], axis=1),
                           jnp.concatenate([kn_ref[:, pair(j)], krn], axis=1)],
                lambda j: [vc_ref[:, pair(j)], vn_ref[:, pair(j)]],
                lambda h: [None, None],
                o_ref, MLA_HEADS // 2)


def _attn_mla_lat(p, kc, vc, krc):
    t = p["mqn"].shape[0]
    b3 = lambda a: a.reshape(DEC_BATCH, DEC_SEQ, a.shape[1])
    cspec = lambda w: pl.BlockSpec((None, PAST_LEN, w), lambda b, q: (b, 0, 0))
    return pl.pallas_call(
        _attn_mla_lat_kernel,
        out_shape=jax.ShapeDtypeStruct((t, MLA_W), BF16),
        grid=(DEC_BATCH, DEC_SEQ // ATT_TQ),
        in_specs=[_qtile_spec(512), _qtile_spec(256), cspec(512), cspec(LANES), cspec(512),
                  _batch_spec(512), _batch_spec(LANES), _batch_spec(512)],
        out_specs=_qtile_spec(MLA_W),
        compiler_params=_params("parallel", "parallel"),
        name="attn_mla_lat",
    )(p["mqn"], p["mqr"], kc, krc, vc, b3(p["kn"]), b3(p["kr4"]), b3(p["vn"]))


def _merge_kernel(x_ref, mod_ref, od_ref, on_ref, om_ref, wg_ref, wbd_ref, wbn_ref, wbm_ref, wo_ref,
                  g_ref, b_ref, x1_ref, h2_ref):
    x = x_ref[...]
    m = mod_ref[0]
    h = (x * (1.0 + m[1:2]) + m[0:1]).astype(BF16)
    gates = _dot(h, wg_ref[...])
    mix = (_sigmoid(gates[:, 0:1024]) * _dot(od_ref[...], wbd_ref[...])
           + _sigmoid(gates[:, 1024:2048]) * _dot(on_ref[...], wbn_ref[...])
           + _sigmoid(gates[:, 2048:3072]) * _dot(om_ref[...], wbm_ref[...]))
    out = _dot(mix.astype(BF16), wo_ref[...])
    x1 = _layer_norm(DN_ALPHA * x + m[2:3] * out, g_ref[...], b_ref[...])
    x1_ref[...] = x1
    h2_ref[...] = _pack_bf16_pairs(x1 * (1.0 + m[4:5]) + m[3:4])


def _merge(x, mod, od, on, om, W, *, latent):
    t = x.shape[0]
    tm = PROJ_TM
    steps_per_batch = (DEC_SEQ if latent else t) // tm
    row = lambda w: pl.BlockSpec((tm, w), lambda i: (i, 0))
    return pl.pallas_call(
        _merge_kernel,
        out_shape=[jax.ShapeDtypeStruct((t, D_MODEL), F32), jax.ShapeDtypeStruct((t, PACKED_W), jnp.int32)],
        grid=(t // tm,),
        in_specs=[row(D_MODEL),
                  pl.BlockSpec((1, 6, D_MODEL), lambda i: (i // steps_per_batch, 0, 0)),
                  row(512), row(512), row(512),
                  _full(W["wg"].shape), _full(W["wbd"].shape), _full(W["wbn"].shape),
                  _full(W["wbm"].shape), _full(W["wo"].shape),
                  _full((1, D_MODEL)), _full((1, D_MODEL))],
        out_specs=[row(D_MODEL), row(PACKED_W)],
        compiler_params=_params("parallel"),
        name="merge",
    )(x, mod, od, on, om, W["wg"], W["wbd"], W["wbn"], W["wbm"], W["wo"], W["ln1_g"], W["ln1_b"])


def _first_index_of_max(vals, idx, sentinel):
    mx = functools.reduce(jnp.maximum, [jnp.max(v, axis=0, keepdims=True) for v in vals])
    cand = [jnp.min(jnp.where(v == mx, i, sentinel), axis=0, keepdims=True) for v, i in zip(vals, idx)]
    return mx, functools.reduce(jnp.minimum, cand)


def _router_kernel(h_ref, wr_ref, bias_ref, tri_ref, eid_ref, rank_ref, wtok_ref, cnt_ref, base_ref):
    tm = h_ref.shape[0]

    @pl.when(pl.program_id(0) == 0)
    def _():
        base_ref[...] = jnp.zeros_like(base_ref)

    logits = _dot_nt(wr_ref[...], _unpack_bf16_pairs(h_ref[...]).astype(BF16))
    scores = _sigmoid(logits)
    biased = scores + bias_ref[...]
    member = lax.broadcasted_iota(jnp.int32, (GROUP_SIZE, tm), 0)
    slabs = [biased[GROUP_SIZE * g:GROUP_SIZE * (g + 1)] for g in range(N_GROUPS)]
    gscore = []
    for s in slabs:
        m1, first = _first_index_of_max([s], [member], GROUP_SIZE)
        m2 = jnp.max(jnp.where(member == first, -jnp.inf, s), axis=0, keepdims=True)
        gscore.append(m1 + m2)
    gs = jnp.concatenate(gscore, axis=0)
    gidx = lax.broadcasted_iota(jnp.int32, (N_GROUPS, tm), 0)
    gsel = jnp.zeros((N_GROUPS, tm), F32)
    for _ in range(TOPK_GROUPS):
        _, first = _first_index_of_max([gs], [gidx], N_GROUPS)
        pick = gidx == first
        gsel = jnp.where(pick, 1.0, gsel)
        gs = jnp.where(pick, -jnp.inf, gs)
    cur = [jnp.where(gsel[g:g + 1] > 0.0, slabs[g], -jnp.inf) for g in range(N_GROUPS)]
    eidx = [member + GROUP_SIZE * g for g in range(N_GROUPS)]
    sel = [jnp.zeros((GROUP_SIZE, tm), F32) for _ in range(N_GROUPS)]
    picks = []
    for _ in range(TOP_K):
        _, first = _first_index_of_max(cur, eidx, N_EXPERTS)
        pick = [eidx[g] == first for g in range(N_GROUPS)]
        picks.append((first, pick))
        for g in range(N_GROUPS):
            sel[g] = jnp.where(pick[g], 1.0, sel[g])
            cur[g] = jnp.where(pick[g], -jnp.inf, cur[g])
    w = [jnp.where(sel[g] > 0.0, scores[GROUP_SIZE * g:GROUP_SIZE * (g + 1)], 0.0) for g in range(N_GROUPS)]
    total = functools.reduce(lambda u, v: u + v, [jnp.sum(x, axis=0, keepdims=True) for x in w])
    w = [x / total * ROUTED_SCALE for x in w]

    sel_all = jnp.concatenate(sel, axis=0)
    incl = _dot(sel_all.astype(BF16), tri_ref[...])
    base = base_ref[:, 0:1]
    rank_all = incl - sel_all + base
    rank = [rank_all[GROUP_SIZE * g:GROUP_SIZE * (g + 1)] for g in range(N_GROUPS)]
    cnt = base + jnp.sum(sel_all, axis=1, keepdims=True)
    base_ref[...] = jnp.broadcast_to(cnt, base_ref.shape)
    cnt_ref[...] = jnp.broadcast_to(cnt, cnt_ref.shape)

    def picked(vals, pick):
        parts = [jnp.sum(jnp.where(p, v, 0.0), axis=0, keepdims=True) for p, v in zip(pick, vals)]
        return functools.reduce(lambda u, v: u + v, parts)

    eid_ref[...] = jnp.concatenate([first for first, _ in picks], axis=0)
    rank_ref[...] = jnp.concatenate([picked(rank, pick) for _, pick in picks], axis=0).astype(jnp.int32)
    w_rows = [picked(w, pick) for _, pick in picks] + [jnp.zeros((LANES - TOP_K, tm), F32)]
    wtok_ref[...] = jnp.concatenate(w_rows, axis=0).T


def _router(h2, wr_t, bias_col):
    t = h2.shape[0]
    tm = PROJ_TM
    tri = jnp.asarray(np.triu(np.ones((tm, tm), np.float32)), BF16)
    slots = lambda dt: jax.ShapeDtypeStruct((TOP_K, t), dt)
    return pl.pallas_call(
        _router_kernel,
        out_shape=[slots(jnp.int32), slots(jnp.int32), jax.ShapeDtypeStruct((t, LANES), F32),
                   jax.ShapeDtypeStruct((N_EXPERTS, LANES), F32)],
        grid=(t // tm,),
        in_specs=[pl.BlockSpec((tm, PACKED_W), lambda i: (i, 0)),
                  _full((N_EXPERTS, D_MODEL)), _full((N_EXPERTS, 1)), _full((tm, tm))],
        out_specs=[pl.BlockSpec((TOP_K, tm), lambda i: (0, i)), pl.BlockSpec((TOP_K, tm), lambda i: (0, i)),
                   pl.BlockSpec((tm, LANES), lambda i: (i, 0)), _full((N_EXPERTS, LANES))],
        scratch_shapes=[pltpu.VMEM((N_EXPERTS, LANES), F32)],
        compiler_params=_params("arbitrary"),
        name="router",
    )(h2, wr_t, bias_col, tri)


def _moe_tile(t):
    mean_rows = t * TOP_K // N_EXPERTS
    return int(min(max(pl.next_power_of_2(mean_rows // 4), MOE_TILE_MIN), MOE_TILE_MAX))


def _moe_tiles(t):
    return (t * TOP_K) // _moe_tile(t) + N_EXPERTS


def _moe_plan(eid, rank, cnt, t):
    tile = _moe_tile(t)
    counts = cnt[:, 0].astype(jnp.int32)
    tiles = jnp.maximum((counts + tile - 1) // tile, 1)
    ends = jnp.cumsum(tiles)
    starts = ends - tiles
    experts = jnp.arange(N_EXPERTS, dtype=jnp.int32)
    pos = rank + jnp.sum(jnp.where(eid[:, :, None] == experts, starts * tile, 0), axis=-1)
    tile_ids = jnp.arange(_moe_tiles(t), dtype=jnp.int32)
    owner = tile_ids[:, None] >= ends[None, :]
    tile_expert = jnp.minimum(jnp.sum(owner, axis=-1), N_EXPERTS - 1)
    is_owner = tile_expert[:, None] == experts[None, :]
    start_of = jnp.sum(jnp.where(is_owner, starts, 0), axis=-1)
    count_of = jnp.sum(jnp.where(is_owner, counts, 0), axis=-1)
    real = jnp.clip(count_of - (tile_ids - start_of) * tile, 0, tile)
    real = jnp.where(tile_ids < ends[-1], real, 0)
    return pos.astype(jnp.int32), tile_expert.astype(jnp.int32), real.astype(jnp.int32)


def _sc_workers():
    info = plsc.get_sparse_core_info()
    return info.num_cores, info.num_subcores


def _sc_mesh():
    return plsc.VectorSubcoreMesh(core_axis_name="core", subcore_axis_name="subcore")


def _sc_worker_id(n_cores):
    return lax.axis_index("subcore") * n_cores + lax.axis_index("core")


def _dispatch(h2, pos):
    t = h2.shape[0]
    n_cores, n_sub = _sc_workers()
    ch = SC_ROWS
    per_worker = t // (n_cores * n_sub)
    n_chunks = per_worker // ch
    pos_chunks = pos.reshape(TOP_K, t // ch, ch).transpose(1, 0, 2)

    @functools.partial(
        pl.kernel, mesh=_sc_mesh(),
        out_type=jax.ShapeDtypeStruct((_moe_tiles(t) * _moe_tile(t), PACKED_W), jnp.int32),
        scratch_types=[pltpu.VMEM((TOP_K, ch), jnp.int32), pltpu.VMEM((ch, PACKED_W), jnp.int32),
                       pltpu.SemaphoreType.DMA],
        name="moe_dispatch",
    )
    def run(h_hbm, pos_hbm, xs_hbm, idx_ref, rows_ref, sem):
        first = _sc_worker_id(n_cores) * n_chunks

        @pl.loop(0, n_chunks)
        def _(j):
            c = first + j
            pltpu.sync_copy(pos_hbm.at[c], idx_ref)
            pltpu.sync_copy(h_hbm.at[pl.ds(pl.multiple_of(c * ch, ch), ch)], rows_ref)
            copies = [pltpu.async_copy(rows_ref, xs_hbm.at[idx_ref.at[k]], sem) for k in range(TOP_K)]
            for cp in copies:
                cp.wait()

    return run(h2, pos_chunks)


def _gather_rows(ys, idx):
    n = idx.shape[0]
    n_cores, n_sub = _sc_workers()
    ch = SC_ROWS
    per_worker = n // (n_cores * n_sub)
    n_chunks = per_worker // ch

    @functools.partial(
        pl.kernel, mesh=_sc_mesh(),
        out_type=jax.ShapeDtypeStruct((n, PACKED_W), jnp.int32),
        scratch_types=[pltpu.VMEM((ch,), jnp.int32), pltpu.VMEM((ch, PACKED_W), jnp.int32),
                       pltpu.SemaphoreType.DMA],
        name="moe_gather",
    )
    def run(ys_hbm, idx_hbm, out_hbm, idx_ref, rows_ref, sem):
        first = _sc_worker_id(n_cores) * per_worker

        @pl.loop(0, n_chunks)
        def _(j):
            off = pl.multiple_of(first + j * ch, ch)
            pltpu.sync_copy(idx_hbm.at[pl.ds(off, ch)], idx_ref)
            pltpu.async_copy(ys_hbm.at[idx_ref], rows_ref, sem).wait()
            pltpu.sync_copy(rows_ref, out_hbm.at[pl.ds(off, ch)])

    return run(ys, idx)


def _ffn_kernel(te_ref, real_ref, xs_ref, weg_ref, weu_ref, wed_ref, ys_ref):
    i = pl.program_id(0)
    real = real_ref[i]

    @pl.when(real > 0)
    def _():
        row = lax.broadcasted_iota(jnp.int32, (xs_ref.shape[0], 1), 0)
        x = _unpack_bf16_pairs(jnp.where(row < real, xs_ref[...], 0)).astype(BF16)
        act = _silu(_dot(x, weg_ref[0])) * _dot(x, weu_ref[0])
        ys_ref[...] = _pack_bf16_pairs(_dot(act.astype(BF16), wed_ref[0]))

    @pl.when(real == 0)
    def _():
        ys_ref[...] = jnp.zeros_like(ys_ref)


def _ffn(xs, tile_expert, tile_real, W):
    tile = xs.shape[0] // tile_expert.shape[0]
    wspec = lambda shape: pl.BlockSpec((1,) + shape, lambda i, te, tr: (te[i], 0, 0))
    return pl.pallas_call(
        _ffn_kernel,
        out_shape=jax.ShapeDtypeStruct(xs.shape, jnp.int32),
        grid_spec=pltpu.PrefetchScalarGridSpec(
            num_scalar_prefetch=2,
            grid=(tile_expert.shape[0],),
            in_specs=[pl.BlockSpec((tile, PACKED_W), lambda i, te, tr: (i, 0)),
                      wspec((D_MODEL, EXPERT_FF)), wspec((D_MODEL, EXPERT_FF)), wspec((EXPERT_FF, D_MODEL))],
            out_specs=pl.BlockSpec((tile, PACKED_W), lambda i, te, tr: (i, 0))),
        compiler_params=_params("parallel"),
        name="moe_ffn",
    )(tile_expert, tile_real, xs, W["weg"], W["weu"], W["wed"])


def _combine_kernel(y_ref, h_ref, wtok_ref, x1_ref, mod_ref, wsg_ref, wsu_ref, wsd_ref,
                    g_ref, b_ref, out_ref):
    wt = wtok_ref[...]
    routed = None
    for k in range(TOP_K):
        part = wt[:, k:k + 1] * _unpack_bf16_pairs(y_ref[k])
        routed = part if routed is None else routed + part
    h = _unpack_bf16_pairs(h_ref[...]).astype(BF16)
    shared = _dot((_silu(_dot(h, wsg_ref[...])) * _dot(h, wsu_ref[...])).astype(BF16), wsd_ref[...])
    m = mod_ref[0]
    y2 = DN_ALPHA * x1_ref[...] + m[5:6] * (routed + shared)
    out_ref[...] = _layer_norm(y2, g_ref[...], b_ref[...])


def _combine(ytok, h2, wtok, x1, mod, W, *, latent):
    t = h2.shape[0]
    ct = COMBINE_TM
    steps_per_batch = (DEC_SEQ if latent else t) // ct
    row = lambda w: pl.BlockSpec((ct, w), lambda i: (i, 0))
    return pl.pallas_call(
        _combine_kernel,
        out_shape=jax.ShapeDtypeStruct((t, D_MODEL), F32),
        grid=(t // ct,),
        in_specs=[pl.BlockSpec((TOP_K, ct, PACKED_W), lambda i: (0, i, 0)),
                  row(PACKED_W), row(LANES), row(D_MODEL),
                  pl.BlockSpec((1, 6, D_MODEL), lambda i: (i // steps_per_batch, 0, 0)),
                  _full((D_MODEL, SHARED_FF)), _full((D_MODEL, SHARED_FF)), _full((SHARED_FF, D_MODEL)),
                  _full((1, D_MODEL)), _full((1, D_MODEL))],
        out_specs=row(D_MODEL),
        compiler_params=_params("parallel"),
        name="moe_combine",
    )(ytok, h2, wtok, x1, mod, W["wsg"], W["wsu"], W["wsd"], W["ln2_g"], W["ln2_b"])


def _moe(h2, x1, mod, W, *, latent):
    t = h2.shape[0]
    eid, rank, wtok, cnt = _router(h2, W["wr_t"], W["rbias"])
    pos, tile_expert, tile_real = _moe_plan(eid, rank, cnt, t)
    xs = _dispatch(h2, pos)
    ys = _ffn(xs, tile_expert, tile_real, W)
    ytok = _gather_rows(ys, pos.reshape(TOP_K * t)).reshape(TOP_K, t, PACKED_W)
    return _combine(ytok, h2, wtok, x1, mod, W, latent=latent)


def _rope_partner(n_blocks, block):
    half = block // 2
    i = np.arange(n_blocks * block)
    return np.where((i % block) < half, i + half, i - half)


_QB_NOPE = np.concatenate([np.arange(MLA_NOPE) + (MLA_NOPE + MLA_ROPE) * h for h in range(MLA_HEADS)])
_QB_ROPE = np.concatenate([np.arange(MLA_ROPE) + (MLA_NOPE + MLA_ROPE) * h + MLA_NOPE for h in range(MLA_HEADS)])
_KVB_NOPE = np.concatenate([np.arange(MLA_NOPE) + (MLA_NOPE + MLA_V) * h for h in range(MLA_HEADS)])
_KVB_V = np.concatenate([np.arange(MLA_V) + (MLA_NOPE + MLA_V) * h + MLA_NOPE for h in range(MLA_HEADS)])
_DIFF_PARTNER = _rope_partner(2 * DIFF_W // 32, 32)
_MLA_PARTNER = _rope_partner(MLA_HEADS * MLA_ROPE // 16, 16)
_KR_PARTNER = _rope_partner(MLA_ROPE // 16, 16)


def _layer_weights(l, w_in, mla_qa_g, mla_wq_b, mla_kva_g, mla_wkv_b, w_branch_diff, w_branch_na,
                   w_branch_mla, w_out, ln1_g, ln1_b, ln2_g, ln2_b, w_router, router_bias,
                   w_exp_gate, w_exp_up, w_exp_down, w_sh_gate, w_sh_up, w_sh_down):
    win = w_in[l].astype(BF16)
    wa = win[:, :3072]
    qa_kva = win[:, 3072:3712]
    kr = win[:, 3712:3744]
    kr4 = jnp.tile(kr, (1, LANES // MLA_ROPE))
    krp4 = jnp.tile(kr[:, _KR_PARTNER], (1, LANES // MLA_ROPE))
    wqb = mla_wq_b[l].astype(BF16)
    wq_rope = wqb[:, _QB_ROPE]
    wkvb = mla_wkv_b[l].astype(BF16)
    row = lambda v: v[l].reshape(1, -1).astype(F32)
    return {
        "wa": wa,
        "wp": wa[:, :2 * DIFF_W][:, _DIFF_PARTNER],
        "wm_ctx": jnp.concatenate([qa_kva, kr4], axis=1),
        "wm_lat": jnp.concatenate([qa_kva, kr4, krp4], axis=1),
        "qag": row(mla_qa_g), "kvag": row(mla_kva_g),
        "wqb": jnp.concatenate([wqb[:, _QB_NOPE], wq_rope], axis=1),
        "wqp": wq_rope[:, _MLA_PARTNER],
        "wkvb": jnp.concatenate([wkvb[:, _KVB_NOPE], wkvb[:, _KVB_V]], axis=1),
        "wg": win[:, 3744:],
        "wbd": w_branch_diff[l].astype(BF16), "wbn": w_branch_na[l].astype(BF16),
        "wbm": w_branch_mla[l].astype(BF16), "wo": w_out[l].astype(BF16),
        "ln1_g": row(ln1_g), "ln1_b": row(ln1_b), "ln2_g": row(ln2_g), "ln2_b": row(ln2_b),
        "wr_t": w_router[l].T.astype(BF16), "rbias": router_bias[l].reshape(N_EXPERTS, 1).astype(F32),
        "weg": w_exp_gate[l].astype(BF16), "weu": w_exp_up[l].astype(BF16), "wed": w_exp_down[l].astype(BF16),
        "wsg": w_sh_gate[l].astype(BF16), "wsu": w_sh_up[l].astype(BF16), "wsd": w_sh_down[l].astype(BF16),
    }


def _rope_tables():
    t = jnp.arange(DEC_SEQ)
    pos = [(t // GRID_W).astype(F32), (t % GRID_W).astype(F32)]

    def table(block):
        half = block // 4
        inv = ROPE_BASE ** (-jnp.arange(half, dtype=F32) / half)
        cos, sin = [], []
        for p in pos:
            ang = p[:, None] * inv[None, :]
            cos += [jnp.cos(ang), jnp.cos(ang)]
            sin += [-jnp.sin(ang), jnp.sin(ang)]
        reps = LANES // block
        return (jnp.tile(jnp.concatenate(cos, axis=1), (1, reps)),
                jnp.tile(jnp.concatenate(sin, axis=1), (1, reps)))

    cd, sd = table(DIFF_DH)
    cm, sm = table(MLA_ROPE)
    return {"cd": cd, "sd": sd, "cm": cm, "sm": sm}


def kernel(x_prompt, x_sample, cache_diff_k, cache_diff_v, cache_na_k, cache_na_v, cache_mla_ckv, cache_mla_krope, c, c_ctx, w_mod, b_mod, w_in, diff_lambda, diff_subln_g, na_rpb, mla_qa_g, mla_wq_b, mla_kva_g, mla_wkv_b, w_branch_diff, w_branch_na, w_branch_mla, w_out, ln1_g, ln1_b, ln2_g, ln2_b, w_router, router_bias, w_exp_gate, w_exp_up, w_exp_down, w_sh_gate, w_sh_up, w_sh_down):
    t_ctx = BATCH * SEQ
    t_lat = DEC_BATCH * DEC_SEQ
    cvec = jnp.concatenate([c, c_ctx[None, :], jnp.zeros((MOD_ROWS - DEC_BATCH - 1, D_MODEL), F32)], axis=0)
    mods = _modulation(cvec, w_mod, b_mod).reshape(DEPTH, MOD_ROWS, 6, D_MODEL)
    rope = _rope_tables()
    ck_d = cache_diff_k.reshape(DEC_BATCH, DEPTH, PAST_LEN, DIFF_W)
    cv_d = cache_diff_v.reshape(DEC_BATCH, DEPTH, PAST_LEN, DIFF_W)
    ck_n = cache_na_k.reshape(DEC_BATCH, DEPTH, PAST_LEN, NA_W)
    cv_n = cache_na_v.reshape(DEC_BATCH, DEPTH, PAST_LEN, NA_W)

    xc = x_prompt.reshape(t_ctx, D_MODEL)
    xl = x_sample.reshape(t_lat, D_MODEL)
    states = []
    for l in range(DEPTH):
        W = _layer_weights(l, w_in, mla_qa_g, mla_wq_b, mla_kva_g, mla_wkv_b, w_branch_diff, w_branch_na,
                           w_branch_mla, w_out, ln1_g, ln1_b, ln2_g, ln2_b, w_router, router_bias,
                           w_exp_gate, w_exp_up, w_exp_down, w_sh_gate, w_sh_up, w_sh_down)
        lam = diff_lambda[l].astype(F32)
        subln = diff_subln_g[l].reshape(1, LANES).astype(F32)
        mod_c = mods[l, DEC_BATCH:DEC_BATCH + 1]
        mod_l = mods[l, :DEC_BATCH]

        Wc = dict(W, wm=W["wm_ctx"])
        names = ["dq", "nq", "mqn", "mqr", "kr4", "kn", "vn", "dk", "dv", "nk", "nv", "ckv", "kr"]
        pc = dict(zip(names, _project(xc, mod_c, Wc, None, latent=False)))
        od, on, om = _attn_ctx(pc, lam, subln, l)
        x1, h2 = _merge(xc, mod_c, od, on, om, W, latent=False)
        xc = _moe(h2, x1, mod_c, W, latent=False)
        states.append(pc)

        Wl = dict(W, wm=W["wm_lat"])
        names = ["dq", "dk", "dv", "nq", "nk", "nv", "mqn", "mqr", "kr4", "kn", "vn"]
        pll = dict(zip(names, _project(xl, mod_l, Wl, rope, latent=True)))
        od = _attn_diff_lat(pll, ck_d, cv_d, lam, subln, l)
        on = _attn_na_lat(pll, ck_n, cv_n, _na_bias(na_rpb[l].astype(F32)), l)
        kc, vc, krc = _mla_cache(cache_mla_ckv, cache_mla_krope, W["wkvb"], l)
        om = _attn_mla_lat(pll, kc, vc, krc)
        x1, h2 = _merge(xl, mod_l, od, on, om, W, latent=True)
        xl = _moe(h2, x1, mod_l, W, latent=True)

    def stack(name, shape):
        return jnp.stack([s[name].reshape((BATCH, SEQ) + shape) for s in states], axis=1)

    return (xc.reshape(BATCH, SEQ, D_MODEL), xl.reshape(DEC_BATCH, DEC_SEQ, D_MODEL),
            stack("dk", (DIFF_HEADS, 2 * DIFF_DH)), stack("dv", (DIFF_HEADS, 2 * DIFF_DH)),
            stack("nk", (NA_HEADS, NA_DH)), stack("nv", (NA_HEADS, NA_DH)),
            stack("ckv", (MLA_KV_LORA,)), stack("kr", (MLA_ROPE,)))
```

```python
import functools

import numpy as np
import jax
import jax.numpy as jnp
from jax import lax
from jax.experimental import pallas as pl
from jax.experimental.pallas import tpu as pltpu
from jax.experimental.pallas import tpu_sc as plsc

D_MODEL = 1024
BATCH = 32
SEQ = 256
DEPTH = 2
DEC_BATCH = 8
DEC_SEQ = 2048
PAST_LEN = 512
GRID_W = 64
GRID_ROWS = DEC_SEQ // GRID_W
ROPE_BASE = 10000.0
DIFF_HEADS = 4
DIFF_DH = 64
DIFF_W = 512
NA_HEADS = 8
NA_DH = 64
NA_W = 512
NA_WIN_ROWS = 8
NA_WIN_COLS = 16
MLA_HEADS = 8
MLA_Q_LORA = 384
MLA_KV_LORA = 256
MLA_NOPE = 64
MLA_ROPE = 32
MLA_V = 64
MLA_W = 512
N_EXPERTS = 64
N_GROUPS = 8
GROUP_SIZE = N_EXPERTS // N_GROUPS
TOPK_GROUPS = 4
TOP_K = 8
EXPERT_FF = 256
SHARED_FF = 256
ROUTED_SCALE = 2.5
DN_ALPHA = (2 * DEPTH) ** 0.25
LN_EPS = 1e-5
RMS_EPS = 1e-6

F32 = jnp.float32
BF16 = jnp.bfloat16

LANES = 128
VMEM_LIMIT_BYTES = 56 * 1024 * 1024
NEG_BIG = -1e30
LOG2E = 1.4426950408889634

DIFF_SCALE = DIFF_DH ** -0.5
NA_SCALE = NA_DH ** -0.5
MLA_SCALE = (MLA_NOPE + MLA_ROPE) ** -0.5

PROJ_TM = 512
ATT_TQ = 256
NA_Q_ROWS = ATT_TQ // GRID_W
NA_KEY_ROWS = NA_Q_ROWS + NA_WIN_ROWS
NA_KEYS = NA_KEY_ROWS * GRID_W
MOE_TILE_MIN = 128
MOE_TILE_MAX = 512
SC_ROWS = 128
COMBINE_TM = 256


def _dot(a, b):
    return jnp.dot(a, b, preferred_element_type=F32)


def _dot_nt(a, b):
    return lax.dot_general(a, b, (((1,), (1,)), ((), ())), preferred_element_type=F32)


def _sigmoid(x):
    return 1.0 / (1.0 + jnp.exp(-x))


def _silu(x):
    return x * _sigmoid(x)


def _params(*sem):
    return pltpu.CompilerParams(dimension_semantics=sem, vmem_limit_bytes=VMEM_LIMIT_BYTES)


def _full(shape):
    n = len(shape)
    return pl.BlockSpec(shape, lambda *_: (0,) * n)


def _layer_norm(y, g, b):
    mu = jnp.mean(y, axis=-1, keepdims=True)
    yc = y - mu
    var = jnp.mean(yc * yc, axis=-1, keepdims=True)
    return yc * lax.rsqrt(var + LN_EPS) * g + b


def _rms(x, g):
    return x * lax.rsqrt(jnp.mean(x * x, axis=-1, keepdims=True) + RMS_EPS) * g


HIGH_HALF = -65536
PACKED_W = D_MODEL // 2


def _pack_bf16_pairs(x):
    n = x.shape[1] // 2
    bits = lax.bitcast_convert_type(x.astype(BF16).astype(F32), jnp.int32)
    return lax.shift_right_logical(bits[:, :n], 16) | (bits[:, n:] & HIGH_HALF)


def _unpack_bf16_pairs(w):
    lo = lax.bitcast_convert_type(lax.shift_left(w, 16), F32)
    hi = lax.bitcast_convert_type(w & HIGH_HALF, F32)
    return jnp.concatenate([lo, hi], axis=1)


MOD_ROWS = 16
MOD_TN = 1536


def _mod_kernel(c_ref, w_ref, b_ref, o_ref):
    s = _silu(c_ref[...]).astype(BF16)
    o_ref[0] = _dot(s, w_ref[0].astype(BF16)) + b_ref[0]


def _modulation(cvec, w_mod, b_mod):
    n = 6 * D_MODEL
    return pl.pallas_call(
        _mod_kernel,
        out_shape=jax.ShapeDtypeStruct((DEPTH, MOD_ROWS, n), F32),
        grid=(DEPTH, n // MOD_TN),
        in_specs=[
            pl.BlockSpec((MOD_ROWS, D_MODEL), lambda l, j: (0, 0)),
            pl.BlockSpec((1, D_MODEL, MOD_TN), lambda l, j: (l, 0, j)),
            pl.BlockSpec((1, 1, MOD_TN), lambda l, j: (l, 0, j)),
        ],
        out_specs=pl.BlockSpec((1, MOD_ROWS, MOD_TN), lambda l, j: (l, 0, j)),
        compiler_params=_params("parallel", "parallel"),
        name="modulation",
    )(cvec, w_mod, b_mod.reshape(DEPTH, 1, n))


def _proj_common(x_ref, mod_ref, wa_ref, wm_ref, qag_ref, kvag_ref, wqb_ref, wkvb_ref):
    m = mod_ref[0]
    h = (x_ref[...] * (1.0 + m[1:2]) + m[0:1]).astype(BF16)
    a = _dot(h, wa_ref[...])
    mm = _dot(h, wm_ref[...])
    qan = _rms(mm[:, :MLA_Q_LORA], qag_ref[...]).astype(BF16)
    mq = _dot(qan, wqb_ref[...])
    ckv = _rms(mm[:, MLA_Q_LORA:MLA_Q_LORA + MLA_KV_LORA], kvag_ref[...])
    kv = _dot(ckv.astype(BF16), wkvb_ref[...])
    return h, a, mm, qan, mq, ckv, kv


def _proj_ctx_kernel(x_ref, mod_ref, wa_ref, wm_ref, qag_ref, kvag_ref, wqb_ref, wkvb_ref,
                     dq_ref, nq_ref, mqn_ref, mqr_ref, kr4_ref, kn_ref, vn_ref,
                     dk_ref, dv_ref, nk_ref, nv_ref, ckv_ref, kr_ref):
    _, a, mm, _, mq, ckv, kv = _proj_common(x_ref, mod_ref, wa_ref, wm_ref, qag_ref, kvag_ref,
                                            wqb_ref, wkvb_ref)
    dq_ref[...] = (a[:, 0:512] * DIFF_SCALE).astype(BF16)
    dk_ref[...] = a[:, 512:1024]
    dv_ref[...] = a[:, 1024:1536]
    nq_ref[...] = (a[:, 1536:2048] * NA_SCALE).astype(BF16)
    nk_ref[...] = a[:, 2048:2560]
    nv_ref[...] = a[:, 2560:3072]
    mqn_ref[...] = (mq[:, :512] * MLA_SCALE).astype(BF16)
    mqr_ref[...] = (mq[:, 512:768] * MLA_SCALE).astype(BF16)
    kr4 = mm[:, 640:768]
    kr4_ref[...] = kr4.astype(BF16)
    kr_ref[...] = kr4[:, :MLA_ROPE]
    ckv_ref[...] = ckv
    kn_ref[...] = kv[:, :512].astype(BF16)
    vn_ref[...] = kv[:, 512:].astype(BF16)


def _proj_lat_kernel(x_ref, mod_ref, wa_ref, wm_ref, qag_ref, kvag_ref, wqb_ref, wkvb_ref,
                     wp_ref, wqp_ref, wvt_ref, wkvvt_ref, cd_ref, sd_ref, cm_ref, sm_ref,
                     dq_ref, dk_ref, dvt_ref, nq_ref, nk_ref, nvt_ref,
                     mqn_ref, mqr_ref, kr4_ref, kn_ref, vnt_ref):
    h, a, mm, qan, mq, ckv, kv = _proj_common(x_ref, mod_ref, wa_ref, wm_ref, qag_ref, kvag_ref,
                                              wqb_ref, wkvb_ref)
    ap = _dot(h, wp_ref[...])
    mqp = _dot(qan, wqp_ref[...])
    cd = cd_ref[...]
    sd = sd_ref[...]
    cm = cm_ref[...]
    sm = sm_ref[...]
    for j in range(DIFF_W // LANES):
        lo, hi = LANES * j, LANES * (j + 1)
        dq_ref[:, lo:hi] = ((a[:, lo:hi] * cd + ap[:, lo:hi] * sd) * (DIFF_SCALE * LOG2E)).astype(BF16)
        dk_ref[:, lo:hi] = (a[:, 512 + lo:512 + hi] * cd + ap[:, 512 + lo:512 + hi] * sd).astype(BF16)
    nq_ref[...] = (a[:, 1024:1536] * (NA_SCALE * LOG2E)).astype(BF16)
    nk_ref[...] = a[:, 1536:2048].astype(BF16)
    vt = _dot_nt(wvt_ref[...], h)
    dvt_ref[...] = vt[:512].astype(BF16)
    nvt_ref[...] = vt[512:].astype(BF16)
    mqn_ref[...] = (mq[:, :512] * (MLA_SCALE * LOG2E)).astype(BF16)
    for j in range(2):
        lo, hi = LANES * j, LANES * (j + 1)
        mqr_ref[:, lo:hi] = ((mq[:, 512 + lo:512 + hi] * cm + mqp[:, lo:hi] * sm)
                             * (MLA_SCALE * LOG2E)).astype(BF16)
    kr4_ref[...] = (mm[:, 640:768] * cm + mm[:, 768:896] * sm).astype(BF16)
    kn_ref[...] = kv.astype(BF16)
    vnt_ref[...] = _dot_nt(wkvvt_ref[...], ckv.astype(BF16)).astype(BF16)


def _project(x, mod, W, rope, *, latent):
    t = x.shape[0]
    tm = PROJ_TM
    tokens_per_batch = DEC_SEQ if latent else t
    steps_per_batch = tokens_per_batch // tm
    row = lambda w: pl.BlockSpec((tm, w), lambda i: (i, 0))
    common_in = [
        row(D_MODEL),
        pl.BlockSpec((1, 6, D_MODEL), lambda i: (i // steps_per_batch, 0, 0)),
        _full(W["wa"].shape), _full(W["wm"].shape), _full((1, MLA_Q_LORA)), _full((1, MLA_KV_LORA)),
        _full(W["wqb"].shape), _full(W["wkvb"].shape),
    ]
    common_args = [x, mod, W["wa"], W["wm"], W["qag"], W["kvag"], W["wqb"], W["wkvb"]]
    bf = lambda w: jax.ShapeDtypeStruct((t, w), BF16)
    f32 = lambda w: jax.ShapeDtypeStruct((t, w), F32)
    if latent:
        tab = pl.BlockSpec((tm, LANES), lambda i: (i % steps_per_batch, 0))
        widths = [512, 512, None, 512, 512, None, 512, 256, 128, 512, None]
        shape = lambda w: jax.ShapeDtypeStruct((512, t), BF16) if w is None else bf(w)
        spec = lambda w: pl.BlockSpec((512, tm), lambda i: (0, i)) if w is None else row(w)
        return pl.pallas_call(
            _proj_lat_kernel,
            out_shape=[shape(w) for w in widths],
            grid=(t // tm,),
            in_specs=common_in + [_full(W["wp"].shape), _full(W["wqp"].shape), _full(W["wvt"].shape),
                                  _full(W["wkvvt"].shape), tab, tab, tab, tab],
            out_specs=[spec(w) for w in widths],
            compiler_params=_params("parallel"),
            name="proj_lat",
        )(*common_args, W["wp"], W["wqp"], W["wvt"], W["wkvvt"],
          rope["cd"], rope["sd"], rope["cm"], rope["sm"])
    bf_w = [512, 512, 512, 256, 128, 512, 512]
    f32_w = [512, 512, 512, 512, 256, 32]
    return pl.pallas_call(
        _proj_ctx_kernel,
        out_shape=[bf(w) for w in bf_w] + [f32(w) for w in f32_w],
        grid=(t // tm,),
        in_specs=common_in,
        out_specs=[row(w) for w in bf_w + f32_w],
        compiler_params=_params("parallel"),
        name="proj_ctx",
    )(*common_args)


def _lane_iota():
    return lax.broadcasted_iota(jnp.int32, (1, LANES), 1)


def _softmax_parts(parts):
    m = functools.reduce(jnp.maximum, [jnp.max(s, axis=-1, keepdims=True) for s in parts])
    es = [jnp.exp(s - m) for s in parts]
    l = functools.reduce(lambda u, v: u + v, [jnp.sum(e, axis=-1, keepdims=True) for e in es])
    return [e / l for e in es]


def _diff_lambda(lam_ref, layer):
    lp = lam_ref[...]
    lam_init = 0.8 - 0.6 * float(np.exp(-0.3 * layer))
    s1 = jnp.sum(lp[0:1] * lp[1:2], axis=-1, keepdims=True)
    s2 = jnp.sum(lp[2:3] * lp[3:4], axis=-1, keepdims=True)
    return jnp.exp(s1) - jnp.exp(s2) + lam_init, lam_init


def _diff_heads(q_ref, ks, vs, lam_ref, g_ref, o_ref, layer):
    lam, lam_init = _diff_lambda(lam_ref, layer)
    first_map = _lane_iota() < DIFF_DH
    g = g_ref[...]
    for h in range(DIFF_HEADS):
        hs = slice(LANES * h, LANES * (h + 1))
        q = q_ref[:, hs]
        q1 = jnp.where(first_map, q, jnp.zeros_like(q))
        q2 = jnp.where(first_map, jnp.zeros_like(q), q)
        kk = [k[:, hs].astype(BF16) for k in ks]
        p1 = _softmax_parts([_dot_nt(q1, k) for k in kk])
        p2 = _softmax_parts([_dot_nt(q2, k) for k in kk])
        o = None
        for a1, a2, v in zip(p1, p2, vs):
            part = _dot((a1 - lam * a2).astype(BF16), v[:, hs].astype(BF16))
            o = part if o is None else o + part
        o = _rms(o, g) * (1.0 - lam_init)
        o_ref[:, hs] = o.astype(BF16)


def _pair_heads(q_of, k_of, v_of, bias_of, o_ref, n_pairs):
    first = _lane_iota() < 64
    for j in range(n_pairs):
        ps = slice(LANES * j, LANES * (j + 1))
        ks = k_of(j)
        vs = v_of(j)
        outs = []
        for hh in range(2):
            q = q_of(j, hh)
            ss = [_dot_nt(q, k) for k in ks]
            bs = bias_of(2 * j + hh)
            ss = [s if b is None else s + b for s, b in zip(ss, bs)]
            ps_ = _softmax_parts(ss)
            o = None
            for p, v in zip(ps_, vs):
                part = _dot(p.astype(BF16), v)
                o = part if o is None else o + part
            outs.append(o)
        o_ref[:, ps] = jnp.where(first, outs[0], outs[1]).astype(BF16)


def _na_q(q_ref):
    first = _lane_iota() < NA_DH

    def q_of(j, hh):
        q = q_ref[:, LANES * j:LANES * (j + 1)]
        keep = first if hh == 0 else jnp.logical_not(first)
        return jnp.where(keep, q, jnp.zeros_like(q))
    return q_of


def _mla_q(qn_ref, qr_ref):
    lane = _lane_iota()
    first = lane < MLA_NOPE

    def q_of(j, hh):
        h = 2 * j + hh
        qn = qn_ref[:, LANES * j:LANES * (j + 1)]
        keep = first if hh == 0 else jnp.logical_not(first)
        qn = jnp.where(keep, qn, jnp.zeros_like(qn))
        qr = qr_ref[:, LANES * (h // 4):LANES * (h // 4 + 1)]
        qr = jnp.where((lane // MLA_ROPE) == (h % 4), qr, jnp.zeros_like(qr))
        return jnp.concatenate([qn, qr], axis=1)
    return q_of


def _attn_ctx_kernel(layer, dq_ref, dk_ref, dv_ref, nq_ref, nk_ref, nv_ref,
                     mqn_ref, mqr_ref, kn_ref, kr4_ref, vn_ref, lam_ref, g_ref,
                     od_ref, on_ref, om_ref):
    _diff_heads(dq_ref, [dk_ref], [dv_ref], lam_ref, g_ref, od_ref, layer)
    none = lambda h: [None]
    pair = lambda j: slice(LANES * j, LANES * (j + 1))
    _pair_heads(_na_q(nq_ref),
                lambda j: [nk_ref[:, pair(j)].astype(BF16)],
                lambda j: [nv_ref[:, pair(j)].astype(BF16)],
                none, on_ref, NA_HEADS // 2)
    kr4 = kr4_ref[...]
    _pair_heads(_mla_q(mqn_ref, mqr_ref),
                lambda j: [jnp.concatenate([kn_ref[:, pair(j)], kr4], axis=1)],
                lambda j: [vn_ref[:, pair(j)]],
                none, om_ref, MLA_HEADS // 2)


def _attn_ctx(p, lam, g, layer):
    t = p["dq"].shape[0]
    row = lambda w: pl.BlockSpec((SEQ, w), lambda b: (b, 0))
    names = ["dq", "dk", "dv", "nq", "nk", "nv", "mqn", "mqr", "kn", "kr4", "vn"]
    out = jax.ShapeDtypeStruct((t, 512), BF16)
    return pl.pallas_call(
        functools.partial(_attn_ctx_kernel, layer),
        out_shape=[out, out, out],
        grid=(t // SEQ,),
        in_specs=[row(p[n].shape[1]) for n in names] + [_full((4, DIFF_DH)), _full((1, LANES))],
        out_specs=[row(512)] * 3,
        compiler_params=_params("parallel"),
        name="attn_ctx",
    )(*[p[n] for n in names], lam, g)


def _cache_spec(width):
    return pl.BlockSpec((None, PAST_LEN, width), lambda b, q: (b, 0, 0))


def _batch_spec(width):
    return pl.BlockSpec((None, DEC_SEQ, width), lambda b, q: (b, 0, 0))


def _batch_t_spec():
    return pl.BlockSpec((512, DEC_SEQ), lambda b, q: (0, b))


def _qtile_spec(width):
    steps = DEC_SEQ // ATT_TQ
    return pl.BlockSpec((ATT_TQ, width), lambda b, q: (b * steps + q, 0))


def _softmax_cols(parts):
    m = functools.reduce(jnp.maximum, [jnp.max(s, axis=0, keepdims=True) for s in parts])
    es = [jnp.exp2(s - m) for s in parts]
    l = functools.reduce(lambda u, v: u + v, [jnp.sum(e, axis=0, keepdims=True) for e in es])
    return es, l


def _attn_diff_lat_kernel(layer, q_ref, kc_ref, vct_ref, kn_ref, vnt_ref, lam_ref, g_ref, o_ref):
    lam, lam_init = _diff_lambda(lam_ref, layer)
    first_map = _lane_iota() < DIFF_DH
    g = g_ref[...]
    for h in range(DIFF_HEADS):
        hs = slice(LANES * h, LANES * (h + 1))
        q = q_ref[:, hs]
        q1 = jnp.where(first_map, q, jnp.zeros_like(q))
        q2 = jnp.where(first_map, jnp.zeros_like(q), q)
        ks = [kc_ref[:, hs], kn_ref[:, hs]]
        e1, l1 = _softmax_cols([_dot_nt(k, q1) for k in ks])
        e2, l2 = _softmax_cols([_dot_nt(k, q2) for k in ks])
        c1 = 1.0 / l1
        c2 = lam / l2
        ot = None
        for a1, a2, vt in zip(e1, e2, [vct_ref[hs, :], vnt_ref[hs, :]]):
            part = _dot(vt, (a1 * c1 - a2 * c2).astype(BF16))
            ot = part if ot is None else ot + part
        ot = ot * lax.rsqrt(jnp.mean(ot * ot, axis=0, keepdims=True) + RMS_EPS) * g * (1.0 - lam_init)
        o_ref[:, hs] = ot.T.astype(BF16)


def _attn_diff_lat(p, cache, lam, g_col, layer):
    t = p["dq"].shape[0]
    b3 = lambda a: a.reshape(DEC_BATCH, DEC_SEQ, a.shape[1])
    return pl.pallas_call(
        functools.partial(_attn_diff_lat_kernel, layer),
        out_shape=jax.ShapeDtypeStruct((t, DIFF_W), BF16),
        grid=(DEC_BATCH, DEC_SEQ // ATT_TQ),
        in_specs=[_qtile_spec(DIFF_W), _cache_spec(DIFF_W), _cache_spec(PAST_LEN),
                  _batch_spec(DIFF_W), _batch_t_spec(),
                  pl.BlockSpec((4, DIFF_DH), lambda b, q: (0, 0)),
                  pl.BlockSpec((LANES, 1), lambda b, q: (0, 0))],
        out_specs=_qtile_spec(DIFF_W),
        compiler_params=_params("parallel", "parallel"),
        name="attn_diff_lat",
    )(p["dq"], cache["dk"], cache["dvt"], b3(p["dk"]), p["dvt"], lam, g_col)


def _na_key_start(q):
    return jnp.clip(q * NA_Q_ROWS - NA_WIN_ROWS // 2, 0, GRID_ROWS - NA_KEY_ROWS)


def _pair_heads_cols(q_of, k_of, vt_of, bias_of, o_ref, n_pairs):
    first = lax.broadcasted_iota(jnp.int32, (LANES, 1), 0) < 64
    for j in range(n_pairs):
        ks = k_of(j)
        vts = vt_of(j)
        outs = []
        for hh in range(2):
            q = q_of(j, hh)
            ss = [_dot_nt(k, q) for k in ks]
            ss = [s if b is None else s + b for s, b in zip(ss, bias_of(2 * j + hh))]
            es, l = _softmax_cols(ss)
            ot = None
            for e, vt in zip(es, vts):
                part = _dot(vt, e.astype(BF16))
                ot = part if ot is None else ot + part
            outs.append(ot / l)
        o_ref[:, LANES * j:LANES * (j + 1)] = jnp.where(first, outs[0], outs[1]).T.astype(BF16)


def _attn_na_lat_kernel(q_ref, kc_ref, vct_ref, kn_ref, vnt_ref, bias_ref, o_ref):
    start = pl.multiple_of(_na_key_start(pl.program_id(1)) * GRID_W, NA_Q_ROWS * GRID_W)
    kw = kn_ref[pl.ds(start, NA_KEYS), :]
    vwt = vnt_ref[:, pl.ds(start, NA_KEYS)]
    pair = lambda j: slice(LANES * j, LANES * (j + 1))
    _pair_heads_cols(_na_q(q_ref),
                     lambda j: [kc_ref[:, pair(j)], kw[:, pair(j)]],
                     lambda j: [vct_ref[pair(j), :], vwt[pair(j), :]],
                     lambda h: [None, bias_ref[h]],
                     o_ref, NA_HEADS // 2)


def _na_bias_tables():
    n_blocks = DEC_SEQ // ATT_TQ
    qr = np.arange(NA_Q_ROWS)
    kr = np.arange(NA_KEY_ROWS)
    row_sel, row_ok = [], []
    for qb in range(n_blocks):
        ks = int(np.clip(qb * NA_Q_ROWS - NA_WIN_ROWS // 2, 0, GRID_ROWS - NA_KEY_ROWS))
        r = qb * NA_Q_ROWS + qr
        r0 = np.clip(r - NA_WIN_ROWS // 2, 0, GRID_ROWS - NA_WIN_ROWS)
        krow = ks + kr
        ok = (krow[None, :] >= r0[:, None]) & (krow[None, :] < r0[:, None] + NA_WIN_ROWS)
        off = krow[None, :] - r[:, None] + NA_WIN_ROWS - 1
        sel = (off[:, :, None] == np.arange(2 * NA_WIN_ROWS - 1)) & ok[:, :, None]
        row_sel.append(sel.astype(np.float32))
        row_ok.append(ok)
    kinds, kind_of_block = [], []
    for qb in range(n_blocks):
        for n, other in enumerate(kinds):
            if np.array_equal(row_sel[qb], row_sel[other]):
                kind_of_block.append(n)
                break
        else:
            kind_of_block.append(len(kinds))
            kinds.append(qb)
    c = np.arange(GRID_W)
    c0 = np.clip(c - NA_WIN_COLS // 2, 0, GRID_W - NA_WIN_COLS)
    col_ok = (c[None, :] >= c0[:, None]) & (c[None, :] < c0[:, None] + NA_WIN_COLS)
    coff = c[None, :] - c[:, None] + NA_WIN_COLS - 1
    col_sel = ((coff[:, :, None] == np.arange(2 * NA_WIN_COLS - 1)) & col_ok[:, :, None]).astype(np.float32)
    rsel = np.stack([row_sel[qb] for qb in kinds])
    valid = np.stack([row_ok[qb].T[:, None, :, None] & col_ok.T[None, :, None, :] for qb in kinds])
    valid = valid.reshape(len(kinds), 1, NA_KEYS, ATT_TQ)
    return rsel, col_sel, valid, kind_of_block


_NA_ROW_SEL, _NA_COL_SEL, _NA_VALID, _NA_KIND_OF_BLOCK = _na_bias_tables()


def _na_bias(rpb):
    hp = lax.Precision.HIGHEST
    cols = jnp.einsum("hij,ckj->hick", rpb, _NA_COL_SEL, precision=hp)
    b = jnp.einsum("nqri,hick->nhrkqc", _NA_ROW_SEL, cols, precision=hp)
    b = b.reshape(_NA_ROW_SEL.shape[0], NA_HEADS, NA_KEYS, ATT_TQ)
    return jnp.where(_NA_VALID, b * LOG2E, NEG_BIG).astype(F32)


def _na_kind(q):
    return (q > 0).astype(jnp.int32) + (q == DEC_SEQ // ATT_TQ - 1).astype(jnp.int32)


def _attn_na_lat(p, cache, bias):
    assert _NA_KIND_OF_BLOCK == [0] + [1] * (DEC_SEQ // ATT_TQ - 2) + [2]
    t = p["nq"].shape[0]
    b3 = lambda a: a.reshape(DEC_BATCH, DEC_SEQ, a.shape[1])
    return pl.pallas_call(
        _attn_na_lat_kernel,
        out_shape=jax.ShapeDtypeStruct((t, NA_W), BF16),
        grid=(DEC_BATCH, DEC_SEQ // ATT_TQ),
        in_specs=[_qtile_spec(NA_W), _cache_spec(NA_W), _cache_spec(PAST_LEN),
                  _batch_spec(NA_W), _batch_t_spec(),
                  pl.BlockSpec((None, NA_HEADS, NA_KEYS, ATT_TQ), lambda b, q: (_na_kind(q), 0, 0, 0))],
        out_specs=_qtile_spec(NA_W),
        compiler_params=_params("parallel", "parallel"),
        name="attn_na_lat",
    )(p["nq"], cache["nk"], cache["nvt"], b3(p["nk"]), p["nvt"], bias)


def _cache_prep_kernel(dk_ref, dv_ref, nk_ref, nv_ref, ckv_ref, kr_ref, wkvb_ref, wkvvt_ref, rep_ref,
                       dko_ref, dvt_ref, nko_ref, nvt_ref, kc_ref, vct_ref, krc_ref):
    dko_ref[...] = dk_ref[...].astype(BF16)
    dvt_ref[...] = dv_ref[...].T.astype(BF16)
    nko_ref[...] = nk_ref[...].astype(BF16)
    nvt_ref[...] = nv_ref[...].T.astype(BF16)
    ckv = ckv_ref[...].astype(BF16)
    kc_ref[...] = _dot(ckv, wkvb_ref[...]).astype(BF16)
    vct_ref[...] = _dot_nt(wkvvt_ref[...], ckv).astype(BF16)
    krc_ref[...] = _dot(kr_ref[...].astype(BF16), rep_ref[...]).astype(BF16)


def _cache_prep(caches, wkvb_nope, wkvvt, layer):
    rep = jnp.asarray(np.tile(np.eye(MLA_ROPE, dtype=np.float32), (1, LANES // MLA_ROPE)), BF16)
    spec_in = lambda w: pl.BlockSpec((None, None, PAST_LEN, w), lambda b: (b, layer, 0, 0))
    spec_out = lambda w: pl.BlockSpec((None, PAST_LEN, w), lambda b: (b, 0, 0))
    shp = lambda w: jax.ShapeDtypeStruct((DEC_BATCH, PAST_LEN, w), BF16)
    in_w = [DIFF_W, DIFF_W, NA_W, NA_W, MLA_KV_LORA, MLA_ROPE]
    out_w = [DIFF_W, PAST_LEN, NA_W, PAST_LEN, 512, PAST_LEN, LANES]
    outs = pl.pallas_call(
        _cache_prep_kernel,
        out_shape=[shp(w) for w in out_w],
        grid=(DEC_BATCH,),
        in_specs=[spec_in(w) for w in in_w] + [_full(wkvb_nope.shape), _full(wkvvt.shape), _full(rep.shape)],
        out_specs=[spec_out(w) for w in out_w],
        compiler_params=_params("parallel"),
        name="cache_prep",
    )(*caches, wkvb_nope, wkvvt, rep)
    return dict(zip(["dk", "dvt", "nk", "nvt", "kc", "vct", "krc"], outs))


def _attn_mla_lat_kernel(qn_ref, qr_ref, kc_ref, krc_ref, vct_ref, kn_ref, krn_ref, vnt_ref, o_ref):
    pair = lambda j: slice(LANES * j, LANES * (j + 1))
    krc = krc_ref[...]
    krn = krn_ref[...]
    _pair_heads_cols(_mla_q(qn_ref, qr_ref),
                     lambda j: [jnp.concatenate([kc_ref[:, pair(j)], krc], axis=1),
                                jnp.concatenate([kn_ref[:, pair(j)], krn], axis=1)],
                     lambda j: [vct_ref[pair(j), :], vnt_ref[pair(j), :]],
                     lambda h: [None, None],
                     o_ref, MLA_HEADS // 2)


def _attn_mla_lat(p, cache):
    t = p["mqn"].shape[0]
    b3 = lambda a: a.reshape(DEC_BATCH, DEC_SEQ, a.shape[1])
    return pl.pallas_call(
        _attn_mla_lat_kernel,
        out_shape=jax.ShapeDtypeStruct((t, MLA_W), BF16),
        grid=(DEC_BATCH, DEC_SEQ // ATT_TQ),
        in_specs=[_qtile_spec(512), _qtile_spec(256), _cache_spec(512), _cache_spec(LANES),
                  _cache_spec(PAST_LEN), _batch_spec(512), _batch_spec(LANES), _batch_t_spec()],
        out_specs=_qtile_spec(MLA_W),
        compiler_params=_params("parallel", "parallel"),
        name="attn_mla_lat",
    )(p["mqn"], p["mqr"], cache["kc"], cache["krc"], cache["vct"],
      b3(p["kn"]), b3(p["kr4"]), p["vnt"])


def _merge_kernel(x_ref, mod_ref, od_ref, on_ref, om_ref, wg_ref, wbd_ref, wbn_ref, wbm_ref, wo_ref,
                  g_ref, b_ref, x1_ref, h2_ref):
    x = x_ref[...]
    m = mod_ref[0]
    h = (x * (1.0 + m[1:2]) + m[0:1]).astype(BF16)
    gates = _dot(h, wg_ref[...])
    mix = (_sigmoid(gates[:, 0:1024]) * _dot(od_ref[...], wbd_ref[...])
           + _sigmoid(gates[:, 1024:2048]) * _dot(on_ref[...], wbn_ref[...])
           + _sigmoid(gates[:, 2048:3072]) * _dot(om_ref[...], wbm_ref[...]))
    out = _dot(mix.astype(BF16), wo_ref[...])
    x1 = _layer_norm(DN_ALPHA * x + m[2:3] * out, g_ref[...], b_ref[...])
    x1_ref[...] = x1
    h2_ref[...] = _pack_bf16_pairs(x1 * (1.0 + m[4:5]) + m[3:4])


def _merge(x, mod, od, on, om, W, *, latent):
    t = x.shape[0]
    tm = PROJ_TM
    steps_per_batch = (DEC_SEQ if latent else t) // tm
    row = lambda w: pl.BlockSpec((tm, w), lambda i: (i, 0))
    return pl.pallas_call(
        _merge_kernel,
        out_shape=[jax.ShapeDtypeStruct((t, D_MODEL), F32), jax.ShapeDtypeStruct((t, PACKED_W), jnp.int32)],
        grid=(t // tm,),
        in_specs=[row(D_MODEL),
                  pl.BlockSpec((1, 6, D_MODEL), lambda i: (i // steps_per_batch, 0, 0)),
                  row(512), row(512), row(512),
                  _full(W["wg"].shape), _full(W["wbd"].shape), _full(W["wbn"].shape),
                  _full(W["wbm"].shape), _full(W["wo"].shape),
                  _full((1, D_MODEL)), _full((1, D_MODEL))],
        out_specs=[row(D_MODEL), row(PACKED_W)],
        compiler_params=_params("parallel"),
        name="merge",
    )(x, mod, od, on, om, W["wg"], W["wbd"], W["wbn"], W["wbm"], W["wo"], W["ln1_g"], W["ln1_b"])


def _first_index_of_max(vals, idx, sentinel):
    mx = functools.reduce(jnp.maximum, [jnp.max(v, axis=0, keepdims=True) for v in vals])
    cand = [jnp.min(jnp.where(v == mx, i, sentinel), axis=0, keepdims=True) for v, i in zip(vals, idx)]
    return mx, functools.reduce(jnp.minimum, cand)


def _router_kernel(h_ref, wr_ref, bias_ref, tri_ref, eid_ref, rank_ref, wtok_ref, cnt_ref, base_ref):
    tm = h_ref.shape[0]

    @pl.when(pl.program_id(0) == 0)
    def _():
        base_ref[...] = jnp.zeros_like(base_ref)

    logits = _dot_nt(wr_ref[...], _unpack_bf16_pairs(h_ref[...]).astype(BF16))
    scores = _sigmoid(logits)
    biased = scores + bias_ref[...]
    member = lax.broadcasted_iota(jnp.int32, (GROUP_SIZE, tm), 0)
    slabs = [biased[GROUP_SIZE * g:GROUP_SIZE * (g + 1)] for g in range(N_GROUPS)]
    gscore = []
    for s in slabs:
        m1, first = _first_index_of_max([s], [member], GROUP_SIZE)
        m2 = jnp.max(jnp.where(member == first, -jnp.inf, s), axis=0, keepdims=True)
        gscore.append(m1 + m2)
    gs = jnp.concatenate(gscore, axis=0)
    gidx = lax.broadcasted_iota(jnp.int32, (N_GROUPS, tm), 0)
    gsel = jnp.zeros((N_GROUPS, tm), F32)
    for _ in range(TOPK_GROUPS):
        _, first = _first_index_of_max([gs], [gidx], N_GROUPS)
        pick = gidx == first
        gsel = jnp.where(pick, 1.0, gsel)
        gs = jnp.where(pick, -jnp.inf, gs)
    cur = [jnp.where(gsel[g:g + 1] > 0.0, slabs[g], -jnp.inf) for g in range(N_GROUPS)]
    eidx = [member + GROUP_SIZE * g for g in range(N_GROUPS)]
    sel = [jnp.zeros((GROUP_SIZE, tm), F32) for _ in range(N_GROUPS)]
    picks = []
    for _ in range(TOP_K):
        _, first = _first_index_of_max(cur, eidx, N_EXPERTS)
        pick = [eidx[g] == first for g in range(N_GROUPS)]
        picks.append((first, pick))
        for g in range(N_GROUPS):
            sel[g] = jnp.where(pick[g], 1.0, sel[g])
            cur[g] = jnp.where(pick[g], -jnp.inf, cur[g])
    w = [jnp.where(sel[g] > 0.0, scores[GROUP_SIZE * g:GROUP_SIZE * (g + 1)], 0.0) for g in range(N_GROUPS)]
    total = functools.reduce(lambda u, v: u + v, [jnp.sum(x, axis=0, keepdims=True) for x in w])
    w = [x / total * ROUTED_SCALE for x in w]

    sel_all = jnp.concatenate(sel, axis=0)
    incl = _dot(sel_all.astype(BF16), tri_ref[...])
    base = base_ref[:, 0:1]
    rank_all = incl - sel_all + base
    rank = [rank_all[GROUP_SIZE * g:GROUP_SIZE * (g + 1)] for g in range(N_GROUPS)]
    cnt = base + jnp.sum(sel_all, axis=1, keepdims=True)
    base_ref[...] = jnp.broadcast_to(cnt, base_ref.shape)
    cnt_ref[...] = jnp.broadcast_to(cnt, cnt_ref.shape)

    def picked(vals, pick):
        parts = [jnp.sum(jnp.where(p, v, 0.0), axis=0, keepdims=True) for p, v in zip(pick, vals)]
        return functools.reduce(lambda u, v: u + v, parts)

    eid_ref[...] = jnp.concatenate([first for first, _ in picks], axis=0)
    rank_ref[...] = jnp.concatenate([picked(rank, pick) for _, pick in picks], axis=0).astype(jnp.int32)
    w_rows = [picked(w, pick) for _, pick in picks] + [jnp.zeros((LANES - TOP_K, tm), F32)]
    wtok_ref[...] = jnp.concatenate(w_rows, axis=0).T


def _router(h2, wr_t, bias_col):
    t = h2.shape[0]
    tm = PROJ_TM
    tri = jnp.asarray(np.triu(np.ones((tm, tm), np.float32)), BF16)
    slots = lambda dt: jax.ShapeDtypeStruct((TOP_K, t), dt)
    return pl.pallas_call(
        _router_kernel,
        out_shape=[slots(jnp.int32), slots(jnp.int32), jax.ShapeDtypeStruct((t, LANES), F32),
                   jax.ShapeDtypeStruct((N_EXPERTS, LANES), F32)],
        grid=(t // tm,),
        in_specs=[pl.BlockSpec((tm, PACKED_W), lambda i: (i, 0)),
                  _full((N_EXPERTS, D_MODEL)), _full((N_EXPERTS, 1)), _full((tm, tm))],
        out_specs=[pl.BlockSpec((TOP_K, tm), lambda i: (0, i)), pl.BlockSpec((TOP_K, tm), lambda i: (0, i)),
                   pl.BlockSpec((tm, LANES), lambda i: (i, 0)), _full((N_EXPERTS, LANES))],
        scratch_shapes=[pltpu.VMEM((N_EXPERTS, LANES), F32)],
        compiler_params=_params("arbitrary"),
        name="router",
    )(h2, wr_t, bias_col, tri)


def _moe_tile(t):
    mean_rows = t * TOP_K // N_EXPERTS
    return int(min(max(pl.next_power_of_2(mean_rows // 4), MOE_TILE_MIN), MOE_TILE_MAX))


def _moe_tiles(t):
    return (t * TOP_K) // _moe_tile(t) + N_EXPERTS


def _moe_plan(eid, rank, cnt, t):
    tile = _moe_tile(t)
    counts = cnt[:, 0].astype(jnp.int32)
    tiles = jnp.maximum((counts + tile - 1) // tile, 1)
    ends = jnp.cumsum(tiles)
    starts = ends - tiles
    experts = jnp.arange(N_EXPERTS, dtype=jnp.int32)
    pos = rank + jnp.sum(jnp.where(eid[:, :, None] == experts, starts * tile, 0), axis=-1)
    tile_ids = jnp.arange(_moe_tiles(t), dtype=jnp.int32)
    owner = tile_ids[:, None] >= ends[None, :]
    tile_expert = jnp.minimum(jnp.sum(owner, axis=-1), N_EXPERTS - 1)
    is_owner = tile_expert[:, None] == experts[None, :]
    start_of = jnp.sum(jnp.where(is_owner, starts, 0), axis=-1)
    count_of = jnp.sum(jnp.where(is_owner, counts, 0), axis=-1)
    real = jnp.clip(count_of - (tile_ids - start_of) * tile, 0, tile)
    real = jnp.where(tile_ids < ends[-1], real, 0)
    return pos.astype(jnp.int32), tile_expert.astype(jnp.int32), real.astype(jnp.int32)


def _sc_workers():
    info = plsc.get_sparse_core_info()
    return info.num_cores, info.num_subcores


def _sc_mesh():
    return plsc.VectorSubcoreMesh(core_axis_name="core", subcore_axis_name="subcore")


def _sc_worker_id(n_cores):
    return lax.axis_index("subcore") * n_cores + lax.axis_index("core")


def _dispatch(h2, pos):
    t = h2.shape[0]
    n_cores, n_sub = _sc_workers()
    ch = SC_ROWS
    per_worker = t // (n_cores * n_sub)
    n_chunks = per_worker // ch
    pos_chunks = pos.reshape(TOP_K, t // ch, ch).transpose(1, 0, 2)

    @functools.partial(
        pl.kernel, mesh=_sc_mesh(),
        out_type=jax.ShapeDtypeStruct((_moe_tiles(t) * _moe_tile(t), PACKED_W), jnp.int32),
        scratch_types=[pltpu.VMEM((TOP_K, ch), jnp.int32), pltpu.VMEM((ch, PACKED_W), jnp.int32),
                       pltpu.SemaphoreType.DMA],
        name="moe_dispatch",
    )
    def run(h_hbm, pos_hbm, xs_hbm, idx_ref, rows_ref, sem):
        first = _sc_worker_id(n_cores) * n_chunks

        @pl.loop(0, n_chunks)
        def _(j):
            c = first + j
            pltpu.sync_copy(pos_hbm.at[c], idx_ref)
            pltpu.sync_copy(h_hbm.at[pl.ds(pl.multiple_of(c * ch, ch), ch)], rows_ref)
            copies = [pltpu.async_copy(rows_ref, xs_hbm.at[idx_ref.at[k]], sem) for k in range(TOP_K)]
            for cp in copies:
                cp.wait()

    return run(h2, pos_chunks)


def _gather_rows(ys, idx):
    n = idx.shape[0]
    n_cores, n_sub = _sc_workers()
    ch = SC_ROWS
    per_worker = n // (n_cores * n_sub)
    n_chunks = per_worker // ch

    @functools.partial(
        pl.kernel, mesh=_sc_mesh(),
        out_type=jax.ShapeDtypeStruct((n, PACKED_W), jnp.int32),
        scratch_types=[pltpu.VMEM((ch,), jnp.int32), pltpu.VMEM((ch, PACKED_W), jnp.int32),
                       pltpu.SemaphoreType.DMA],
        name="moe_gather",
    )
    def run(ys_hbm, idx_hbm, out_hbm, idx_ref, rows_ref, sem):
        first = _sc_worker_id(n_cores) * per_worker

        @pl.loop(0, n_chunks)
        def _(j):
            off = pl.multiple_of(first + j * ch, ch)
            pltpu.sync_copy(idx_hbm.at[pl.ds(off, ch)], idx_ref)
            pltpu.async_copy(ys_hbm.at[idx_ref], rows_ref, sem).wait()
            pltpu.sync_copy(rows_ref, out_hbm.at[pl.ds(off, ch)])

    return run(ys, idx)


def _ffn_kernel(te_ref, real_ref, xs_ref, weg_ref, weu_ref, wed_ref, ys_ref):
    i = pl.program_id(0)
    real = real_ref[i]

    @pl.when(real > 0)
    def _():
        row = lax.broadcasted_iota(jnp.int32, (xs_ref.shape[0], 1), 0)
        x = _unpack_bf16_pairs(jnp.where(row < real, xs_ref[...], 0)).astype(BF16)
        act = _silu(_dot(x, weg_ref[0])) * _dot(x, weu_ref[0])
        ys_ref[...] = _pack_bf16_pairs(_dot(act.astype(BF16), wed_ref[0]))

    @pl.when(real == 0)
    def _():
        ys_ref[...] = jnp.zeros_like(ys_ref)


def _ffn(xs, tile_expert, tile_real, W):
    tile = xs.shape[0] // tile_expert.shape[0]
    wspec = lambda shape: pl.BlockSpec((1,) + shape, lambda i, te, tr: (te[i], 0, 0))
    return pl.pallas_call(
        _ffn_kernel,
        out_shape=jax.ShapeDtypeStruct(xs.shape, jnp.int32),
        grid_spec=pltpu.PrefetchScalarGridSpec(
            num_scalar_prefetch=2,
            grid=(tile_expert.shape[0],),
            in_specs=[pl.BlockSpec((tile, PACKED_W), lambda i, te, tr: (i, 0)),
                      wspec((D_MODEL, EXPERT_FF)), wspec((D_MODEL, EXPERT_FF)), wspec((EXPERT_FF, D_MODEL))],
            out_specs=pl.BlockSpec((tile, PACKED_W), lambda i, te, tr: (i, 0))),
        compiler_params=_params("parallel"),
        name="moe_ffn",
    )(tile_expert, tile_real, xs, W["weg"], W["weu"], W["wed"])


def _combine_kernel(y_ref, h_ref, wtok_ref, x1_ref, mod_ref, wsg_ref, wsu_ref, wsd_ref,
                    g_ref, b_ref, out_ref):
    wt = wtok_ref[...]
    routed = None
    for k in range(TOP_K):
        part = wt[:, k:k + 1] * _unpack_bf16_pairs(y_ref[k])
        routed = part if routed is None else routed + part
    h = _unpack_bf16_pairs(h_ref[...]).astype(BF16)
    shared = _dot((_silu(_dot(h, wsg_ref[...])) * _dot(h, wsu_ref[...])).astype(BF16), wsd_ref[...])
    m = mod_ref[0]
    y2 = DN_ALPHA * x1_ref[...] + m[5:6] * (routed + shared)
    out_ref[...] = _layer_norm(y2, g_ref[...], b_ref[...])


def _combine(ytok, h2, wtok, x1, mod, W, *, latent):
    t = h2.shape[0]
    ct = COMBINE_TM
    steps_per_batch = (DEC_SEQ if latent else t) // ct
    row = lambda w: pl.BlockSpec((ct, w), lambda i: (i, 0))
    return pl.pallas_call(
        _combine_kernel,
        out_shape=jax.ShapeDtypeStruct((t, D_MODEL), F32),
        grid=(t // ct,),
        in_specs=[pl.BlockSpec((TOP_K, ct, PACKED_W), lambda i: (0, i, 0)),
                  row(PACKED_W), row(LANES), row(D_MODEL),
                  pl.BlockSpec((1, 6, D_MODEL), lambda i: (i // steps_per_batch, 0, 0)),
                  _full((D_MODEL, SHARED_FF)), _full((D_MODEL, SHARED_FF)), _full((SHARED_FF, D_MODEL)),
                  _full((1, D_MODEL)), _full((1, D_MODEL))],
        out_specs=row(D_MODEL),
        compiler_params=_params("parallel"),
        name="moe_combine",
    )(ytok, h2, wtok, x1, mod, W["wsg"], W["wsu"], W["wsd"], W["ln2_g"], W["ln2_b"])


def _moe(h2, x1, mod, W, *, latent):
    t = h2.shape[0]
    eid, rank, wtok, cnt = _router(h2, W["wr_t"], W["rbias"])
    pos, tile_expert, tile_real = _moe_plan(eid, rank, cnt, t)
    xs = _dispatch(h2, pos)
    ys = _ffn(xs, tile_expert, tile_real, W)
    ytok = _gather_rows(ys, pos.reshape(TOP_K * t)).reshape(TOP_K, t, PACKED_W)
    return _combine(ytok, h2, wtok, x1, mod, W, latent=latent)


def _rope_partner(n_blocks, block):
    half = block // 2
    i = np.arange(n_blocks * block)
    return np.where((i % block) < half, i + half, i - half)


_QB_NOPE = np.concatenate([np.arange(MLA_NOPE) + (MLA_NOPE + MLA_ROPE) * h for h in range(MLA_HEADS)])
_QB_ROPE = np.concatenate([np.arange(MLA_ROPE) + (MLA_NOPE + MLA_ROPE) * h + MLA_NOPE for h in range(MLA_HEADS)])
_KVB_NOPE = np.concatenate([np.arange(MLA_NOPE) + (MLA_NOPE + MLA_V) * h for h in range(MLA_HEADS)])
_KVB_V = np.concatenate([np.arange(MLA_V) + (MLA_NOPE + MLA_V) * h + MLA_NOPE for h in range(MLA_HEADS)])
_DIFF_PARTNER = _rope_partner(2 * DIFF_W // 32, 32)
_MLA_PARTNER = _rope_partner(MLA_HEADS * MLA_ROPE // 16, 16)
_KR_PARTNER = _rope_partner(MLA_ROPE // 16, 16)


def _layer_weights(l, w_in, mla_qa_g, mla_wq_b, mla_kva_g, mla_wkv_b, w_branch_diff, w_branch_na,
                   w_branch_mla, w_out, ln1_g, ln1_b, ln2_g, ln2_b, w_router, router_bias,
                   w_exp_gate, w_exp_up, w_exp_down, w_sh_gate, w_sh_up, w_sh_down):
    win = w_in[l].astype(BF16)
    wa = win[:, :3072]
    qa_kva = win[:, 3072:3712]
    kr = win[:, 3712:3744]
    kr4 = jnp.tile(kr, (1, LANES // MLA_ROPE))
    krp4 = jnp.tile(kr[:, _KR_PARTNER], (1, LANES // MLA_ROPE))
    wqb = mla_wq_b[l].astype(BF16)
    wq_rope = wqb[:, _QB_ROPE]
    wkvb = mla_wkv_b[l].astype(BF16)
    row = lambda v: v[l].reshape(1, -1).astype(F32)
    return {
        "wa": wa,
        "wp": wa[:, :2 * DIFF_W][:, _DIFF_PARTNER],
        "wm_ctx": jnp.concatenate([qa_kva, kr4], axis=1),
        "wm_lat": jnp.concatenate([qa_kva, kr4, krp4], axis=1),
        "qag": row(mla_qa_g), "kvag": row(mla_kva_g),
        "wqb": jnp.concatenate([wqb[:, _QB_NOPE], wq_rope], axis=1),
        "wqp": wq_rope[:, _MLA_PARTNER],
        "wkvb": jnp.concatenate([wkvb[:, _KVB_NOPE], wkvb[:, _KVB_V]], axis=1),
        "wa_lat": jnp.concatenate([wa[:, 0:1024], wa[:, 1536:2560]], axis=1),
        "wvt": jnp.concatenate([wa[:, 1024:1536], wa[:, 2560:3072]], axis=1).T,
        "wkvb_nope": wkvb[:, _KVB_NOPE],
        "wkvvt": wkvb[:, _KVB_V].T,
        "wg": win[:, 3744:],
        "wbd": w_branch_diff[l].astype(BF16), "wbn": w_branch_na[l].astype(BF16),
        "wbm": w_branch_mla[l].astype(BF16), "wo": w_out[l].astype(BF16),
        "ln1_g": row(ln1_g), "ln1_b": row(ln1_b), "ln2_g": row(ln2_g), "ln2_b": row(ln2_b),
        "wr_t": w_router[l].T.astype(BF16), "rbias": router_bias[l].reshape(N_EXPERTS, 1).astype(F32),
        "weg": w_exp_gate[l].astype(BF16), "weu": w_exp_up[l].astype(BF16), "wed": w_exp_down[l].astype(BF16),
        "wsg": w_sh_gate[l].astype(BF16), "wsu": w_sh_up[l].astype(BF16), "wsd": w_sh_down[l].astype(BF16),
    }


def _rope_tables():
    t = jnp.arange(DEC_SEQ)
    pos = [(t // GRID_W).astype(F32), (t % GRID_W).astype(F32)]

    def table(block):
        half = block // 4
        inv = ROPE_BASE ** (-jnp.arange(half, dtype=F32) / half)
        cos, sin = [], []
        for p in pos:
            ang = p[:, None] * inv[None, :]
            cos += [jnp.cos(ang), jnp.cos(ang)]
            sin += [-jnp.sin(ang), jnp.sin(ang)]
        reps = LANES // block
        return (jnp.tile(jnp.concatenate(cos, axis=1), (1, reps)),
                jnp.tile(jnp.concatenate(sin, axis=1), (1, reps)))

    cd, sd = table(DIFF_DH)
    cm, sm = table(MLA_ROPE)
    return {"cd": cd, "sd": sd, "cm": cm, "sm": sm}


def kernel(x_prompt, x_sample, cache_diff_k, cache_diff_v, cache_na_k, cache_na_v, cache_mla_ckv, cache_mla_krope, c, c_ctx, w_mod, b_mod, w_in, diff_lambda, diff_subln_g, na_rpb, mla_qa_g, mla_wq_b, mla_kva_g, mla_wkv_b, w_branch_diff, w_branch_na, w_branch_mla, w_out, ln1_g, ln1_b, ln2_g, ln2_b, w_router, router_bias, w_exp_gate, w_exp_up, w_exp_down, w_sh_gate, w_sh_up, w_sh_down):
    t_ctx = BATCH * SEQ
    t_lat = DEC_BATCH * DEC_SEQ
    cvec = jnp.concatenate([c, c_ctx[None, :], jnp.zeros((MOD_ROWS - DEC_BATCH - 1, D_MODEL), F32)], axis=0)
    mods = _modulation(cvec, w_mod, b_mod).reshape(DEPTH, MOD_ROWS, 6, D_MODEL)
    rope = _rope_tables()
    ck_d = cache_diff_k.reshape(DEC_BATCH, DEPTH, PAST_LEN, DIFF_W)
    cv_d = cache_diff_v.reshape(DEC_BATCH, DEPTH, PAST_LEN, DIFF_W)
    ck_n = cache_na_k.reshape(DEC_BATCH, DEPTH, PAST_LEN, NA_W)
    cv_n = cache_na_v.reshape(DEC_BATCH, DEPTH, PAST_LEN, NA_W)

    xc = x_prompt.reshape(t_ctx, D_MODEL)
    xl = x_sample.reshape(t_lat, D_MODEL)
    states = []
    for l in range(DEPTH):
        W = _layer_weights(l, w_in, mla_qa_g, mla_wq_b, mla_kva_g, mla_wkv_b, w_branch_diff, w_branch_na,
                           w_branch_mla, w_out, ln1_g, ln1_b, ln2_g, ln2_b, w_router, router_bias,
                           w_exp_gate, w_exp_up, w_exp_down, w_sh_gate, w_sh_up, w_sh_down)
        lam = diff_lambda[l].astype(F32)
        subln = diff_subln_g[l].reshape(1, LANES).astype(F32)
        mod_c = mods[l, DEC_BATCH:DEC_BATCH + 1]
        mod_l = mods[l, :DEC_BATCH]

        Wc = dict(W, wm=W["wm_ctx"])
        names = ["dq", "nq", "mqn", "mqr", "kr4", "kn", "vn", "dk", "dv", "nk", "nv", "ckv", "kr"]
        pc = dict(zip(names, _project(xc, mod_c, Wc, None, latent=False)))
        od, on, om = _attn_ctx(pc, lam, subln, l)
        x1, h2 = _merge(xc, mod_c, od, on, om, W, latent=False)
        xc = _moe(h2, x1, mod_c, W, latent=False)
        states.append(pc)

        Wl = dict(W, wm=W["wm_lat"], wa=W["wa_lat"], wkvb=W["wkvb_nope"])
        names = ["dq", "dk", "dvt", "nq", "nk", "nvt", "mqn", "mqr", "kr4", "kn", "vnt"]
        pll = dict(zip(names, _project(xl, mod_l, Wl, rope, latent=True)))
        cache = _cache_prep([ck_d, cv_d, ck_n, cv_n, cache_mla_ckv, cache_mla_krope],
                            W["wkvb_nope"], W["wkvvt"], l)
        od = _attn_diff_lat(pll, cache, lam, subln.reshape(LANES, 1), l)
        on = _attn_na_lat(pll, cache, _na_bias(na_rpb[l].astype(F32)))
        om = _attn_mla_lat(pll, cache)
        x1, h2 = _merge(xl, mod_l, od, on, om, W, latent=True)
        xl = _moe(h2, x1, mod_l, W, latent=True)

    def stack(name, shape):
        return jnp.stack([s[name].reshape((BATCH, SEQ) + shape) for s in states], axis=1)

    return (xc.reshape(BATCH, SEQ, D_MODEL), xl.reshape(DEC_BATCH, DEC_SEQ, D_MODEL),
            stack("dk", (DIFF_HEADS, 2 * DIFF_DH)), stack("dv", (DIFF_HEADS, 2 * DIFF_DH)),
            stack("nk", (NA_HEADS, NA_DH)), stack("nv", (NA_HEADS, NA_DH)),
            stack("ckv", (MLA_KV_LORA,)), stack("kr", (MLA_ROPE,)))
```

```python
import functools

import numpy as np
import jax
import jax.numpy as jnp
from jax import lax
from jax.experimental import pallas as pl
from jax.experimental.pallas import tpu as pltpu
from jax.experimental.pallas import tpu_sc as plsc

D_MODEL = 1024
BATCH = 32
SEQ = 256
DEPTH = 2
DEC_BATCH = 8
DEC_SEQ = 2048
PAST_LEN = 512
GRID_W = 64
GRID_ROWS = DEC_SEQ // GRID_W
ROPE_BASE = 10000.0
DIFF_HEADS = 4
DIFF_DH = 64
DIFF_W = 512
NA_HEADS = 8
NA_DH = 64
NA_W = 512
NA_WIN_ROWS = 8
NA_WIN_COLS = 16
MLA_HEADS = 8
MLA_Q_LORA = 384
MLA_KV_LORA = 256
MLA_NOPE = 64
MLA_ROPE = 32
MLA_V = 64
MLA_W = 512
N_EXPERTS = 64
N_GROUPS = 8
GROUP_SIZE = N_EXPERTS // N_GROUPS
TOPK_GROUPS = 4
TOP_K = 8
EXPERT_FF = 256
SHARED_FF = 256
ROUTED_SCALE = 2.5
DN_ALPHA = (2 * DEPTH) ** 0.25
LN_EPS = 1e-5
RMS_EPS = 1e-6

F32 = jnp.float32
BF16 = jnp.bfloat16

LANES = 128
VMEM_LIMIT_BYTES = 56 * 1024 * 1024
NEG_BIG = -1e30

DIFF_SCALE = DIFF_DH ** -0.5
NA_SCALE = NA_DH ** -0.5
MLA_SCALE = (MLA_NOPE + MLA_ROPE) ** -0.5

PROJ_TM = 512
ATT_TQ = 512
NA_TQ = 256
NA_Q_ROWS = NA_TQ // GRID_W
NA_KEY_ROWS = NA_Q_ROWS + NA_WIN_ROWS
NA_KEYS = NA_KEY_ROWS * GRID_W
MOE_TILE_MIN = 128
MOE_TILE_MAX = 512
SC_ROWS = 128
COMBINE_TM = 256


def _dot(a, b):
    return jnp.dot(a, b, preferred_element_type=F32)


def _dot_nt(a, b):
    return lax.dot_general(a, b, (((1,), (1,)), ((), ())), preferred_element_type=F32)


def _sigmoid(x):
    return 1.0 / (1.0 + jnp.exp(-x))


def _silu(x):
    return x * _sigmoid(x)


def _params(*sem):
    return pltpu.CompilerParams(dimension_semantics=sem, vmem_limit_bytes=VMEM_LIMIT_BYTES)


def _full(shape):
    n = len(shape)
    return pl.BlockSpec(shape, lambda *_: (0,) * n)


def _layer_norm(y, g, b):
    mu = jnp.mean(y, axis=-1, keepdims=True)
    yc = y - mu
    var = jnp.mean(yc * yc, axis=-1, keepdims=True)
    return yc * lax.rsqrt(var + LN_EPS) * g + b


def _rms(x, g):
    return x * lax.rsqrt(jnp.mean(x * x, axis=-1, keepdims=True) + RMS_EPS) * g


HIGH_HALF = -65536
PACKED_W = D_MODEL // 2


def _pack_bf16_pairs(x):
    n = x.shape[1] // 2
    bits = lax.bitcast_convert_type(x.astype(BF16).astype(F32), jnp.int32)
    return lax.shift_right_logical(bits[:, :n], 16) | (bits[:, n:] & HIGH_HALF)


def _unpack_bf16_pairs(w):
    lo = lax.bitcast_convert_type(lax.shift_left(w, 16), F32)
    hi = lax.bitcast_convert_type(w & HIGH_HALF, F32)
    return jnp.concatenate([lo, hi], axis=1)


MOD_ROWS = 16
MOD_TN = 1536


def _mod_kernel(c_ref, w_ref, b_ref, o_ref):
    s = _silu(c_ref[...]).astype(BF16)
    o_ref[0] = _dot(s, w_ref[0].astype(BF16)) + b_ref[0]


def _modulation(cvec, w_mod, b_mod):
    n = 6 * D_MODEL
    return pl.pallas_call(
        _mod_kernel,
        out_shape=jax.ShapeDtypeStruct((DEPTH, MOD_ROWS, n), F32),
        grid=(DEPTH, n // MOD_TN),
        in_specs=[
            pl.BlockSpec((MOD_ROWS, D_MODEL), lambda l, j: (0, 0)),
            pl.BlockSpec((1, D_MODEL, MOD_TN), lambda l, j: (l, 0, j)),
            pl.BlockSpec((1, 1, MOD_TN), lambda l, j: (l, 0, j)),
        ],
        out_specs=pl.BlockSpec((1, MOD_ROWS, MOD_TN), lambda l, j: (l, 0, j)),
        compiler_params=_params("parallel", "parallel"),
        name="modulation",
    )(cvec, w_mod, b_mod.reshape(DEPTH, 1, n))


def _proj_common(x_ref, mod_ref, wa_ref, wm_ref, qag_ref, kvag_ref, wqb_ref, wkvb_ref):
    m = mod_ref[0]
    h = (x_ref[...] * (1.0 + m[1:2]) + m[0:1]).astype(BF16)
    a = _dot(h, wa_ref[...])
    mm = _dot(h, wm_ref[...])
    qan = _rms(mm[:, :MLA_Q_LORA], qag_ref[...]).astype(BF16)
    mq = _dot(qan, wqb_ref[...])
    ckv = _rms(mm[:, MLA_Q_LORA:MLA_Q_LORA + MLA_KV_LORA], kvag_ref[...])
    kv = _dot(ckv.astype(BF16), wkvb_ref[...])
    return h, a, mm, qan, mq, ckv, kv


def _proj_ctx_kernel(x_ref, mod_ref, wa_ref, wm_ref, qag_ref, kvag_ref, wqb_ref, wkvb_ref,
                     dq_ref, nq_ref, mqn_ref, mqr_ref, kr4_ref, kn_ref, vn_ref,
                     dk_ref, dv_ref, nk_ref, nv_ref, ckv_ref, kr_ref):
    _, a, mm, _, mq, ckv, kv = _proj_common(x_ref, mod_ref, wa_ref, wm_ref, qag_ref, kvag_ref,
                                            wqb_ref, wkvb_ref)
    dq_ref[...] = (a[:, 0:512] * DIFF_SCALE).astype(BF16)
    dk_ref[...] = a[:, 512:1024]
    dv_ref[...] = a[:, 1024:1536]
    nq_ref[...] = (a[:, 1536:2048] * NA_SCALE).astype(BF16)
    nk_ref[...] = a[:, 2048:2560]
    nv_ref[...] = a[:, 2560:3072]
    mqn_ref[...] = (mq[:, :512] * MLA_SCALE).astype(BF16)
    mqr_ref[...] = (mq[:, 512:768] * MLA_SCALE).astype(BF16)
    kr4 = mm[:, 640:768]
    kr4_ref[...] = kr4.astype(BF16)
    kr_ref[...] = kr4[:, :MLA_ROPE]
    ckv_ref[...] = ckv
    kn_ref[...] = kv[:, :512].astype(BF16)
    vn_ref[...] = kv[:, 512:].astype(BF16)


def _proj_lat_kernel(x_ref, mod_ref, wa_ref, wm_ref, qag_ref, kvag_ref, wqb_ref, wkvb_ref,
                     wp_ref, wqp_ref, cd_ref, sd_ref, cm_ref, sm_ref,
                     dq_ref, dk_ref, dv_ref, nq_ref, nk_ref, nv_ref,
                     mqn_ref, mqr_ref, kr4_ref, kn_ref, vn_ref):
    h, a, mm, qan, mq, _, kv = _proj_common(x_ref, mod_ref, wa_ref, wm_ref, qag_ref, kvag_ref,
                                            wqb_ref, wkvb_ref)
    ap = _dot(h, wp_ref[...])
    mqp = _dot(qan, wqp_ref[...])
    cd = cd_ref[...]
    sd = sd_ref[...]
    cm = cm_ref[...]
    sm = sm_ref[...]
    for j in range(DIFF_W // LANES):
        lo, hi = LANES * j, LANES * (j + 1)
        dq_ref[:, lo:hi] = ((a[:, lo:hi] * cd + ap[:, lo:hi] * sd) * DIFF_SCALE).astype(BF16)
        dk_ref[:, lo:hi] = (a[:, 512 + lo:512 + hi] * cd + ap[:, 512 + lo:512 + hi] * sd).astype(BF16)
    dv_ref[...] = a[:, 1024:1536].astype(BF16)
    nq_ref[...] = (a[:, 1536:2048] * NA_SCALE).astype(BF16)
    nk_ref[...] = a[:, 2048:2560].astype(BF16)
    nv_ref[...] = a[:, 2560:3072].astype(BF16)
    mqn_ref[...] = (mq[:, :512] * MLA_SCALE).astype(BF16)
    for j in range(2):
        lo, hi = LANES * j, LANES * (j + 1)
        mqr_ref[:, lo:hi] = ((mq[:, 512 + lo:512 + hi] * cm + mqp[:, lo:hi] * sm) * MLA_SCALE).astype(BF16)
    kr4_ref[...] = (mm[:, 640:768] * cm + mm[:, 768:896] * sm).astype(BF16)
    kn_ref[...] = kv[:, :512].astype(BF16)
    vn_ref[...] = kv[:, 512:].astype(BF16)


def _project(x, mod, W, rope, *, latent):
    t = x.shape[0]
    tm = PROJ_TM
    tokens_per_batch = DEC_SEQ if latent else t
    steps_per_batch = tokens_per_batch // tm
    row = lambda w: pl.BlockSpec((tm, w), lambda i: (i, 0))
    common_in = [
        row(D_MODEL),
        pl.BlockSpec((1, 6, D_MODEL), lambda i: (i // steps_per_batch, 0, 0)),
        _full(W["wa"].shape), _full(W["wm"].shape), _full((1, MLA_Q_LORA)), _full((1, MLA_KV_LORA)),
        _full(W["wqb"].shape), _full(W["wkvb"].shape),
    ]
    common_args = [x, mod, W["wa"], W["wm"], W["qag"], W["kvag"], W["wqb"], W["wkvb"]]
    bf = lambda w: jax.ShapeDtypeStruct((t, w), BF16)
    f32 = lambda w: jax.ShapeDtypeStruct((t, w), F32)
    if latent:
        tab = pl.BlockSpec((tm, LANES), lambda i: (i % steps_per_batch, 0))
        widths = [512, 512, 512, 512, 512, 512, 512, 256, 128, 512, 512]
        return pl.pallas_call(
            _proj_lat_kernel,
            out_shape=[bf(w) for w in widths],
            grid=(t // tm,),
            in_specs=common_in + [_full(W["wp"].shape), _full(W["wqp"].shape), tab, tab, tab, tab],
            out_specs=[row(w) for w in widths],
            compiler_params=_params("parallel"),
            name="proj_lat",
        )(*common_args, W["wp"], W["wqp"], rope["cd"], rope["sd"], rope["cm"], rope["sm"])
    bf_w = [512, 512, 512, 256, 128, 512, 512]
    f32_w = [512, 512, 512, 512, 256, 32]
    return pl.pallas_call(
        _proj_ctx_kernel,
        out_shape=[bf(w) for w in bf_w] + [f32(w) for w in f32_w],
        grid=(t // tm,),
        in_specs=common_in,
        out_specs=[row(w) for w in bf_w + f32_w],
        compiler_params=_params("parallel"),
        name="proj_ctx",
    )(*common_args)


def _lane_iota():
    return lax.broadcasted_iota(jnp.int32, (1, LANES), 1)


def _softmax_parts(parts):
    m = functools.reduce(jnp.maximum, [jnp.max(s, axis=-1, keepdims=True) for s in parts])
    es = [jnp.exp(s - m) for s in parts]
    l = functools.reduce(lambda u, v: u + v, [jnp.sum(e, axis=-1, keepdims=True) for e in es])
    return [e / l for e in es]


def _diff_lambda(lam_ref, layer):
    lp = lam_ref[...]
    lam_init = 0.8 - 0.6 * float(np.exp(-0.3 * layer))
    s1 = jnp.sum(lp[0:1] * lp[1:2], axis=-1, keepdims=True)
    s2 = jnp.sum(lp[2:3] * lp[3:4], axis=-1, keepdims=True)
    return jnp.exp(s1) - jnp.exp(s2) + lam_init, lam_init


def _diff_heads(q_ref, ks, vs, lam_ref, g_ref, o_ref, layer):
    lam, lam_init = _diff_lambda(lam_ref, layer)
    first_map = _lane_iota() < DIFF_DH
    g = g_ref[...]
    for h in range(DIFF_HEADS):
        hs = slice(LANES * h, LANES * (h + 1))
        q = q_ref[:, hs]
        q1 = jnp.where(first_map, q, jnp.zeros_like(q))
        q2 = jnp.where(first_map, jnp.zeros_like(q), q)
        kk = [k[:, hs].astype(BF16) for k in ks]
        p1 = _softmax_parts([_dot_nt(q1, k) for k in kk])
        p2 = _softmax_parts([_dot_nt(q2, k) for k in kk])
        o = None
        for a1, a2, v in zip(p1, p2, vs):
            part = _dot((a1 - lam * a2).astype(BF16), v[:, hs].astype(BF16))
            o = part if o is None else o + part
        o = _rms(o, g) * (1.0 - lam_init)
        o_ref[:, hs] = o.astype(BF16)


def _pair_heads(q_of, k_of, v_of, bias_of, o_ref, n_pairs):
    first = _lane_iota() < 64
    for j in range(n_pairs):
        ps = slice(LANES * j, LANES * (j + 1))
        ks = k_of(j)
        vs = v_of(j)
        outs = []
        for hh in range(2):
            q = q_of(j, hh)
            ss = [_dot_nt(q, k) for k in ks]
            bs = bias_of(2 * j + hh)
            ss = [s if b is None else s + b for s, b in zip(ss, bs)]
            ps_ = _softmax_parts(ss)
            o = None
            for p, v in zip(ps_, vs):
                part = _dot(p.astype(BF16), v)
                o = part if o is None else o + part
            outs.append(o)
        o_ref[:, ps] = jnp.where(first, outs[0], outs[1]).astype(BF16)


def _na_q(q_ref):
    first = _lane_iota() < NA_DH

    def q_of(j, hh):
        q = q_ref[:, LANES * j:LANES * (j + 1)]
        keep = first if hh == 0 else jnp.logical_not(first)
        return jnp.where(keep, q, jnp.zeros_like(q))
    return q_of


def _mla_q(qn_ref, qr_ref):
    lane = _lane_iota()
    first = lane < MLA_NOPE

    def q_of(j, hh):
        h = 2 * j + hh
        qn = qn_ref[:, LANES * j:LANES * (j + 1)]
        keep = first if hh == 0 else jnp.logical_not(first)
        qn = jnp.where(keep, qn, jnp.zeros_like(qn))
        qr = qr_ref[:, LANES * (h // 4):LANES * (h // 4 + 1)]
        qr = jnp.where((lane // MLA_ROPE) == (h % 4), qr, jnp.zeros_like(qr))
        return jnp.concatenate([qn, qr], axis=1)
    return q_of


def _attn_ctx_kernel(layer, dq_ref, dk_ref, dv_ref, nq_ref, nk_ref, nv_ref,
                     mqn_ref, mqr_ref, kn_ref, kr4_ref, vn_ref, lam_ref, g_ref,
                     od_ref, on_ref, om_ref):
    _diff_heads(dq_ref, [dk_ref], [dv_ref], lam_ref, g_ref, od_ref, layer)
    none = lambda h: [None]
    pair = lambda j: slice(LANES * j, LANES * (j + 1))
    _pair_heads(_na_q(nq_ref),
                lambda j: [nk_ref[:, pair(j)].astype(BF16)],
                lambda j: [nv_ref[:, pair(j)].astype(BF16)],
                none, on_ref, NA_HEADS // 2)
    kr4 = kr4_ref[...]
    _pair_heads(_mla_q(mqn_ref, mqr_ref),
                lambda j: [jnp.concatenate([kn_ref[:, pair(j)], kr4], axis=1)],
                lambda j: [vn_ref[:, pair(j)]],
                none, om_ref, MLA_HEADS // 2)


def _attn_ctx(p, lam, g, layer):
    t = p["dq"].shape[0]
    row = lambda w: pl.BlockSpec((SEQ, w), lambda b: (b, 0))
    names = ["dq", "dk", "dv", "nq", "nk", "nv", "mqn", "mqr", "kn", "kr4", "vn"]
    out = jax.ShapeDtypeStruct((t, 512), BF16)
    return pl.pallas_call(
        functools.partial(_attn_ctx_kernel, layer),
        out_shape=[out, out, out],
        grid=(t // SEQ,),
        in_specs=[row(p[n].shape[1]) for n in names] + [_full((4, DIFF_DH)), _full((1, LANES))],
        out_specs=[row(512)] * 3,
        compiler_params=_params("parallel"),
        name="attn_ctx",
    )(*[p[n] for n in names], lam, g)


def _cache_spec(width, layer):
    return pl.BlockSpec((None, None, PAST_LEN, width), lambda b, q: (b, layer, 0, 0))


def _batch_spec(width):
    return pl.BlockSpec((None, DEC_SEQ, width), lambda b, q: (b, 0, 0))


def _qtile_spec(width, tq):
    steps = DEC_SEQ // tq
    return pl.BlockSpec((tq, width), lambda b, q: (b * steps + q, 0))


def _attn_diff_lat_kernel(layer, q_ref, kc_ref, vc_ref, kn_ref, vn_ref, lam_ref, g_ref, o_ref):
    _diff_heads(q_ref, [kc_ref, kn_ref], [vc_ref, vn_ref], lam_ref, g_ref, o_ref, layer)


def _attn_diff_lat(p, cache_k, cache_v, lam, g, layer):
    t = p["dq"].shape[0]
    b3 = lambda a: a.reshape(DEC_BATCH, DEC_SEQ, a.shape[1])
    return pl.pallas_call(
        functools.partial(_attn_diff_lat_kernel, layer),
        out_shape=jax.ShapeDtypeStruct((t, DIFF_W), BF16),
        grid=(DEC_BATCH, DEC_SEQ // ATT_TQ),
        in_specs=[_qtile_spec(DIFF_W, ATT_TQ), _cache_spec(DIFF_W, layer), _cache_spec(DIFF_W, layer),
                  _batch_spec(DIFF_W), _batch_spec(DIFF_W),
                  pl.BlockSpec((4, DIFF_DH), lambda b, q: (0, 0)),
                  pl.BlockSpec((1, LANES), lambda b, q: (0, 0))],
        out_specs=_qtile_spec(DIFF_W, ATT_TQ),
        compiler_params=_params("parallel", "parallel"),
        name="attn_diff_lat",
    )(p["dq"], cache_k, cache_v, b3(p["dk"]), b3(p["dv"]), lam, g)


def _na_key_start(q):
    return jnp.clip(q * NA_Q_ROWS - NA_WIN_ROWS // 2, 0, GRID_ROWS - NA_KEY_ROWS)


def _attn_na_lat_kernel(q_ref, kc_ref, vc_ref, kn_ref, vn_ref, bias_ref, o_ref):
    start = pl.multiple_of(_na_key_start(pl.program_id(1)) * GRID_W, GRID_W)
    kw = kn_ref[pl.ds(start, NA_KEYS), :]
    vw = vn_ref[pl.ds(start, NA_KEYS), :]
    pair = lambda j: slice(LANES * j, LANES * (j + 1))
    _pair_heads(_na_q(q_ref),
                lambda j: [kc_ref[:, pair(j)].astype(BF16), kw[:, pair(j)]],
                lambda j: [vc_ref[:, pair(j)].astype(BF16), vw[:, pair(j)]],
                lambda h: [None, bias_ref[h]],
                o_ref, NA_HEADS // 2)


def _na_bias_tables():
    n_blocks = DEC_SEQ // NA_TQ
    qr = np.arange(NA_Q_ROWS)
    kr = np.arange(NA_KEY_ROWS)
    row_sel, row_ok = [], []
    for qb in range(n_blocks):
        ks = int(np.clip(qb * NA_Q_ROWS - NA_WIN_ROWS // 2, 0, GRID_ROWS - NA_KEY_ROWS))
        r = qb * NA_Q_ROWS + qr
        r0 = np.clip(r - NA_WIN_ROWS // 2, 0, GRID_ROWS - NA_WIN_ROWS)
        krow = ks + kr
        ok = (krow[None, :] >= r0[:, None]) & (krow[None, :] < r0[:, None] + NA_WIN_ROWS)
        off = krow[None, :] - r[:, None] + NA_WIN_ROWS - 1
        sel = (off[:, :, None] == np.arange(2 * NA_WIN_ROWS - 1)) & ok[:, :, None]
        row_sel.append(sel.astype(np.float32))
        row_ok.append(ok)
    kinds, kind_of_block = [], []
    for qb in range(n_blocks):
        for n, other in enumerate(kinds):
            if np.array_equal(row_sel[qb], row_sel[other]):
                kind_of_block.append(n)
                break
        else:
            kind_of_block.append(len(kinds))
            kinds.append(qb)
    c = np.arange(GRID_W)
    c0 = np.clip(c - NA_WIN_COLS // 2, 0, GRID_W - NA_WIN_COLS)
    col_ok = (c[None, :] >= c0[:, None]) & (c[None, :] < c0[:, None] + NA_WIN_COLS)
    coff = c[None, :] - c[:, None] + NA_WIN_COLS - 1
    col_sel = ((coff[:, :, None] == np.arange(2 * NA_WIN_COLS - 1)) & col_ok[:, :, None]).astype(np.float32)
    rsel = np.stack([row_sel[qb] for qb in kinds])
    valid = np.stack([row_ok[qb][:, None, :, None] & col_ok[None, :, None, :] for qb in kinds])
    valid = valid.reshape(len(kinds), 1, NA_TQ, NA_KEYS)
    return rsel, col_sel, valid, kind_of_block


_NA_ROW_SEL, _NA_COL_SEL, _NA_VALID, _NA_KIND_OF_BLOCK = _na_bias_tables()


def _na_bias(rpb):
    hp = lax.Precision.HIGHEST
    cols = jnp.einsum("hij,ckj->hick", rpb, _NA_COL_SEL, precision=hp)
    b = jnp.einsum("nqri,hick->nhqcrk", _NA_ROW_SEL, cols, precision=hp)
    b = b.reshape(_NA_ROW_SEL.shape[0], NA_HEADS, NA_TQ, NA_KEYS)
    return jnp.where(_NA_VALID, b, NEG_BIG).astype(F32)


def _na_kind(q):
    return (q > 0).astype(jnp.int32) + (q == DEC_SEQ // NA_TQ - 1).astype(jnp.int32)


def _attn_na_lat(p, cache_k, cache_v, bias, layer):
    assert _NA_KIND_OF_BLOCK == [0] + [1] * (DEC_SEQ // NA_TQ - 2) + [2]
    t = p["nq"].shape[0]
    b3 = lambda a: a.reshape(DEC_BATCH, DEC_SEQ, a.shape[1])
    return pl.pallas_call(
        _attn_na_lat_kernel,
        out_shape=jax.ShapeDtypeStruct((t, NA_W), BF16),
        grid=(DEC_BATCH, DEC_SEQ // NA_TQ),
        in_specs=[_qtile_spec(NA_W, NA_TQ), _cache_spec(NA_W, layer), _cache_spec(NA_W, layer),
                  _batch_spec(NA_W), _batch_spec(NA_W),
                  pl.BlockSpec((None, NA_HEADS, NA_TQ, NA_KEYS), lambda b, q: (_na_kind(q), 0, 0, 0))],
        out_specs=_qtile_spec(NA_W, NA_TQ),
        compiler_params=_params("parallel", "parallel"),
        name="attn_na_lat",
    )(p["nq"], cache_k, cache_v, b3(p["nk"]), b3(p["nv"]), bias)


def _mla_cache_kernel(ckv_ref, kr_ref, wkvb_ref, rep_ref, kc_ref, vc_ref, krc_ref):
    kv = _dot(ckv_ref[...].astype(BF16), wkvb_ref[...])
    kc_ref[...] = kv[:, :512].astype(BF16)
    vc_ref[...] = kv[:, 512:].astype(BF16)
    krc_ref[...] = _dot(kr_ref[...].astype(BF16), rep_ref[...]).astype(BF16)


def _mla_cache(cache_ckv, cache_kr, wkvb, layer):
    rep = jnp.asarray(np.tile(np.eye(MLA_ROPE, dtype=np.float32), (1, LANES // MLA_ROPE)), BF16)
    spec_in = lambda w: pl.BlockSpec((None, None, PAST_LEN, w), lambda b: (b, layer, 0, 0))
    spec_out = lambda w: pl.BlockSpec((None, PAST_LEN, w), lambda b: (b, 0, 0))
    shp = lambda w: jax.ShapeDtypeStruct((DEC_BATCH, PAST_LEN, w), BF16)
    return pl.pallas_call(
        _mla_cache_kernel,
        out_shape=[shp(512), shp(512), shp(LANES)],
        grid=(DEC_BATCH,),
        in_specs=[spec_in(MLA_KV_LORA), spec_in(MLA_ROPE), _full(wkvb.shape), _full(rep.shape)],
        out_specs=[spec_out(512), spec_out(512), spec_out(LANES)],
        compiler_params=_params("parallel"),
        name="mla_cache",
    )(cache_ckv, cache_kr, wkvb, rep)


def _attn_mla_lat_kernel(qn_ref, qr_ref, kc_ref, krc_ref, vc_ref, kn_ref, krn_ref, vn_ref, o_ref):
    pair = lambda j: slice(LANES * j, LANES * (j + 1))
    krc = krc_ref[...]
    krn = krn_ref[...]
    _pair_heads(_mla_q(qn_ref, qr_ref),
                lambda j: [jnp.concatenate([kc_ref[:, pair(j)], krc], axis=1),
                           jnp.concatenate([kn_ref[:, pair(j)], krn], axis=1)],
                lambda j: [vc_ref[:, pair(j)], vn_ref[:, pair(j)]],
                lambda h: [None, None],
                o_ref, MLA_HEADS // 2)


def _attn_mla_lat(p, kc, vc, krc):
    t = p["mqn"].shape[0]
    b3 = lambda a: a.reshape(DEC_BATCH, DEC_SEQ, a.shape[1])
    cspec = lambda w: pl.BlockSpec((None, PAST_LEN, w), lambda b, q: (b, 0, 0))
    return pl.pallas_call(
        _attn_mla_lat_kernel,
        out_shape=jax.ShapeDtypeStruct((t, MLA_W), BF16),
        grid=(DEC_BATCH, DEC_SEQ // ATT_TQ),
        in_specs=[_qtile_spec(512, ATT_TQ), _qtile_spec(256, ATT_TQ), cspec(512), cspec(LANES), cspec(512),
                  _batch_spec(512), _batch_spec(LANES), _batch_spec(512)],
        out_specs=_qtile_spec(MLA_W, ATT_TQ),
        compiler_params=_params("parallel", "parallel"),
        name="attn_mla_lat",
    )(p["mqn"], p["mqr"], kc, krc, vc, b3(p["kn"]), b3(p["kr4"]), b3(p["vn"]))


def _merge_kernel(x_ref, mod_ref, od_ref, on_ref, om_ref, wg_ref, wbd_ref, wbn_ref, wbm_ref, wo_ref,
                  g_ref, b_ref, x1_ref, h2_ref):
    x = x_ref[...]
    m = mod_ref[0]
    h = (x * (1.0 + m[1:2]) + m[0:1]).astype(BF16)
    gates = _dot(h, wg_ref[...])
    mix = (_sigmoid(gates[:, 0:1024]) * _dot(od_ref[...], wbd_ref[...])
           + _sigmoid(gates[:, 1024:2048]) * _dot(on_ref[...], wbn_ref[...])
           + _sigmoid(gates[:, 2048:3072]) * _dot(om_ref[...], wbm_ref[...]))
    out = _dot(mix.astype(BF16), wo_ref[...])
    x1 = _layer_norm(DN_ALPHA * x + m[2:3] * out, g_ref[...], b_ref[...])
    x1_ref[...] = x1
    h2_ref[...] = _pack_bf16_pairs(x1 * (1.0 + m[4:5]) + m[3:4])


def _merge(x, mod, od, on, om, W, *, latent):
    t = x.shape[0]
    tm = PROJ_TM
    steps_per_batch = (DEC_SEQ if latent else t) // tm
    row = lambda w: pl.BlockSpec((tm, w), lambda i: (i, 0))
    return pl.pallas_call(
        _merge_kernel,
        out_shape=[jax.ShapeDtypeStruct((t, D_MODEL), F32), jax.ShapeDtypeStruct((t, PACKED_W), jnp.int32)],
        grid=(t // tm,),
        in_specs=[row(D_MODEL),
                  pl.BlockSpec((1, 6, D_MODEL), lambda i: (i // steps_per_batch, 0, 0)),
                  row(512), row(512), row(512),
                  _full(W["wg"].shape), _full(W["wbd"].shape), _full(W["wbn"].shape),
                  _full(W["wbm"].shape), _full(W["wo"].shape),
                  _full((1, D_MODEL)), _full((1, D_MODEL))],
        out_specs=[row(D_MODEL), row(PACKED_W)],
        compiler_params=_params("parallel"),
        name="merge",
    )(x, mod, od, on, om, W["wg"], W["wbd"], W["wbn"], W["wbm"], W["wo"], W["ln1_g"], W["ln1_b"])


def _first_index_of_max(vals, idx, sentinel):
    mx = functools.reduce(jnp.maximum, [jnp.max(v, axis=0, keepdims=True) for v in vals])
    cand = [jnp.min(jnp.where(v == mx, i, sentinel), axis=0, keepdims=True) for v, i in zip(vals, idx)]
    return mx, functools.reduce(jnp.minimum, cand)


def _router_kernel(h_ref, wr_ref, bias_ref, tri_ref, eid_ref, rank_ref, wtok_ref, cnt_ref, base_ref):
    tm = h_ref.shape[0]

    @pl.when(pl.program_id(0) == 0)
    def _():
        base_ref[...] = jnp.zeros_like(base_ref)

    logits = _dot_nt(wr_ref[...], _unpack_bf16_pairs(h_ref[...]).astype(BF16))
    scores = _sigmoid(logits)
    biased = scores + bias_ref[...]
    member = lax.broadcasted_iota(jnp.int32, (GROUP_SIZE, tm), 0)
    slabs = [biased[GROUP_SIZE * g:GROUP_SIZE * (g + 1)] for g in range(N_GROUPS)]
    gscore = []
    for s in slabs:
        m1, first = _first_index_of_max([s], [member], GROUP_SIZE)
        m2 = jnp.max(jnp.where(member == first, -jnp.inf, s), axis=0, keepdims=True)
        gscore.append(m1 + m2)
    gs = jnp.concatenate(gscore, axis=0)
    gidx = lax.broadcasted_iota(jnp.int32, (N_GROUPS, tm), 0)
    gsel = jnp.zeros((N_GROUPS, tm), F32)
    for _ in range(TOPK_GROUPS):
        _, first = _first_index_of_max([gs], [gidx], N_GROUPS)
        pick = gidx == first
        gsel = jnp.where(pick, 1.0, gsel)
        gs = jnp.where(pick, -jnp.inf, gs)
    cur = [jnp.where(gsel[g:g + 1] > 0.0, slabs[g], -jnp.inf) for g in range(N_GROUPS)]
    eidx = [member + GROUP_SIZE * g for g in range(N_GROUPS)]
    sel = [jnp.zeros((GROUP_SIZE, tm), F32) for _ in range(N_GROUPS)]
    picks = []
    for _ in range(TOP_K):
        _, first = _first_index_of_max(cur, eidx, N_EXPERTS)
        pick = [eidx[g] == first for g in range(N_GROUPS)]
        picks.append((first, pick))
        for g in range(N_GROUPS):
            sel[g] = jnp.where(pick[g], 1.0, sel[g])
            cur[g] = jnp.where(pick[g], -jnp.inf, cur[g])
    w = [jnp.where(sel[g] > 0.0, scores[GROUP_SIZE * g:GROUP_SIZE * (g + 1)], 0.0) for g in range(N_GROUPS)]
    total = functools.reduce(lambda u, v: u + v, [jnp.sum(x, axis=0, keepdims=True) for x in w])
    w = [x / total * ROUTED_SCALE for x in w]

    sel_all = jnp.concatenate(sel, axis=0)
    incl = _dot(sel_all.astype(BF16), tri_ref[...])
    base = base_ref[:, 0:1]
    rank_all = incl - sel_all + base
    rank = [rank_all[GROUP_SIZE * g:GROUP_SIZE * (g + 1)] for g in range(N_GROUPS)]
    cnt = base + jnp.sum(sel_all, axis=1, keepdims=True)
    base_ref[...] = jnp.broadcast_to(cnt, base_ref.shape)
    cnt_ref[...] = jnp.broadcast_to(cnt, cnt_ref.shape)

    def picked(vals, pick):
        parts = [jnp.sum(jnp.where(p, v, 0.0), axis=0, keepdims=True) for p, v in zip(pick, vals)]
        return functools.reduce(lambda u, v: u + v, parts)

    eid_ref[...] = jnp.concatenate([first for first, _ in picks], axis=0)
    rank_ref[...] = jnp.concatenate([picked(rank, pick) for _, pick in picks], axis=0).astype(jnp.int32)
    w_rows = [picked(w, pick) for _, pick in picks] + [jnp.zeros((LANES - TOP_K, tm), F32)]
    wtok_ref[...] = jnp.concatenate(w_rows, axis=0).T


def _router(h2, wr_t, bias_col):
    t = h2.shape[0]
    tm = PROJ_TM
    tri = jnp.asarray(np.triu(np.ones((tm, tm), np.float32)), BF16)
    slots = lambda dt: jax.ShapeDtypeStruct((TOP_K, t), dt)
    return pl.pallas_call(
        _router_kernel,
        out_shape=[slots(jnp.int32), slots(jnp.int32), jax.ShapeDtypeStruct((t, LANES), F32),
                   jax.ShapeDtypeStruct((N_EXPERTS, LANES), F32)],
        grid=(t // tm,),
        in_specs=[pl.BlockSpec((tm, PACKED_W), lambda i: (i, 0)),
                  _full((N_EXPERTS, D_MODEL)), _full((N_EXPERTS, 1)), _full((tm, tm))],
        out_specs=[pl.BlockSpec((TOP_K, tm), lambda i: (0, i)), pl.BlockSpec((TOP_K, tm), lambda i: (0, i)),
                   pl.BlockSpec((tm, LANES), lambda i: (i, 0)), _full((N_EXPERTS, LANES))],
        scratch_shapes=[pltpu.VMEM((N_EXPERTS, LANES), F32)],
        compiler_params=_params("arbitrary"),
        name="router",
    )(h2, wr_t, bias_col, tri)


def _moe_tile(t):
    mean_rows = t * TOP_K // N_EXPERTS
    return int(min(max(pl.next_power_of_2(mean_rows // 2), MOE_TILE_MIN), MOE_TILE_MAX))


def _moe_tiles(t):
    return (t * TOP_K) // _moe_tile(t) + N_EXPERTS


def _moe_plan(eid, rank, cnt, t):
    tile = _moe_tile(t)
    counts = cnt[:, 0].astype(jnp.int32)
    tiles = jnp.maximum((counts + tile - 1) // tile, 1)
    ends = jnp.cumsum(tiles)
    starts = ends - tiles
    experts = jnp.arange(N_EXPERTS, dtype=jnp.int32)
    pos = rank + jnp.sum(jnp.where(eid[:, :, None] == experts, starts * tile, 0), axis=-1)
    tile_ids = jnp.arange(_moe_tiles(t), dtype=jnp.int32)
    owner = tile_ids[:, None] >= ends[None, :]
    tile_expert = jnp.minimum(jnp.sum(owner, axis=-1), N_EXPERTS - 1)
    is_owner = tile_expert[:, None] == experts[None, :]
    start_of = jnp.sum(jnp.where(is_owner, starts, 0), axis=-1)
    count_of = jnp.sum(jnp.where(is_owner, counts, 0), axis=-1)
    real = jnp.clip(count_of - (tile_ids - start_of) * tile, 0, tile)
    real = jnp.where(tile_ids < ends[-1], real, 0)
    return pos.astype(jnp.int32), tile_expert.astype(jnp.int32), real.astype(jnp.int32)


def _sc_workers():
    info = plsc.get_sparse_core_info()
    return info.num_cores, info.num_subcores


def _sc_mesh():
    return plsc.VectorSubcoreMesh(core_axis_name="core", subcore_axis_name="subcore")


def _sc_worker_id(n_cores):
    return lax.axis_index("subcore") * n_cores + lax.axis_index("core")


def _dispatch(h2, pos):
    t = h2.shape[0]
    n_cores, n_sub = _sc_workers()
    ch = SC_ROWS
    per_worker = t // (n_cores * n_sub)
    n_chunks = per_worker // ch
    pos_chunks = pos.reshape(TOP_K, t // ch, ch).transpose(1, 0, 2)

    @functools.partial(
        pl.kernel, mesh=_sc_mesh(),
        out_type=jax.ShapeDtypeStruct((_moe_tiles(t) * _moe_tile(t), PACKED_W), jnp.int32),
        scratch_types=[pltpu.VMEM((TOP_K, ch), jnp.int32), pltpu.VMEM((ch, PACKED_W), jnp.int32),
                       pltpu.SemaphoreType.DMA],
        name="moe_dispatch",
    )
    def run(h_hbm, pos_hbm, xs_hbm, idx_ref, rows_ref, sem):
        first = _sc_worker_id(n_cores) * n_chunks

        @pl.loop(0, n_chunks)
        def _(j):
            c = first + j
            pltpu.sync_copy(pos_hbm.at[c], idx_ref)
            pltpu.sync_copy(h_hbm.at[pl.ds(pl.multiple_of(c * ch, ch), ch)], rows_ref)
            copies = [pltpu.async_copy(rows_ref, xs_hbm.at[idx_ref.at[k]], sem) for k in range(TOP_K)]
            for cp in copies:
                cp.wait()

    return run(h2, pos_chunks)


def _gather_rows(ys, idx):
    n = idx.shape[0]
    n_cores, n_sub = _sc_workers()
    ch = SC_ROWS
    per_worker = n // (n_cores * n_sub)
    n_chunks = per_worker // ch

    @functools.partial(
        pl.kernel, mesh=_sc_mesh(),
        out_type=jax.ShapeDtypeStruct((n, PACKED_W), jnp.int32),
        scratch_types=[pltpu.VMEM((ch,), jnp.int32), pltpu.VMEM((ch, PACKED_W), jnp.int32),
                       pltpu.SemaphoreType.DMA],
        name="moe_gather",
    )
    def run(ys_hbm, idx_hbm, out_hbm, idx_ref, rows_ref, sem):
        first = _sc_worker_id(n_cores) * per_worker

        @pl.loop(0, n_chunks)
        def _(j):
            off = pl.multiple_of(first + j * ch, ch)
            pltpu.sync_copy(idx_hbm.at[pl.ds(off, ch)], idx_ref)
            pltpu.async_copy(ys_hbm.at[idx_ref], rows_ref, sem).wait()
            pltpu.sync_copy(rows_ref, out_hbm.at[pl.ds(off, ch)])

    return run(ys, idx)


def _ffn_kernel(te_ref, real_ref, xs_ref, weg_ref, weu_ref, wed_ref, ys_ref):
    i = pl.program_id(0)
    real = real_ref[i]

    @pl.when(real > 0)
    def _():
        row = lax.broadcasted_iota(jnp.int32, (xs_ref.shape[0], 1), 0)
        x = _unpack_bf16_pairs(jnp.where(row < real, xs_ref[...], 0)).astype(BF16)
        act = _silu(_dot(x, weg_ref[0])) * _dot(x, weu_ref[0])
        ys_ref[...] = _pack_bf16_pairs(_dot(act.astype(BF16), wed_ref[0]))

    @pl.when(real == 0)
    def _():
        ys_ref[...] = jnp.zeros_like(ys_ref)


def _ffn(xs, tile_expert, tile_real, W):
    tile = xs.shape[0] // tile_expert.shape[0]
    wspec = lambda shape: pl.BlockSpec((1,) + shape, lambda i, te, tr: (te[i], 0, 0))
    return pl.pallas_call(
        _ffn_kernel,
        out_shape=jax.ShapeDtypeStruct(xs.shape, jnp.int32),
        grid_spec=pltpu.PrefetchScalarGridSpec(
            num_scalar_prefetch=2,
            grid=(tile_expert.shape[0],),
            in_specs=[pl.BlockSpec((tile, PACKED_W), lambda i, te, tr: (i, 0)),
                      wspec((D_MODEL, EXPERT_FF)), wspec((D_MODEL, EXPERT_FF)), wspec((EXPERT_FF, D_MODEL))],
            out_specs=pl.BlockSpec((tile, PACKED_W), lambda i, te, tr: (i, 0))),
        compiler_params=_params("parallel"),
        name="moe_ffn",
    )(tile_expert, tile_real, xs, W["weg"], W["weu"], W["wed"])


def _combine_kernel(y_ref, h_ref, wtok_ref, x1_ref, mod_ref, wsg_ref, wsu_ref, wsd_ref,
                    g_ref, b_ref, out_ref):
    wt = wtok_ref[...]
    routed = None
    for k in range(TOP_K):
        part = wt[:, k:k + 1] * _unpack_bf16_pairs(y_ref[k])
        routed = part if routed is None else routed + part
    h = _unpack_bf16_pairs(h_ref[...]).astype(BF16)
    shared = _dot((_silu(_dot(h, wsg_ref[...])) * _dot(h, wsu_ref[...])).astype(BF16), wsd_ref[...])
    m = mod_ref[0]
    y2 = DN_ALPHA * x1_ref[...] + m[5:6] * (routed + shared)
    out_ref[...] = _layer_norm(y2, g_ref[...], b_ref[...])


def _combine(ytok, h2, wtok, x1, mod, W, *, latent):
    t = h2.shape[0]
    ct = COMBINE_TM
    steps_per_batch = (DEC_SEQ if latent else t) // ct
    row = lambda w: pl.BlockSpec((ct, w), lambda i: (i, 0))
    return pl.pallas_call(
        _combine_kernel,
        out_shape=jax.ShapeDtypeStruct((t, D_MODEL), F32),
        grid=(t // ct,),
        in_specs=[pl.BlockSpec((TOP_K, ct, PACKED_W), lambda i: (0, i, 0)),
                  row(PACKED_W), row(LANES), row(D_MODEL),
                  pl.BlockSpec((1, 6, D_MODEL), lambda i: (i // steps_per_batch, 0, 0)),
                  _full((D_MODEL, SHARED_FF)), _full((D_MODEL, SHARED_FF)), _full((SHARED_FF, D_MODEL)),
                  _full((1, D_MODEL)), _full((1, D_MODEL))],
        out_specs=row(D_MODEL),
        compiler_params=_params("parallel"),
        name="moe_combine",
    )(ytok, h2, wtok, x1, mod, W["wsg"], W["wsu"], W["wsd"], W["ln2_g"], W["ln2_b"])


def _moe(h2, x1, mod, W, *, latent):
    t = h2.shape[0]
    eid, rank, wtok, cnt = _router(h2, W["wr_t"], W["rbias"])
    pos, tile_expert, tile_real = _moe_plan(eid, rank, cnt, t)
    xs = _dispatch(h2, pos)
    ys = _ffn(xs, tile_expert, tile_real, W)
    ytok = _gather_rows(ys, pos.reshape(TOP_K * t)).reshape(TOP_K, t, PACKED_W)
    return _combine(ytok, h2, wtok, x1, mod, W, latent=latent)


def _rope_partner(n_blocks, block):
    half = block // 2
    i = np.arange(n_blocks * block)
    return np.where((i % block) < half, i + half, i - half)


_QB_NOPE = np.concatenate([np.arange(MLA_NOPE) + (MLA_NOPE + MLA_ROPE) * h for h in range(MLA_HEADS)])
_QB_ROPE = np.concatenate([np.arange(MLA_ROPE) + (MLA_NOPE + MLA_ROPE) * h + MLA_NOPE for h in range(MLA_HEADS)])
_KVB_NOPE = np.concatenate([np.arange(MLA_NOPE) + (MLA_NOPE + MLA_V) * h for h in range(MLA_HEADS)])
_KVB_V = np.concatenate([np.arange(MLA_V) + (MLA_NOPE + MLA_V) * h + MLA_NOPE for h in range(MLA_HEADS)])
_DIFF_PARTNER = _rope_partner(2 * DIFF_W // 32, 32)
_MLA_PARTNER = _rope_partner(MLA_HEADS * MLA_ROPE // 16, 16)
_KR_PARTNER = _rope_partner(MLA_ROPE // 16, 16)


def _layer_weights(l, w_in, mla_qa_g, mla_wq_b, mla_kva_g, mla_wkv_b, w_branch_diff, w_branch_na,
                   w_branch_mla, w_out, ln1_g, ln1_b, ln2_g, ln2_b, w_router, router_bias,
                   w_exp_gate, w_exp_up, w_exp_down, w_sh_gate, w_sh_up, w_sh_down):
    win = w_in[l].astype(BF16)
    wa = win[:, :3072]
    qa_kva = win[:, 3072:3712]
    kr = win[:, 3712:3744]
    kr4 = jnp.tile(kr, (1, LANES // MLA_ROPE))
    krp4 = jnp.tile(kr[:, _KR_PARTNER], (1, LANES // MLA_ROPE))
    wqb = mla_wq_b[l].astype(BF16)
    wq_rope = wqb[:, _QB_ROPE]
    wkvb = mla_wkv_b[l].astype(BF16)
    row = lambda v: v[l].reshape(1, -1).astype(F32)
    return {
        "wa": wa,
        "wp": wa[:, :2 * DIFF_W][:, _DIFF_PARTNER],
        "wm_ctx": jnp.concatenate([qa_kva, kr4], axis=1),
        "wm_lat": jnp.concatenate([qa_kva, kr4, krp4], axis=1),
        "qag": row(mla_qa_g), "kvag": row(mla_kva_g),
        "wqb": jnp.concatenate([wqb[:, _QB_NOPE], wq_rope], axis=1),
        "wqp": wq_rope[:, _MLA_PARTNER],
        "wkvb": jnp.concatenate([wkvb[:, _KVB_NOPE], wkvb[:, _KVB_V]], axis=1),
        "wg": win[:, 3744:],
        "wbd": w_branch_diff[l].astype(BF16), "wbn": w_branch_na[l].astype(BF16),
        "wbm": w_branch_mla[l].astype(BF16), "wo": w_out[l].astype(BF16),
        "ln1_g": row(ln1_g), "ln1_b": row(ln1_b), "ln2_g": row(ln2_g), "ln2_b": row(ln2_b),
        "wr_t": w_router[l].T.astype(BF16), "rbias": router_bias[l].reshape(N_EXPERTS, 1).astype(F32),
        "weg": w_exp_gate[l].astype(BF16), "weu": w_exp_up[l].astype(BF16), "wed": w_exp_down[l].astype(BF16),
        "wsg": w_sh_gate[l].astype(BF16), "wsu": w_sh_up[l].astype(BF16), "wsd": w_sh_down[l].astype(BF16),
    }


def _rope_tables():
    t = jnp.arange(DEC_SEQ)
    pos = [(t // GRID_W).astype(F32), (t % GRID_W).astype(F32)]

    def table(block):
        half = block // 4
        inv = ROPE_BASE ** (-jnp.arange(half, dtype=F32) / half)
        cos, sin = [], []
        for p in pos:
            ang = p[:, None] * inv[None, :]
            cos += [jnp.cos(ang), jnp.cos(ang)]
            sin += [-jnp.sin(ang), jnp.sin(ang)]
        reps = LANES // block
        return (jnp.tile(jnp.concatenate(cos, axis=1), (1, reps)),
                jnp.tile(jnp.concatenate(sin, axis=1), (1, reps)))

    cd, sd = table(DIFF_DH)
    cm, sm = table(MLA_ROPE)
    return {"cd": cd, "sd": sd, "cm": cm, "sm": sm}


def kernel(x_prompt, x_sample, cache_diff_k, cache_diff_v, cache_na_k, cache_na_v, cache_mla_ckv, cache_mla_krope, c, c_ctx, w_mod, b_mod, w_in, diff_lambda, diff_subln_g, na_rpb, mla_qa_g, mla_wq_b, mla_kva_g, mla_wkv_b, w_branch_diff, w_branch_na, w_branch_mla, w_out, ln1_g, ln1_b, ln2_g, ln2_b, w_router, router_bias, w_exp_gate, w_exp_up, w_exp_down, w_sh_gate, w_sh_up, w_sh_down):
    t_ctx = BATCH * SEQ
    t_lat = DEC_BATCH * DEC_SEQ
    cvec = jnp.concatenate([c, c_ctx[None, :], jnp.zeros((MOD_ROWS - DEC_BATCH - 1, D_MODEL), F32)], axis=0)
    mods = _modulation(cvec, w_mod, b_mod).reshape(DEPTH, MOD_ROWS, 6, D_MODEL)
    rope = _rope_tables()
    ck_d = cache_diff_k.reshape(DEC_BATCH, DEPTH, PAST_LEN, DIFF_W)
    cv_d = cache_diff_v.reshape(DEC_BATCH, DEPTH, PAST_LEN, DIFF_W)
    ck_n = cache_na_k.reshape(DEC_BATCH, DEPTH, PAST_LEN, NA_W)
    cv_n = cache_na_v.reshape(DEC_BATCH, DEPTH, PAST_LEN, NA_W)

    xc = x_prompt.reshape(t_ctx, D_MODEL)
    xl = x_sample.reshape(t_lat, D_MODEL)
    states = []
    for l in range(DEPTH):
        W = _layer_weights(l, w_in, mla_qa_g, mla_wq_b, mla_kva_g, mla_wkv_b, w_branch_diff, w_branch_na,
                           w_branch_mla, w_out, ln1_g, ln1_b, ln2_g, ln2_b, w_router, router_bias,
                           w_exp_gate, w_exp_up, w_exp_down, w_sh_gate, w_sh_up, w_sh_down)
        lam = diff_lambda[l].astype(F32)
        subln = diff_subln_g[l].reshape(1, LANES).astype(F32)
        mod_c = mods[l, DEC_BATCH:DEC_BATCH + 1]
        mod_l = mods[l, :DEC_BATCH]

        Wc = dict(W, wm=W["wm_ctx"])
        names = ["dq", "nq", "mqn", "mqr", "kr4", "kn", "vn", "dk", "dv", "nk", "nv", "ckv", "kr"]
        pc = dict(zip(names, _project(xc, mod_c, Wc, None, latent=False)))
        od, on, om = _attn_ctx(pc, lam, subln, l)
        x1, h2 = _merge(xc, mod_c, od, on, om, W, latent=False)
        xc = _moe(h2, x1, mod_c, W, latent=False)
        states.append(pc)

        Wl = dict(W, wm=W["wm_lat"])
        names = ["dq", "dk", "dv", "nq", "nk", "nv", "mqn", "mqr", "kr4", "kn", "vn"]
        pll = dict(zip(names, _project(xl, mod_l, Wl, rope, latent=True)))
        od = _attn_diff_lat(pll, ck_d, cv_d, lam, subln, l)
        on = _attn_na_lat(pll, ck_n, cv_n, _na_bias(na_rpb[l].astype(F32)), l)
        kc, vc, krc = _mla_cache(cache_mla_ckv, cache_mla_krope, W["wkvb"], l)
        om = _attn_mla_lat(pll, kc, vc, krc)
        x1, h2 = _merge(xl, mod_l, od, on, om, W, latent=True)
        xl = _moe(h2, x1, mod_l, W, latent=True)

    def stack(name, shape):
        return jnp.stack([s[name].reshape((BATCH, SEQ) + shape) for s in states], axis=1)

    return (xc.reshape(BATCH, SEQ, D_MODEL), xl.reshape(DEC_BATCH, DEC_SEQ, D_MODEL),
            stack("dk", (DIFF_HEADS, 2 * DIFF_DH)), stack("dv", (DIFF_HEADS, 2 * DIFF_DH)),
            stack("nk", (NA_HEADS, NA_DH)), stack("nv", (NA_HEADS, NA_DH)),
            stack("ckv", (MLA_KV_LORA,)), stack("kr", (MLA_ROPE,)))
```

```python
import functools

import numpy as np
import jax
import jax.numpy as jnp
from jax import lax
from jax.experimental import pallas as pl
from jax.experimental.pallas import tpu as pltpu
from jax.experimental.pallas import tpu_sc as plsc

D_MODEL = 1024
BATCH = 32
SEQ = 256
DEPTH = 2
DEC_BATCH = 8
DEC_SEQ = 2048
PAST_LEN = 512
GRID_W = 64
GRID_ROWS = DEC_SEQ // GRID_W
ROPE_BASE = 10000.0
DIFF_HEADS = 4
DIFF_DH = 64
DIFF_W = 512
NA_HEADS = 8
NA_DH = 64
NA_W = 512
NA_WIN_ROWS = 8
NA_WIN_COLS = 16
MLA_HEADS = 8
MLA_Q_LORA = 384
MLA_KV_LORA = 256
MLA_NOPE = 64
MLA_ROPE = 32
MLA_V = 64
MLA_W = 512
N_EXPERTS = 64
N_GROUPS = 8
GROUP_SIZE = N_EXPERTS // N_GROUPS
TOPK_GROUPS = 4
TOP_K = 8
EXPERT_FF = 256
SHARED_FF = 256
ROUTED_SCALE = 2.5
DN_ALPHA = (2 * DEPTH) ** 0.25
LN_EPS = 1e-5
RMS_EPS = 1e-6

F32 = jnp.float32
BF16 = jnp.bfloat16

LANES = 128
VMEM_LIMIT_BYTES = 56 * 1024 * 1024
NEG_BIG = -1e30
LOG2E = 1.4426950408889634

DIFF_SCALE = DIFF_DH ** -0.5
NA_SCALE = NA_DH ** -0.5
MLA_SCALE = (MLA_NOPE + MLA_ROPE) ** -0.5

PROJ_TM = 512
ATT_TQ = 512
NA_TQ = 256
NA_Q_ROWS = NA_TQ // GRID_W
NA_KEY_ROWS = NA_Q_ROWS + NA_WIN_ROWS
NA_KEYS = NA_KEY_ROWS * GRID_W
MOE_TILE_MIN = 128
MOE_TILE_MAX = 512
SC_ROWS = 128
COMBINE_TM = 256


def _dot(a, b):
    return jnp.dot(a, b, preferred_element_type=F32)


def _dot_nt(a, b):
    return lax.dot_general(a, b, (((1,), (1,)), ((), ())), preferred_element_type=F32)


def _sigmoid(x):
    return 1.0 / (1.0 + jnp.exp(-x))


def _silu(x):
    return x * _sigmoid(x)


def _params(*sem):
    return pltpu.CompilerParams(dimension_semantics=sem, vmem_limit_bytes=VMEM_LIMIT_BYTES)


def _full(shape):
    n = len(shape)
    return pl.BlockSpec(shape, lambda *_: (0,) * n)


def _layer_norm(y, g, b):
    mu = jnp.mean(y, axis=-1, keepdims=True)
    yc = y - mu
    var = jnp.mean(yc * yc, axis=-1, keepdims=True)
    return yc * lax.rsqrt(var + LN_EPS) * g + b


def _rms(x, g):
    return x * lax.rsqrt(jnp.mean(x * x, axis=-1, keepdims=True) + RMS_EPS) * g


HIGH_HALF = -65536
PACKED_W = D_MODEL // 2


def _pack_bf16_pairs(x):
    n = x.shape[1] // 2
    bits = lax.bitcast_convert_type(x.astype(BF16).astype(F32), jnp.int32)
    return lax.shift_right_logical(bits[:, :n], 16) | (bits[:, n:] & HIGH_HALF)


def _unpack_bf16_pairs(w):
    lo = lax.bitcast_convert_type(lax.shift_left(w, 16), F32)
    hi = lax.bitcast_convert_type(w & HIGH_HALF, F32)
    return jnp.concatenate([lo, hi], axis=1)


MOD_ROWS = 16
MOD_TN = 1536


def _mod_kernel(c_ref, w_ref, b_ref, o_ref):
    s = _silu(c_ref[...]).astype(BF16)
    o_ref[0] = _dot(s, w_ref[0].astype(BF16)) + b_ref[0]


def _modulation(cvec, w_mod, b_mod):
    n = 6 * D_MODEL
    return pl.pallas_call(
        _mod_kernel,
        out_shape=jax.ShapeDtypeStruct((DEPTH, MOD_ROWS, n), F32),
        grid=(DEPTH, n // MOD_TN),
        in_specs=[
            pl.BlockSpec((MOD_ROWS, D_MODEL), lambda l, j: (0, 0)),
            pl.BlockSpec((1, D_MODEL, MOD_TN), lambda l, j: (l, 0, j)),
            pl.BlockSpec((1, 1, MOD_TN), lambda l, j: (l, 0, j)),
        ],
        out_specs=pl.BlockSpec((1, MOD_ROWS, MOD_TN), lambda l, j: (l, 0, j)),
        compiler_params=_params("parallel", "parallel"),
        name="modulation",
    )(cvec, w_mod, b_mod.reshape(DEPTH, 1, n))


def _proj_common(x_ref, mod_ref, wa_ref, wm_ref, qag_ref, kvag_ref, wqb_ref, wkvb_ref):
    m = mod_ref[0]
    h = (x_ref[...] * (1.0 + m[1:2]) + m[0:1]).astype(BF16)
    a = _dot(h, wa_ref[...])
    mm = _dot(h, wm_ref[...])
    qan = _rms(mm[:, :MLA_Q_LORA], qag_ref[...]).astype(BF16)
    mq = _dot(qan, wqb_ref[...])
    ckv = _rms(mm[:, MLA_Q_LORA:MLA_Q_LORA + MLA_KV_LORA], kvag_ref[...])
    kv = _dot(ckv.astype(BF16), wkvb_ref[...])
    return h, a, mm, qan, mq, ckv, kv


def _proj_ctx_kernel(x_ref, mod_ref, wa_ref, wm_ref, qag_ref, kvag_ref, wqb_ref, wkvb_ref,
                     dq_ref, nq_ref, mqn_ref, mqr_ref, kr4_ref, kn_ref, vn_ref,
                     dk_ref, dv_ref, nk_ref, nv_ref, ckv_ref, kr_ref):
    _, a, mm, _, mq, ckv, kv = _proj_common(x_ref, mod_ref, wa_ref, wm_ref, qag_ref, kvag_ref,
                                            wqb_ref, wkvb_ref)
    dq_ref[...] = (a[:, 0:512] * DIFF_SCALE).astype(BF16)
    dk_ref[...] = a[:, 512:1024]
    dv_ref[...] = a[:, 1024:1536]
    nq_ref[...] = (a[:, 1536:2048] * NA_SCALE).astype(BF16)
    nk_ref[...] = a[:, 2048:2560]
    nv_ref[...] = a[:, 2560:3072]
    mqn_ref[...] = (mq[:, :512] * MLA_SCALE).astype(BF16)
    mqr_ref[...] = (mq[:, 512:768] * MLA_SCALE).astype(BF16)
    kr4 = mm[:, 640:768]
    kr4_ref[...] = kr4.astype(BF16)
    kr_ref[...] = kr4[:, :MLA_ROPE]
    ckv_ref[...] = ckv
    kn_ref[...] = kv[:, :512].astype(BF16)
    vn_ref[...] = kv[:, 512:].astype(BF16)


def _proj_lat_kernel(x_ref, mod_ref, wa_ref, wm_ref, qag_ref, kvag_ref, wqb_ref, wkvb_ref,
                     wp_ref, wqp_ref, cd_ref, sd_ref, cm_ref, sm_ref,
                     dq_ref, dk_ref, dv_ref, nq_ref, nk_ref, nv_ref,
                     mqn_ref, mqr_ref, kr4_ref, kn_ref, vn_ref):
    h, a, mm, qan, mq, _, kv = _proj_common(x_ref, mod_ref, wa_ref, wm_ref, qag_ref, kvag_ref,
                                            wqb_ref, wkvb_ref)
    ap = _dot(h, wp_ref[...])
    mqp = _dot(qan, wqp_ref[...])
    cd = cd_ref[...]
    sd = sd_ref[...]
    cm = cm_ref[...]
    sm = sm_ref[...]
    for j in range(DIFF_W // LANES):
        lo, hi = LANES * j, LANES * (j + 1)
        dq_ref[:, lo:hi] = ((a[:, lo:hi] * cd + ap[:, lo:hi] * sd) * (DIFF_SCALE * LOG2E)).astype(BF16)
        dk_ref[:, lo:hi] = (a[:, 512 + lo:512 + hi] * cd + ap[:, 512 + lo:512 + hi] * sd).astype(BF16)
    dv_ref[...] = a[:, 1024:1536].astype(BF16)
    nq_ref[...] = (a[:, 1536:2048] * (NA_SCALE * LOG2E)).astype(BF16)
    nk_ref[...] = a[:, 2048:2560].astype(BF16)
    nv_ref[...] = a[:, 2560:3072].astype(BF16)
    mqn_ref[...] = (mq[:, :512] * (MLA_SCALE * LOG2E)).astype(BF16)
    for j in range(2):
        lo, hi = LANES * j, LANES * (j + 1)
        mqr_ref[:, lo:hi] = ((mq[:, 512 + lo:512 + hi] * cm + mqp[:, lo:hi] * sm)
                             * (MLA_SCALE * LOG2E)).astype(BF16)
    kr4_ref[...] = (mm[:, 640:768] * cm + mm[:, 768:896] * sm).astype(BF16)
    kn_ref[...] = kv[:, :512].astype(BF16)
    vn_ref[...] = kv[:, 512:].astype(BF16)


def _project(x, mod, W, rope, *, latent):
    t = x.shape[0]
    tm = PROJ_TM
    tokens_per_batch = DEC_SEQ if latent else t
    steps_per_batch = tokens_per_batch // tm
    row = lambda w: pl.BlockSpec((tm, w), lambda i: (i, 0))
    common_in = [
        row(D_MODEL),
        pl.BlockSpec((1, 6, D_MODEL), lambda i: (i // steps_per_batch, 0, 0)),
        _full(W["wa"].shape), _full(W["wm"].shape), _full((1, MLA_Q_LORA)), _full((1, MLA_KV_LORA)),
        _full(W["wqb"].shape), _full(W["wkvb"].shape),
    ]
    common_args = [x, mod, W["wa"], W["wm"], W["qag"], W["kvag"], W["wqb"], W["wkvb"]]
    bf = lambda w: jax.ShapeDtypeStruct((t, w), BF16)
    f32 = lambda w: jax.ShapeDtypeStruct((t, w), F32)
    if latent:
        tab = pl.BlockSpec((tm, LANES), lambda i: (i % steps_per_batch, 0))
        widths = [512, 512, 512, 512, 512, 512, 512, 256, 128, 512, 512]
        return pl.pallas_call(
            _proj_lat_kernel,
            out_shape=[bf(w) for w in widths],
            grid=(t // tm,),
            in_specs=common_in + [_full(W["wp"].shape), _full(W["wqp"].shape), tab, tab, tab, tab],
            out_specs=[row(w) for w in widths],
            compiler_params=_params("parallel"),
            name="proj_lat",
        )(*common_args, W["wp"], W["wqp"], rope["cd"], rope["sd"], rope["cm"], rope["sm"])
    bf_w = [512, 512, 512, 256, 128, 512, 512]
    f32_w = [512, 512, 512, 512, 256, 32]
    return pl.pallas_call(
        _proj_ctx_kernel,
        out_shape=[bf(w) for w in bf_w] + [f32(w) for w in f32_w],
        grid=(t // tm,),
        in_specs=common_in,
        out_specs=[row(w) for w in bf_w + f32_w],
        compiler_params=_params("parallel"),
        name="proj_ctx",
    )(*common_args)


def _lane_iota():
    return lax.broadcasted_iota(jnp.int32, (1, LANES), 1)


def _softmax_parts(parts, exp):
    m = functools.reduce(jnp.maximum, [jnp.max(s, axis=-1, keepdims=True) for s in parts])
    es = [exp(s - m) for s in parts]
    l = functools.reduce(lambda u, v: u + v, [jnp.sum(e, axis=-1, keepdims=True) for e in es])
    return es, l


def _diff_lambda(lam_ref, layer):
    lp = lam_ref[...]
    lam_init = 0.8 - 0.6 * float(np.exp(-0.3 * layer))
    s1 = jnp.sum(lp[0:1] * lp[1:2], axis=-1, keepdims=True)
    s2 = jnp.sum(lp[2:3] * lp[3:4], axis=-1, keepdims=True)
    return jnp.exp(s1) - jnp.exp(s2) + lam_init, lam_init


def _diff_heads(q_ref, ks, vs, lam_ref, g_ref, o_ref, layer, exp):
    lam, lam_init = _diff_lambda(lam_ref, layer)
    first_map = _lane_iota() < DIFF_DH
    g = g_ref[...]
    for h in range(DIFF_HEADS):
        hs = slice(LANES * h, LANES * (h + 1))
        q = q_ref[:, hs]
        q1 = jnp.where(first_map, q, jnp.zeros_like(q))
        q2 = jnp.where(first_map, jnp.zeros_like(q), q)
        kk = [k[:, hs].astype(BF16) for k in ks]
        e1, l1 = _softmax_parts([_dot_nt(q1, k) for k in kk], exp)
        e2, l2 = _softmax_parts([_dot_nt(q2, k) for k in kk], exp)
        c1 = 1.0 / l1
        c2 = lam / l2
        o = None
        for a1, a2, v in zip(e1, e2, vs):
            part = _dot((a1 * c1 - a2 * c2).astype(BF16), v[:, hs].astype(BF16))
            o = part if o is None else o + part
        o = _rms(o, g) * (1.0 - lam_init)
        o_ref[:, hs] = o.astype(BF16)


def _pair_heads(q_of, k_of, v_of, bias_of, o_ref, n_pairs, exp):
    first = _lane_iota() < 64
    for j in range(n_pairs):
        ps = slice(LANES * j, LANES * (j + 1))
        ks = k_of(j)
        vs = v_of(j)
        outs = []
        for hh in range(2):
            q = q_of(j, hh)
            ss = [_dot_nt(q, k) for k in ks]
            bs = bias_of(2 * j + hh)
            ss = [s if b is None else s + b for s, b in zip(ss, bs)]
            es, l = _softmax_parts(ss, exp)
            o = None
            for e, v in zip(es, vs):
                part = _dot(e.astype(BF16), v)
                o = part if o is None else o + part
            outs.append(o / l)
        o_ref[:, ps] = jnp.where(first, outs[0], outs[1]).astype(BF16)


def _na_q(q_ref):
    first = _lane_iota() < NA_DH

    def q_of(j, hh):
        q = q_ref[:, LANES * j:LANES * (j + 1)]
        keep = first if hh == 0 else jnp.logical_not(first)
        return jnp.where(keep, q, jnp.zeros_like(q))
    return q_of


def _mla_q(qn_ref, qr_ref):
    lane = _lane_iota()
    first = lane < MLA_NOPE

    def q_of(j, hh):
        h = 2 * j + hh
        qn = qn_ref[:, LANES * j:LANES * (j + 1)]
        keep = first if hh == 0 else jnp.logical_not(first)
        qn = jnp.where(keep, qn, jnp.zeros_like(qn))
        qr = qr_ref[:, LANES * (h // 4):LANES * (h // 4 + 1)]
        qr = jnp.where((lane // MLA_ROPE) == (h % 4), qr, jnp.zeros_like(qr))
        return jnp.concatenate([qn, qr], axis=1)
    return q_of


def _attn_ctx_kernel(layer, dq_ref, dk_ref, dv_ref, nq_ref, nk_ref, nv_ref,
                     mqn_ref, mqr_ref, kn_ref, kr4_ref, vn_ref, lam_ref, g_ref,
                     od_ref, on_ref, om_ref):
    _diff_heads(dq_ref, [dk_ref], [dv_ref], lam_ref, g_ref, od_ref, layer, jnp.exp)
    none = lambda h: [None]
    pair = lambda j: slice(LANES * j, LANES * (j + 1))
    _pair_heads(_na_q(nq_ref),
                lambda j: [nk_ref[:, pair(j)].astype(BF16)],
                lambda j: [nv_ref[:, pair(j)].astype(BF16)],
                none, on_ref, NA_HEADS // 2, jnp.exp)
    kr4 = kr4_ref[...]
    _pair_heads(_mla_q(mqn_ref, mqr_ref),
                lambda j: [jnp.concatenate([kn_ref[:, pair(j)], kr4], axis=1)],
                lambda j: [vn_ref[:, pair(j)]],
                none, om_ref, MLA_HEADS // 2, jnp.exp)


def _attn_ctx(p, lam, g, layer):
    t = p["dq"].shape[0]
    row = lambda w: pl.BlockSpec((SEQ, w), lambda b: (b, 0))
    names = ["dq", "dk", "dv", "nq", "nk", "nv", "mqn", "mqr", "kn", "kr4", "vn"]
    out = jax.ShapeDtypeStruct((t, 512), BF16)
    return pl.pallas_call(
        functools.partial(_attn_ctx_kernel, layer),
        out_shape=[out, out, out],
        grid=(t // SEQ,),
        in_specs=[row(p[n].shape[1]) for n in names] + [_full((4, DIFF_DH)), _full((1, LANES))],
        out_specs=[row(512)] * 3,
        compiler_params=_params("parallel"),
        name="attn_ctx",
    )(*[p[n] for n in names], lam, g)


def _cache_spec(width, layer):
    return pl.BlockSpec((None, None, PAST_LEN, width), lambda b, q: (b, layer, 0, 0))


def _batch_spec(width):
    return pl.BlockSpec((None, DEC_SEQ, width), lambda b, q: (b, 0, 0))


def _qtile_spec(width, tq):
    steps = DEC_SEQ // tq
    return pl.BlockSpec((tq, width), lambda b, q: (b * steps + q, 0))


def _attn_diff_lat_kernel(layer, q_ref, kc_ref, vc_ref, kn_ref, vn_ref, lam_ref, g_ref, o_ref):
    _diff_heads(q_ref, [kc_ref, kn_ref], [vc_ref, vn_ref], lam_ref, g_ref, o_ref, layer, jnp.exp2)


def _attn_diff_lat(p, cache_k, cache_v, lam, g, layer):
    t = p["dq"].shape[0]
    b3 = lambda a: a.reshape(DEC_BATCH, DEC_SEQ, a.shape[1])
    return pl.pallas_call(
        functools.partial(_attn_diff_lat_kernel, layer),
        out_shape=jax.ShapeDtypeStruct((t, DIFF_W), BF16),
        grid=(DEC_BATCH, DEC_SEQ // ATT_TQ),
        in_specs=[_qtile_spec(DIFF_W, ATT_TQ), _cache_spec(DIFF_W, layer), _cache_spec(DIFF_W, layer),
                  _batch_spec(DIFF_W), _batch_spec(DIFF_W),
                  pl.BlockSpec((4, DIFF_DH), lambda b, q: (0, 0)),
                  pl.BlockSpec((1, LANES), lambda b, q: (0, 0))],
        out_specs=_qtile_spec(DIFF_W, ATT_TQ),
        compiler_params=_params("parallel", "parallel"),
        name="attn_diff_lat",
    )(p["dq"], cache_k, cache_v, b3(p["dk"]), b3(p["dv"]), lam, g)


def _na_key_start(q):
    return jnp.clip(q * NA_Q_ROWS - NA_WIN_ROWS // 2, 0, GRID_ROWS - NA_KEY_ROWS)


def _attn_na_lat_kernel(q_ref, kc_ref, vc_ref, kn_ref, vn_ref, bias_ref, o_ref):
    start = pl.multiple_of(_na_key_start(pl.program_id(1)) * GRID_W, GRID_W)
    kw = kn_ref[pl.ds(start, NA_KEYS), :]
    vw = vn_ref[pl.ds(start, NA_KEYS), :]
    pair = lambda j: slice(LANES * j, LANES * (j + 1))
    _pair_heads(_na_q(q_ref),
                lambda j: [kc_ref[:, pair(j)].astype(BF16), kw[:, pair(j)]],
                lambda j: [vc_ref[:, pair(j)].astype(BF16), vw[:, pair(j)]],
                lambda h: [None, bias_ref[h]],
                o_ref, NA_HEADS // 2, jnp.exp2)


def _na_bias_tables():
    n_blocks = DEC_SEQ // NA_TQ
    qr = np.arange(NA_Q_ROWS)
    kr = np.arange(NA_KEY_ROWS)
    row_sel, row_ok = [], []
    for qb in range(n_blocks):
        ks = int(np.clip(qb * NA_Q_ROWS - NA_WIN_ROWS // 2, 0, GRID_ROWS - NA_KEY_ROWS))
        r = qb * NA_Q_ROWS + qr
        r0 = np.clip(r - NA_WIN_ROWS // 2, 0, GRID_ROWS - NA_WIN_ROWS)
        krow = ks + kr
        ok = (krow[None, :] >= r0[:, None]) & (krow[None, :] < r0[:, None] + NA_WIN_ROWS)
        off = krow[None, :] - r[:, None] + NA_WIN_ROWS - 1
        sel = (off[:, :, None] == np.arange(2 * NA_WIN_ROWS - 1)) & ok[:, :, None]
        row_sel.append(sel.astype(np.float32))
        row_ok.append(ok)
    kinds, kind_of_block = [], []
    for qb in range(n_blocks):
        for n, other in enumerate(kinds):
            if np.array_equal(row_sel[qb], row_sel[other]):
                kind_of_block.append(n)
                break
        else:
            kind_of_block.append(len(kinds))
            kinds.append(qb)
    c = np.arange(GRID_W)
    c0 = np.clip(c - NA_WIN_COLS // 2, 0, GRID_W - NA_WIN_COLS)
    col_ok = (c[None, :] >= c0[:, None]) & (c[None, :] < c0[:, None] + NA_WIN_COLS)
    coff = c[None, :] - c[:, None] + NA_WIN_COLS - 1
    col_sel = ((coff[:, :, None] == np.arange(2 * NA_WIN_COLS - 1)) & col_ok[:, :, None]).astype(np.float32)
    rsel = np.stack([row_sel[qb] for qb in kinds])
    valid = np.stack([row_ok[qb][:, None, :, None] & col_ok[None, :, None, :] for qb in kinds])
    valid = valid.reshape(len(kinds), 1, NA_TQ, NA_KEYS)
    return rsel, col_sel, valid, kind_of_block


_NA_ROW_SEL, _NA_COL_SEL, _NA_VALID, _NA_KIND_OF_BLOCK = _na_bias_tables()


def _na_bias(rpb):
    hp = lax.Precision.HIGHEST
    cols = jnp.einsum("hij,ckj->hick", rpb, _NA_COL_SEL, precision=hp)
    b = jnp.einsum("nqri,hick->nhqcrk", _NA_ROW_SEL, cols, precision=hp)
    b = b.reshape(_NA_ROW_SEL.shape[0], NA_HEADS, NA_TQ, NA_KEYS)
    return jnp.where(_NA_VALID, b * LOG2E, NEG_BIG).astype(F32)


def _na_kind(q):
    return (q > 0).astype(jnp.int32) + (q == DEC_SEQ // NA_TQ - 1).astype(jnp.int32)


def _attn_na_lat(p, cache_k, cache_v, bias, layer):
    assert _NA_KIND_OF_BLOCK == [0] + [1] * (DEC_SEQ // NA_TQ - 2) + [2]
    t = p["nq"].shape[0]
    b3 = lambda a: a.reshape(DEC_BATCH, DEC_SEQ, a.shape[1])
    return pl.pallas_call(
        _attn_na_lat_kernel,
        out_shape=jax.ShapeDtypeStruct((t, NA_W), BF16),
        grid=(DEC_BATCH, DEC_SEQ // NA_TQ),
        in_specs=[_qtile_spec(NA_W, NA_TQ), _cache_spec(NA_W, layer), _cache_spec(NA_W, layer),
                  _batch_spec(NA_W), _batch_spec(NA_W),
                  pl.BlockSpec((None, NA_HEADS, NA_TQ, NA_KEYS), lambda b, q: (_na_kind(q), 0, 0, 0))],
        out_specs=_qtile_spec(NA_W, NA_TQ),
        compiler_params=_params("parallel", "parallel"),
        name="attn_na_lat",
    )(p["nq"], cache_k, cache_v, b3(p["nk"]), b3(p["nv"]), bias)


def _mla_cache_kernel(ckv_ref, kr_ref, wkvb_ref, rep_ref, kc_ref, vc_ref, krc_ref):
    kv = _dot(ckv_ref[...].astype(BF16), wkvb_ref[...])
    kc_ref[...] = kv[:, :512].astype(BF16)
    vc_ref[...] = kv[:, 512:].astype(BF16)
    krc_ref[...] = _dot(kr_ref[...].astype(BF16), rep_ref[...]).astype(BF16)


def _mla_cache(cache_ckv, cache_kr, wkvb, layer):
    rep = jnp.asarray(np.tile(np.eye(MLA_ROPE, dtype=np.float32), (1, LANES // MLA_ROPE)), BF16)
    spec_in = lambda w: pl.BlockSpec((None, None, PAST_LEN, w), lambda b: (b, layer, 0, 0))
    spec_out = lambda w: pl.BlockSpec((None, PAST_LEN, w), lambda b: (b, 0, 0))
    shp = lambda w: jax.ShapeDtypeStruct((DEC_BATCH, PAST_LEN, w), BF16)
    return pl.pallas_call(
        _mla_cache_kernel,
        out_shape=[shp(512), shp(512), shp(LANES)],
        grid=(DEC_BATCH,),
        in_specs=[spec_in(MLA_KV_LORA), spec_in(MLA_ROPE), _full(wkvb.shape), _full(rep.shape)],
        out_specs=[spec_out(512), spec_out(512), spec_out(LANES)],
        compiler_params=_params("parallel"),
        name="mla_cache",
    )(cache_ckv, cache_kr, wkvb, rep)


def _attn_mla_lat_kernel(qn_ref, qr_ref, kc_ref, krc_ref, vc_ref, kn_ref, krn_ref, vn_ref, o_ref):
    pair = lambda j: slice(LANES * j, LANES * (j + 1))
    krc = krc_ref[...]
    krn = krn_ref[...]
    _pair_heads(_mla_q(qn_ref, qr_ref),
                lambda j: [jnp.concatenate([kc_ref[:, pair(j)], krc], axis=1),
                           jnp.concatenate([kn_ref[:, pair(j)], krn], axis=1)],
                lambda j: [vc_ref[:, pair(j)], vn_ref[:, pair(j)]],
                lambda h: [None, None],
                o_ref, MLA_HEADS // 2, jnp.exp2)


def _attn_mla_lat(p, kc, vc, krc):
    t = p["mqn"].shape[0]
    b3 = lambda a: a.reshape(DEC_BATCH, DEC_SEQ, a.shape[1])
    cspec = lambda w: pl.BlockSpec((None, PAST_LEN, w), lambda b, q: (b, 0, 0))
    return pl.pallas_call(
        _attn_mla_lat_kernel,
        out_shape=jax.ShapeDtypeStruct((t, MLA_W), BF16),
        grid=(DEC_BATCH, DEC_SEQ // ATT_TQ),
        in_specs=[_qtile_spec(512, ATT_TQ), _qtile_spec(256, ATT_TQ), cspec(512), cspec(LANES), cspec(512),
                  _batch_spec(512), _batch_spec(LANES), _batch_spec(512)],
        out_specs=_qtile_spec(MLA_W, ATT_TQ),
        compiler_params=_params("parallel", "parallel"),
        name="attn_mla_lat",
    )(p["mqn"], p["mqr"], kc, krc, vc, b3(p["kn"]), b3(p["kr4"]), b3(p["vn"]))


def _merge_kernel(x_ref, mod_ref, od_ref, on_ref, om_ref, wg_ref, wbd_ref, wbn_ref, wbm_ref, wo_ref,
                  g_ref, b_ref, x1_ref, h2_ref):
    x = x_ref[...]
    m = mod_ref[0]
    h = (x * (1.0 + m[1:2]) + m[0:1]).astype(BF16)
    gates = _dot(h, wg_ref[...])
    mix = (_sigmoid(gates[:, 0:1024]) * _dot(od_ref[...], wbd_ref[...])
           + _sigmoid(gates[:, 1024:2048]) * _dot(on_ref[...], wbn_ref[...])
           + _sigmoid(gates[:, 2048:3072]) * _dot(om_ref[...], wbm_ref[...]))
    out = _dot(mix.astype(BF16), wo_ref[...])
    x1 = _layer_norm(DN_ALPHA * x + m[2:3] * out, g_ref[...], b_ref[...])
    x1_ref[...] = x1
    h2_ref[...] = _pack_bf16_pairs(x1 * (1.0 + m[4:5]) + m[3:4])


def _merge(x, mod, od, on, om, W, *, latent):
    t = x.shape[0]
    tm = PROJ_TM
    steps_per_batch = (DEC_SEQ if latent else t) // tm
    row = lambda w: pl.BlockSpec((tm, w), lambda i: (i, 0))
    return pl.pallas_call(
        _merge_kernel,
        out_shape=[jax.ShapeDtypeStruct((t, D_MODEL), F32), jax.ShapeDtypeStruct((t, PACKED_W), jnp.int32)],
        grid=(t // tm,),
        in_specs=[row(D_MODEL),
                  pl.BlockSpec((1, 6, D_MODEL), lambda i: (i // steps_per_batch, 0, 0)),
                  row(512), row(512), row(512),
                  _full(W["wg"].shape), _full(W["wbd"].shape), _full(W["wbn"].shape),
                  _full(W["wbm"].shape), _full(W["wo"].shape),
                  _full((1, D_MODEL)), _full((1, D_MODEL))],
        out_specs=[row(D_MODEL), row(PACKED_W)],
        compiler_params=_params("parallel"),
        name="merge",
    )(x, mod, od, on, om, W["wg"], W["wbd"], W["wbn"], W["wbm"], W["wo"], W["ln1_g"], W["ln1_b"])


def _first_index_of_max(vals, idx, sentinel):
    mx = functools.reduce(jnp.maximum, [jnp.max(v, axis=0, keepdims=True) for v in vals])
    cand = [jnp.min(jnp.where(v == mx, i, sentinel), axis=0, keepdims=True) for v, i in zip(vals, idx)]
    return mx, functools.reduce(jnp.minimum, cand)


def _router_kernel(h_ref, wr_ref, bias_ref, tri_ref, eid_ref, rank_ref, wtok_ref, cnt_ref, base_ref):
    tm = h_ref.shape[0]

    @pl.when(pl.program_id(0) == 0)
    def _():
        base_ref[...] = jnp.zeros_like(base_ref)

    logits = _dot_nt(wr_ref[...], _unpack_bf16_pairs(h_ref[...]).astype(BF16))
    scores = _sigmoid(logits)
    biased = scores + bias_ref[...]
    member = lax.broadcasted_iota(jnp.int32, (GROUP_SIZE, tm), 0)
    slabs = [biased[GROUP_SIZE * g:GROUP_SIZE * (g + 1)] for g in range(N_GROUPS)]
    gscore = []
    for s in slabs:
        m1, first = _first_index_of_max([s], [member], GROUP_SIZE)
        m2 = jnp.max(jnp.where(member == first, -jnp.inf, s), axis=0, keepdims=True)
        gscore.append(m1 + m2)
    gs = jnp.concatenate(gscore, axis=0)
    gidx = lax.broadcasted_iota(jnp.int32, (N_GROUPS, tm), 0)
    gsel = jnp.zeros((N_GROUPS, tm), F32)
    for _ in range(TOPK_GROUPS):
        _, first = _first_index_of_max([gs], [gidx], N_GROUPS)
        pick = gidx == first
        gsel = jnp.where(pick, 1.0, gsel)
        gs = jnp.where(pick, -jnp.inf, gs)
    cur = [jnp.where(gsel[g:g + 1] > 0.0, slabs[g], -jnp.inf) for g in range(N_GROUPS)]
    eidx = [member + GROUP_SIZE * g for g in range(N_GROUPS)]
    sel = [jnp.zeros((GROUP_SIZE, tm), F32) for _ in range(N_GROUPS)]
    picks = []
    for _ in range(TOP_K):
        _, first = _first_index_of_max(cur, eidx, N_EXPERTS)
        pick = [eidx[g] == first for g in range(N_GROUPS)]
        picks.append((first, pick))
        for g in range(N_GROUPS):
            sel[g] = jnp.where(pick[g], 1.0, sel[g])
            cur[g] = jnp.where(pick[g], -jnp.inf, cur[g])
    w = [jnp.where(sel[g] > 0.0, scores[GROUP_SIZE * g:GROUP_SIZE * (g + 1)], 0.0) for g in range(N_GROUPS)]
    total = functools.reduce(lambda u, v: u + v, [jnp.sum(x, axis=0, keepdims=True) for x in w])
    w = [x / total * ROUTED_SCALE for x in w]

    sel_all = jnp.concatenate(sel, axis=0)
    incl = _dot(sel_all.astype(BF16), tri_ref[...])
    base = base_ref[:, 0:1]
    rank_all = incl - sel_all + base
    rank = [rank_all[GROUP_SIZE * g:GROUP_SIZE * (g + 1)] for g in range(N_GROUPS)]
    cnt = base + jnp.sum(sel_all, axis=1, keepdims=True)
    base_ref[...] = jnp.broadcast_to(cnt, base_ref.shape)
    cnt_ref[...] = jnp.broadcast_to(cnt, cnt_ref.shape)

    def picked(vals, pick):
        parts = [jnp.sum(jnp.where(p, v, 0.0), axis=0, keepdims=True) for p, v in zip(pick, vals)]
        return functools.reduce(lambda u, v: u + v, parts)

    eid_ref[...] = jnp.concatenate([first for first, _ in picks], axis=0)
    rank_ref[...] = jnp.concatenate([picked(rank, pick) for _, pick in picks], axis=0).astype(jnp.int32)
    w_rows = [picked(w, pick) for _, pick in picks] + [jnp.zeros((LANES - TOP_K, tm), F32)]
    wtok_ref[...] = jnp.concatenate(w_rows, axis=0).T


def _router(h2, wr_t, bias_col):
    t = h2.shape[0]
    tm = PROJ_TM
    tri = jnp.asarray(np.triu(np.ones((tm, tm), np.float32)), BF16)
    slots = lambda dt: jax.ShapeDtypeStruct((TOP_K, t), dt)
    return pl.pallas_call(
        _router_kernel,
        out_shape=[slots(jnp.int32), slots(jnp.int32), jax.ShapeDtypeStruct((t, LANES), F32),
                   jax.ShapeDtypeStruct((N_EXPERTS, LANES), F32)],
        grid=(t // tm,),
        in_specs=[pl.BlockSpec((tm, PACKED_W), lambda i: (i, 0)),
                  _full((N_EXPERTS, D_MODEL)), _full((N_EXPERTS, 1)), _full((tm, tm))],
        out_specs=[pl.BlockSpec((TOP_K, tm), lambda i: (0, i)), pl.BlockSpec((TOP_K, tm), lambda i: (0, i)),
                   pl.BlockSpec((tm, LANES), lambda i: (i, 0)), _full((N_EXPERTS, LANES))],
        scratch_shapes=[pltpu.VMEM((N_EXPERTS, LANES), F32)],
        compiler_params=_params("arbitrary"),
        name="router",
    )(h2, wr_t, bias_col, tri)


def _moe_tile(t):
    mean_rows = t * TOP_K // N_EXPERTS
    return int(min(max(pl.next_power_of_2(mean_rows // 2), MOE_TILE_MIN), MOE_TILE_MAX))


def _moe_tiles(t):
    return (t * TOP_K) // _moe_tile(t) + N_EXPERTS


def _moe_plan(eid, rank, cnt, t):
    tile = _moe_tile(t)
    counts = cnt[:, 0].astype(jnp.int32)
    tiles = jnp.maximum((counts + tile - 1) // tile, 1)
    ends = jnp.cumsum(tiles)
    starts = ends - tiles
    experts = jnp.arange(N_EXPERTS, dtype=jnp.int32)
    pos = rank + jnp.sum(jnp.where(eid[:, :, None] == experts, starts * tile, 0), axis=-1)
    tile_ids = jnp.arange(_moe_tiles(t), dtype=jnp.int32)
    owner = tile_ids[:, None] >= ends[None, :]
    tile_expert = jnp.minimum(jnp.sum(owner, axis=-1), N_EXPERTS - 1)
    is_owner = tile_expert[:, None] == experts[None, :]
    start_of = jnp.sum(jnp.where(is_owner, starts, 0), axis=-1)
    count_of = jnp.sum(jnp.where(is_owner, counts, 0), axis=-1)
    real = jnp.clip(count_of - (tile_ids - start_of) * tile, 0, tile)
    real = jnp.where(tile_ids < ends[-1], real, 0)
    return pos.astype(jnp.int32), tile_expert.astype(jnp.int32), real.astype(jnp.int32)


def _sc_workers():
    info = plsc.get_sparse_core_info()
    return info.num_cores, info.num_subcores


def _sc_mesh():
    return plsc.VectorSubcoreMesh(core_axis_name="core", subcore_axis_name="subcore")


def _sc_worker_id(n_cores):
    return lax.axis_index("subcore") * n_cores + lax.axis_index("core")


def _dispatch(h2, pos):
    t = h2.shape[0]
    n_cores, n_sub = _sc_workers()
    ch = SC_ROWS
    per_worker = t // (n_cores * n_sub)
    n_chunks = per_worker // ch
    pos_chunks = pos.reshape(TOP_K, t // ch, ch).transpose(1, 0, 2)

    @functools.partial(
        pl.kernel, mesh=_sc_mesh(),
        out_type=jax.ShapeDtypeStruct((_moe_tiles(t) * _moe_tile(t), PACKED_W), jnp.int32),
        scratch_types=[pltpu.VMEM((TOP_K, ch), jnp.int32), pltpu.VMEM((ch, PACKED_W), jnp.int32),
                       pltpu.SemaphoreType.DMA],
        name="moe_dispatch",
    )
    def run(h_hbm, pos_hbm, xs_hbm, idx_ref, rows_ref, sem):
        first = _sc_worker_id(n_cores) * n_chunks

        @pl.loop(0, n_chunks)
        def _(j):
            c = first + j
            pltpu.sync_copy(pos_hbm.at[c], idx_ref)
            pltpu.sync_copy(h_hbm.at[pl.ds(pl.multiple_of(c * ch, ch), ch)], rows_ref)
            copies = [pltpu.async_copy(rows_ref, xs_hbm.at[idx_ref.at[k]], sem) for k in range(TOP_K)]
            for cp in copies:
                cp.wait()

    return run(h2, pos_chunks)


def _gather_rows(ys, idx):
    n = idx.shape[0]
    n_cores, n_sub = _sc_workers()
    ch = SC_ROWS
    per_worker = n // (n_cores * n_sub)
    n_chunks = per_worker // ch

    @functools.partial(
        pl.kernel, mesh=_sc_mesh(),
        out_type=jax.ShapeDtypeStruct((n, PACKED_W), jnp.int32),
        scratch_types=[pltpu.VMEM((ch,), jnp.int32), pltpu.VMEM((ch, PACKED_W), jnp.int32),
                       pltpu.SemaphoreType.DMA],
        name="moe_gather",
    )
    def run(ys_hbm, idx_hbm, out_hbm, idx_ref, rows_ref, sem):
        first = _sc_worker_id(n_cores) * per_worker

        @pl.loop(0, n_chunks)
        def _(j):
            off = pl.multiple_of(first + j * ch, ch)
            pltpu.sync_copy(idx_hbm.at[pl.ds(off, ch)], idx_ref)
            pltpu.async_copy(ys_hbm.at[idx_ref], rows_ref, sem).wait()
            pltpu.sync_copy(rows_ref, out_hbm.at[pl.ds(off, ch)])

    return run(ys, idx)


def _ffn_kernel(te_ref, real_ref, xs_ref, weg_ref, weu_ref, wed_ref, ys_ref):
    i = pl.program_id(0)
    real = real_ref[i]

    @pl.when(real > 0)
    def _():
        row = lax.broadcasted_iota(jnp.int32, (xs_ref.shape[0], 1), 0)
        x = _unpack_bf16_pairs(jnp.where(row < real, xs_ref[...], 0)).astype(BF16)
        act = _silu(_dot(x, weg_ref[0])) * _dot(x, weu_ref[0])
        ys_ref[...] = _pack_bf16_pairs(_dot(act.astype(BF16), wed_ref[0]))

    @pl.when(real == 0)
    def _():
        ys_ref[...] = jnp.zeros_like(ys_ref)


def _ffn(xs, tile_expert, tile_real, W):
    tile = xs.shape[0] // tile_expert.shape[0]
    wspec = lambda shape: pl.BlockSpec((1,) + shape, lambda i, te, tr: (te[i], 0, 0))
    return pl.pallas_call(
        _ffn_kernel,
        out_shape=jax.ShapeDtypeStruct(xs.shape, jnp.int32),
        grid_spec=pltpu.PrefetchScalarGridSpec(
            num_scalar_prefetch=2,
            grid=(tile_expert.shape[0],),
            in_specs=[pl.BlockSpec((tile, PACKED_W), lambda i, te, tr: (i, 0)),
                      wspec((D_MODEL, EXPERT_FF)), wspec((D_MODEL, EXPERT_FF)), wspec((EXPERT_FF, D_MODEL))],
            out_specs=pl.BlockSpec((tile, PACKED_W), lambda i, te, tr: (i, 0))),
        compiler_params=_params("parallel"),
        name="moe_ffn",
    )(tile_expert, tile_real, xs, W["weg"], W["weu"], W["wed"])


def _combine_kernel(y_ref, h_ref, wtok_ref, x1_ref, mod_ref, wsg_ref, wsu_ref, wsd_ref,
                    g_ref, b_ref, out_ref):
    wt = wtok_ref[...]
    routed = None
    for k in range(TOP_K):
        part = wt[:, k:k + 1] * _unpack_bf16_pairs(y_ref[k])
        routed = part if routed is None else routed + part
    h = _unpack_bf16_pairs(h_ref[...]).astype(BF16)
    shared = _dot((_silu(_dot(h, wsg_ref[...])) * _dot(h, wsu_ref[...])).astype(BF16), wsd_ref[...])
    m = mod_ref[0]
    y2 = DN_ALPHA * x1_ref[...] + m[5:6] * (routed + shared)
    out_ref[...] = _layer_norm(y2, g_ref[...], b_ref[...])


def _combine(ytok, h2, wtok, x1, mod, W, *, latent):
    t = h2.shape[0]
    ct = COMBINE_TM
    steps_per_batch = (DEC_SEQ if latent else t) // ct
    row = lambda w: pl.BlockSpec((ct, w), lambda i: (i, 0))
    return pl.pallas_call(
        _combine_kernel,
        out_shape=jax.ShapeDtypeStruct((t, D_MODEL), F32),
        grid=(t // ct,),
        in_specs=[pl.BlockSpec((TOP_K, ct, PACKED_W), lambda i: (0, i, 0)),
                  row(PACKED_W), row(LANES), row(D_MODEL),
                  pl.BlockSpec((1, 6, D_MODEL), lambda i: (i // steps_per_batch, 0, 0)),
                  _full((D_MODEL, SHARED_FF)), _full((D_MODEL, SHARED_FF)), _full((SHARED_FF, D_MODEL)),
                  _full((1, D_MODEL)), _full((1, D_MODEL))],
        out_specs=row(D_MODEL),
        compiler_params=_params("parallel"),
        name="moe_combine",
    )(ytok, h2, wtok, x1, mod, W["wsg"], W["wsu"], W["wsd"], W["ln2_g"], W["ln2_b"])


def _moe(h2, x1, mod, W, *, latent):
    t = h2.shape[0]
    eid, rank, wtok, cnt = _router(h2, W["wr_t"], W["rbias"])
    pos, tile_expert, tile_real = _moe_plan(eid, rank, cnt, t)
    xs = _dispatch(h2, pos)
    ys = _ffn(xs, tile_expert, tile_real, W)
    ytok = _gather_rows(ys, pos.reshape(TOP_K * t)).reshape(TOP_K, t, PACKED_W)
    return _combine(ytok, h2, wtok, x1, mod, W, latent=latent)


def _rope_partner(n_blocks, block):
    half = block // 2
    i = np.arange(n_blocks * block)
    return np.where((i % block) < half, i + half, i - half)


_QB_NOPE = np.concatenate([np.arange(MLA_NOPE) + (MLA_NOPE + MLA_ROPE) * h for h in range(MLA_HEADS)])
_QB_ROPE = np.concatenate([np.arange(MLA_ROPE) + (MLA_NOPE + MLA_ROPE) * h + MLA_NOPE for h in range(MLA_HEADS)])
_KVB_NOPE = np.concatenate([np.arange(MLA_NOPE) + (MLA_NOPE + MLA_V) * h for h in range(MLA_HEADS)])
_KVB_V = np.concatenate([np.arange(MLA_V) + (MLA_NOPE + MLA_V) * h + MLA_NOPE for h in range(MLA_HEADS)])
_DIFF_PARTNER = _rope_partner(2 * DIFF_W // 32, 32)
_MLA_PARTNER = _rope_partner(MLA_HEADS * MLA_ROPE // 16, 16)
_KR_PARTNER = _rope_partner(MLA_ROPE // 16, 16)


def _layer_weights(l, w_in, mla_qa_g, mla_wq_b, mla_kva_g, mla_wkv_b, w_branch_diff, w_branch_na,
                   w_branch_mla, w_out, ln1_g, ln1_b, ln2_g, ln2_b, w_router, router_bias,
                   w_exp_gate, w_exp_up, w_exp_down, w_sh_gate, w_sh_up, w_sh_down):
    win = w_in[l].astype(BF16)
    wa = win[:, :3072]
    qa_kva = win[:, 3072:3712]
    kr = win[:, 3712:3744]
    kr4 = jnp.tile(kr, (1, LANES // MLA_ROPE))
    krp4 = jnp.tile(kr[:, _KR_PARTNER], (1, LANES // MLA_ROPE))
    wqb = mla_wq_b[l].astype(BF16)
    wq_rope = wqb[:, _QB_ROPE]
    wkvb = mla_wkv_b[l].astype(BF16)
    row = lambda v: v[l].reshape(1, -1).astype(F32)
    return {
        "wa": wa,
        "wp": wa[:, :2 * DIFF_W][:, _DIFF_PARTNER],
        "wm_ctx": jnp.concatenate([qa_kva, kr4], axis=1),
        "wm_lat": jnp.concatenate([qa_kva, kr4, krp4], axis=1),
        "qag": row(mla_qa_g), "kvag": row(mla_kva_g),
        "wqb": jnp.concatenate([wqb[:, _QB_NOPE], wq_rope], axis=1),
        "wqp": wq_rope[:, _MLA_PARTNER],
        "wkvb": jnp.concatenate([wkvb[:, _KVB_NOPE], wkvb[:, _KVB_V]], axis=1),
        "wg": win[:, 3744:],
        "wbd": w_branch_diff[l].astype(BF16), "wbn": w_branch_na[l].astype(BF16),
        "wbm": w_branch_mla[l].astype(BF16), "wo": w_out[l].astype(BF16),
        "ln1_g": row(ln1_g), "ln1_b": row(ln1_b), "ln2_g": row(ln2_g), "ln2_b": row(ln2_b),
        "wr_t": w_router[l].T.astype(BF16), "rbias": router_bias[l].reshape(N_EXPERTS, 1).astype(F32),
        "weg": w_exp_gate[l].astype(BF16), "weu": w_exp_up[l].astype(BF16), "wed": w_exp_down[l].astype(BF16),
        "wsg": w_sh_gate[l].astype(BF16), "wsu": w_sh_up[l].astype(BF16), "wsd": w_sh_down[l].astype(BF16),
    }


def _rope_tables():
    t = jnp.arange(DEC_SEQ)
    pos = [(t // GRID_W).astype(F32), (t % GRID_W).astype(F32)]

    def table(block):
        half = block // 4
        inv = ROPE_BASE ** (-jnp.arange(half, dtype=F32) / half)
        cos, sin = [], []
        for p in pos:
            ang = p[:, None] * inv[None, :]
            cos += [jnp.cos(ang), jnp.cos(ang)]
            sin += [-jnp.sin(ang), jnp.sin(ang)]
        reps = LANES // block
        return (jnp.tile(jnp.concatenate(cos, axis=1), (1, reps)),
                jnp.tile(jnp.concatenate(sin, axis=1), (1, reps)))

    cd, sd = table(DIFF_DH)
    cm, sm = table(MLA_ROPE)
    return {"cd": cd, "sd": sd, "cm": cm, "sm": sm}


def kernel(x_prompt, x_sample, cache_diff_k, cache_diff_v, cache_na_k, cache_na_v, cache_mla_ckv, cache_mla_krope, c, c_ctx, w_mod, b_mod, w_in, diff_lambda, diff_subln_g, na_rpb, mla_qa_g, mla_wq_b, mla_kva_g, mla_wkv_b, w_branch_diff, w_branch_na, w_branch_mla, w_out, ln1_g, ln1_b, ln2_g, ln2_b, w_router, router_bias, w_exp_gate, w_exp_up, w_exp_down, w_sh_gate, w_sh_up, w_sh_down):
    t_ctx = BATCH * SEQ
    t_lat = DEC_BATCH * DEC_SEQ
    cvec = jnp.concatenate([c, c_ctx[None, :], jnp.zeros((MOD_ROWS - DEC_BATCH - 1, D_MODEL), F32)], axis=0)
    mods = _modulation(cvec, w_mod, b_mod).reshape(DEPTH, MOD_ROWS, 6, D_MODEL)
    rope = _rope_tables()
    ck_d = cache_diff_k.reshape(DEC_BATCH, DEPTH, PAST_LEN, DIFF_W)
    cv_d = cache_diff_v.reshape(DEC_BATCH, DEPTH, PAST_LEN, DIFF_W)
    ck_n = cache_na_k.reshape(DEC_BATCH, DEPTH, PAST_LEN, NA_W)
    cv_n = cache_na_v.reshape(DEC_BATCH, DEPTH, PAST_LEN, NA_W)

    xc = x_prompt.reshape(t_ctx, D_MODEL)
    xl = x_sample.reshape(t_lat, D_MODEL)
    states = []
    for l in range(DEPTH):
        W = _layer_weights(l, w_in, mla_qa_g, mla_wq_b, mla_kva_g, mla_wkv_b, w_branch_diff, w_branch_na,
                           w_branch_mla, w_out, ln1_g, ln1_b, ln2_g, ln2_b, w_router, router_bias,
                           w_exp_gate, w_exp_up, w_exp_down, w_sh_gate, w_sh_up, w_sh_down)
        lam = diff_lambda[l].astype(F32)
        subln = diff_subln_g[l].reshape(1, LANES).astype(F32)
        mod_c = mods[l, DEC_BATCH:DEC_BATCH + 1]
        mod_l = mods[l, :DEC_BATCH]

        Wc = dict(W, wm=W["wm_ctx"])
        names = ["dq", "nq", "mqn", "mqr", "kr4", "kn", "vn", "dk", "dv", "nk", "nv", "ckv", "kr"]
        pc = dict(zip(names, _project(xc, mod_c, Wc, None, latent=False)))
        od, on, om = _attn_ctx(pc, lam, subln, l)
        x1, h2 = _merge(xc, mod_c, od, on, om, W, latent=False)
        xc = _moe(h2, x1, mod_c, W, latent=False)
        states.append(pc)

        Wl = dict(W, wm=W["wm_lat"])
        names = ["dq", "dk", "dv", "nq", "nk", "nv", "mqn", "mqr", "kr4", "kn", "vn"]
        pll = dict(zip(names, _project(xl, mod_l, Wl, rope, latent=True)))
        od = _attn_diff_lat(pll, ck_d, cv_d, lam, subln, l)
        on = _attn_na_lat(pll, ck_n, cv_n, _na_bias(na_rpb[l].astype(F32)), l)
        kc, vc, krc = _mla_cache(cache_mla_ckv, cache_mla_krope, W["wkvb"], l)
        om = _attn_mla_lat(pll, kc, vc, krc)
        x1, h2 = _merge(xl, mod_l, od, on, om, W, latent=True)
        xl = _moe(h2, x1, mod_l, W, latent=True)

    def stack(name, shape):
        return jnp.stack([s[name].reshape((BATCH, SEQ) + shape) for s in states], axis=1)

    return (xc.reshape(BATCH, SEQ, D_MODEL), xl.reshape(DEC_BATCH, DEC_SEQ, D_MODEL),
            stack("dk", (DIFF_HEADS, 2 * DIFF_DH)), stack("dv", (DIFF_HEADS, 2 * DIFF_DH)),
            stack("nk", (NA_HEADS, NA_DH)), stack("nv", (NA_HEADS, NA_DH)),
            stack("ckv", (MLA_KV_LORA,)), stack("kr", (MLA_ROPE,)))
```

```python
import functools

import numpy as np
import jax
import jax.numpy as jnp
from jax import lax
from jax.experimental import pallas as pl
from jax.experimental.pallas import tpu as pltpu
from jax.experimental.pallas import tpu_sc as plsc

D_MODEL = 1024
BATCH = 32
SEQ = 256
DEPTH = 2
DEC_BATCH = 8
DEC_SEQ = 2048
PAST_LEN = 512
GRID_W = 64
GRID_ROWS = DEC_SEQ // GRID_W
ROPE_BASE = 10000.0
DIFF_HEADS = 4
DIFF_DH = 64
DIFF_W = 512
NA_HEADS = 8
NA_DH = 64
NA_W = 512
NA_WIN_ROWS = 8
NA_WIN_COLS = 16
MLA_HEADS = 8
MLA_Q_LORA = 384
MLA_KV_LORA = 256
MLA_NOPE = 64
MLA_ROPE = 32
MLA_V = 64
MLA_W = 512
N_EXPERTS = 64
N_GROUPS = 8
GROUP_SIZE = N_EXPERTS // N_GROUPS
TOPK_GROUPS = 4
TOP_K = 8
EXPERT_FF = 256
SHARED_FF = 256
ROUTED_SCALE = 2.5
DN_ALPHA = (2 * DEPTH) ** 0.25
LN_EPS = 1e-5
RMS_EPS = 1e-6

F32 = jnp.float32
BF16 = jnp.bfloat16

LANES = 128
VMEM_LIMIT_BYTES = 56 * 1024 * 1024
NEG_BIG = -1e30
LOG2E = 1.4426950408889634

DIFF_SCALE = DIFF_DH ** -0.5
NA_SCALE = NA_DH ** -0.5
MLA_SCALE = (MLA_NOPE + MLA_ROPE) ** -0.5

PROJ_TM = 512
ATT_TQ = 512
NA_TQ = 256
NA_Q_ROWS = NA_TQ // GRID_W
NA_KEY_ROWS = NA_Q_ROWS + NA_WIN_ROWS
NA_KEYS = NA_KEY_ROWS * GRID_W
MOE_TILE_MIN = 128
MOE_TILE_MAX = 512
SC_ROWS = 128
COMBINE_TM = 256


def _dot(a, b):
    return jnp.dot(a, b, preferred_element_type=F32)


def _dot_nt(a, b):
    return lax.dot_general(a, b, (((1,), (1,)), ((), ())), preferred_element_type=F32)


def _sigmoid(x):
    return 1.0 / (1.0 + jnp.exp(-x))


def _silu(x):
    return x * _sigmoid(x)


def _params(*sem):
    return pltpu.CompilerParams(dimension_semantics=sem, vmem_limit_bytes=VMEM_LIMIT_BYTES)


def _full(shape):
    n = len(shape)
    return pl.BlockSpec(shape, lambda *_: (0,) * n)


def _layer_norm(y, g, b):
    mu = jnp.mean(y, axis=-1, keepdims=True)
    yc = y - mu
    var = jnp.mean(yc * yc, axis=-1, keepdims=True)
    return yc * lax.rsqrt(var + LN_EPS) * g + b


def _rms(x, g):
    return x * lax.rsqrt(jnp.mean(x * x, axis=-1, keepdims=True) + RMS_EPS) * g


HIGH_HALF = -65536
PACKED_W = D_MODEL // 2


def _pack_bf16_pairs(x):
    n = x.shape[1] // 2
    bits = lax.bitcast_convert_type(x.astype(BF16).astype(F32), jnp.int32)
    return lax.shift_right_logical(bits[:, :n], 16) | (bits[:, n:] & HIGH_HALF)


def _unpack_bf16_pairs(w):
    lo = lax.bitcast_convert_type(lax.shift_left(w, 16), F32)
    hi = lax.bitcast_convert_type(w & HIGH_HALF, F32)
    return jnp.concatenate([lo, hi], axis=1)


MOD_ROWS = 16
MOD_TN = 1536


def _mod_kernel(c_ref, w_ref, b_ref, o_ref):
    s = _silu(c_ref[...]).astype(BF16)
    o_ref[0] = _dot(s, w_ref[0].astype(BF16)) + b_ref[0]


def _modulation(cvec, w_mod, b_mod):
    n = 6 * D_MODEL
    return pl.pallas_call(
        _mod_kernel,
        out_shape=jax.ShapeDtypeStruct((DEPTH, MOD_ROWS, n), F32),
        grid=(DEPTH, n // MOD_TN),
        in_specs=[
            pl.BlockSpec((MOD_ROWS, D_MODEL), lambda l, j: (0, 0)),
            pl.BlockSpec((1, D_MODEL, MOD_TN), lambda l, j: (l, 0, j)),
            pl.BlockSpec((1, 1, MOD_TN), lambda l, j: (l, 0, j)),
        ],
        out_specs=pl.BlockSpec((1, MOD_ROWS, MOD_TN), lambda l, j: (l, 0, j)),
        compiler_params=_params("parallel", "parallel"),
        name="modulation",
    )(cvec, w_mod, b_mod.reshape(DEPTH, 1, n))


def _proj_common(x_ref, mod_ref, wa_ref, wm_ref, qag_ref, kvag_ref, wqb_ref, wkvb_ref):
    m = mod_ref[0]
    h = (x_ref[...] * (1.0 + m[1:2]) + m[0:1]).astype(BF16)
    a = _dot(h, wa_ref[...])
    mm = _dot(h, wm_ref[...])
    qan = _rms(mm[:, :MLA_Q_LORA], qag_ref[...]).astype(BF16)
    mq = _dot(qan, wqb_ref[...])
    ckv = _rms(mm[:, MLA_Q_LORA:MLA_Q_LORA + MLA_KV_LORA], kvag_ref[...])
    kv = _dot(ckv.astype(BF16), wkvb_ref[...])
    return h, a, mm, qan, mq, ckv, kv


def _proj_ctx_kernel(x_ref, mod_ref, wa_ref, wm_ref, qag_ref, kvag_ref, wqb_ref, wkvb_ref,
                     dq_ref, nq_ref, mqn_ref, mqr_ref, kr4_ref, kn_ref, vn_ref,
                     dk_ref, dv_ref, nk_ref, nv_ref, ckv_ref, kr_ref):
    _, a, mm, _, mq, ckv, kv = _proj_common(x_ref, mod_ref, wa_ref, wm_ref, qag_ref, kvag_ref,
                                            wqb_ref, wkvb_ref)
    dq_ref[...] = (a[:, 0:512] * DIFF_SCALE).astype(BF16)
    dk_ref[...] = a[:, 512:1024]
    dv_ref[...] = a[:, 1024:1536]
    nq_ref[...] = (a[:, 1536:2048] * NA_SCALE).astype(BF16)
    nk_ref[...] = a[:, 2048:2560]
    nv_ref[...] = a[:, 2560:3072]
    mqn_ref[...] = (mq[:, :512] * MLA_SCALE).astype(BF16)
    mqr_ref[...] = (mq[:, 512:768] * MLA_SCALE).astype(BF16)
    kr4 = mm[:, 640:768]
    kr4_ref[...] = kr4.astype(BF16)
    kr_ref[...] = kr4[:, :MLA_ROPE]
    ckv_ref[...] = ckv
    kn_ref[...] = kv[:, :512].astype(BF16)
    vn_ref[...] = kv[:, 512:].astype(BF16)


def _proj_lat_kernel(x_ref, mod_ref, wa_ref, wm_ref, qag_ref, kvag_ref, wqb_ref, wkvb_ref,
                     wp_ref, wqp_ref, cd_ref, sd_ref, cm_ref, sm_ref,
                     dq_ref, dk_ref, dv_ref, nq_ref, nk_ref, nv_ref,
                     mqn_ref, mqr_ref, kr4_ref, kn_ref, vn_ref):
    h, a, mm, qan, mq, _, kv = _proj_common(x_ref, mod_ref, wa_ref, wm_ref, qag_ref, kvag_ref,
                                            wqb_ref, wkvb_ref)
    ap = _dot(h, wp_ref[...])
    mqp = _dot(qan, wqp_ref[...])
    cd = cd_ref[...]
    sd = sd_ref[...]
    cm = cm_ref[...]
    sm = sm_ref[...]
    for j in range(DIFF_W // LANES):
        lo, hi = LANES * j, LANES * (j + 1)
        dq_ref[:, lo:hi] = ((a[:, lo:hi] * cd + ap[:, lo:hi] * sd) * (DIFF_SCALE * LOG2E)).astype(BF16)
        dk_ref[:, lo:hi] = (a[:, 512 + lo:512 + hi] * cd + ap[:, 512 + lo:512 + hi] * sd).astype(BF16)
    dv_ref[...] = a[:, 1024:1536].astype(BF16)
    nq_ref[...] = (a[:, 1536:2048] * (NA_SCALE * LOG2E)).astype(BF16)
    nk_ref[...] = a[:, 2048:2560].astype(BF16)
    nv_ref[...] = a[:, 2560:3072].astype(BF16)
    mqn_ref[...] = (mq[:, :512] * (MLA_SCALE * LOG2E)).astype(BF16)
    for j in range(2):
        lo, hi = LANES * j, LANES * (j + 1)
        mqr_ref[:, lo:hi] = ((mq[:, 512 + lo:512 + hi] * cm + mqp[:, lo:hi] * sm)
                             * (MLA_SCALE * LOG2E)).astype(BF16)
    kr4_ref[...] = (mm[:, 640:768] * cm + mm[:, 768:896] * sm).astype(BF16)
    kn_ref[...] = kv[:, :512].astype(BF16)
    vn_ref[...] = kv[:, 512:].astype(BF16)


def _project(x, mod, W, rope, *, latent):
    t = x.shape[0]
    tm = PROJ_TM
    tokens_per_batch = DEC_SEQ if latent else t
    steps_per_batch = tokens_per_batch // tm
    row = lambda w: pl.BlockSpec((tm, w), lambda i: (i, 0))
    common_in = [
        row(D_MODEL),
        pl.BlockSpec((1, 6, D_MODEL), lambda i: (i // steps_per_batch, 0, 0)),
        _full(W["wa"].shape), _full(W["wm"].shape), _full((1, MLA_Q_LORA)), _full((1, MLA_KV_LORA)),
        _full(W["wqb"].shape), _full(W["wkvb"].shape),
    ]
    common_args = [x, mod, W["wa"], W["wm"], W["qag"], W["kvag"], W["wqb"], W["wkvb"]]
    bf = lambda w: jax.ShapeDtypeStruct((t, w), BF16)
    f32 = lambda w: jax.ShapeDtypeStruct((t, w), F32)
    if latent:
        tab = pl.BlockSpec((tm, LANES), lambda i: (i % steps_per_batch, 0))
        widths = [512, 512, 512, 512, 512, 512, 512, 256, 128, 512, 512]
        return pl.pallas_call(
            _proj_lat_kernel,
            out_shape=[bf(w) for w in widths],
            grid=(t // tm,),
            in_specs=common_in + [_full(W["wp"].shape), _full(W["wqp"].shape), tab, tab, tab, tab],
            out_specs=[row(w) for w in widths],
            compiler_params=_params("parallel"),
            name="proj_lat",
        )(*common_args, W["wp"], W["wqp"], rope["cd"], rope["sd"], rope["cm"], rope["sm"])
    bf_w = [512, 512, 512, 256, 128, 512, 512]
    f32_w = [512, 512, 512, 512, 256, 32]
    return pl.pallas_call(
        _proj_ctx_kernel,
        out_shape=[bf(w) for w in bf_w] + [f32(w) for w in f32_w],
        grid=(t // tm,),
        in_specs=common_in,
        out_specs=[row(w) for w in bf_w + f32_w],
        compiler_params=_params("parallel"),
        name="proj_ctx",
    )(*common_args)


def _lane_iota():
    return lax.broadcasted_iota(jnp.int32, (1, LANES), 1)


def _softmax_parts(parts, exp):
    m = functools.reduce(jnp.maximum, [jnp.max(s, axis=-1, keepdims=True) for s in parts])
    es = [exp(s - m) for s in parts]
    l = functools.reduce(lambda u, v: u + v, [jnp.sum(e, axis=-1, keepdims=True) for e in es])
    return es, l


def _diff_lambda(lam_ref, layer):
    lp = lam_ref[...]
    lam_init = 0.8 - 0.6 * float(np.exp(-0.3 * layer))
    s1 = jnp.sum(lp[0:1] * lp[1:2], axis=-1, keepdims=True)
    s2 = jnp.sum(lp[2:3] * lp[3:4], axis=-1, keepdims=True)
    return jnp.exp(s1) - jnp.exp(s2) + lam_init, lam_init


def _diff_heads(q_ref, ks, vs, lam_ref, g_ref, o_ref, layer, exp):
    lam, lam_init = _diff_lambda(lam_ref, layer)
    first_map = _lane_iota() < DIFF_DH
    g = g_ref[...]
    for h in range(DIFF_HEADS):
        hs = slice(LANES * h, LANES * (h + 1))
        q = q_ref[:, hs]
        q1 = jnp.where(first_map, q, jnp.zeros_like(q))
        q2 = jnp.where(first_map, jnp.zeros_like(q), q)
        kk = [k[:, hs].astype(BF16) for k in ks]
        e1, l1 = _softmax_parts([_dot_nt(q1, k) for k in kk], exp)
        e2, l2 = _softmax_parts([_dot_nt(q2, k) for k in kk], exp)
        c1 = 1.0 / l1
        c2 = lam / l2
        o = None
        for a1, a2, v in zip(e1, e2, vs):
            part = _dot((a1 * c1 - a2 * c2).astype(BF16), v[:, hs].astype(BF16))
            o = part if o is None else o + part
        o = _rms(o, g) * (1.0 - lam_init)
        o_ref[:, hs] = o.astype(BF16)


def _pair_heads(q_of, k_of, v_of, bias_of, o_ref, n_pairs, exp):
    first = _lane_iota() < 64
    for j in range(n_pairs):
        ps = slice(LANES * j, LANES * (j + 1))
        ks = k_of(j)
        vs = v_of(j)
        outs = []
        for hh in range(2):
            q = q_of(j, hh)
            ss = [_dot_nt(q, k) for k in ks]
            bs = bias_of(2 * j + hh)
            ss = [s if b is None else s + b for s, b in zip(ss, bs)]
            es, l = _softmax_parts(ss, exp)
            o = None
            for e, v in zip(es, vs):
                part = _dot(e.astype(BF16), v)
                o = part if o is None else o + part
            outs.append(o / l)
        o_ref[:, ps] = jnp.where(first, outs[0], outs[1]).astype(BF16)


def _na_q(q_ref):
    first = _lane_iota() < NA_DH

    def q_of(j, hh):
        q = q_ref[:, LANES * j:LANES * (j + 1)]
        keep = first if hh == 0 else jnp.logical_not(first)
        return jnp.where(keep, q, jnp.zeros_like(q))
    return q_of


def _mla_q(qn_ref, qr_ref):
    lane = _lane_iota()
    first = lane < MLA_NOPE

    def q_of(j, hh):
        h = 2 * j + hh
        qn = qn_ref[:, LANES * j:LANES * (j + 1)]
        keep = first if hh == 0 else jnp.logical_not(first)
        qn = jnp.where(keep, qn, jnp.zeros_like(qn))
        qr = qr_ref[:, LANES * (h // 4):LANES * (h // 4 + 1)]
        qr = jnp.where((lane // MLA_ROPE) == (h % 4), qr, jnp.zeros_like(qr))
        return jnp.concatenate([qn, qr], axis=1)
    return q_of


def _attn_ctx_kernel(layer, dq_ref, dk_ref, dv_ref, nq_ref, nk_ref, nv_ref,
                     mqn_ref, mqr_ref, kn_ref, kr4_ref, vn_ref, lam_ref, g_ref,
                     od_ref, on_ref, om_ref):
    _diff_heads(dq_ref, [dk_ref], [dv_ref], lam_ref, g_ref, od_ref, layer, jnp.exp)
    none = lambda h: [None]
    pair = lambda j: slice(LANES * j, LANES * (j + 1))
    _pair_heads(_na_q(nq_ref),
                lambda j: [nk_ref[:, pair(j)].astype(BF16)],
                lambda j: [nv_ref[:, pair(j)].astype(BF16)],
                none, on_ref, NA_HEADS // 2, jnp.exp)
    kr4 = kr4_ref[...]
    _pair_heads(_mla_q(mqn_ref, mqr_ref),
                lambda j: [jnp.concatenate([kn_ref[:, pair(j)], kr4], axis=1)],
                lambda j: [vn_ref[:, pair(j)]],
                none, om_ref, MLA_HEADS // 2, jnp.exp)


def _attn_ctx(p, lam, g, layer):
    t = p["dq"].shape[0]
    row = lambda w: pl.BlockSpec((SEQ, w), lambda b: (b, 0))
    names = ["dq", "dk", "dv", "nq", "nk", "nv", "mqn", "mqr", "kn", "kr4", "vn"]
    out = jax.ShapeDtypeStruct((t, 512), BF16)
    return pl.pallas_call(
        functools.partial(_attn_ctx_kernel, layer),
        out_shape=[out, out, out],
        grid=(t // SEQ,),
        in_specs=[row(p[n].shape[1]) for n in names] + [_full((4, DIFF_DH)), _full((1, LANES))],
        out_specs=[row(512)] * 3,
        compiler_params=_params("parallel"),
        name="attn_ctx",
    )(*[p[n] for n in names], lam, g)


def _cache_spec(width, layer):
    return pl.BlockSpec((None, None, PAST_LEN, width), lambda b, q: (b, layer, 0, 0))


def _batch_spec(width):
    return pl.BlockSpec((None, DEC_SEQ, width), lambda b, q: (b, 0, 0))


def _qtile_spec(width, tq):
    steps = DEC_SEQ // tq
    return pl.BlockSpec((tq, width), lambda b, q: (b * steps + q, 0))


def _attn_diff_lat_kernel(layer, q_ref, kc_ref, vc_ref, kn_ref, vn_ref, lam_ref, g_ref, o_ref):
    _diff_heads(q_ref, [kc_ref, kn_ref], [vc_ref, vn_ref], lam_ref, g_ref, o_ref, layer, jnp.exp2)


def _attn_diff_lat(p, cache_k, cache_v, lam, g, layer):
    t = p["dq"].shape[0]
    b3 = lambda a: a.reshape(DEC_BATCH, DEC_SEQ, a.shape[1])
    return pl.pallas_call(
        functools.partial(_attn_diff_lat_kernel, layer),
        out_shape=jax.ShapeDtypeStruct((t, DIFF_W), BF16),
        grid=(DEC_BATCH, DEC_SEQ // ATT_TQ),
        in_specs=[_qtile_spec(DIFF_W, ATT_TQ), _cache_spec(DIFF_W, layer), _cache_spec(DIFF_W, layer),
                  _batch_spec(DIFF_W), _batch_spec(DIFF_W),
                  pl.BlockSpec((4, DIFF_DH), lambda b, q: (0, 0)),
                  pl.BlockSpec((1, LANES), lambda b, q: (0, 0))],
        out_specs=_qtile_spec(DIFF_W, ATT_TQ),
        compiler_params=_params("parallel", "parallel"),
        name="attn_diff_lat",
    )(p["dq"], cache_k, cache_v, b3(p["dk"]), b3(p["dv"]), lam, g)


def _na_key_start(q):
    return jnp.clip(q * NA_Q_ROWS - NA_WIN_ROWS // 2, 0, GRID_ROWS - NA_KEY_ROWS)


def _attn_na_lat_kernel(q_ref, kc_ref, vc_ref, kn_ref, vn_ref, bias_ref, o_ref):
    start = pl.multiple_of(_na_key_start(pl.program_id(1)) * GRID_W, GRID_W)
    kw = kn_ref[pl.ds(start, NA_KEYS), :]
    vw = vn_ref[pl.ds(start, NA_KEYS), :]
    pair = lambda j: slice(LANES * j, LANES * (j + 1))
    _pair_heads(_na_q(q_ref),
                lambda j: [kc_ref[:, pair(j)].astype(BF16), kw[:, pair(j)]],
                lambda j: [vc_ref[:, pair(j)].astype(BF16), vw[:, pair(j)]],
                lambda h: [None, bias_ref[h]],
                o_ref, NA_HEADS // 2, jnp.exp2)


def _na_bias_tables():
    n_blocks = DEC_SEQ // NA_TQ
    qr = np.arange(NA_Q_ROWS)
    kr = np.arange(NA_KEY_ROWS)
    row_sel, row_ok = [], []
    for qb in range(n_blocks):
        ks = int(np.clip(qb * NA_Q_ROWS - NA_WIN_ROWS // 2, 0, GRID_ROWS - NA_KEY_ROWS))
        r = qb * NA_Q_ROWS + qr
        r0 = np.clip(r - NA_WIN_ROWS // 2, 0, GRID_ROWS - NA_WIN_ROWS)
        krow = ks + kr
        ok = (krow[None, :] >= r0[:, None]) & (krow[None, :] < r0[:, None] + NA_WIN_ROWS)
        off = krow[None, :] - r[:, None] + NA_WIN_ROWS - 1
        sel = (off[:, :, None] == np.arange(2 * NA_WIN_ROWS - 1)) & ok[:, :, None]
        row_sel.append(sel.astype(np.float32))
        row_ok.append(ok)
    kinds, kind_of_block = [], []
    for qb in range(n_blocks):
        for n, other in enumerate(kinds):
            if np.array_equal(row_sel[qb], row_sel[other]):
                kind_of_block.append(n)
                break
        else:
            kind_of_block.append(len(kinds))
            kinds.append(qb)
    c = np.arange(GRID_W)
    c0 = np.clip(c - NA_WIN_COLS // 2, 0, GRID_W - NA_WIN_COLS)
    col_ok = (c[None, :] >= c0[:, None]) & (c[None, :] < c0[:, None] + NA_WIN_COLS)
    coff = c[None, :] - c[:, None] + NA_WIN_COLS - 1
    col_sel = ((coff[:, :, None] == np.arange(2 * NA_WIN_COLS - 1)) & col_ok[:, :, None]).astype(np.float32)
    rsel = np.stack([row_sel[qb] for qb in kinds])
    valid = np.stack([row_ok[qb][:, None, :, None] & col_ok[None, :, None, :] for qb in kinds])
    valid = valid.reshape(len(kinds), 1, NA_TQ, NA_KEYS)
    return rsel, col_sel, valid, kind_of_block


_NA_ROW_SEL, _NA_COL_SEL, _NA_VALID, _NA_KIND_OF_BLOCK = _na_bias_tables()


def _na_bias(rpb):
    hp = lax.Precision.HIGHEST
    cols = jnp.einsum("hij,ckj->hick", rpb, _NA_COL_SEL, precision=hp)
    b = jnp.einsum("nqri,hick->nhqcrk", _NA_ROW_SEL, cols, precision=hp)
    b = b.reshape(_NA_ROW_SEL.shape[0], NA_HEADS, NA_TQ, NA_KEYS)
    return jnp.where(_NA_VALID, b * LOG2E, NEG_BIG).astype(F32)


def _na_kind(q):
    return (q > 0).astype(jnp.int32) + (q == DEC_SEQ // NA_TQ - 1).astype(jnp.int32)


def _attn_na_lat(p, cache_k, cache_v, bias, layer):
    assert _NA_KIND_OF_BLOCK == [0] + [1] * (DEC_SEQ // NA_TQ - 2) + [2]
    t = p["nq"].shape[0]
    b3 = lambda a: a.reshape(DEC_BATCH, DEC_SEQ, a.shape[1])
    return pl.pallas_call(
        _attn_na_lat_kernel,
        out_shape=jax.ShapeDtypeStruct((t, NA_W), BF16),
        grid=(DEC_BATCH, DEC_SEQ // NA_TQ),
        in_specs=[_qtile_spec(NA_W, NA_TQ), _cache_spec(NA_W, layer), _cache_spec(NA_W, layer),
                  _batch_spec(NA_W), _batch_spec(NA_W),
                  pl.BlockSpec((None, NA_HEADS, NA_TQ, NA_KEYS), lambda b, q: (_na_kind(q), 0, 0, 0))],
        out_specs=_qtile_spec(NA_W, NA_TQ),
        compiler_params=_params("parallel", "parallel"),
        name="attn_na_lat",
    )(p["nq"], cache_k, cache_v, b3(p["nk"]), b3(p["nv"]), bias)


def _mla_cache_kernel(ckv_ref, kr_ref, wkvb_ref, rep_ref, kc_ref, vc_ref, krc_ref):
    kv = _dot(ckv_ref[...].astype(BF16), wkvb_ref[...])
    kc_ref[...] = kv[:, :512].astype(BF16)
    vc_ref[...] = kv[:, 512:].astype(BF16)
    krc_ref[...] = _dot(kr_ref[...].astype(BF16), rep_ref[...]).astype(BF16)


def _mla_cache(cache_ckv, cache_kr, wkvb, layer):
    rep = jnp.asarray(np.tile(np.eye(MLA_ROPE, dtype=np.float32), (1, LANES // MLA_ROPE)), BF16)
    spec_in = lambda w: pl.BlockSpec((None, None, PAST_LEN, w), lambda b: (b, layer, 0, 0))
    spec_out = lambda w: pl.BlockSpec((None, PAST_LEN, w), lambda b: (b, 0, 0))
    shp = lambda w: jax.ShapeDtypeStruct((DEC_BATCH, PAST_LEN, w), BF16)
    return pl.pallas_call(
        _mla_cache_kernel,
        out_shape=[shp(512), shp(512), shp(LANES)],
        grid=(DEC_BATCH,),
        in_specs=[spec_in(MLA_KV_LORA), spec_in(MLA_ROPE), _full(wkvb.shape), _full(rep.shape)],
        out_specs=[spec_out(512), spec_out(512), spec_out(LANES)],
        compiler_params=_params("parallel"),
        name="mla_cache",
    )(cache_ckv, cache_kr, wkvb, rep)


def _attn_mla_lat_kernel(qn_ref, qr_ref, kc_ref, krc_ref, vc_ref, kn_ref, krn_ref, vn_ref, o_ref):
    pair = lambda j: slice(LANES * j, LANES * (j + 1))
    krc = krc_ref[...]
    krn = krn_ref[...]
    _pair_heads(_mla_q(qn_ref, qr_ref),
                lambda j: [jnp.concatenate([kc_ref[:, pair(j)], krc], axis=1),
                           jnp.concatenate([kn_ref[:, pair(j)], krn], axis=1)],
                lambda j: [vc_ref[:, pair(j)], vn_ref[:, pair(j)]],
                lambda h: [None, None],
                o_ref, MLA_HEADS // 2, jnp.exp2)


def _attn_mla_lat(p, kc, vc, krc):
    t = p["mqn"].shape[0]
    b3 = lambda a: a.reshape(DEC_BATCH, DEC_SEQ, a.shape[1])
    cspec = lambda w: pl.BlockSpec((None, PAST_LEN, w), lambda b, q: (b, 0, 0))
    return pl.pallas_call(
        _attn_mla_lat_kernel,
        out_shape=jax.ShapeDtypeStruct((t, MLA_W), BF16),
        grid=(DEC_BATCH, DEC_SEQ // ATT_TQ),
        in_specs=[_qtile_spec(512, ATT_TQ), _qtile_spec(256, ATT_TQ), cspec(512), cspec(LANES), cspec(512),
                  _batch_spec(512), _batch_spec(LANES), _batch_spec(512)],
        out_specs=_qtile_spec(MLA_W, ATT_TQ),
        compiler_params=_params("parallel", "parallel"),
        name="attn_mla_lat",
    )(p["mqn"], p["mqr"], kc, krc, vc, b3(p["kn"]), b3(p["kr4"]), b3(p["vn"]))


def _merge_kernel(x_ref, mod_ref, od_ref, on_ref, om_ref, wg_ref, wbd_ref, wbn_ref, wbm_ref, wo_ref,
                  g_ref, b_ref, x1_ref, h2_ref):
    x = x_ref[...]
    m = mod_ref[0]
    h = (x * (1.0 + m[1:2]) + m[0:1]).astype(BF16)
    gates = _dot(h, wg_ref[...])
    mix = (_sigmoid(gates[:, 0:1024]) * _dot(od_ref[...], wbd_ref[...])
           + _sigmoid(gates[:, 1024:2048]) * _dot(on_ref[...], wbn_ref[...])
           + _sigmoid(gates[:, 2048:3072]) * _dot(om_ref[...], wbm_ref[...]))
    out = _dot(mix.astype(BF16), wo_ref[...])
    x1 = _layer_norm(DN_ALPHA * x + m[2:3] * out, g_ref[...], b_ref[...])
    x1_ref[...] = x1
    h2_ref[...] = _pack_bf16_pairs(x1 * (1.0 + m[4:5]) + m[3:4])


def _merge(x, mod, od, on, om, W, *, latent):
    t = x.shape[0]
    tm = PROJ_TM
    steps_per_batch = (DEC_SEQ if latent else t) // tm
    row = lambda w: pl.BlockSpec((tm, w), lambda i: (i, 0))
    return pl.pallas_call(
        _merge_kernel,
        out_shape=[jax.ShapeDtypeStruct((t, D_MODEL), F32), jax.ShapeDtypeStruct((t, PACKED_W), jnp.int32)],
        grid=(t // tm,),
        in_specs=[row(D_MODEL),
                  pl.BlockSpec((1, 6, D_MODEL), lambda i: (i // steps_per_batch, 0, 0)),
                  row(512), row(512), row(512),
                  _full(W["wg"].shape), _full(W["wbd"].shape), _full(W["wbn"].shape),
                  _full(W["wbm"].shape), _full(W["wo"].shape),
                  _full((1, D_MODEL)), _full((1, D_MODEL))],
        out_specs=[row(D_MODEL), row(PACKED_W)],
        compiler_params=_params("parallel"),
        name="merge",
    )(x, mod, od, on, om, W["wg"], W["wbd"], W["wbn"], W["wbm"], W["wo"], W["ln1_g"], W["ln1_b"])


def _first_index_of_max(vals, idx, sentinel):
    mx = functools.reduce(jnp.maximum, [jnp.max(v, axis=0, keepdims=True) for v in vals])
    cand = [jnp.min(jnp.where(v == mx, i, sentinel), axis=0, keepdims=True) for v, i in zip(vals, idx)]
    return mx, functools.reduce(jnp.minimum, cand)


def _router_kernel(h_ref, wr_ref, bias_ref, tri_ref, eid_ref, rank_ref, wtok_ref, cnt_ref, base_ref):
    tm = h_ref.shape[0]

    @pl.when(pl.program_id(0) == 0)
    def _():
        base_ref[...] = jnp.zeros_like(base_ref)

    logits = _dot_nt(wr_ref[...], _unpack_bf16_pairs(h_ref[...]).astype(BF16))
    scores = _sigmoid(logits)
    biased = scores + bias_ref[...]
    member = lax.broadcasted_iota(jnp.int32, (GROUP_SIZE, tm), 0)
    slabs = [biased[GROUP_SIZE * g:GROUP_SIZE * (g + 1)] for g in range(N_GROUPS)]
    gscore = []
    for s in slabs:
        m1, first = _first_index_of_max([s], [member], GROUP_SIZE)
        m2 = jnp.max(jnp.where(member == first, -jnp.inf, s), axis=0, keepdims=True)
        gscore.append(m1 + m2)
    gs = jnp.concatenate(gscore, axis=0)
    gidx = lax.broadcasted_iota(jnp.int32, (N_GROUPS, tm), 0)
    gsel = jnp.zeros((N_GROUPS, tm), F32)
    for _ in range(TOPK_GROUPS):
        _, first = _first_index_of_max([gs], [gidx], N_GROUPS)
        pick = gidx == first
        gsel = jnp.where(pick, 1.0, gsel)
        gs = jnp.where(pick, -jnp.inf, gs)
    cur = [jnp.where(gsel[g:g + 1] > 0.0, slabs[g], -jnp.inf) for g in range(N_GROUPS)]
    eidx = [member + GROUP_SIZE * g for g in range(N_GROUPS)]
    sel = [jnp.zeros((GROUP_SIZE, tm), F32) for _ in range(N_GROUPS)]
    picks = []
    for _ in range(TOP_K):
        _, first = _first_index_of_max(cur, eidx, N_EXPERTS)
        pick = [eidx[g] == first for g in range(N_GROUPS)]
        picks.append((first, pick))
        for g in range(N_GROUPS):
            sel[g] = jnp.where(pick[g], 1.0, sel[g])
            cur[g] = jnp.where(pick[g], -jnp.inf, cur[g])
    w = [jnp.where(sel[g] > 0.0, scores[GROUP_SIZE * g:GROUP_SIZE * (g + 1)], 0.0) for g in range(N_GROUPS)]
    total = functools.reduce(lambda u, v: u + v, [jnp.sum(x, axis=0, keepdims=True) for x in w])
    w = [x / total * ROUTED_SCALE for x in w]

    sel_all = jnp.concatenate(sel, axis=0)
    incl = _dot(sel_all.astype(BF16), tri_ref[...])
    base = base_ref[:, 0:1]
    rank_all = incl - sel_all + base
    rank = [rank_all[GROUP_SIZE * g:GROUP_SIZE * (g + 1)] for g in range(N_GROUPS)]
    cnt = base + jnp.sum(sel_all, axis=1, keepdims=True)
    base_ref[...] = jnp.broadcast_to(cnt, base_ref.shape)
    cnt_ref[...] = jnp.broadcast_to(cnt, cnt_ref.shape)

    def picked(vals, pick):
        parts = [jnp.sum(jnp.where(p, v, 0.0), axis=0, keepdims=True) for p, v in zip(pick, vals)]
        return functools.reduce(lambda u, v: u + v, parts)

    eid_ref[...] = jnp.concatenate([first for first, _ in picks], axis=0)
    rank_ref[...] = jnp.concatenate([picked(rank, pick) for _, pick in picks], axis=0).astype(jnp.int32)
    w_rows = [picked(w, pick) for _, pick in picks] + [jnp.zeros((LANES - TOP_K, tm), F32)]
    wtok_ref[...] = jnp.concatenate(w_rows, axis=0).T


def _router(h2, wr_t, bias_col):
    t = h2.shape[0]
    tm = PROJ_TM
    tri = jnp.asarray(np.triu(np.ones((tm, tm), np.float32)), BF16)
    slots = lambda dt: jax.ShapeDtypeStruct((TOP_K, t), dt)
    return pl.pallas_call(
        _router_kernel,
        out_shape=[slots(jnp.int32), slots(jnp.int32), jax.ShapeDtypeStruct((t, LANES), F32),
                   jax.ShapeDtypeStruct((N_EXPERTS, LANES), F32)],
        grid=(t // tm,),
        in_specs=[pl.BlockSpec((tm, PACKED_W), lambda i: (i, 0)),
                  _full((N_EXPERTS, D_MODEL)), _full((N_EXPERTS, 1)), _full((tm, tm))],
        out_specs=[pl.BlockSpec((TOP_K, tm), lambda i: (0, i)), pl.BlockSpec((TOP_K, tm), lambda i: (0, i)),
                   pl.BlockSpec((tm, LANES), lambda i: (i, 0)), _full((N_EXPERTS, LANES))],
        scratch_shapes=[pltpu.VMEM((N_EXPERTS, LANES), F32)],
        compiler_params=_params("arbitrary"),
        name="router",
    )(h2, wr_t, bias_col, tri)


def _moe_tile(t):
    mean_rows = t * TOP_K // N_EXPERTS
    return int(min(max(pl.next_power_of_2(mean_rows // 2), MOE_TILE_MIN), MOE_TILE_MAX))


def _moe_tiles(t):
    return (t * TOP_K) // _moe_tile(t) + N_EXPERTS


def _moe_plan(eid, rank, cnt, t):
    tile = _moe_tile(t)
    counts = cnt[:, 0].astype(jnp.int32)
    tiles = jnp.maximum((counts + tile - 1) // tile, 1)
    ends = jnp.cumsum(tiles)
    starts = ends - tiles
    experts = jnp.arange(N_EXPERTS, dtype=jnp.int32)
    pos = rank + jnp.sum(jnp.where(eid[:, :, None] == experts, starts * tile, 0), axis=-1)
    tile_ids = jnp.arange(_moe_tiles(t), dtype=jnp.int32)
    owner = tile_ids[:, None] >= ends[None, :]
    tile_expert = jnp.minimum(jnp.sum(owner, axis=-1), N_EXPERTS - 1)
    is_owner = tile_expert[:, None] == experts[None, :]
    start_of = jnp.sum(jnp.where(is_owner, starts, 0), axis=-1)
    count_of = jnp.sum(jnp.where(is_owner, counts, 0), axis=-1)
    real = jnp.clip(count_of - (tile_ids - start_of) * tile, 0, tile)
    real = jnp.where(tile_ids < ends[-1], real, 0)
    return pos.astype(jnp.int32), tile_expert.astype(jnp.int32), real.astype(jnp.int32)


def _sc_workers():
    info = plsc.get_sparse_core_info()
    return info.num_cores, info.num_subcores


def _sc_mesh():
    return plsc.VectorSubcoreMesh(core_axis_name="core", subcore_axis_name="subcore")


def _sc_worker_id(n_cores):
    return lax.axis_index("subcore") * n_cores + lax.axis_index("core")


def _dispatch(h2, pos):
    t = h2.shape[0]
    n_cores, n_sub = _sc_workers()
    ch = SC_ROWS
    per_worker = t // (n_cores * n_sub)
    n_chunks = per_worker // ch
    pos_chunks = pos.reshape(TOP_K, t // ch, ch).transpose(1, 0, 2)

    @functools.partial(
        pl.kernel, mesh=_sc_mesh(),
        out_type=jax.ShapeDtypeStruct((_moe_tiles(t) * _moe_tile(t), PACKED_W), jnp.int32),
        scratch_types=[pltpu.VMEM((TOP_K, ch), jnp.int32), pltpu.VMEM((ch, PACKED_W), jnp.int32),
                       pltpu.SemaphoreType.DMA],
        name="moe_dispatch",
    )
    def run(h_hbm, pos_hbm, xs_hbm, idx_ref, rows_ref, sem):
        first = _sc_worker_id(n_cores) * n_chunks

        @pl.loop(0, n_chunks)
        def _(j):
            c = first + j
            pltpu.sync_copy(pos_hbm.at[c], idx_ref)
            pltpu.sync_copy(h_hbm.at[pl.ds(pl.multiple_of(c * ch, ch), ch)], rows_ref)
            copies = [pltpu.async_copy(rows_ref, xs_hbm.at[idx_ref.at[k]], sem) for k in range(TOP_K)]
            for cp in copies:
                cp.wait()

    return run(h2, pos_chunks)


def _gather_rows(ys, idx):
    n = idx.shape[0]
    n_cores, n_sub = _sc_workers()
    ch = SC_ROWS
    per_worker = n // (n_cores * n_sub)
    n_chunks = per_worker // ch

    @functools.partial(
        pl.kernel, mesh=_sc_mesh(),
        out_type=jax.ShapeDtypeStruct((n, PACKED_W), jnp.int32),
        scratch_types=[pltpu.VMEM((ch,), jnp.int32), pltpu.VMEM((ch, PACKED_W), jnp.int32),
                       pltpu.SemaphoreType.DMA],
        name="moe_gather",
    )
    def run(ys_hbm, idx_hbm, out_hbm, idx_ref, rows_ref, sem):
        first = _sc_worker_id(n_cores) * per_worker

        @pl.loop(0, n_chunks)
        def _(j):
            off = pl.multiple_of(first + j * ch, ch)
            pltpu.sync_copy(idx_hbm.at[pl.ds(off, ch)], idx_ref)
            pltpu.async_copy(ys_hbm.at[idx_ref], rows_ref, sem).wait()
            pltpu.sync_copy(rows_ref, out_hbm.at[pl.ds(off, ch)])

    return run(ys, idx)


def _ffn_kernel(te_ref, real_ref, xs_ref, weg_ref, weu_ref, wed_ref, ys_ref, wg_ref, wu_ref, wd_ref):
    i = pl.program_id(0)
    real = real_ref[i]

    @pl.when(jnp.logical_or(i == 0, te_ref[i] != te_ref[jnp.maximum(i - 1, 0)]))
    def _():
        wg_ref[...] = weg_ref[0].astype(BF16)
        wu_ref[...] = weu_ref[0].astype(BF16)
        wd_ref[...] = wed_ref[0].astype(BF16)

    @pl.when(real > 0)
    def _():
        row = lax.broadcasted_iota(jnp.int32, (xs_ref.shape[0], 1), 0)
        x = _unpack_bf16_pairs(jnp.where(row < real, xs_ref[...], 0)).astype(BF16)
        act = _silu(_dot(x, wg_ref[...])) * _dot(x, wu_ref[...])
        ys_ref[...] = _pack_bf16_pairs(_dot(act.astype(BF16), wd_ref[...]))

    @pl.when(real == 0)
    def _():
        ys_ref[...] = jnp.zeros_like(ys_ref)


def _ffn(xs, tile_expert, tile_real, W):
    tile = xs.shape[0] // tile_expert.shape[0]
    wspec = lambda shape: pl.BlockSpec((1,) + shape, lambda i, te, tr: (te[i], 0, 0))
    return pl.pallas_call(
        _ffn_kernel,
        out_shape=jax.ShapeDtypeStruct(xs.shape, jnp.int32),
        grid_spec=pltpu.PrefetchScalarGridSpec(
            num_scalar_prefetch=2,
            grid=(tile_expert.shape[0],),
            in_specs=[pl.BlockSpec((tile, PACKED_W), lambda i, te, tr: (i, 0)),
                      wspec((D_MODEL, EXPERT_FF)), wspec((D_MODEL, EXPERT_FF)), wspec((EXPERT_FF, D_MODEL))],
            out_specs=pl.BlockSpec((tile, PACKED_W), lambda i, te, tr: (i, 0)),
            scratch_shapes=[pltpu.VMEM((D_MODEL, EXPERT_FF), BF16), pltpu.VMEM((D_MODEL, EXPERT_FF), BF16),
                            pltpu.VMEM((EXPERT_FF, D_MODEL), BF16)]),
        compiler_params=_params("arbitrary"),
        name="moe_ffn",
    )(tile_expert, tile_real, xs, W["weg"], W["weu"], W["wed"])


def _combine_kernel(y_ref, h_ref, wtok_ref, x1_ref, mod_ref, wsg_ref, wsu_ref, wsd_ref,
                    g_ref, b_ref, out_ref):
    wt = wtok_ref[...]
    routed = None
    for k in range(TOP_K):
        part = wt[:, k:k + 1] * _unpack_bf16_pairs(y_ref[k])
        routed = part if routed is None else routed + part
    h = _unpack_bf16_pairs(h_ref[...]).astype(BF16)
    shared = _dot((_silu(_dot(h, wsg_ref[...])) * _dot(h, wsu_ref[...])).astype(BF16), wsd_ref[...])
    m = mod_ref[0]
    y2 = DN_ALPHA * x1_ref[...] + m[5:6] * (routed + shared)
    out_ref[...] = _layer_norm(y2, g_ref[...], b_ref[...])


def _combine(ytok, h2, wtok, x1, mod, W, *, latent):
    t = h2.shape[0]
    ct = COMBINE_TM
    steps_per_batch = (DEC_SEQ if latent else t) // ct
    row = lambda w: pl.BlockSpec((ct, w), lambda i: (i, 0))
    return pl.pallas_call(
        _combine_kernel,
        out_shape=jax.ShapeDtypeStruct((t, D_MODEL), F32),
        grid=(t // ct,),
        in_specs=[pl.BlockSpec((TOP_K, ct, PACKED_W), lambda i: (0, i, 0)),
                  row(PACKED_W), row(LANES), row(D_MODEL),
                  pl.BlockSpec((1, 6, D_MODEL), lambda i: (i // steps_per_batch, 0, 0)),
                  _full((D_MODEL, SHARED_FF)), _full((D_MODEL, SHARED_FF)), _full((SHARED_FF, D_MODEL)),
                  _full((1, D_MODEL)), _full((1, D_MODEL))],
        out_specs=row(D_MODEL),
        compiler_params=_params("parallel"),
        name="moe_combine",
    )(ytok, h2, wtok, x1, mod, W["wsg"], W["wsu"], W["wsd"], W["ln2_g"], W["ln2_b"])


def _moe(h2, x1, mod, W, *, latent):
    t = h2.shape[0]
    eid, rank, wtok, cnt = _router(h2, W["wr_t"], W["rbias"])
    pos, tile_expert, tile_real = _moe_plan(eid, rank, cnt, t)
    xs = _dispatch(h2, pos)
    ys = _ffn(xs, tile_expert, tile_real, W)
    ytok = _gather_rows(ys, pos.reshape(TOP_K * t)).reshape(TOP_K, t, PACKED_W)
    return _combine(ytok, h2, wtok, x1, mod, W, latent=latent)


def _rope_partner(n_blocks, block):
    half = block // 2
    i = np.arange(n_blocks * block)
    return np.where((i % block) < half, i + half, i - half)


_QB_NOPE = np.concatenate([np.arange(MLA_NOPE) + (MLA_NOPE + MLA_ROPE) * h for h in range(MLA_HEADS)])
_QB_ROPE = np.concatenate([np.arange(MLA_ROPE) + (MLA_NOPE + MLA_ROPE) * h + MLA_NOPE for h in range(MLA_HEADS)])
_KVB_NOPE = np.concatenate([np.arange(MLA_NOPE) + (MLA_NOPE + MLA_V) * h for h in range(MLA_HEADS)])
_KVB_V = np.concatenate([np.arange(MLA_V) + (MLA_NOPE + MLA_V) * h + MLA_NOPE for h in range(MLA_HEADS)])
_DIFF_PARTNER = _rope_partner(2 * DIFF_W // 32, 32)
_MLA_PARTNER = _rope_partner(MLA_HEADS * MLA_ROPE // 16, 16)
_KR_PARTNER = _rope_partner(MLA_ROPE // 16, 16)


def _layer_weights(l, w_in, mla_qa_g, mla_wq_b, mla_kva_g, mla_wkv_b, w_branch_diff, w_branch_na,
                   w_branch_mla, w_out, ln1_g, ln1_b, ln2_g, ln2_b, w_router, router_bias,
                   w_exp_gate, w_exp_up, w_exp_down, w_sh_gate, w_sh_up, w_sh_down):
    win = w_in[l].astype(BF16)
    wa = win[:, :3072]
    qa_kva = win[:, 3072:3712]
    kr = win[:, 3712:3744]
    kr4 = jnp.tile(kr, (1, LANES // MLA_ROPE))
    krp4 = jnp.tile(kr[:, _KR_PARTNER], (1, LANES // MLA_ROPE))
    wqb = mla_wq_b[l].astype(BF16)
    wq_rope = wqb[:, _QB_ROPE]
    wkvb = mla_wkv_b[l].astype(BF16)
    row = lambda v: v[l].reshape(1, -1).astype(F32)
    return {
        "wa": wa,
        "wp": wa[:, :2 * DIFF_W][:, _DIFF_PARTNER],
        "wm_ctx": jnp.concatenate([qa_kva, kr4], axis=1),
        "wm_lat": jnp.concatenate([qa_kva, kr4, krp4], axis=1),
        "qag": row(mla_qa_g), "kvag": row(mla_kva_g),
        "wqb": jnp.concatenate([wqb[:, _QB_NOPE], wq_rope], axis=1),
        "wqp": wq_rope[:, _MLA_PARTNER],
        "wkvb": jnp.concatenate([wkvb[:, _KVB_NOPE], wkvb[:, _KVB_V]], axis=1),
        "wg": win[:, 3744:],
        "wbd": w_branch_diff[l].astype(BF16), "wbn": w_branch_na[l].astype(BF16),
        "wbm": w_branch_mla[l].astype(BF16), "wo": w_out[l].astype(BF16),
        "ln1_g": row(ln1_g), "ln1_b": row(ln1_b), "ln2_g": row(ln2_g), "ln2_b": row(ln2_b),
        "wr_t": w_router[l].T.astype(BF16), "rbias": router_bias[l].reshape(N_EXPERTS, 1).astype(F32),
        "weg": w_exp_gate[l], "weu": w_exp_up[l], "wed": w_exp_down[l],
        "wsg": w_sh_gate[l].astype(BF16), "wsu": w_sh_up[l].astype(BF16), "wsd": w_sh_down[l].astype(BF16),
    }


def _rope_tables():
    t = jnp.arange(DEC_SEQ)
    pos = [(t // GRID_W).astype(F32), (t % GRID_W).astype(F32)]

    def table(block):
        half = block // 4
        inv = ROPE_BASE ** (-jnp.arange(half, dtype=F32) / half)
        cos, sin = [], []
        for p in pos:
            ang = p[:, None] * inv[None, :]
            cos += [jnp.cos(ang), jnp.cos(ang)]
            sin += [-jnp.sin(ang), jnp.sin(ang)]
        reps = LANES // block
        return (jnp.tile(jnp.concatenate(cos, axis=1), (1, reps)),
                jnp.tile(jnp.concatenate(sin, axis=1), (1, reps)))

    cd, sd = table(DIFF_DH)
    cm, sm = table(MLA_ROPE)
    return {"cd": cd, "sd": sd, "cm": cm, "sm": sm}


def kernel(x_prompt, x_sample, cache_diff_k, cache_diff_v, cache_na_k, cache_na_v, cache_mla_ckv, cache_mla_krope, c, c_ctx, w_mod, b_mod, w_in, diff_lambda, diff_subln_g, na_rpb, mla_qa_g, mla_wq_b, mla_kva_g, mla_wkv_b, w_branch_diff, w_branch_na, w_branch_mla, w_out, ln1_g, ln1_b, ln2_g, ln2_b, w_router, router_bias, w_exp_gate, w_exp_up, w_exp_down, w_sh_gate, w_sh_up, w_sh_down):
    t_ctx = BATCH * SEQ
    t_lat = DEC_BATCH * DEC_SEQ
    cvec = jnp.concatenate([c, c_ctx[None, :], jnp.zeros((MOD_ROWS - DEC_BATCH - 1, D_MODEL), F32)], axis=0)
    mods = _modulation(cvec, w_mod, b_mod).reshape(DEPTH, MOD_ROWS, 6, D_MODEL)
    rope = _rope_tables()
    ck_d = cache_diff_k.reshape(DEC_BATCH, DEPTH, PAST_LEN, DIFF_W)
    cv_d = cache_diff_v.reshape(DEC_BATCH, DEPTH, PAST_LEN, DIFF_W)
    ck_n = cache_na_k.reshape(DEC_BATCH, DEPTH, PAST_LEN, NA_W)
    cv_n = cache_na_v.reshape(DEC_BATCH, DEPTH, PAST_LEN, NA_W)

    xc = x_prompt.reshape(t_ctx, D_MODEL)
    xl = x_sample.reshape(t_lat, D_MODEL)
    states = []
    for l in range(DEPTH):
        W = _layer_weights(l, w_in, mla_qa_g, mla_wq_b, mla_kva_g, mla_wkv_b, w_branch_diff, w_branch_na,
                           w_branch_mla, w_out, ln1_g, ln1_b, ln2_g, ln2_b, w_router, router_bias,
                           w_exp_gate, w_exp_up, w_exp_down, w_sh_gate, w_sh_up, w_sh_down)
        lam = diff_lambda[l].astype(F32)
        subln = diff_subln_g[l].reshape(1, LANES).astype(F32)
        mod_c = mods[l, DEC_BATCH:DEC_BATCH + 1]
        mod_l = mods[l, :DEC_BATCH]

        Wc = dict(W, wm=W["wm_ctx"])
        names = ["dq", "nq", "mqn", "mqr", "kr4", "kn", "vn", "dk", "dv", "nk", "nv", "ckv", "kr"]
        pc = dict(zip(names, _project(xc, mod_c, Wc, None, latent=False)))
        od, on, om = _attn_ctx(pc, lam, subln, l)
        x1, h2 = _merge(xc, mod_c, od, on, om, W, latent=False)
        xc = _moe(h2, x1, mod_c, W, latent=False)
        states.append(pc)

        Wl = dict(W, wm=W["wm_lat"])
        names = ["dq", "dk", "dv", "nq", "nk", "nv", "mqn", "mqr", "kr4", "kn", "vn"]
        pll = dict(zip(names, _project(xl, mod_l, Wl, rope, latent=True)))
        od = _attn_diff_lat(pll, ck_d, cv_d, lam, subln, l)
        on = _attn_na_lat(pll, ck_n, cv_n, _na_bias(na_rpb[l].astype(F32)), l)
        kc, vc, krc = _mla_cache(cache_mla_ckv, cache_mla_krope, W["wkvb"], l)
        om = _attn_mla_lat(pll, kc, vc, krc)
        x1, h2 = _merge(xl, mod_l, od, on, om, W, latent=True)
        xl = _moe(h2, x1, mod_l, W, latent=True)

    def stack(name, shape):
        return jnp.stack([s[name].reshape((BATCH, SEQ) + shape) for s in states], axis=1)

    return (xc.reshape(BATCH, SEQ, D_MODEL), xl.reshape(DEC_BATCH, DEC_SEQ, D_MODEL),
            stack("dk", (DIFF_HEADS, 2 * DIFF_DH)), stack("dv", (DIFF_HEADS, 2 * DIFF_DH)),
            stack("nk", (NA_HEADS, NA_DH)), stack("nv", (NA_HEADS, NA_DH)),
            stack("ckv", (MLA_KV_LORA,)), stack("kr", (MLA_ROPE,)))
```

```python
import functools

import numpy as np
import jax
import jax.numpy as jnp
from jax import lax
from jax.experimental import pallas as pl
from jax.experimental.pallas import tpu as pltpu
from jax.experimental.pallas import tpu_sc as plsc

D_MODEL = 1024
BATCH = 32
SEQ = 256
DEPTH = 2
DEC_BATCH = 8
DEC_SEQ = 2048
PAST_LEN = 512
GRID_W = 64
GRID_ROWS = DEC_SEQ // GRID_W
ROPE_BASE = 10000.0
DIFF_HEADS = 4
DIFF_DH = 64
DIFF_W = 512
NA_HEADS = 8
NA_DH = 64
NA_W = 512
NA_WIN_ROWS = 8
NA_WIN_COLS = 16
MLA_HEADS = 8
MLA_Q_LORA = 384
MLA_KV_LORA = 256
MLA_NOPE = 64
MLA_ROPE = 32
MLA_V = 64
MLA_W = 512
N_EXPERTS = 64
N_GROUPS = 8
GROUP_SIZE = N_EXPERTS // N_GROUPS
TOPK_GROUPS = 4
TOP_K = 8
EXPERT_FF = 256
SHARED_FF = 256
ROUTED_SCALE = 2.5
DN_ALPHA = (2 * DEPTH) ** 0.25
LN_EPS = 1e-5
RMS_EPS = 1e-6

F32 = jnp.float32
BF16 = jnp.bfloat16

LANES = 128
VMEM_LIMIT_BYTES = 56 * 1024 * 1024
NEG_BIG = -1e30
LOG2E = 1.4426950408889634

DIFF_SCALE = DIFF_DH ** -0.5
NA_SCALE = NA_DH ** -0.5
MLA_SCALE = (MLA_NOPE + MLA_ROPE) ** -0.5

PROJ_TM = 512
ATT_TQ = 512
NA_TQ = 256
NA_Q_ROWS = NA_TQ // GRID_W
NA_KEY_ROWS = NA_Q_ROWS + NA_WIN_ROWS
NA_KEYS = NA_KEY_ROWS * GRID_W
MOE_TILE_MIN = 128
MOE_TILE_MAX = 512
SC_ROWS = 128
COMBINE_TM = 256


def _dot(a, b):
    return jnp.dot(a, b, preferred_element_type=F32)


def _dot_nt(a, b):
    return lax.dot_general(a, b, (((1,), (1,)), ((), ())), preferred_element_type=F32)


def _sigmoid(x):
    return 1.0 / (1.0 + jnp.exp(-x))


def _silu(x):
    return x * _sigmoid(x)


def _params(*sem):
    return pltpu.CompilerParams(dimension_semantics=sem, vmem_limit_bytes=VMEM_LIMIT_BYTES)


def _full(shape):
    n = len(shape)
    return pl.BlockSpec(shape, lambda *_: (0,) * n)


def _layer_norm(y, g, b):
    mu = jnp.mean(y, axis=-1, keepdims=True)
    yc = y - mu
    var = jnp.mean(yc * yc, axis=-1, keepdims=True)
    return yc * lax.rsqrt(var + LN_EPS) * g + b


def _rms(x, g):
    return x * lax.rsqrt(jnp.mean(x * x, axis=-1, keepdims=True) + RMS_EPS) * g


HIGH_HALF = -65536
PACKED_W = D_MODEL // 2


def _pack_bf16_pairs(x):
    n = x.shape[1] // 2
    bits = lax.bitcast_convert_type(x.astype(BF16).astype(F32), jnp.int32)
    return lax.shift_right_logical(bits[:, :n], 16) | (bits[:, n:] & HIGH_HALF)


def _unpack_bf16_pairs(w):
    lo = lax.bitcast_convert_type(lax.shift_left(w, 16), F32)
    hi = lax.bitcast_convert_type(w & HIGH_HALF, F32)
    return jnp.concatenate([lo, hi], axis=1)


MOD_ROWS = 16
MOD_TN = 1536


def _mod_kernel(c_ref, w_ref, b_ref, o_ref):
    s = _silu(c_ref[...]).astype(BF16)
    o_ref[0] = _dot(s, w_ref[0].astype(BF16)) + b_ref[0]


def _modulation(cvec, w_mod, b_mod):
    n = 6 * D_MODEL
    return pl.pallas_call(
        _mod_kernel,
        out_shape=jax.ShapeDtypeStruct((DEPTH, MOD_ROWS, n), F32),
        grid=(DEPTH, n // MOD_TN),
        in_specs=[
            pl.BlockSpec((MOD_ROWS, D_MODEL), lambda l, j: (0, 0)),
            pl.BlockSpec((1, D_MODEL, MOD_TN), lambda l, j: (l, 0, j)),
            pl.BlockSpec((1, 1, MOD_TN), lambda l, j: (l, 0, j)),
        ],
        out_specs=pl.BlockSpec((1, MOD_ROWS, MOD_TN), lambda l, j: (l, 0, j)),
        compiler_params=_params("parallel", "parallel"),
        name="modulation",
    )(cvec, w_mod, b_mod.reshape(DEPTH, 1, n))


def _proj_common(x_ref, mod_ref, wa_ref, wm_ref, qag_ref, kvag_ref, wqb_ref, wkvb_ref):
    m = mod_ref[0]
    h = (x_ref[...] * (1.0 + m[1:2]) + m[0:1]).astype(BF16)
    a = _dot(h, wa_ref[...])
    mm = _dot(h, wm_ref[...])
    qan = _rms(mm[:, :MLA_Q_LORA], qag_ref[...]).astype(BF16)
    mq = _dot(qan, wqb_ref[...])
    ckv = _rms(mm[:, MLA_Q_LORA:MLA_Q_LORA + MLA_KV_LORA], kvag_ref[...])
    kv = _dot(ckv.astype(BF16), wkvb_ref[...])
    return h, a, mm, qan, mq, ckv, kv


def _proj_ctx_kernel(x_ref, mod_ref, wa_ref, wm_ref, qag_ref, kvag_ref, wqb_ref, wkvb_ref,
                     dq_ref, nq_ref, mqn_ref, mqr_ref, kr4_ref, kn_ref, vn_ref,
                     dk_ref, dv_ref, nk_ref, nv_ref, ckv_ref, kr_ref):
    _, a, mm, _, mq, ckv, kv = _proj_common(x_ref, mod_ref, wa_ref, wm_ref, qag_ref, kvag_ref,
                                            wqb_ref, wkvb_ref)
    dq_ref[...] = (a[:, 0:512] * DIFF_SCALE).astype(BF16)
    dk_ref[...] = a[:, 512:1024]
    dv_ref[...] = a[:, 1024:1536]
    nq_ref[...] = (a[:, 1536:2048] * NA_SCALE).astype(BF16)
    nk_ref[...] = a[:, 2048:2560]
    nv_ref[...] = a[:, 2560:3072]
    mqn_ref[...] = (mq[:, :512] * MLA_SCALE).astype(BF16)
    mqr_ref[...] = (mq[:, 512:768] * MLA_SCALE).astype(BF16)
    kr4 = mm[:, 640:768]
    kr4_ref[...] = kr4.astype(BF16)
    kr_ref[...] = kr4[:, :MLA_ROPE]
    ckv_ref[...] = ckv
    kn_ref[...] = kv[:, :512].astype(BF16)
    vn_ref[...] = kv[:, 512:].astype(BF16)


def _proj_lat_kernel(x_ref, mod_ref, wa_ref, wm_ref, qag_ref, kvag_ref, wqb_ref, wkvb_ref,
                     wp_ref, wqp_ref, cd_ref, sd_ref, cm_ref, sm_ref,
                     dq_ref, dk_ref, dv_ref, nq_ref, nk_ref, nv_ref,
                     mqn_ref, mqr_ref, kr4_ref, kn_ref, vn_ref):
    h, a, mm, qan, mq, _, kv = _proj_common(x_ref, mod_ref, wa_ref, wm_ref, qag_ref, kvag_ref,
                                            wqb_ref, wkvb_ref)
    ap = _dot(h, wp_ref[...])
    mqp = _dot(qan, wqp_ref[...])
    cd = cd_ref[...]
    sd = sd_ref[...]
    cm = cm_ref[...]
    sm = sm_ref[...]
    for j in range(DIFF_W // LANES):
        lo, hi = LANES * j, LANES * (j + 1)
        dq_ref[:, lo:hi] = ((a[:, lo:hi] * cd + ap[:, lo:hi] * sd) * (DIFF_SCALE * LOG2E)).astype(BF16)
        dk_ref[:, lo:hi] = (a[:, 512 + lo:512 + hi] * cd + ap[:, 512 + lo:512 + hi] * sd).astype(BF16)
    dv_ref[...] = a[:, 1024:1536].astype(BF16)
    nq_ref[...] = (a[:, 1536:2048] * (NA_SCALE * LOG2E)).astype(BF16)
    nk_ref[...] = a[:, 2048:2560].astype(BF16)
    nv_ref[...] = a[:, 2560:3072].astype(BF16)
    mqn_ref[...] = (mq[:, :512] * (MLA_SCALE * LOG2E)).astype(BF16)
    for j in range(2):
        lo, hi = LANES * j, LANES * (j + 1)
        mqr_ref[:, lo:hi] = ((mq[:, 512 + lo:512 + hi] * cm + mqp[:, lo:hi] * sm)
                             * (MLA_SCALE * LOG2E)).astype(BF16)
    kr4_ref[...] = (mm[:, 640:768] * cm + mm[:, 768:896] * sm).astype(BF16)
    kn_ref[...] = kv[:, :512].astype(BF16)
    vn_ref[...] = kv[:, 512:].astype(BF16)


def _project(x, mod, W, rope, *, latent):
    t = x.shape[0]
    tm = PROJ_TM
    tokens_per_batch = DEC_SEQ if latent else t
    steps_per_batch = tokens_per_batch // tm
    row = lambda w: pl.BlockSpec((tm, w), lambda i: (i, 0))
    common_in = [
        row(D_MODEL),
        pl.BlockSpec((1, 6, D_MODEL), lambda i: (i // steps_per_batch, 0, 0)),
        _full(W["wa"].shape), _full(W["wm"].shape), _full((1, MLA_Q_LORA)), _full((1, MLA_KV_LORA)),
        _full(W["wqb"].shape), _full(W["wkvb"].shape),
    ]
    common_args = [x, mod, W["wa"], W["wm"], W["qag"], W["kvag"], W["wqb"], W["wkvb"]]
    bf = lambda w: jax.ShapeDtypeStruct((t, w), BF16)
    f32 = lambda w: jax.ShapeDtypeStruct((t, w), F32)
    if latent:
        tab = pl.BlockSpec((tm, LANES), lambda i: (i % steps_per_batch, 0))
        widths = [512, 512, 512, 512, 512, 512, 512, 256, 128, 512, 512]
        return pl.pallas_call(
            _proj_lat_kernel,
            out_shape=[bf(w) for w in widths],
            grid=(t // tm,),
            in_specs=common_in + [_full(W["wp"].shape), _full(W["wqp"].shape), tab, tab, tab, tab],
            out_specs=[row(w) for w in widths],
            compiler_params=_params("parallel"),
            name="proj_lat",
        )(*common_args, W["wp"], W["wqp"], rope["cd"], rope["sd"], rope["cm"], rope["sm"])
    bf_w = [512, 512, 512, 256, 128, 512, 512]
    f32_w = [512, 512, 512, 512, 256, 32]
    return pl.pallas_call(
        _proj_ctx_kernel,
        out_shape=[bf(w) for w in bf_w] + [f32(w) for w in f32_w],
        grid=(t // tm,),
        in_specs=common_in,
        out_specs=[row(w) for w in bf_w + f32_w],
        compiler_params=_params("parallel"),
        name="proj_ctx",
    )(*common_args)


def _lane_iota():
    return lax.broadcasted_iota(jnp.int32, (1, LANES), 1)


def _softmax_parts(parts, exp):
    m = functools.reduce(jnp.maximum, [jnp.max(s, axis=-1, keepdims=True) for s in parts])
    es = [exp(s - m) for s in parts]
    l = functools.reduce(lambda u, v: u + v, [jnp.sum(e, axis=-1, keepdims=True) for e in es])
    return es, l


def _diff_lambda(lam_ref, layer):
    lp = lam_ref[...]
    lam_init = 0.8 - 0.6 * float(np.exp(-0.3 * layer))
    s1 = jnp.sum(lp[0:1] * lp[1:2], axis=-1, keepdims=True)
    s2 = jnp.sum(lp[2:3] * lp[3:4], axis=-1, keepdims=True)
    return jnp.exp(s1) - jnp.exp(s2) + lam_init, lam_init


def _diff_heads(q_ref, ks, vs, lam_ref, g_ref, o_ref, layer, exp):
    lam, lam_init = _diff_lambda(lam_ref, layer)
    first_map = _lane_iota() < DIFF_DH
    g = g_ref[...]
    for h in range(DIFF_HEADS):
        hs = slice(LANES * h, LANES * (h + 1))
        q = q_ref[:, hs]
        q1 = jnp.where(first_map, q, jnp.zeros_like(q))
        q2 = jnp.where(first_map, jnp.zeros_like(q), q)
        kk = [k[:, hs].astype(BF16) for k in ks]
        e1, l1 = _softmax_parts([_dot_nt(q1, k) for k in kk], exp)
        e2, l2 = _softmax_parts([_dot_nt(q2, k) for k in kk], exp)
        c1 = 1.0 / l1
        c2 = lam / l2
        o = None
        for a1, a2, v in zip(e1, e2, vs):
            part = _dot((a1 * c1 - a2 * c2).astype(BF16), v[:, hs].astype(BF16))
            o = part if o is None else o + part
        o = _rms(o, g) * (1.0 - lam_init)
        o_ref[:, hs] = o.astype(BF16)


def _pair_heads(q_of, k_of, v_of, bias_of, o_ref, n_pairs, exp, stack=False):
    first = _lane_iota() < 64
    for j in range(n_pairs):
        ps = slice(LANES * j, LANES * (j + 1))
        ks = k_of(j)
        vs = v_of(j)
        qs = [q_of(j, 0), q_of(j, 1)]
        if stack:
            qs = [jnp.concatenate(qs, axis=0)]
        outs = []
        for q in qs:
            ss = [_dot_nt(q, k) for k in ks]
            ss = [s if b is None else s + b for s, b in zip(ss, bias_of(j))]
            es, l = _softmax_parts(ss, exp)
            o = None
            for e, v in zip(es, vs):
                part = _dot(e.astype(BF16), v)
                o = part if o is None else o + part
            outs.append(o / l)
        if stack:
            n = outs[0].shape[0] // 2
            outs = [outs[0][:n], outs[0][n:]]
        o_ref[:, ps] = jnp.where(first, outs[0], outs[1]).astype(BF16)


def _na_q(q_ref):
    first = _lane_iota() < NA_DH

    def q_of(j, hh):
        q = q_ref[:, LANES * j:LANES * (j + 1)]
        keep = first if hh == 0 else jnp.logical_not(first)
        return jnp.where(keep, q, jnp.zeros_like(q))
    return q_of


def _mla_q(qn_ref, qr_ref):
    lane = _lane_iota()
    first = lane < MLA_NOPE

    def q_of(j, hh):
        h = 2 * j + hh
        qn = qn_ref[:, LANES * j:LANES * (j + 1)]
        keep = first if hh == 0 else jnp.logical_not(first)
        qn = jnp.where(keep, qn, jnp.zeros_like(qn))
        qr = qr_ref[:, LANES * (h // 4):LANES * (h // 4 + 1)]
        qr = jnp.where((lane // MLA_ROPE) == (h % 4), qr, jnp.zeros_like(qr))
        return jnp.concatenate([qn, qr], axis=1)
    return q_of


def _attn_ctx_kernel(layer, dq_ref, dk_ref, dv_ref, nq_ref, nk_ref, nv_ref,
                     mqn_ref, mqr_ref, kn_ref, kr4_ref, vn_ref, lam_ref, g_ref,
                     od_ref, on_ref, om_ref):
    _diff_heads(dq_ref, [dk_ref], [dv_ref], lam_ref, g_ref, od_ref, layer, jnp.exp)
    none = lambda j: [None]
    pair = lambda j: slice(LANES * j, LANES * (j + 1))
    _pair_heads(_na_q(nq_ref),
                lambda j: [nk_ref[:, pair(j)].astype(BF16)],
                lambda j: [nv_ref[:, pair(j)].astype(BF16)],
                none, on_ref, NA_HEADS // 2, jnp.exp)
    kr4 = kr4_ref[...]
    _pair_heads(_mla_q(mqn_ref, mqr_ref),
                lambda j: [jnp.concatenate([kn_ref[:, pair(j)], kr4], axis=1)],
                lambda j: [vn_ref[:, pair(j)]],
                none, om_ref, MLA_HEADS // 2, jnp.exp)


def _attn_ctx(p, lam, g, layer):
    t = p["dq"].shape[0]
    row = lambda w: pl.BlockSpec((SEQ, w), lambda b: (b, 0))
    names = ["dq", "dk", "dv", "nq", "nk", "nv", "mqn", "mqr", "kn", "kr4", "vn"]
    out = jax.ShapeDtypeStruct((t, 512), BF16)
    return pl.pallas_call(
        functools.partial(_attn_ctx_kernel, layer),
        out_shape=[out, out, out],
        grid=(t // SEQ,),
        in_specs=[row(p[n].shape[1]) for n in names] + [_full((4, DIFF_DH)), _full((1, LANES))],
        out_specs=[row(512)] * 3,
        compiler_params=_params("parallel"),
        name="attn_ctx",
    )(*[p[n] for n in names], lam, g)


def _cache_spec(width, layer):
    return pl.BlockSpec((None, None, PAST_LEN, width), lambda b, q: (b, layer, 0, 0))


def _batch_spec(width):
    return pl.BlockSpec((None, DEC_SEQ, width), lambda b, q: (b, 0, 0))


def _qtile_spec(width, tq):
    steps = DEC_SEQ // tq
    return pl.BlockSpec((tq, width), lambda b, q: (b * steps + q, 0))


def _attn_diff_lat_kernel(layer, q_ref, kc_ref, vc_ref, kn_ref, vn_ref, lam_ref, g_ref, o_ref):
    _diff_heads(q_ref, [kc_ref, kn_ref], [vc_ref, vn_ref], lam_ref, g_ref, o_ref, layer, jnp.exp2)


def _attn_diff_lat(p, cache_k, cache_v, lam, g, layer):
    t = p["dq"].shape[0]
    b3 = lambda a: a.reshape(DEC_BATCH, DEC_SEQ, a.shape[1])
    return pl.pallas_call(
        functools.partial(_attn_diff_lat_kernel, layer),
        out_shape=jax.ShapeDtypeStruct((t, DIFF_W), BF16),
        grid=(DEC_BATCH, DEC_SEQ // ATT_TQ),
        in_specs=[_qtile_spec(DIFF_W, ATT_TQ), _cache_spec(DIFF_W, layer), _cache_spec(DIFF_W, layer),
                  _batch_spec(DIFF_W), _batch_spec(DIFF_W),
                  pl.BlockSpec((4, DIFF_DH), lambda b, q: (0, 0)),
                  pl.BlockSpec((1, LANES), lambda b, q: (0, 0))],
        out_specs=_qtile_spec(DIFF_W, ATT_TQ),
        compiler_params=_params("parallel", "parallel"),
        name="attn_diff_lat",
    )(p["dq"], cache_k, cache_v, b3(p["dk"]), b3(p["dv"]), lam, g)


def _na_key_start(q):
    return jnp.clip(q * NA_Q_ROWS - NA_WIN_ROWS // 2, 0, GRID_ROWS - NA_KEY_ROWS)


def _attn_na_lat_kernel(q_ref, kc_ref, vc_ref, kn_ref, vn_ref, bias_ref, o_ref):
    start = pl.multiple_of(_na_key_start(pl.program_id(1)) * GRID_W, GRID_W)
    kw = kn_ref[pl.ds(start, NA_KEYS), :]
    vw = vn_ref[pl.ds(start, NA_KEYS), :]
    pair = lambda j: slice(LANES * j, LANES * (j + 1))
    _pair_heads(_na_q(q_ref),
                lambda j: [kc_ref[:, pair(j)].astype(BF16), kw[:, pair(j)]],
                lambda j: [vc_ref[:, pair(j)].astype(BF16), vw[:, pair(j)]],
                lambda j: [None, bias_ref[j]],
                o_ref, NA_HEADS // 2, jnp.exp2, stack=True)


def _na_bias_tables():
    n_blocks = DEC_SEQ // NA_TQ
    qr = np.arange(NA_Q_ROWS)
    kr = np.arange(NA_KEY_ROWS)
    row_sel, row_ok = [], []
    for qb in range(n_blocks):
        ks = int(np.clip(qb * NA_Q_ROWS - NA_WIN_ROWS // 2, 0, GRID_ROWS - NA_KEY_ROWS))
        r = qb * NA_Q_ROWS + qr
        r0 = np.clip(r - NA_WIN_ROWS // 2, 0, GRID_ROWS - NA_WIN_ROWS)
        krow = ks + kr
        ok = (krow[None, :] >= r0[:, None]) & (krow[None, :] < r0[:, None] + NA_WIN_ROWS)
        off = np.where(ok, krow[None, :] - r[:, None] + NA_WIN_ROWS - 1, 2 * NA_WIN_ROWS - 1)
        sel = off[:, :, None] == np.arange(2 * NA_WIN_ROWS)
        row_sel.append(sel.astype(np.float32))
        row_ok.append(ok)
    kinds, kind_of_block = [], []
    for qb in range(n_blocks):
        for n, other in enumerate(kinds):
            if np.array_equal(row_sel[qb], row_sel[other]):
                kind_of_block.append(n)
                break
        else:
            kind_of_block.append(len(kinds))
            kinds.append(qb)
    c = np.arange(GRID_W)
    c0 = np.clip(c - NA_WIN_COLS // 2, 0, GRID_W - NA_WIN_COLS)
    col_ok = (c[None, :] >= c0[:, None]) & (c[None, :] < c0[:, None] + NA_WIN_COLS)
    coff = np.where(col_ok, c[None, :] - c[:, None] + NA_WIN_COLS - 1, 2 * NA_WIN_COLS - 1)
    col_sel = (coff[:, :, None] == np.arange(2 * NA_WIN_COLS)).astype(np.float32)
    rsel = np.stack([row_sel[qb] for qb in kinds])
    return rsel, col_sel, kind_of_block


_NA_ROW_SEL, _NA_COL_SEL, _NA_KIND_OF_BLOCK = _na_bias_tables()


def _na_bias(rpb):
    hp = lax.Precision.HIGHEST
    table = jnp.pad(rpb * LOG2E, ((0, 0), (0, 1), (0, 1)), constant_values=NEG_BIG)
    cols = jnp.einsum("hij,ckj->hick", table, _NA_COL_SEL, precision=hp)
    b = jnp.einsum("nqri,hick->nhqcrk", _NA_ROW_SEL, cols, precision=hp)
    return b.reshape(_NA_ROW_SEL.shape[0], NA_HEADS // 2, 2 * NA_TQ, NA_KEYS).astype(F32)


def _na_kind(q):
    return (q > 0).astype(jnp.int32) + (q == DEC_SEQ // NA_TQ - 1).astype(jnp.int32)


def _attn_na_lat(p, cache_k, cache_v, bias, layer):
    assert _NA_KIND_OF_BLOCK == [0] + [1] * (DEC_SEQ // NA_TQ - 2) + [2]
    t = p["nq"].shape[0]
    b3 = lambda a: a.reshape(DEC_BATCH, DEC_SEQ, a.shape[1])
    return pl.pallas_call(
        _attn_na_lat_kernel,
        out_shape=jax.ShapeDtypeStruct((t, NA_W), BF16),
        grid=(DEC_BATCH, DEC_SEQ // NA_TQ),
        in_specs=[_qtile_spec(NA_W, NA_TQ), _cache_spec(NA_W, layer), _cache_spec(NA_W, layer),
                  _batch_spec(NA_W), _batch_spec(NA_W),
                  pl.BlockSpec((None, NA_HEADS // 2, 2 * NA_TQ, NA_KEYS),
                               lambda b, q: (_na_kind(q), 0, 0, 0))],
        out_specs=_qtile_spec(NA_W, NA_TQ),
        compiler_params=_params("parallel", "parallel"),
        name="attn_na_lat",
    )(p["nq"], cache_k, cache_v, b3(p["nk"]), b3(p["nv"]), bias)


def _mla_cache_kernel(ckv_ref, kr_ref, wkvb_ref, rep_ref, kc_ref, vc_ref, krc_ref):
    kv = _dot(ckv_ref[...].astype(BF16), wkvb_ref[...])
    kc_ref[...] = kv[:, :512].astype(BF16)
    vc_ref[...] = kv[:, 512:].astype(BF16)
    krc_ref[...] = _dot(kr_ref[...].astype(BF16), rep_ref[...]).astype(BF16)


def _mla_cache(cache_ckv, cache_kr, wkvb, layer):
    rep = jnp.asarray(np.tile(np.eye(MLA_ROPE, dtype=np.float32), (1, LANES // MLA_ROPE)), BF16)
    spec_in = lambda w: pl.BlockSpec((None, None, PAST_LEN, w), lambda b: (b, layer, 0, 0))
    spec_out = lambda w: pl.BlockSpec((None, PAST_LEN, w), lambda b: (b, 0, 0))
    shp = lambda w: jax.ShapeDtypeStruct((DEC_BATCH, PAST_LEN, w), BF16)
    return pl.pallas_call(
        _mla_cache_kernel,
        out_shape=[shp(512), shp(512), shp(LANES)],
        grid=(DEC_BATCH,),
        in_specs=[spec_in(MLA_KV_LORA), spec_in(MLA_ROPE), _full(wkvb.shape), _full(rep.shape)],
        out_specs=[spec_out(512), spec_out(512), spec_out(LANES)],
        compiler_params=_params("parallel"),
        name="mla_cache",
    )(cache_ckv, cache_kr, wkvb, rep)


def _attn_mla_lat_kernel(qn_ref, qr_ref, kc_ref, krc_ref, vc_ref, kn_ref, krn_ref, vn_ref, o_ref):
    pair = lambda j: slice(LANES * j, LANES * (j + 1))
    krc = krc_ref[...]
    krn = krn_ref[...]
    _pair_heads(_mla_q(qn_ref, qr_ref),
                lambda j: [jnp.concatenate([kc_ref[:, pair(j)], krc], axis=1),
                           jnp.concatenate([kn_ref[:, pair(j)], krn], axis=1)],
                lambda j: [vc_ref[:, pair(j)], vn_ref[:, pair(j)]],
                lambda j: [None, None],
                o_ref, MLA_HEADS // 2, jnp.exp2)


def _attn_mla_lat(p, kc, vc, krc):
    t = p["mqn"].shape[0]
    b3 = lambda a: a.reshape(DEC_BATCH, DEC_SEQ, a.shape[1])
    cspec = lambda w: pl.BlockSpec((None, PAST_LEN, w), lambda b, q: (b, 0, 0))
    return pl.pallas_call(
        _attn_mla_lat_kernel,
        out_shape=jax.ShapeDtypeStruct((t, MLA_W), BF16),
        grid=(DEC_BATCH, DEC_SEQ // ATT_TQ),
        in_specs=[_qtile_spec(512, ATT_TQ), _qtile_spec(256, ATT_TQ), cspec(512), cspec(LANES), cspec(512),
                  _batch_spec(512), _batch_spec(LANES), _batch_spec(512)],
        out_specs=_qtile_spec(MLA_W, ATT_TQ),
        compiler_params=_params("parallel", "parallel"),
        name="attn_mla_lat",
    )(p["mqn"], p["mqr"], kc, krc, vc, b3(p["kn"]), b3(p["kr4"]), b3(p["vn"]))


def _merge_kernel(x_ref, mod_ref, od_ref, on_ref, om_ref, wg_ref, wbd_ref, wbn_ref, wbm_ref, wo_ref,
                  g_ref, b_ref, x1_ref, h2_ref):
    x = x_ref[...]
    m = mod_ref[0]
    h = (x * (1.0 + m[1:2]) + m[0:1]).astype(BF16)
    gates = _dot(h, wg_ref[...])
    mix = (_sigmoid(gates[:, 0:1024]) * _dot(od_ref[...], wbd_ref[...])
           + _sigmoid(gates[:, 1024:2048]) * _dot(on_ref[...], wbn_ref[...])
           + _sigmoid(gates[:, 2048:3072]) * _dot(om_ref[...], wbm_ref[...]))
    out = _dot(mix.astype(BF16), wo_ref[...])
    x1 = _layer_norm(DN_ALPHA * x + m[2:3] * out, g_ref[...], b_ref[...])
    x1_ref[...] = x1
    h2_ref[...] = _pack_bf16_pairs(x1 * (1.0 + m[4:5]) + m[3:4])


def _merge(x, mod, od, on, om, W, *, latent):
    t = x.shape[0]
    tm = PROJ_TM
    steps_per_batch = (DEC_SEQ if latent else t) // tm
    row = lambda w: pl.BlockSpec((tm, w), lambda i: (i, 0))
    return pl.pallas_call(
        _merge_kernel,
        out_shape=[jax.ShapeDtypeStruct((t, D_MODEL), F32), jax.ShapeDtypeStruct((t, PACKED_W), jnp.int32)],
        grid=(t // tm,),
        in_specs=[row(D_MODEL),
                  pl.BlockSpec((1, 6, D_MODEL), lambda i: (i // steps_per_batch, 0, 0)),
                  row(512), row(512), row(512),
                  _full(W["wg"].shape), _full(W["wbd"].shape), _full(W["wbn"].shape),
                  _full(W["wbm"].shape), _full(W["wo"].shape),
                  _full((1, D_MODEL)), _full((1, D_MODEL))],
        out_specs=[row(D_MODEL), row(PACKED_W)],
        compiler_params=_params("parallel"),
        name="merge",
    )(x, mod, od, on, om, W["wg"], W["wbd"], W["wbn"], W["wbm"], W["wo"], W["ln1_g"], W["ln1_b"])


def _first_index_of_max(vals, idx, sentinel):
    mx = functools.reduce(jnp.maximum, [jnp.max(v, axis=0, keepdims=True) for v in vals])
    cand = [jnp.min(jnp.where(v == mx, i, sentinel), axis=0, keepdims=True) for v, i in zip(vals, idx)]
    return mx, functools.reduce(jnp.minimum, cand)


def _router_kernel(h_ref, wr_ref, bias_ref, tri_ref, eid_ref, rank_ref, wtok_ref, cnt_ref, base_ref):
    tm = h_ref.shape[0]

    @pl.when(pl.program_id(0) == 0)
    def _():
        base_ref[...] = jnp.zeros_like(base_ref)

    logits = _dot_nt(wr_ref[...], _unpack_bf16_pairs(h_ref[...]).astype(BF16))
    scores = _sigmoid(logits)
    biased = scores + bias_ref[...]
    member = lax.broadcasted_iota(jnp.int32, (GROUP_SIZE, tm), 0)
    slabs = [biased[GROUP_SIZE * g:GROUP_SIZE * (g + 1)] for g in range(N_GROUPS)]
    gscore = []
    for s in slabs:
        m1, first = _first_index_of_max([s], [member], GROUP_SIZE)
        m2 = jnp.max(jnp.where(member == first, -jnp.inf, s), axis=0, keepdims=True)
        gscore.append(m1 + m2)
    gs = jnp.concatenate(gscore, axis=0)
    gidx = lax.broadcasted_iota(jnp.int32, (N_GROUPS, tm), 0)
    gsel = jnp.zeros((N_GROUPS, tm), F32)
    for _ in range(TOPK_GROUPS):
        _, first = _first_index_of_max([gs], [gidx], N_GROUPS)
        pick = gidx == first
        gsel = jnp.where(pick, 1.0, gsel)
        gs = jnp.where(pick, -jnp.inf, gs)
    cur = [jnp.where(gsel[g:g + 1] > 0.0, slabs[g], -jnp.inf) for g in range(N_GROUPS)]
    eidx = [member + GROUP_SIZE * g for g in range(N_GROUPS)]
    sel = [jnp.zeros((GROUP_SIZE, tm), F32) for _ in range(N_GROUPS)]
    picks = []
    for _ in range(TOP_K):
        _, first = _first_index_of_max(cur, eidx, N_EXPERTS)
        pick = [eidx[g] == first for g in range(N_GROUPS)]
        picks.append((first, pick))
        for g in range(N_GROUPS):
            sel[g] = jnp.where(pick[g], 1.0, sel[g])
            cur[g] = jnp.where(pick[g], -jnp.inf, cur[g])
    w = [jnp.where(sel[g] > 0.0, scores[GROUP_SIZE * g:GROUP_SIZE * (g + 1)], 0.0) for g in range(N_GROUPS)]
    total = functools.reduce(lambda u, v: u + v, [jnp.sum(x, axis=0, keepdims=True) for x in w])
    w = [x / total * ROUTED_SCALE for x in w]

    sel_all = jnp.concatenate(sel, axis=0)
    incl = _dot(sel_all.astype(BF16), tri_ref[...])
    base = base_ref[:, 0:1]
    rank_all = incl - sel_all + base
    rank = [rank_all[GROUP_SIZE * g:GROUP_SIZE * (g + 1)] for g in range(N_GROUPS)]
    cnt = base + jnp.sum(sel_all, axis=1, keepdims=True)
    base_ref[...] = jnp.broadcast_to(cnt, base_ref.shape)
    cnt_ref[...] = jnp.broadcast_to(cnt, cnt_ref.shape)

    def picked(vals, pick):
        parts = [jnp.sum(jnp.where(p, v, 0.0), axis=0, keepdims=True) for p, v in zip(pick, vals)]
        return functools.reduce(lambda u, v: u + v, parts)

    eid_ref[...] = jnp.concatenate([first for first, _ in picks], axis=0)
    rank_ref[...] = jnp.concatenate([picked(rank, pick) for _, pick in picks], axis=0).astype(jnp.int32)
    w_rows = [picked(w, pick) for _, pick in picks] + [jnp.zeros((LANES - TOP_K, tm), F32)]
    wtok_ref[...] = jnp.concatenate(w_rows, axis=0).T


def _router(h2, wr_t, bias_col):
    t = h2.shape[0]
    tm = PROJ_TM
    tri = jnp.asarray(np.triu(np.ones((tm, tm), np.float32)), BF16)
    slots = lambda dt: jax.ShapeDtypeStruct((TOP_K, t), dt)
    return pl.pallas_call(
        _router_kernel,
        out_shape=[slots(jnp.int32), slots(jnp.int32), jax.ShapeDtypeStruct((t, LANES), F32),
                   jax.ShapeDtypeStruct((N_EXPERTS, LANES), F32)],
        grid=(t // tm,),
        in_specs=[pl.BlockSpec((tm, PACKED_W), lambda i: (i, 0)),
                  _full((N_EXPERTS, D_MODEL)), _full((N_EXPERTS, 1)), _full((tm, tm))],
        out_specs=[pl.BlockSpec((TOP_K, tm), lambda i: (0, i)), pl.BlockSpec((TOP_K, tm), lambda i: (0, i)),
                   pl.BlockSpec((tm, LANES), lambda i: (i, 0)), _full((N_EXPERTS, LANES))],
        scratch_shapes=[pltpu.VMEM((N_EXPERTS, LANES), F32)],
        compiler_params=_params("arbitrary"),
        name="router",
    )(h2, wr_t, bias_col, tri)


def _moe_tile(t):
    mean_rows = t * TOP_K // N_EXPERTS
    return int(min(max(pl.next_power_of_2(mean_rows // 2), MOE_TILE_MIN), MOE_TILE_MAX))


def _moe_tiles(t):
    return (t * TOP_K) // _moe_tile(t) + N_EXPERTS


def _moe_plan(eid, rank, cnt, t):
    tile = _moe_tile(t)
    counts = cnt[:, 0].astype(jnp.int32)
    tiles = jnp.maximum((counts + tile - 1) // tile, 1)
    ends = jnp.cumsum(tiles)
    starts = ends - tiles
    experts = jnp.arange(N_EXPERTS, dtype=jnp.int32)
    pos = rank + jnp.sum(jnp.where(eid[:, :, None] == experts, starts * tile, 0), axis=-1)
    tile_ids = jnp.arange(_moe_tiles(t), dtype=jnp.int32)
    owner = tile_ids[:, None] >= ends[None, :]
    tile_expert = jnp.minimum(jnp.sum(owner, axis=-1), N_EXPERTS - 1)
    is_owner = tile_expert[:, None] == experts[None, :]
    start_of = jnp.sum(jnp.where(is_owner, starts, 0), axis=-1)
    count_of = jnp.sum(jnp.where(is_owner, counts, 0), axis=-1)
    real = jnp.clip(count_of - (tile_ids - start_of) * tile, 0, tile)
    real = jnp.where(tile_ids < ends[-1], real, 0)
    return pos.astype(jnp.int32), tile_expert.astype(jnp.int32), real.astype(jnp.int32)


def _sc_workers():
    info = plsc.get_sparse_core_info()
    return info.num_cores, info.num_subcores


def _sc_mesh():
    return plsc.VectorSubcoreMesh(core_axis_name="core", subcore_axis_name="subcore")


def _sc_worker_id(n_cores):
    return lax.axis_index("subcore") * n_cores + lax.axis_index("core")


def _dispatch(h2, pos):
    t = h2.shape[0]
    n_cores, n_sub = _sc_workers()
    ch = SC_ROWS
    per_worker = t // (n_cores * n_sub)
    n_chunks = per_worker // ch
    pos_chunks = pos.reshape(TOP_K, t // ch, ch).transpose(1, 0, 2)

    @functools.partial(
        pl.kernel, mesh=_sc_mesh(),
        out_type=jax.ShapeDtypeStruct((_moe_tiles(t) * _moe_tile(t), PACKED_W), jnp.int32),
        scratch_types=[pltpu.VMEM((TOP_K, ch), jnp.int32), pltpu.VMEM((ch, PACKED_W), jnp.int32),
                       pltpu.SemaphoreType.DMA],
        name="moe_dispatch",
    )
    def run(h_hbm, pos_hbm, xs_hbm, idx_ref, rows_ref, sem):
        first = _sc_worker_id(n_cores) * n_chunks

        @pl.loop(0, n_chunks)
        def _(j):
            c = first + j
            pltpu.sync_copy(pos_hbm.at[c], idx_ref)
            pltpu.sync_copy(h_hbm.at[pl.ds(pl.multiple_of(c * ch, ch), ch)], rows_ref)
            copies = [pltpu.async_copy(rows_ref, xs_hbm.at[idx_ref.at[k]], sem) for k in range(TOP_K)]
            for cp in copies:
                cp.wait()

    return run(h2, pos_chunks)


def _gather_rows(ys, idx):
    n = idx.shape[0]
    n_cores, n_sub = _sc_workers()
    ch = SC_ROWS
    per_worker = n // (n_cores * n_sub)
    n_chunks = per_worker // ch

    @functools.partial(
        pl.kernel, mesh=_sc_mesh(),
        out_type=jax.ShapeDtypeStruct((n, PACKED_W), jnp.int32),
        scratch_types=[pltpu.VMEM((ch,), jnp.int32), pltpu.VMEM((ch, PACKED_W), jnp.int32),
                       pltpu.SemaphoreType.DMA],
        name="moe_gather",
    )
    def run(ys_hbm, idx_hbm, out_hbm, idx_ref, rows_ref, sem):
        first = _sc_worker_id(n_cores) * per_worker

        @pl.loop(0, n_chunks)
        def _(j):
            off = pl.multiple_of(first + j * ch, ch)
            pltpu.sync_copy(idx_hbm.at[pl.ds(off, ch)], idx_ref)
            pltpu.async_copy(ys_hbm.at[idx_ref], rows_ref, sem).wait()
            pltpu.sync_copy(rows_ref, out_hbm.at[pl.ds(off, ch)])

    return run(ys, idx)


def _ffn_kernel(te_ref, real_ref, xs_ref, weg_ref, weu_ref, wed_ref, ys_ref):
    i = pl.program_id(0)
    real = real_ref[i]

    @pl.when(real > 0)
    def _():
        row = lax.broadcasted_iota(jnp.int32, (xs_ref.shape[0], 1), 0)
        x = _unpack_bf16_pairs(jnp.where(row < real, xs_ref[...], 0)).astype(BF16)
        act = _silu(_dot(x, weg_ref[0])) * _dot(x, weu_ref[0])
        ys_ref[...] = _pack_bf16_pairs(_dot(act.astype(BF16), wed_ref[0]))

    @pl.when(real == 0)
    def _():
        ys_ref[...] = jnp.zeros_like(ys_ref)


def _ffn(xs, tile_expert, tile_real, W):
    tile = xs.shape[0] // tile_expert.shape[0]
    wspec = lambda shape: pl.BlockSpec((1,) + shape, lambda i, te, tr: (te[i], 0, 0))
    return pl.pallas_call(
        _ffn_kernel,
        out_shape=jax.ShapeDtypeStruct(xs.shape, jnp.int32),
        grid_spec=pltpu.PrefetchScalarGridSpec(
            num_scalar_prefetch=2,
            grid=(tile_expert.shape[0],),
            in_specs=[pl.BlockSpec((tile, PACKED_W), lambda i, te, tr: (i, 0)),
                      wspec((D_MODEL, EXPERT_FF)), wspec((D_MODEL, EXPERT_FF)), wspec((EXPERT_FF, D_MODEL))],
            out_specs=pl.BlockSpec((tile, PACKED_W), lambda i, te, tr: (i, 0))),
        compiler_params=_params("parallel"),
        name="moe_ffn",
    )(tile_expert, tile_real, xs, W["weg"], W["weu"], W["wed"])


def _combine_kernel(y_ref, h_ref, wtok_ref, x1_ref, mod_ref, wsg_ref, wsu_ref, wsd_ref,
                    g_ref, b_ref, out_ref):
    wt = wtok_ref[...]
    routed = None
    for k in range(TOP_K):
        part = wt[:, k:k + 1] * _unpack_bf16_pairs(y_ref[k])
        routed = part if routed is None else routed + part
    h = _unpack_bf16_pairs(h_ref[...]).astype(BF16)
    shared = _dot((_silu(_dot(h, wsg_ref[...])) * _dot(h, wsu_ref[...])).astype(BF16), wsd_ref[...])
    m = mod_ref[0]
    y2 = DN_ALPHA * x1_ref[...] + m[5:6] * (routed + shared)
    out_ref[...] = _layer_norm(y2, g_ref[...], b_ref[...])


def _combine(ytok, h2, wtok, x1, mod, W, *, latent):
    t = h2.shape[0]
    ct = COMBINE_TM
    steps_per_batch = (DEC_SEQ if latent else t) // ct
    row = lambda w: pl.BlockSpec((ct, w), lambda i: (i, 0))
    return pl.pallas_call(
        _combine_kernel,
        out_shape=jax.ShapeDtypeStruct((t, D_MODEL), F32),
        grid=(t // ct,),
        in_specs=[pl.BlockSpec((TOP_K, ct, PACKED_W), lambda i: (0, i, 0)),
                  row(PACKED_W), row(LANES), row(D_MODEL),
                  pl.BlockSpec((1, 6, D_MODEL), lambda i: (i // steps_per_batch, 0, 0)),
                  _full((D_MODEL, SHARED_FF)), _full((D_MODEL, SHARED_FF)), _full((SHARED_FF, D_MODEL)),
                  _full((1, D_MODEL)), _full((1, D_MODEL))],
        out_specs=row(D_MODEL),
        compiler_params=_params("parallel"),
        name="moe_combine",
    )(ytok, h2, wtok, x1, mod, W["wsg"], W["wsu"], W["wsd"], W["ln2_g"], W["ln2_b"])


def _moe(h2, x1, mod, W, *, latent):
    t = h2.shape[0]
    eid, rank, wtok, cnt = _router(h2, W["wr_t"], W["rbias"])
    pos, tile_expert, tile_real = _moe_plan(eid, rank, cnt, t)
    xs = _dispatch(h2, pos)
    ys = _ffn(xs, tile_expert, tile_real, W)
    ytok = _gather_rows(ys, pos.reshape(TOP_K * t)).reshape(TOP_K, t, PACKED_W)
    return _combine(ytok, h2, wtok, x1, mod, W, latent=latent)


def _rope_partner(n_blocks, block):
    half = block // 2
    i = np.arange(n_blocks * block)
    return np.where((i % block) < half, i + half, i - half)


_QB_NOPE = np.concatenate([np.arange(MLA_NOPE) + (MLA_NOPE + MLA_ROPE) * h for h in range(MLA_HEADS)])
_QB_ROPE = np.concatenate([np.arange(MLA_ROPE) + (MLA_NOPE + MLA_ROPE) * h + MLA_NOPE for h in range(MLA_HEADS)])
_KVB_NOPE = np.concatenate([np.arange(MLA_NOPE) + (MLA_NOPE + MLA_V) * h for h in range(MLA_HEADS)])
_KVB_V = np.concatenate([np.arange(MLA_V) + (MLA_NOPE + MLA_V) * h + MLA_NOPE for h in range(MLA_HEADS)])
_DIFF_PARTNER = _rope_partner(2 * DIFF_W // 32, 32)
_MLA_PARTNER = _rope_partner(MLA_HEADS * MLA_ROPE // 16, 16)
_KR_PARTNER = _rope_partner(MLA_ROPE // 16, 16)


def _layer_weights(l, w_in, mla_qa_g, mla_wq_b, mla_kva_g, mla_wkv_b, w_branch_diff, w_branch_na,
                   w_branch_mla, w_out, ln1_g, ln1_b, ln2_g, ln2_b, w_router, router_bias,
                   w_exp_gate, w_exp_up, w_exp_down, w_sh_gate, w_sh_up, w_sh_down):
    win = w_in[l].astype(BF16)
    wa = win[:, :3072]
    qa_kva = win[:, 3072:3712]
    kr = win[:, 3712:3744]
    kr4 = jnp.tile(kr, (1, LANES // MLA_ROPE))
    krp4 = jnp.tile(kr[:, _KR_PARTNER], (1, LANES // MLA_ROPE))
    wqb = mla_wq_b[l].astype(BF16)
    wq_rope = wqb[:, _QB_ROPE]
    wkvb = mla_wkv_b[l].astype(BF16)
    row = lambda v: v[l].reshape(1, -1).astype(F32)
    return {
        "wa": wa,
        "wp": wa[:, :2 * DIFF_W][:, _DIFF_PARTNER],
        "wm_ctx": jnp.concatenate([qa_kva, kr4], axis=1),
        "wm_lat": jnp.concatenate([qa_kva, kr4, krp4], axis=1),
        "qag": row(mla_qa_g), "kvag": row(mla_kva_g),
        "wqb": jnp.concatenate([wqb[:, _QB_NOPE], wq_rope], axis=1),
        "wqp": wq_rope[:, _MLA_PARTNER],
        "wkvb": jnp.concatenate([wkvb[:, _KVB_NOPE], wkvb[:, _KVB_V]], axis=1),
        "wg": win[:, 3744:],
        "wbd": w_branch_diff[l].astype(BF16), "wbn": w_branch_na[l].astype(BF16),
        "wbm": w_branch_mla[l].astype(BF16), "wo": w_out[l].astype(BF16),
        "ln1_g": row(ln1_g), "ln1_b": row(ln1_b), "ln2_g": row(ln2_g), "ln2_b": row(ln2_b),
        "wr_t": w_router[l].T.astype(BF16), "rbias": router_bias[l].reshape(N_EXPERTS, 1).astype(F32),
        "weg": w_exp_gate[l].astype(BF16), "weu": w_exp_up[l].astype(BF16), "wed": w_exp_down[l].astype(BF16),
        "wsg": w_sh_gate[l].astype(BF16), "wsu": w_sh_up[l].astype(BF16), "wsd": w_sh_down[l].astype(BF16),
    }


def _rope_tables():
    t = jnp.arange(DEC_SEQ)
    pos = [(t // GRID_W).astype(F32), (t % GRID_W).astype(F32)]

    def table(block):
        half = block // 4
        inv = ROPE_BASE ** (-jnp.arange(half, dtype=F32) / half)
        cos, sin = [], []
        for p in pos:
            ang = p[:, None] * inv[None, :]
            cos += [jnp.cos(ang), jnp.cos(ang)]
            sin += [-jnp.sin(ang), jnp.sin(ang)]
        reps = LANES // block
        return (jnp.tile(jnp.concatenate(cos, axis=1), (1, reps)),
                jnp.tile(jnp.concatenate(sin, axis=1), (1, reps)))

    cd, sd = table(DIFF_DH)
    cm, sm = table(MLA_ROPE)
    return {"cd": cd, "sd": sd, "cm": cm, "sm": sm}


def kernel(x_prompt, x_sample, cache_diff_k, cache_diff_v, cache_na_k, cache_na_v, cache_mla_ckv, cache_mla_krope, c, c_ctx, w_mod, b_mod, w_in, diff_lambda, diff_subln_g, na_rpb, mla_qa_g, mla_wq_b, mla_kva_g, mla_wkv_b, w_branch_diff, w_branch_na, w_branch_mla, w_out, ln1_g, ln1_b, ln2_g, ln2_b, w_router, router_bias, w_exp_gate, w_exp_up, w_exp_down, w_sh_gate, w_sh_up, w_sh_down):
    t_ctx = BATCH * SEQ
    t_lat = DEC_BATCH * DEC_SEQ
    cvec = jnp.concatenate([c, c_ctx[None, :], jnp.zeros((MOD_ROWS - DEC_BATCH - 1, D_MODEL), F32)], axis=0)
    mods = _modulation(cvec, w_mod, b_mod).reshape(DEPTH, MOD_ROWS, 6, D_MODEL)
    rope = _rope_tables()
    ck_d = cache_diff_k.reshape(DEC_BATCH, DEPTH, PAST_LEN, DIFF_W)
    cv_d = cache_diff_v.reshape(DEC_BATCH, DEPTH, PAST_LEN, DIFF_W)
    ck_n = cache_na_k.reshape(DEC_BATCH, DEPTH, PAST_LEN, NA_W)
    cv_n = cache_na_v.reshape(DEC_BATCH, DEPTH, PAST_LEN, NA_W)

    xc = x_prompt.reshape(t_ctx, D_MODEL)
    xl = x_sample.reshape(t_lat, D_MODEL)
    states = []
    for l in range(DEPTH):
        W = _layer_weights(l, w_in, mla_qa_g, mla_wq_b, mla_kva_g, mla_wkv_b, w_branch_diff, w_branch_na,
                           w_branch_mla, w_out, ln1_g, ln1_b, ln2_g, ln2_b, w_router, router_bias,
                           w_exp_gate, w_exp_up, w_exp_down, w_sh_gate, w_sh_up, w_sh_down)
        lam = diff_lambda[l].astype(F32)
        subln = diff_subln_g[l].reshape(1, LANES).astype(F32)
        mod_c = mods[l, DEC_BATCH:DEC_BATCH + 1]
        mod_l = mods[l, :DEC_BATCH]

        Wc = dict(W, wm=W["wm_ctx"])
        names = ["dq", "nq", "mqn", "mqr", "kr4", "kn", "vn", "dk", "dv", "nk", "nv", "ckv", "kr"]
        pc = dict(zip(names, _project(xc, mod_c, Wc, None, latent=False)))
        od, on, om = _attn_ctx(pc, lam, subln, l)
        x1, h2 = _merge(xc, mod_c, od, on, om, W, latent=False)
        xc = _moe(h2, x1, mod_c, W, latent=False)
        states.append(pc)

        Wl = dict(W, wm=W["wm_lat"])
        names = ["dq", "dk", "dv", "nq", "nk", "nv", "mqn", "mqr", "kr4", "kn", "vn"]
        pll = dict(zip(names, _project(xl, mod_l, Wl, rope, latent=True)))
        od = _attn_diff_lat(pll, ck_d, cv_d, lam, subln, l)
        on = _attn_na_lat(pll, ck_n, cv_n, _na_bias(na_rpb[l].astype(F32)), l)
        kc, vc, krc = _mla_cache(cache_mla_ckv, cache_mla_krope, W["wkvb"], l)
        om = _attn_mla_lat(pll, kc, vc, krc)
        x1, h2 = _merge(xl, mod_l, od, on, om, W, latent=True)
        xl = _moe(h2, x1, mod_l, W, latent=True)

    def stack(name, shape):
        return jnp.stack([s[name].reshape((BATCH, SEQ) + shape) for s in states], axis=1)

    return (xc.reshape(BATCH, SEQ, D_MODEL), xl.reshape(DEC_BATCH, DEC_SEQ, D_MODEL),
            stack("dk", (DIFF_HEADS, 2 * DIFF_DH)), stack("dv", (DIFF_HEADS, 2 * DIFF_DH)),
            stack("nk", (NA_HEADS, NA_DH)), stack("nv", (NA_HEADS, NA_DH)),
            stack("ckv", (MLA_KV_LORA,)), stack("kr", (MLA_ROPE,)))
```

```python
import functools

import numpy as np
import jax
import jax.numpy as jnp
from jax import lax
from jax.experimental import pallas as pl
from jax.experimental.pallas import tpu as pltpu
from jax.experimental.pallas import tpu_sc as plsc

D_MODEL = 1024
BATCH = 32
SEQ = 256
DEPTH = 2
DEC_BATCH = 8
DEC_SEQ = 2048
PAST_LEN = 512
GRID_W = 64
GRID_ROWS = DEC_SEQ // GRID_W
ROPE_BASE = 10000.0
DIFF_HEADS = 4
DIFF_DH = 64
DIFF_W = 512
NA_HEADS = 8
NA_DH = 64
NA_W = 512
NA_WIN_ROWS = 8
NA_WIN_COLS = 16
MLA_HEADS = 8
MLA_Q_LORA = 384
MLA_KV_LORA = 256
MLA_NOPE = 64
MLA_ROPE = 32
MLA_V = 64
MLA_W = 512
N_EXPERTS = 64
N_GROUPS = 8
GROUP_SIZE = N_EXPERTS // N_GROUPS
TOPK_GROUPS = 4
TOP_K = 8
EXPERT_FF = 256
SHARED_FF = 256
ROUTED_SCALE = 2.5
DN_ALPHA = (2 * DEPTH) ** 0.25
LN_EPS = 1e-5
RMS_EPS = 1e-6

F32 = jnp.float32
BF16 = jnp.bfloat16

LANES = 128
VMEM_LIMIT_BYTES = 56 * 1024 * 1024
NEG_BIG = -1e30
LOG2E = 1.4426950408889634

DIFF_SCALE = DIFF_DH ** -0.5
NA_SCALE = NA_DH ** -0.5
MLA_SCALE = (MLA_NOPE + MLA_ROPE) ** -0.5

PROJ_TM = 512
ATT_TQ = 512
NA_TQ = 256
NA_Q_ROWS = NA_TQ // GRID_W
NA_KEY_ROWS = NA_Q_ROWS + NA_WIN_ROWS
NA_KEYS = NA_KEY_ROWS * GRID_W
MOE_TILE_MIN = 128
MOE_TILE_MAX = 1024
SC_ROWS = 128
COMBINE_TM = 256


def _dot(a, b):
    return jnp.dot(a, b, preferred_element_type=F32)


def _dot_nt(a, b):
    return lax.dot_general(a, b, (((1,), (1,)), ((), ())), preferred_element_type=F32)


def _sigmoid(x):
    return 1.0 / (1.0 + jnp.exp(-x))


def _silu(x):
    return x * _sigmoid(x)


def _params(*sem):
    return pltpu.CompilerParams(dimension_semantics=sem, vmem_limit_bytes=VMEM_LIMIT_BYTES)


def _full(shape):
    n = len(shape)
    return pl.BlockSpec(shape, lambda *_: (0,) * n)


def _layer_norm(y, g, b):
    mu = jnp.mean(y, axis=-1, keepdims=True)
    yc = y - mu
    var = jnp.mean(yc * yc, axis=-1, keepdims=True)
    return yc * lax.rsqrt(var + LN_EPS) * g + b


def _rms(x, g):
    return x * lax.rsqrt(jnp.mean(x * x, axis=-1, keepdims=True) + RMS_EPS) * g


HIGH_HALF = -65536
PACKED_W = D_MODEL // 2


def _pack_bf16_pairs(x):
    n = x.shape[1] // 2
    bits = lax.bitcast_convert_type(x.astype(BF16).astype(F32), jnp.int32)
    return lax.shift_right_logical(bits[:, :n], 16) | (bits[:, n:] & HIGH_HALF)


def _unpack_bf16_pairs(w):
    lo = lax.bitcast_convert_type(lax.shift_left(w, 16), F32)
    hi = lax.bitcast_convert_type(w & HIGH_HALF, F32)
    return jnp.concatenate([lo, hi], axis=1)


MOD_ROWS = 16
MOD_TN = 1536


def _mod_kernel(c_ref, w_ref, b_ref, o_ref):
    s = _silu(c_ref[...]).astype(BF16)
    o_ref[0] = _dot(s, w_ref[0].astype(BF16)) + b_ref[0]


def _modulation(cvec, w_mod, b_mod):
    n = 6 * D_MODEL
    return pl.pallas_call(
        _mod_kernel,
        out_shape=jax.ShapeDtypeStruct((DEPTH, MOD_ROWS, n), F32),
        grid=(DEPTH, n // MOD_TN),
        in_specs=[
            pl.BlockSpec((MOD_ROWS, D_MODEL), lambda l, j: (0, 0)),
            pl.BlockSpec((1, D_MODEL, MOD_TN), lambda l, j: (l, 0, j)),
            pl.BlockSpec((1, 1, MOD_TN), lambda l, j: (l, 0, j)),
        ],
        out_specs=pl.BlockSpec((1, MOD_ROWS, MOD_TN), lambda l, j: (l, 0, j)),
        compiler_params=_params("parallel", "parallel"),
        name="modulation",
    )(cvec, w_mod, b_mod.reshape(DEPTH, 1, n))


def _proj_common(x_ref, mod_ref, wa_ref, wm_ref, qag_ref, kvag_ref, wqb_ref, wkvb_ref):
    m = mod_ref[0]
    h = (x_ref[...] * (1.0 + m[1:2]) + m[0:1]).astype(BF16)
    a = _dot(h, wa_ref[...])
    mm = _dot(h, wm_ref[...])
    qan = _rms(mm[:, :MLA_Q_LORA], qag_ref[...]).astype(BF16)
    mq = _dot(qan, wqb_ref[...])
    ckv = _rms(mm[:, MLA_Q_LORA:MLA_Q_LORA + MLA_KV_LORA], kvag_ref[...])
    kv = _dot(ckv.astype(BF16), wkvb_ref[...])
    return h, a, mm, qan, mq, ckv, kv


def _proj_ctx_kernel(x_ref, mod_ref, wa_ref, wm_ref, qag_ref, kvag_ref, wqb_ref, wkvb_ref,
                     dq_ref, nq_ref, mqn_ref, mqr_ref, kr4_ref, kn_ref, vn_ref,
                     dk_ref, dv_ref, nk_ref, nv_ref, ckv_ref, kr_ref):
    _, a, mm, _, mq, ckv, kv = _proj_common(x_ref, mod_ref, wa_ref, wm_ref, qag_ref, kvag_ref,
                                            wqb_ref, wkvb_ref)
    dq_ref[...] = (a[:, 0:512] * DIFF_SCALE).astype(BF16)
    dk_ref[...] = a[:, 512:1024]
    dv_ref[...] = a[:, 1024:1536]
    nq_ref[...] = (a[:, 1536:2048] * NA_SCALE).astype(BF16)
    nk_ref[...] = a[:, 2048:2560]
    nv_ref[...] = a[:, 2560:3072]
    mqn_ref[...] = (mq[:, :512] * MLA_SCALE).astype(BF16)
    mqr_ref[...] = (mq[:, 512:768] * MLA_SCALE).astype(BF16)
    kr4 = mm[:, 640:768]
    kr4_ref[...] = kr4.astype(BF16)
    kr_ref[...] = kr4[:, :MLA_ROPE]
    ckv_ref[...] = ckv
    kn_ref[...] = kv[:, :512].astype(BF16)
    vn_ref[...] = kv[:, 512:].astype(BF16)


def _proj_lat_kernel(x_ref, mod_ref, wa_ref, wm_ref, qag_ref, kvag_ref, wqb_ref, wkvb_ref,
                     wp_ref, wqp_ref, cd_ref, sd_ref, cm_ref, sm_ref,
                     dq_ref, dk_ref, dv_ref, nq_ref, nk_ref, nv_ref,
                     mqn_ref, mqr_ref, kr4_ref, kn_ref, vn_ref):
    h, a, mm, qan, mq, _, kv = _proj_common(x_ref, mod_ref, wa_ref, wm_ref, qag_ref, kvag_ref,
                                            wqb_ref, wkvb_ref)
    ap = _dot(h, wp_ref[...])
    mqp = _dot(qan, wqp_ref[...])
    cd = cd_ref[...]
    sd = sd_ref[...]
    cm = cm_ref[...]
    sm = sm_ref[...]
    for j in range(DIFF_W // LANES):
        lo, hi = LANES * j, LANES * (j + 1)
        dq_ref[:, lo:hi] = ((a[:, lo:hi] * cd + ap[:, lo:hi] * sd) * (DIFF_SCALE * LOG2E)).astype(BF16)
        dk_ref[:, lo:hi] = (a[:, 512 + lo:512 + hi] * cd + ap[:, 512 + lo:512 + hi] * sd).astype(BF16)
    dv_ref[...] = a[:, 1024:1536].astype(BF16)
    nq_ref[...] = (a[:, 1536:2048] * (NA_SCALE * LOG2E)).astype(BF16)
    nk_ref[...] = a[:, 2048:2560].astype(BF16)
    nv_ref[...] = a[:, 2560:3072].astype(BF16)
    mqn_ref[...] = (mq[:, :512] * (MLA_SCALE * LOG2E)).astype(BF16)
    for j in range(2):
        lo, hi = LANES * j, LANES * (j + 1)
        mqr_ref[:, lo:hi] = ((mq[:, 512 + lo:512 + hi] * cm + mqp[:, lo:hi] * sm)
                             * (MLA_SCALE * LOG2E)).astype(BF16)
    kr4_ref[...] = (mm[:, 640:768] * cm + mm[:, 768:896] * sm).astype(BF16)
    kn_ref[...] = kv[:, :512].astype(BF16)
    vn_ref[...] = kv[:, 512:].astype(BF16)


def _project(x, mod, W, rope, *, latent):
    t = x.shape[0]
    tm = PROJ_TM
    tokens_per_batch = DEC_SEQ if latent else t
    steps_per_batch = tokens_per_batch // tm
    row = lambda w: pl.BlockSpec((tm, w), lambda i: (i, 0))
    common_in = [
        row(D_MODEL),
        pl.BlockSpec((1, 6, D_MODEL), lambda i: (i // steps_per_batch, 0, 0)),
        _full(W["wa"].shape), _full(W["wm"].shape), _full((1, MLA_Q_LORA)), _full((1, MLA_KV_LORA)),
        _full(W["wqb"].shape), _full(W["wkvb"].shape),
    ]
    common_args = [x, mod, W["wa"], W["wm"], W["qag"], W["kvag"], W["wqb"], W["wkvb"]]
    bf = lambda w: jax.ShapeDtypeStruct((t, w), BF16)
    f32 = lambda w: jax.ShapeDtypeStruct((t, w), F32)
    if latent:
        tab = pl.BlockSpec((tm, LANES), lambda i: (i % steps_per_batch, 0))
        widths = [512, 512, 512, 512, 512, 512, 512, 256, 128, 512, 512]
        return pl.pallas_call(
            _proj_lat_kernel,
            out_shape=[bf(w) for w in widths],
            grid=(t // tm,),
            in_specs=common_in + [_full(W["wp"].shape), _full(W["wqp"].shape), tab, tab, tab, tab],
            out_specs=[row(w) for w in widths],
            compiler_params=_params("parallel"),
            name="proj_lat",
        )(*common_args, W["wp"], W["wqp"], rope["cd"], rope["sd"], rope["cm"], rope["sm"])
    bf_w = [512, 512, 512, 256, 128, 512, 512]
    f32_w = [512, 512, 512, 512, 256, 32]
    return pl.pallas_call(
        _proj_ctx_kernel,
        out_shape=[bf(w) for w in bf_w] + [f32(w) for w in f32_w],
        grid=(t // tm,),
        in_specs=common_in,
        out_specs=[row(w) for w in bf_w + f32_w],
        compiler_params=_params("parallel"),
        name="proj_ctx",
    )(*common_args)


def _lane_iota():
    return lax.broadcasted_iota(jnp.int32, (1, LANES), 1)


def _softmax_parts(parts, exp):
    m = functools.reduce(jnp.maximum, [jnp.max(s, axis=-1, keepdims=True) for s in parts])
    es = [exp(s - m) for s in parts]
    l = functools.reduce(lambda u, v: u + v, [jnp.sum(e, axis=-1, keepdims=True) for e in es])
    return es, l


def _diff_lambda(lam_ref, layer):
    lp = lam_ref[...]
    lam_init = 0.8 - 0.6 * float(np.exp(-0.3 * layer))
    s1 = jnp.sum(lp[0:1] * lp[1:2], axis=-1, keepdims=True)
    s2 = jnp.sum(lp[2:3] * lp[3:4], axis=-1, keepdims=True)
    return jnp.exp(s1) - jnp.exp(s2) + lam_init, lam_init


def _diff_heads(q_ref, ks, vs, lam_ref, g_ref, o_ref, layer, exp):
    lam, lam_init = _diff_lambda(lam_ref, layer)
    first_map = _lane_iota() < DIFF_DH
    g = g_ref[...]
    for h in range(DIFF_HEADS):
        hs = slice(LANES * h, LANES * (h + 1))
        q = q_ref[:, hs]
        q1 = jnp.where(first_map, q, jnp.zeros_like(q))
        q2 = jnp.where(first_map, jnp.zeros_like(q), q)
        kk = [k[:, hs].astype(BF16) for k in ks]
        e1, l1 = _softmax_parts([_dot_nt(q1, k) for k in kk], exp)
        e2, l2 = _softmax_parts([_dot_nt(q2, k) for k in kk], exp)
        c1 = 1.0 / l1
        c2 = lam / l2
        o = None
        for a1, a2, v in zip(e1, e2, vs):
            part = _dot((a1 * c1 - a2 * c2).astype(BF16), v[:, hs].astype(BF16))
            o = part if o is None else o + part
        o = _rms(o, g) * (1.0 - lam_init)
        o_ref[:, hs] = o.astype(BF16)


def _pair_heads(q_of, k_of, v_of, bias_of, o_ref, n_pairs, exp, stack=False):
    first = _lane_iota() < 64
    for j in range(n_pairs):
        ps = slice(LANES * j, LANES * (j + 1))
        ks = k_of(j)
        vs = v_of(j)
        qs = [q_of(j, 0), q_of(j, 1)]
        if stack:
            qs = [jnp.concatenate(qs, axis=0)]
        outs = []
        for q in qs:
            ss = [_dot_nt(q, k) for k in ks]
            ss = [s if b is None else s + b for s, b in zip(ss, bias_of(j))]
            es, l = _softmax_parts(ss, exp)
            o = None
            for e, v in zip(es, vs):
                part = _dot(e.astype(BF16), v)
                o = part if o is None else o + part
            outs.append(o / l)
        if stack:
            n = outs[0].shape[0] // 2
            outs = [outs[0][:n], outs[0][n:]]
        o_ref[:, ps] = jnp.where(first, outs[0], outs[1]).astype(BF16)


def _na_q(q_ref):
    first = _lane_iota() < NA_DH

    def q_of(j, hh):
        q = q_ref[:, LANES * j:LANES * (j + 1)]
        keep = first if hh == 0 else jnp.logical_not(first)
        return jnp.where(keep, q, jnp.zeros_like(q))
    return q_of


def _mla_q(qn_ref, qr_ref):
    lane = _lane_iota()
    first = lane < MLA_NOPE

    def q_of(j, hh):
        h = 2 * j + hh
        qn = qn_ref[:, LANES * j:LANES * (j + 1)]
        keep = first if hh == 0 else jnp.logical_not(first)
        qn = jnp.where(keep, qn, jnp.zeros_like(qn))
        qr = qr_ref[:, LANES * (h // 4):LANES * (h // 4 + 1)]
        qr = jnp.where((lane // MLA_ROPE) == (h % 4), qr, jnp.zeros_like(qr))
        return jnp.concatenate([qn, qr], axis=1)
    return q_of


def _attn_ctx_kernel(layer, dq_ref, dk_ref, dv_ref, nq_ref, nk_ref, nv_ref,
                     mqn_ref, mqr_ref, kn_ref, kr4_ref, vn_ref, lam_ref, g_ref,
                     od_ref, on_ref, om_ref):
    _diff_heads(dq_ref, [dk_ref], [dv_ref], lam_ref, g_ref, od_ref, layer, jnp.exp)
    none = lambda j: [None]
    pair = lambda j: slice(LANES * j, LANES * (j + 1))
    _pair_heads(_na_q(nq_ref),
                lambda j: [nk_ref[:, pair(j)].astype(BF16)],
                lambda j: [nv_ref[:, pair(j)].astype(BF16)],
                none, on_ref, NA_HEADS // 2, jnp.exp, stack=True)
    kr4 = kr4_ref[...]
    _pair_heads(_mla_q(mqn_ref, mqr_ref),
                lambda j: [jnp.concatenate([kn_ref[:, pair(j)], kr4], axis=1)],
                lambda j: [vn_ref[:, pair(j)]],
                none, om_ref, MLA_HEADS // 2, jnp.exp, stack=True)


def _attn_ctx(p, lam, g, layer):
    t = p["dq"].shape[0]
    row = lambda w: pl.BlockSpec((SEQ, w), lambda b: (b, 0))
    names = ["dq", "dk", "dv", "nq", "nk", "nv", "mqn", "mqr", "kn", "kr4", "vn"]
    out = jax.ShapeDtypeStruct((t, 512), BF16)
    return pl.pallas_call(
        functools.partial(_attn_ctx_kernel, layer),
        out_shape=[out, out, out],
        grid=(t // SEQ,),
        in_specs=[row(p[n].shape[1]) for n in names] + [_full((4, DIFF_DH)), _full((1, LANES))],
        out_specs=[row(512)] * 3,
        compiler_params=_params("parallel"),
        name="attn_ctx",
    )(*[p[n] for n in names], lam, g)


def _cache_spec(width, layer):
    return pl.BlockSpec((None, None, PAST_LEN, width), lambda b, q: (b, layer, 0, 0))


def _batch_spec(width):
    return pl.BlockSpec((None, DEC_SEQ, width), lambda b, q: (b, 0, 0))


def _qtile_spec(width, tq):
    steps = DEC_SEQ // tq
    return pl.BlockSpec((tq, width), lambda b, q: (b * steps + q, 0))


def _attn_diff_lat_kernel(layer, q_ref, kc_ref, vc_ref, kn_ref, vn_ref, lam_ref, g_ref, o_ref):
    _diff_heads(q_ref, [kc_ref, kn_ref], [vc_ref, vn_ref], lam_ref, g_ref, o_ref, layer, jnp.exp2)


def _attn_diff_lat(p, cache_k, cache_v, lam, g, layer):
    t = p["dq"].shape[0]
    b3 = lambda a: a.reshape(DEC_BATCH, DEC_SEQ, a.shape[1])
    return pl.pallas_call(
        functools.partial(_attn_diff_lat_kernel, layer),
        out_shape=jax.ShapeDtypeStruct((t, DIFF_W), BF16),
        grid=(DEC_BATCH, DEC_SEQ // ATT_TQ),
        in_specs=[_qtile_spec(DIFF_W, ATT_TQ), _cache_spec(DIFF_W, layer), _cache_spec(DIFF_W, layer),
                  _batch_spec(DIFF_W), _batch_spec(DIFF_W),
                  pl.BlockSpec((4, DIFF_DH), lambda b, q: (0, 0)),
                  pl.BlockSpec((1, LANES), lambda b, q: (0, 0))],
        out_specs=_qtile_spec(DIFF_W, ATT_TQ),
        compiler_params=_params("parallel", "parallel"),
        name="attn_diff_lat",
    )(p["dq"], cache_k, cache_v, b3(p["dk"]), b3(p["dv"]), lam, g)


def _na_key_start(q):
    return jnp.clip(q * NA_Q_ROWS - NA_WIN_ROWS // 2, 0, GRID_ROWS - NA_KEY_ROWS)


def _attn_na_lat_kernel(q_ref, kc_ref, vc_ref, kn_ref, vn_ref, bias_ref, o_ref):
    start = pl.multiple_of(_na_key_start(pl.program_id(1)) * GRID_W, GRID_W)
    kw = kn_ref[pl.ds(start, NA_KEYS), :]
    vw = vn_ref[pl.ds(start, NA_KEYS), :]
    pair = lambda j: slice(LANES * j, LANES * (j + 1))
    _pair_heads(_na_q(q_ref),
                lambda j: [kc_ref[:, pair(j)].astype(BF16), kw[:, pair(j)]],
                lambda j: [vc_ref[:, pair(j)].astype(BF16), vw[:, pair(j)]],
                lambda j: [None, bias_ref[j]],
                o_ref, NA_HEADS // 2, jnp.exp2, stack=True)


def _na_bias_tables():
    n_blocks = DEC_SEQ // NA_TQ
    qr = np.arange(NA_Q_ROWS)
    kr = np.arange(NA_KEY_ROWS)
    row_sel, row_ok = [], []
    for qb in range(n_blocks):
        ks = int(np.clip(qb * NA_Q_ROWS - NA_WIN_ROWS // 2, 0, GRID_ROWS - NA_KEY_ROWS))
        r = qb * NA_Q_ROWS + qr
        r0 = np.clip(r - NA_WIN_ROWS // 2, 0, GRID_ROWS - NA_WIN_ROWS)
        krow = ks + kr
        ok = (krow[None, :] >= r0[:, None]) & (krow[None, :] < r0[:, None] + NA_WIN_ROWS)
        off = np.where(ok, krow[None, :] - r[:, None] + NA_WIN_ROWS - 1, 2 * NA_WIN_ROWS - 1)
        sel = off[:, :, None] == np.arange(2 * NA_WIN_ROWS)
        row_sel.append(sel.astype(np.float32))
        row_ok.append(ok)
    kinds, kind_of_block = [], []
    for qb in range(n_blocks):
        for n, other in enumerate(kinds):
            if np.array_equal(row_sel[qb], row_sel[other]):
                kind_of_block.append(n)
                break
        else:
            kind_of_block.append(len(kinds))
            kinds.append(qb)
    c = np.arange(GRID_W)
    c0 = np.clip(c - NA_WIN_COLS // 2, 0, GRID_W - NA_WIN_COLS)
    col_ok = (c[None, :] >= c0[:, None]) & (c[None, :] < c0[:, None] + NA_WIN_COLS)
    coff = np.where(col_ok, c[None, :] - c[:, None] + NA_WIN_COLS - 1, 2 * NA_WIN_COLS - 1)
    col_sel = (coff[:, :, None] == np.arange(2 * NA_WIN_COLS)).astype(np.float32)
    rsel = np.stack([row_sel[qb] for qb in kinds])
    return rsel, col_sel, kind_of_block


_NA_ROW_SEL, _NA_COL_SEL, _NA_KIND_OF_BLOCK = _na_bias_tables()


def _na_bias(rpb):
    hp = lax.Precision.HIGHEST
    table = jnp.pad(rpb * LOG2E, ((0, 0), (0, 1), (0, 1)), constant_values=NEG_BIG)
    cols = jnp.einsum("hij,ckj->hick", table, _NA_COL_SEL, precision=hp)
    b = jnp.einsum("nqri,hick->nhqcrk", _NA_ROW_SEL, cols, precision=hp)
    return b.reshape(_NA_ROW_SEL.shape[0], NA_HEADS // 2, 2 * NA_TQ, NA_KEYS).astype(F32)


def _na_kind(q):
    return (q > 0).astype(jnp.int32) + (q == DEC_SEQ // NA_TQ - 1).astype(jnp.int32)


def _attn_na_lat(p, cache_k, cache_v, bias, layer):
    assert _NA_KIND_OF_BLOCK == [0] + [1] * (DEC_SEQ // NA_TQ - 2) + [2]
    t = p["nq"].shape[0]
    b3 = lambda a: a.reshape(DEC_BATCH, DEC_SEQ, a.shape[1])
    return pl.pallas_call(
        _attn_na_lat_kernel,
        out_shape=jax.ShapeDtypeStruct((t, NA_W), BF16),
        grid=(DEC_BATCH, DEC_SEQ // NA_TQ),
        in_specs=[_qtile_spec(NA_W, NA_TQ), _cache_spec(NA_W, layer), _cache_spec(NA_W, layer),
                  _batch_spec(NA_W), _batch_spec(NA_W),
                  pl.BlockSpec((None, NA_HEADS // 2, 2 * NA_TQ, NA_KEYS),
                               lambda b, q: (_na_kind(q), 0, 0, 0))],
        out_specs=_qtile_spec(NA_W, NA_TQ),
        compiler_params=_params("parallel", "parallel"),
        name="attn_na_lat",
    )(p["nq"], cache_k, cache_v, b3(p["nk"]), b3(p["nv"]), bias)


def _mla_cache_kernel(ckv_ref, kr_ref, wkvb_ref, rep_ref, kc_ref, vc_ref, krc_ref):
    kv = _dot(ckv_ref[...].astype(BF16), wkvb_ref[...])
    kc_ref[...] = kv[:, :512].astype(BF16)
    vc_ref[...] = kv[:, 512:].astype(BF16)
    krc_ref[...] = _dot(kr_ref[...].astype(BF16), rep_ref[...]).astype(BF16)


def _mla_cache(cache_ckv, cache_kr, wkvb, layer):
    rep = jnp.asarray(np.tile(np.eye(MLA_ROPE, dtype=np.float32), (1, LANES // MLA_ROPE)), BF16)
    spec_in = lambda w: pl.BlockSpec((None, None, PAST_LEN, w), lambda b: (b, layer, 0, 0))
    spec_out = lambda w: pl.BlockSpec((None, PAST_LEN, w), lambda b: (b, 0, 0))
    shp = lambda w: jax.ShapeDtypeStruct((DEC_BATCH, PAST_LEN, w), BF16)
    return pl.pallas_call(
        _mla_cache_kernel,
        out_shape=[shp(512), shp(512), shp(LANES)],
        grid=(DEC_BATCH,),
        in_specs=[spec_in(MLA_KV_LORA), spec_in(MLA_ROPE), _full(wkvb.shape), _full(rep.shape)],
        out_specs=[spec_out(512), spec_out(512), spec_out(LANES)],
        compiler_params=_params("parallel"),
        name="mla_cache",
    )(cache_ckv, cache_kr, wkvb, rep)


def _attn_mla_lat_kernel(qn_ref, qr_ref, kc_ref, krc_ref, vc_ref, kn_ref, krn_ref, vn_ref, o_ref):
    pair = lambda j: slice(LANES * j, LANES * (j + 1))
    krc = krc_ref[...]
    krn = krn_ref[...]
    _pair_heads(_mla_q(qn_ref, qr_ref),
                lambda j: [jnp.concatenate([kc_ref[:, pair(j)], krc], axis=1),
                           jnp.concatenate([kn_ref[:, pair(j)], krn], axis=1)],
                lambda j: [vc_ref[:, pair(j)], vn_ref[:, pair(j)]],
                lambda j: [None, None],
                o_ref, MLA_HEADS // 2, jnp.exp2)


def _attn_mla_lat(p, kc, vc, krc):
    t = p["mqn"].shape[0]
    b3 = lambda a: a.reshape(DEC_BATCH, DEC_SEQ, a.shape[1])
    cspec = lambda w: pl.BlockSpec((None, PAST_LEN, w), lambda b, q: (b, 0, 0))
    return pl.pallas_call(
        _attn_mla_lat_kernel,
        out_shape=jax.ShapeDtypeStruct((t, MLA_W), BF16),
        grid=(DEC_BATCH, DEC_SEQ // ATT_TQ),
        in_specs=[_qtile_spec(512, ATT_TQ), _qtile_spec(256, ATT_TQ), cspec(512), cspec(LANES), cspec(512),
                  _batch_spec(512), _batch_spec(LANES), _batch_spec(512)],
        out_specs=_qtile_spec(MLA_W, ATT_TQ),
        compiler_params=_params("parallel", "parallel"),
        name="attn_mla_lat",
    )(p["mqn"], p["mqr"], kc, krc, vc, b3(p["kn"]), b3(p["kr4"]), b3(p["vn"]))


def _merge_kernel(x_ref, mod_ref, od_ref, on_ref, om_ref, wg_ref, wbd_ref, wbn_ref, wbm_ref, wo_ref,
                  g_ref, b_ref, x1_ref, h2_ref):
    x = x_ref[...]
    m = mod_ref[0]
    h = (x * (1.0 + m[1:2]) + m[0:1]).astype(BF16)
    gates = _dot(h, wg_ref[...])
    mix = (_sigmoid(gates[:, 0:1024]) * _dot(od_ref[...], wbd_ref[...])
           + _sigmoid(gates[:, 1024:2048]) * _dot(on_ref[...], wbn_ref[...])
           + _sigmoid(gates[:, 2048:3072]) * _dot(om_ref[...], wbm_ref[...]))
    out = _dot(mix.astype(BF16), wo_ref[...])
    x1 = _layer_norm(DN_ALPHA * x + m[2:3] * out, g_ref[...], b_ref[...])
    x1_ref[...] = x1
    h2_ref[...] = _pack_bf16_pairs(x1 * (1.0 + m[4:5]) + m[3:4])


def _merge(x, mod, od, on, om, W, *, latent):
    t = x.shape[0]
    tm = PROJ_TM
    steps_per_batch = (DEC_SEQ if latent else t) // tm
    row = lambda w: pl.BlockSpec((tm, w), lambda i: (i, 0))
    return pl.pallas_call(
        _merge_kernel,
        out_shape=[jax.ShapeDtypeStruct((t, D_MODEL), F32), jax.ShapeDtypeStruct((t, PACKED_W), jnp.int32)],
        grid=(t // tm,),
        in_specs=[row(D_MODEL),
                  pl.BlockSpec((1, 6, D_MODEL), lambda i: (i // steps_per_batch, 0, 0)),
                  row(512), row(512), row(512),
                  _full(W["wg"].shape), _full(W["wbd"].shape), _full(W["wbn"].shape),
                  _full(W["wbm"].shape), _full(W["wo"].shape),
                  _full((1, D_MODEL)), _full((1, D_MODEL))],
        out_specs=[row(D_MODEL), row(PACKED_W)],
        compiler_params=_params("parallel"),
        name="merge",
    )(x, mod, od, on, om, W["wg"], W["wbd"], W["wbn"], W["wbm"], W["wo"], W["ln1_g"], W["ln1_b"])


def _first_index_of_max(vals, idx, sentinel):
    mx = functools.reduce(jnp.maximum, [jnp.max(v, axis=0, keepdims=True) for v in vals])
    cand = [jnp.min(jnp.where(v == mx, i, sentinel), axis=0, keepdims=True) for v, i in zip(vals, idx)]
    return mx, functools.reduce(jnp.minimum, cand)


def _router_kernel(h_ref, wr_ref, bias_ref, tri_ref, eid_ref, rank_ref, wtok_ref, cnt_ref, base_ref):
    tm = h_ref.shape[0]

    @pl.when(pl.program_id(0) == 0)
    def _():
        base_ref[...] = jnp.zeros_like(base_ref)

    logits = _dot_nt(wr_ref[...], _unpack_bf16_pairs(h_ref[...]).astype(BF16))
    scores = _sigmoid(logits)
    biased = scores + bias_ref[...]
    member = lax.broadcasted_iota(jnp.int32, (GROUP_SIZE, tm), 0)
    slabs = [biased[GROUP_SIZE * g:GROUP_SIZE * (g + 1)] for g in range(N_GROUPS)]
    gscore = []
    for s in slabs:
        m1, first = _first_index_of_max([s], [member], GROUP_SIZE)
        m2 = jnp.max(jnp.where(member == first, -jnp.inf, s), axis=0, keepdims=True)
        gscore.append(m1 + m2)
    gs = jnp.concatenate(gscore, axis=0)
    gidx = lax.broadcasted_iota(jnp.int32, (N_GROUPS, tm), 0)
    gsel = jnp.zeros((N_GROUPS, tm), F32)
    for _ in range(TOPK_GROUPS):
        _, first = _first_index_of_max([gs], [gidx], N_GROUPS)
        pick = gidx == first
        gsel = jnp.where(pick, 1.0, gsel)
        gs = jnp.where(pick, -jnp.inf, gs)
    cur = [jnp.where(gsel[g:g + 1] > 0.0, slabs[g], -jnp.inf) for g in range(N_GROUPS)]
    eidx = [member + GROUP_SIZE * g for g in range(N_GROUPS)]
    sel = [jnp.zeros((GROUP_SIZE, tm), F32) for _ in range(N_GROUPS)]
    picks = []
    for _ in range(TOP_K):
        _, first = _first_index_of_max(cur, eidx, N_EXPERTS)
        pick = [eidx[g] == first for g in range(N_GROUPS)]
        picks.append((first, pick))
        for g in range(N_GROUPS):
            sel[g] = jnp.where(pick[g], 1.0, sel[g])
            cur[g] = jnp.where(pick[g], -jnp.inf, cur[g])
    w = [jnp.where(sel[g] > 0.0, scores[GROUP_SIZE * g:GROUP_SIZE * (g + 1)], 0.0) for g in range(N_GROUPS)]
    total = functools.reduce(lambda u, v: u + v, [jnp.sum(x, axis=0, keepdims=True) for x in w])
    w = [x / total * ROUTED_SCALE for x in w]

    sel_all = jnp.concatenate(sel, axis=0)
    incl = _dot(sel_all.astype(BF16), tri_ref[...])
    base = base_ref[:, 0:1]
    rank_all = incl - sel_all + base
    rank = [rank_all[GROUP_SIZE * g:GROUP_SIZE * (g + 1)] for g in range(N_GROUPS)]
    cnt = base + jnp.sum(sel_all, axis=1, keepdims=True)
    base_ref[...] = jnp.broadcast_to(cnt, base_ref.shape)
    cnt_ref[...] = jnp.broadcast_to(cnt, cnt_ref.shape)

    def picked(vals, pick):
        parts = [jnp.sum(jnp.where(p, v, 0.0), axis=0, keepdims=True) for p, v in zip(pick, vals)]
        return functools.reduce(lambda u, v: u + v, parts)

    eid_ref[...] = jnp.concatenate([first for first, _ in picks], axis=0)
    rank_ref[...] = jnp.concatenate([picked(rank, pick) for _, pick in picks], axis=0).astype(jnp.int32)
    w_rows = [picked(w, pick) for _, pick in picks] + [jnp.zeros((LANES - TOP_K, tm), F32)]
    wtok_ref[...] = jnp.concatenate(w_rows, axis=0).T


def _router(h2, wr_t, bias_col):
    t = h2.shape[0]
    tm = PROJ_TM
    tri = jnp.asarray(np.triu(np.ones((tm, tm), np.float32)), BF16)
    slots = lambda dt: jax.ShapeDtypeStruct((TOP_K, t), dt)
    return pl.pallas_call(
        _router_kernel,
        out_shape=[slots(jnp.int32), slots(jnp.int32), jax.ShapeDtypeStruct((t, LANES), F32),
                   jax.ShapeDtypeStruct((N_EXPERTS, LANES), F32)],
        grid=(t // tm,),
        in_specs=[pl.BlockSpec((tm, PACKED_W), lambda i: (i, 0)),
                  _full((N_EXPERTS, D_MODEL)), _full((N_EXPERTS, 1)), _full((tm, tm))],
        out_specs=[pl.BlockSpec((TOP_K, tm), lambda i: (0, i)), pl.BlockSpec((TOP_K, tm), lambda i: (0, i)),
                   pl.BlockSpec((tm, LANES), lambda i: (i, 0)), _full((N_EXPERTS, LANES))],
        scratch_shapes=[pltpu.VMEM((N_EXPERTS, LANES), F32)],
        compiler_params=_params("arbitrary"),
        name="router",
    )(h2, wr_t, bias_col, tri)


def _moe_tile(t):
    mean_rows = t * TOP_K // N_EXPERTS
    return int(min(max(pl.next_power_of_2(mean_rows // 2), MOE_TILE_MIN), MOE_TILE_MAX))


def _moe_tiles(t):
    return (t * TOP_K) // _moe_tile(t) + N_EXPERTS


def _moe_plan(eid, rank, cnt, t):
    tile = _moe_tile(t)
    counts = cnt[:, 0].astype(jnp.int32)
    tiles = jnp.maximum((counts + tile - 1) // tile, 1)
    ends = jnp.cumsum(tiles)
    starts = ends - tiles
    experts = jnp.arange(N_EXPERTS, dtype=jnp.int32)
    pos = rank + jnp.sum(jnp.where(eid[:, :, None] == experts, starts * tile, 0), axis=-1)
    tile_ids = jnp.arange(_moe_tiles(t), dtype=jnp.int32)
    owner = tile_ids[:, None] >= ends[None, :]
    tile_expert = jnp.minimum(jnp.sum(owner, axis=-1), N_EXPERTS - 1)
    is_owner = tile_expert[:, None] == experts[None, :]
    start_of = jnp.sum(jnp.where(is_owner, starts, 0), axis=-1)
    count_of = jnp.sum(jnp.where(is_owner, counts, 0), axis=-1)
    real = jnp.clip(count_of - (tile_ids - start_of) * tile, 0, tile)
    real = jnp.where(tile_ids < ends[-1], real, 0)
    return pos.astype(jnp.int32), tile_expert.astype(jnp.int32), real.astype(jnp.int32)


def _sc_workers():
    info = plsc.get_sparse_core_info()
    return info.num_cores, info.num_subcores


def _sc_mesh():
    return plsc.VectorSubcoreMesh(core_axis_name="core", subcore_axis_name="subcore")


def _sc_worker_id(n_cores):
    return lax.axis_index("subcore") * n_cores + lax.axis_index("core")


def _dispatch(h2, pos):
    t = h2.shape[0]
    n_cores, n_sub = _sc_workers()
    ch = SC_ROWS
    per_worker = t // (n_cores * n_sub)
    n_chunks = per_worker // ch
    pos_chunks = pos.reshape(TOP_K, t // ch, ch).transpose(1, 0, 2)

    @functools.partial(
        pl.kernel, mesh=_sc_mesh(),
        out_type=jax.ShapeDtypeStruct((_moe_tiles(t) * _moe_tile(t), PACKED_W), jnp.int32),
        scratch_types=[pltpu.VMEM((TOP_K, ch), jnp.int32), pltpu.VMEM((ch, PACKED_W), jnp.int32),
                       pltpu.SemaphoreType.DMA],
        name="moe_dispatch",
    )
    def run(h_hbm, pos_hbm, xs_hbm, idx_ref, rows_ref, sem):
        first = _sc_worker_id(n_cores) * n_chunks

        @pl.loop(0, n_chunks)
        def _(j):
            c = first + j
            pltpu.sync_copy(pos_hbm.at[c], idx_ref)
            pltpu.sync_copy(h_hbm.at[pl.ds(pl.multiple_of(c * ch, ch), ch)], rows_ref)
            copies = [pltpu.async_copy(rows_ref, xs_hbm.at[idx_ref.at[k]], sem) for k in range(TOP_K)]
            for cp in copies:
                cp.wait()

    return run(h2, pos_chunks)


def _gather_rows(ys, idx):
    n = idx.shape[0]
    n_cores, n_sub = _sc_workers()
    ch = SC_ROWS
    per_worker = n // (n_cores * n_sub)
    n_chunks = per_worker // ch

    @functools.partial(
        pl.kernel, mesh=_sc_mesh(),
        out_type=jax.ShapeDtypeStruct((n, PACKED_W), jnp.int32),
        scratch_types=[pltpu.VMEM((ch,), jnp.int32), pltpu.VMEM((ch, PACKED_W), jnp.int32),
                       pltpu.SemaphoreType.DMA],
        name="moe_gather",
    )
    def run(ys_hbm, idx_hbm, out_hbm, idx_ref, rows_ref, sem):
        first = _sc_worker_id(n_cores) * per_worker

        @pl.loop(0, n_chunks)
        def _(j):
            off = pl.multiple_of(first + j * ch, ch)
            pltpu.sync_copy(idx_hbm.at[pl.ds(off, ch)], idx_ref)
            pltpu.async_copy(ys_hbm.at[idx_ref], rows_ref, sem).wait()
            pltpu.sync_copy(rows_ref, out_hbm.at[pl.ds(off, ch)])

    return run(ys, idx)


def _ffn_kernel(te_ref, real_ref, xs_ref, weg_ref, weu_ref, wed_ref, ys_ref):
    i = pl.program_id(0)
    real = real_ref[i]

    @pl.when(real > 0)
    def _():
        row = lax.broadcasted_iota(jnp.int32, (xs_ref.shape[0], 1), 0)
        x = _unpack_bf16_pairs(jnp.where(row < real, xs_ref[...], 0)).astype(BF16)
        act = _silu(_dot(x, weg_ref[0])) * _dot(x, weu_ref[0])
        ys_ref[...] = _pack_bf16_pairs(_dot(act.astype(BF16), wed_ref[0]))

    @pl.when(real == 0)
    def _():
        ys_ref[...] = jnp.zeros_like(ys_ref)


def _ffn(xs, tile_expert, tile_real, W):
    tile = xs.shape[0] // tile_expert.shape[0]
    wspec = lambda shape: pl.BlockSpec((1,) + shape, lambda i, te, tr: (te[i], 0, 0))
    return pl.pallas_call(
        _ffn_kernel,
        out_shape=jax.ShapeDtypeStruct(xs.shape, jnp.int32),
        grid_spec=pltpu.PrefetchScalarGridSpec(
            num_scalar_prefetch=2,
            grid=(tile_expert.shape[0],),
            in_specs=[pl.BlockSpec((tile, PACKED_W), lambda i, te, tr: (i, 0)),
                      wspec((D_MODEL, EXPERT_FF)), wspec((D_MODEL, EXPERT_FF)), wspec((EXPERT_FF, D_MODEL))],
            out_specs=pl.BlockSpec((tile, PACKED_W), lambda i, te, tr: (i, 0))),
        compiler_params=_params("parallel"),
        name="moe_ffn",
    )(tile_expert, tile_real, xs, W["weg"], W["weu"], W["wed"])


def _combine_kernel(y_ref, h_ref, wtok_ref, x1_ref, mod_ref, wsg_ref, wsu_ref, wsd_ref,
                    g_ref, b_ref, out_ref):
    wt = wtok_ref[...]
    routed = None
    for k in range(TOP_K):
        part = wt[:, k:k + 1] * _unpack_bf16_pairs(y_ref[k])
        routed = part if routed is None else routed + part
    h = _unpack_bf16_pairs(h_ref[...]).astype(BF16)
    shared = _dot((_silu(_dot(h, wsg_ref[...])) * _dot(h, wsu_ref[...])).astype(BF16), wsd_ref[...])
    m = mod_ref[0]
    y2 = DN_ALPHA * x1_ref[...] + m[5:6] * (routed + shared)
    out_ref[...] = _layer_norm(y2, g_ref[...], b_ref[...])


def _combine(ytok, h2, wtok, x1, mod, W, *, latent):
    t = h2.shape[0]
    ct = COMBINE_TM
    steps_per_batch = (DEC_SEQ if latent else t) // ct
    row = lambda w: pl.BlockSpec((ct, w), lambda i: (i, 0))
    return pl.pallas_call(
        _combine_kernel,
        out_shape=jax.ShapeDtypeStruct((t, D_MODEL), F32),
        grid=(t // ct,),
        in_specs=[pl.BlockSpec((TOP_K, ct, PACKED_W), lambda i: (0, i, 0)),
                  row(PACKED_W), row(LANES), row(D_MODEL),
                  pl.BlockSpec((1, 6, D_MODEL), lambda i: (i // steps_per_batch, 0, 0)),
                  _full((D_MODEL, SHARED_FF)), _full((D_MODEL, SHARED_FF)), _full((SHARED_FF, D_MODEL)),
                  _full((1, D_MODEL)), _full((1, D_MODEL))],
        out_specs=row(D_MODEL),
        compiler_params=_params("parallel"),
        name="moe_combine",
    )(ytok, h2, wtok, x1, mod, W["wsg"], W["wsu"], W["wsd"], W["ln2_g"], W["ln2_b"])


def _moe(h2, x1, mod, W, *, latent):
    t = h2.shape[0]
    eid, rank, wtok, cnt = _router(h2, W["wr_t"], W["rbias"])
    pos, tile_expert, tile_real = _moe_plan(eid, rank, cnt, t)
    xs = _dispatch(h2, pos)
    ys = _ffn(xs, tile_expert, tile_real, W)
    ytok = _gather_rows(ys, pos.reshape(TOP_K * t)).reshape(TOP_K, t, PACKED_W)
    return _combine(ytok, h2, wtok, x1, mod, W, latent=latent)


def _rope_partner(n_blocks, block):
    half = block // 2
    i = np.arange(n_blocks * block)
    return np.where((i % block) < half, i + half, i - half)


_QB_NOPE = np.concatenate([np.arange(MLA_NOPE) + (MLA_NOPE + MLA_ROPE) * h for h in range(MLA_HEADS)])
_QB_ROPE = np.concatenate([np.arange(MLA_ROPE) + (MLA_NOPE + MLA_ROPE) * h + MLA_NOPE for h in range(MLA_HEADS)])
_KVB_NOPE = np.concatenate([np.arange(MLA_NOPE) + (MLA_NOPE + MLA_V) * h for h in range(MLA_HEADS)])
_KVB_V = np.concatenate([np.arange(MLA_V) + (MLA_NOPE + MLA_V) * h + MLA_NOPE for h in range(MLA_HEADS)])
_DIFF_PARTNER = _rope_partner(2 * DIFF_W // 32, 32)
_MLA_PARTNER = _rope_partner(MLA_HEADS * MLA_ROPE // 16, 16)
_KR_PARTNER = _rope_partner(MLA_ROPE // 16, 16)


def _layer_weights(l, w_in, mla_qa_g, mla_wq_b, mla_kva_g, mla_wkv_b, w_branch_diff, w_branch_na,
                   w_branch_mla, w_out, ln1_g, ln1_b, ln2_g, ln2_b, w_router, router_bias,
                   w_exp_gate, w_exp_up, w_exp_down, w_sh_gate, w_sh_up, w_sh_down):
    win = w_in[l].astype(BF16)
    wa = win[:, :3072]
    qa_kva = win[:, 3072:3712]
    kr = win[:, 3712:3744]
    kr4 = jnp.tile(kr, (1, LANES // MLA_ROPE))
    krp4 = jnp.tile(kr[:, _KR_PARTNER], (1, LANES // MLA_ROPE))
    wqb = mla_wq_b[l].astype(BF16)
    wq_rope = wqb[:, _QB_ROPE]
    wkvb = mla_wkv_b[l].astype(BF16)
    row = lambda v: v[l].reshape(1, -1).astype(F32)
    return {
        "wa": wa,
        "wp": wa[:, :2 * DIFF_W][:, _DIFF_PARTNER],
        "wm_ctx": jnp.concatenate([qa_kva, kr4], axis=1),
        "wm_lat": jnp.concatenate([qa_kva, kr4, krp4], axis=1),
        "qag": row(mla_qa_g), "kvag": row(mla_kva_g),
        "wqb": jnp.concatenate([wqb[:, _QB_NOPE], wq_rope], axis=1),
        "wqp": wq_rope[:, _MLA_PARTNER],
        "wkvb": jnp.concatenate([wkvb[:, _KVB_NOPE], wkvb[:, _KVB_V]], axis=1),
        "wg": win[:, 3744:],
        "wbd": w_branch_diff[l].astype(BF16), "wbn": w_branch_na[l].astype(BF16),
        "wbm": w_branch_mla[l].astype(BF16), "wo": w_out[l].astype(BF16),
        "ln1_g": row(ln1_g), "ln1_b": row(ln1_b), "ln2_g": row(ln2_g), "ln2_b": row(ln2_b),
        "wr_t": w_router[l].T.astype(BF16), "rbias": router_bias[l].reshape(N_EXPERTS, 1).astype(F32),
        "weg": w_exp_gate[l].astype(BF16), "weu": w_exp_up[l].astype(BF16), "wed": w_exp_down[l].astype(BF16),
        "wsg": w_sh_gate[l].astype(BF16), "wsu": w_sh_up[l].astype(BF16), "wsd": w_sh_down[l].astype(BF16),
    }


def _rope_tables():
    t = jnp.arange(DEC_SEQ)
    pos = [(t // GRID_W).astype(F32), (t % GRID_W).astype(F32)]

    def table(block):
        half = block // 4
        inv = ROPE_BASE ** (-jnp.arange(half, dtype=F32) / half)
        cos, sin = [], []
        for p in pos:
            ang = p[:, None] * inv[None, :]
            cos += [jnp.cos(ang), jnp.cos(ang)]
            sin += [-jnp.sin(ang), jnp.sin(ang)]
        reps = LANES // block
        return (jnp.tile(jnp.concatenate(cos, axis=1), (1, reps)),
                jnp.tile(jnp.concatenate(sin, axis=1), (1, reps)))

    cd, sd = table(DIFF_DH)
    cm, sm = table(MLA_ROPE)
    return {"cd": cd, "sd": sd, "cm": cm, "sm": sm}


def kernel(x_prompt, x_sample, cache_diff_k, cache_diff_v, cache_na_k, cache_na_v, cache_mla_ckv, cache_mla_krope, c, c_ctx, w_mod, b_mod, w_in, diff_lambda, diff_subln_g, na_rpb, mla_qa_g, mla_wq_b, mla_kva_g, mla_wkv_b, w_branch_diff, w_branch_na, w_branch_mla, w_out, ln1_g, ln1_b, ln2_g, ln2_b, w_router, router_bias, w_exp_gate, w_exp_up, w_exp_down, w_sh_gate, w_sh_up, w_sh_down):
    t_ctx = BATCH * SEQ
    t_lat = DEC_BATCH * DEC_SEQ
    cvec = jnp.concatenate([c, c_ctx[None, :], jnp.zeros((MOD_ROWS - DEC_BATCH - 1, D_MODEL), F32)], axis=0)
    mods = _modulation(cvec, w_mod, b_mod).reshape(DEPTH, MOD_ROWS, 6, D_MODEL)
    rope = _rope_tables()
    ck_d = cache_diff_k.reshape(DEC_BATCH, DEPTH, PAST_LEN, DIFF_W)
    cv_d = cache_diff_v.reshape(DEC_BATCH, DEPTH, PAST_LEN, DIFF_W)
    ck_n = cache_na_k.reshape(DEC_BATCH, DEPTH, PAST_LEN, NA_W)
    cv_n = cache_na_v.reshape(DEC_BATCH, DEPTH, PAST_LEN, NA_W)

    xc = x_prompt.reshape(t_ctx, D_MODEL)
    xl = x_sample.reshape(t_lat, D_MODEL)
    states = []
    for l in range(DEPTH):
        W = _layer_weights(l, w_in, mla_qa_g, mla_wq_b, mla_kva_g, mla_wkv_b, w_branch_diff, w_branch_na,
                           w_branch_mla, w_out, ln1_g, ln1_b, ln2_g, ln2_b, w_router, router_bias,
                           w_exp_gate, w_exp_up, w_exp_down, w_sh_gate, w_sh_up, w_sh_down)
        lam = diff_lambda[l].astype(F32)
        subln = diff_subln_g[l].reshape(1, LANES).astype(F32)
        mod_c = mods[l, DEC_BATCH:DEC_BATCH + 1]
        mod_l = mods[l, :DEC_BATCH]

        Wc = dict(W, wm=W["wm_ctx"])
        names = ["dq", "nq", "mqn", "mqr", "kr4", "kn", "vn", "dk", "dv", "nk", "nv", "ckv", "kr"]
        pc = dict(zip(names, _project(xc, mod_c, Wc, None, latent=False)))
        od, on, om = _attn_ctx(pc, lam, subln, l)
        x1, h2 = _merge(xc, mod_c, od, on, om, W, latent=False)
        xc = _moe(h2, x1, mod_c, W, latent=False)
        states.append(pc)

        Wl = dict(W, wm=W["wm_lat"])
        names = ["dq", "dk", "dv", "nq", "nk", "nv", "mqn", "mqr", "kr4", "kn", "vn"]
        pll = dict(zip(names, _project(xl, mod_l, Wl, rope, latent=True)))
        od = _attn_diff_lat(pll, ck_d, cv_d, lam, subln, l)
        on = _attn_na_lat(pll, ck_n, cv_n, _na_bias(na_rpb[l].astype(F32)), l)
        kc, vc, krc = _mla_cache(cache_mla_ckv, cache_mla_krope, W["wkvb"], l)
        om = _attn_mla_lat(pll, kc, vc, krc)
        x1, h2 = _merge(xl, mod_l, od, on, om, W, latent=True)
        xl = _moe(h2, x1, mod_l, W, latent=True)

    def stack(name, shape):
        return jnp.stack([s[name].reshape((BATCH, SEQ) + shape) for s in states], axis=1)

    return (xc.reshape(BATCH, SEQ, D_MODEL), xl.reshape(DEC_BATCH, DEC_SEQ, D_MODEL),
            stack("dk", (DIFF_HEADS, 2 * DIFF_DH)), stack("dv", (DIFF_HEADS, 2 * DIFF_DH)),
            stack("nk", (NA_HEADS, NA_DH)), stack("nv", (NA_HEADS, NA_DH)),
            stack("ckv", (MLA_KV_LORA,)), stack("kr", (MLA_ROPE,)))
```

```python
import functools

import numpy as np
import jax
import jax.numpy as jnp
from jax import lax
from jax.experimental import pallas as pl
from jax.experimental.pallas import tpu as pltpu
from jax.experimental.pallas import tpu_sc as plsc

D_MODEL = 1024
BATCH = 32
SEQ = 256
DEPTH = 2
DEC_BATCH = 8
DEC_SEQ = 2048
PAST_LEN = 512
GRID_W = 64
GRID_ROWS = DEC_SEQ // GRID_W
ROPE_BASE = 10000.0
DIFF_HEADS = 4
DIFF_DH = 64
DIFF_W = 512
NA_HEADS = 8
NA_DH = 64
NA_W = 512
NA_WIN_ROWS = 8
NA_WIN_COLS = 16
MLA_HEADS = 8
MLA_Q_LORA = 384
MLA_KV_LORA = 256
MLA_NOPE = 64
MLA_ROPE = 32
MLA_V = 64
MLA_W = 512
N_EXPERTS = 64
N_GROUPS = 8
GROUP_SIZE = N_EXPERTS // N_GROUPS
TOPK_GROUPS = 4
TOP_K = 8
EXPERT_FF = 256
SHARED_FF = 256
ROUTED_SCALE = 2.5
DN_ALPHA = (2 * DEPTH) ** 0.25
LN_EPS = 1e-5
RMS_EPS = 1e-6

F32 = jnp.float32
BF16 = jnp.bfloat16

LANES = 128
VMEM_LIMIT_BYTES = 56 * 1024 * 1024
NEG_BIG = -1e30
LOG2E = 1.4426950408889634

DIFF_SCALE = DIFF_DH ** -0.5
NA_SCALE = NA_DH ** -0.5
MLA_SCALE = (MLA_NOPE + MLA_ROPE) ** -0.5

PROJ_TM = 512
ATT_TQ = 512
NA_TQ = 256
NA_Q_ROWS = NA_TQ // GRID_W
NA_KEY_ROWS = NA_Q_ROWS + NA_WIN_ROWS
NA_KEYS = NA_KEY_ROWS * GRID_W
MOE_TILE_MIN = 128
MOE_TILE_MAX = 1024
SC_ROWS = 128
COMBINE_TM = 256


def _dot(a, b):
    return jnp.dot(a, b, preferred_element_type=F32)


def _dot_nt(a, b):
    return lax.dot_general(a, b, (((1,), (1,)), ((), ())), preferred_element_type=F32)


def _sigmoid(x):
    return 1.0 / (1.0 + jnp.exp(-x))


def _silu(x):
    return x * _sigmoid(x)


def _params(*sem):
    return pltpu.CompilerParams(dimension_semantics=sem, vmem_limit_bytes=VMEM_LIMIT_BYTES)


def _full(shape):
    n = len(shape)
    return pl.BlockSpec(shape, lambda *_: (0,) * n)


def _layer_norm(y, g, b):
    mu = jnp.mean(y, axis=-1, keepdims=True)
    yc = y - mu
    var = jnp.mean(yc * yc, axis=-1, keepdims=True)
    return yc * lax.rsqrt(var + LN_EPS) * g + b


def _rms(x, g):
    return x * lax.rsqrt(jnp.mean(x * x, axis=-1, keepdims=True) + RMS_EPS) * g


HIGH_HALF = -65536
PACKED_W = D_MODEL // 2


def _pack_bf16_pairs(x):
    n = x.shape[1] // 2
    bits = lax.bitcast_convert_type(x.astype(BF16).astype(F32), jnp.int32)
    return lax.shift_right_logical(bits[:, :n], 16) | (bits[:, n:] & HIGH_HALF)


def _unpack_bf16_pairs(w):
    lo = lax.bitcast_convert_type(lax.shift_left(w, 16), F32)
    hi = lax.bitcast_convert_type(w & HIGH_HALF, F32)
    return jnp.concatenate([lo, hi], axis=1)


MOD_ROWS = 16
MOD_TN = 1536


def _mod_kernel(c_ref, w_ref, b_ref, o_ref):
    s = _silu(c_ref[...]).astype(BF16)
    o_ref[0] = _dot(s, w_ref[0].astype(BF16)) + b_ref[0]


def _modulation(cvec, w_mod, b_mod):
    n = 6 * D_MODEL
    return pl.pallas_call(
        _mod_kernel,
        out_shape=jax.ShapeDtypeStruct((DEPTH, MOD_ROWS, n), F32),
        grid=(DEPTH, n // MOD_TN),
        in_specs=[
            pl.BlockSpec((MOD_ROWS, D_MODEL), lambda l, j: (0, 0)),
            pl.BlockSpec((1, D_MODEL, MOD_TN), lambda l, j: (l, 0, j)),
            pl.BlockSpec((1, 1, MOD_TN), lambda l, j: (l, 0, j)),
        ],
        out_specs=pl.BlockSpec((1, MOD_ROWS, MOD_TN), lambda l, j: (l, 0, j)),
        compiler_params=_params("parallel", "parallel"),
        name="modulation",
    )(cvec, w_mod, b_mod.reshape(DEPTH, 1, n))


def _proj_common(x_ref, mod_ref, wa_ref, wm_ref, qag_ref, kvag_ref, wqb_ref, wkvb_ref):
    m = mod_ref[0]
    h = (x_ref[...] * (1.0 + m[1:2]) + m[0:1]).astype(BF16)
    a = _dot(h, wa_ref[...])
    mm = _dot(h, wm_ref[...])
    qan = _rms(mm[:, :MLA_Q_LORA], qag_ref[...]).astype(BF16)
    mq = _dot(qan, wqb_ref[...])
    ckv = _rms(mm[:, MLA_Q_LORA:MLA_Q_LORA + MLA_KV_LORA], kvag_ref[...])
    kv = _dot(ckv.astype(BF16), wkvb_ref[...])
    return h, a, mm, qan, mq, ckv, kv


def _proj_ctx_kernel(x_ref, mod_ref, wa_ref, wm_ref, qag_ref, kvag_ref, wqb_ref, wkvb_ref,
                     dq_ref, nq_ref, mqn_ref, mqr_ref, kr4_ref, kn_ref, vn_ref,
                     dk_ref, dv_ref, nk_ref, nv_ref, ckv_ref, kr_ref):
    _, a, mm, _, mq, ckv, kv = _proj_common(x_ref, mod_ref, wa_ref, wm_ref, qag_ref, kvag_ref,
                                            wqb_ref, wkvb_ref)
    dq_ref[...] = (a[:, 0:512] * DIFF_SCALE).astype(BF16)
    dk_ref[...] = a[:, 512:1024]
    dv_ref[...] = a[:, 1024:1536]
    nq_ref[...] = (a[:, 1536:2048] * NA_SCALE).astype(BF16)
    nk_ref[...] = a[:, 2048:2560]
    nv_ref[...] = a[:, 2560:3072]
    mqn_ref[...] = (mq[:, :512] * MLA_SCALE).astype(BF16)
    mqr_ref[...] = (mq[:, 512:768] * MLA_SCALE).astype(BF16)
    kr4 = mm[:, 640:768]
    kr4_ref[...] = kr4.astype(BF16)
    kr_ref[...] = kr4[:, :MLA_ROPE]
    ckv_ref[...] = ckv
    kn_ref[...] = kv[:, :512].astype(BF16)
    vn_ref[...] = kv[:, 512:].astype(BF16)


def _proj_lat_kernel(x_ref, mod_ref, wa_ref, wm_ref, qag_ref, kvag_ref, wqb_ref, wkvb_ref,
                     wp_ref, wqp_ref, cd_ref, sd_ref, cm_ref, sm_ref,
                     dq_ref, dk_ref, dv_ref, nq_ref, nk_ref, nv_ref,
                     mqn_ref, mqr_ref, kr4_ref, kn_ref, vn_ref):
    h, a, mm, qan, mq, _, kv = _proj_common(x_ref, mod_ref, wa_ref, wm_ref, qag_ref, kvag_ref,
                                            wqb_ref, wkvb_ref)
    ap = _dot(h, wp_ref[...])
    mqp = _dot(qan, wqp_ref[...])
    cd = cd_ref[...]
    sd = sd_ref[...]
    cm = cm_ref[...]
    sm = sm_ref[...]
    for j in range(DIFF_W // LANES):
        lo, hi = LANES * j, LANES * (j + 1)
        dq_ref[:, lo:hi] = ((a[:, lo:hi] * cd + ap[:, lo:hi] * sd) * (DIFF_SCALE * LOG2E)).astype(BF16)
        dk_ref[:, lo:hi] = (a[:, 512 + lo:512 + hi] * cd + ap[:, 512 + lo:512 + hi] * sd).astype(BF16)
    dv_ref[...] = a[:, 1024:1536].astype(BF16)
    nq_ref[...] = (a[:, 1536:2048] * (NA_SCALE * LOG2E)).astype(BF16)
    nk_ref[...] = a[:, 2048:2560].astype(BF16)
    nv_ref[...] = a[:, 2560:3072].astype(BF16)
    mqn_ref[...] = (mq[:, :512] * (MLA_SCALE * LOG2E)).astype(BF16)
    for j in range(2):
        lo, hi = LANES * j, LANES * (j + 1)
        mqr_ref[:, lo:hi] = ((mq[:, 512 + lo:512 + hi] * cm + mqp[:, lo:hi] * sm)
                             * (MLA_SCALE * LOG2E)).astype(BF16)
    kr4_ref[...] = (mm[:, 640:768] * cm + mm[:, 768:896] * sm).astype(BF16)
    kn_ref[...] = kv[:, :512].astype(BF16)
    vn_ref[...] = kv[:, 512:].astype(BF16)


def _project(x, mod, W, rope, *, latent):
    t = x.shape[0]
    tm = PROJ_TM
    tokens_per_batch = DEC_SEQ if latent else t
    steps_per_batch = tokens_per_batch // tm
    row = lambda w: pl.BlockSpec((tm, w), lambda i: (i, 0))
    common_in = [
        row(D_MODEL),
        pl.BlockSpec((1, 6, D_MODEL), lambda i: (i // steps_per_batch, 0, 0)),
        _full(W["wa"].shape), _full(W["wm"].shape), _full((1, MLA_Q_LORA)), _full((1, MLA_KV_LORA)),
        _full(W["wqb"].shape), _full(W["wkvb"].shape),
    ]
    common_args = [x, mod, W["wa"], W["wm"], W["qag"], W["kvag"], W["wqb"], W["wkvb"]]
    bf = lambda w: jax.ShapeDtypeStruct((t, w), BF16)
    f32 = lambda w: jax.ShapeDtypeStruct((t, w), F32)
    if latent:
        tab = pl.BlockSpec((tm, LANES), lambda i: (i % steps_per_batch, 0))
        widths = [512, 512, 512, 512, 512, 512, 512, 256, 128, 512, 512]
        return pl.pallas_call(
            _proj_lat_kernel,
            out_shape=[bf(w) for w in widths],
            grid=(t // tm,),
            in_specs=common_in + [_full(W["wp"].shape), _full(W["wqp"].shape), tab, tab, tab, tab],
            out_specs=[row(w) for w in widths],
            compiler_params=_params("parallel"),
            name="proj_lat",
        )(*common_args, W["wp"], W["wqp"], rope["cd"], rope["sd"], rope["cm"], rope["sm"])
    bf_w = [512, 512, 512, 256, 128, 512, 512]
    f32_w = [512, 512, 512, 512, 256, 32]
    return pl.pallas_call(
        _proj_ctx_kernel,
        out_shape=[bf(w) for w in bf_w] + [f32(w) for w in f32_w],
        grid=(t // tm,),
        in_specs=common_in,
        out_specs=[row(w) for w in bf_w + f32_w],
        compiler_params=_params("parallel"),
        name="proj_ctx",
    )(*common_args)


def _lane_iota():
    return lax.broadcasted_iota(jnp.int32, (1, LANES), 1)


def _softmax_parts(parts, exp):
    m = functools.reduce(jnp.maximum, [jnp.max(s, axis=-1, keepdims=True) for s in parts])
    es = [exp(s - m) for s in parts]
    l = functools.reduce(lambda u, v: u + v, [jnp.sum(e, axis=-1, keepdims=True) for e in es])
    return es, l


def _diff_lambda(lam_ref, layer):
    lp = lam_ref[...]
    lam_init = 0.8 - 0.6 * float(np.exp(-0.3 * layer))
    s1 = jnp.sum(lp[0:1] * lp[1:2], axis=-1, keepdims=True)
    s2 = jnp.sum(lp[2:3] * lp[3:4], axis=-1, keepdims=True)
    return jnp.exp(s1) - jnp.exp(s2) + lam_init, lam_init


def _diff_heads(q_ref, ks, vs, lam_ref, g_ref, o_ref, layer, exp):
    lam, lam_init = _diff_lambda(lam_ref, layer)
    first_map = _lane_iota() < DIFF_DH
    g = g_ref[...]
    for h in range(DIFF_HEADS):
        hs = slice(LANES * h, LANES * (h + 1))
        q = q_ref[:, hs]
        q1 = jnp.where(first_map, q, jnp.zeros_like(q))
        q2 = jnp.where(first_map, jnp.zeros_like(q), q)
        kk = [k[:, hs].astype(BF16) for k in ks]
        e1, l1 = _softmax_parts([_dot_nt(q1, k) for k in kk], exp)
        e2, l2 = _softmax_parts([_dot_nt(q2, k) for k in kk], exp)
        c1 = 1.0 / l1
        c2 = lam / l2
        o = None
        for a1, a2, v in zip(e1, e2, vs):
            part = _dot((a1 * c1 - a2 * c2).astype(BF16), v[:, hs].astype(BF16))
            o = part if o is None else o + part
        o = _rms(o, g) * (1.0 - lam_init)
        o_ref[:, hs] = o.astype(BF16)


def _pair_heads(q_of, k_of, v_of, bias_of, o_ref, n_pairs, exp, stack=False):
    first = _lane_iota() < 64
    for j in range(n_pairs):
        ps = slice(LANES * j, LANES * (j + 1))
        ks = k_of(j)
        vs = v_of(j)
        qs = [q_of(j, 0), q_of(j, 1)]
        if stack:
            qs = [jnp.concatenate(qs, axis=0)]
        outs = []
        for q in qs:
            ss = [_dot_nt(q, k) for k in ks]
            ss = [s if b is None else s + b for s, b in zip(ss, bias_of(j))]
            es, l = _softmax_parts(ss, exp)
            o = None
            for e, v in zip(es, vs):
                part = _dot(e.astype(BF16), v)
                o = part if o is None else o + part
            outs.append(o / l)
        if stack:
            n = outs[0].shape[0] // 2
            outs = [outs[0][:n], outs[0][n:]]
        o_ref[:, ps] = jnp.where(first, outs[0], outs[1]).astype(BF16)


def _na_q(q_ref):
    first = _lane_iota() < NA_DH

    def q_of(j, hh):
        q = q_ref[:, LANES * j:LANES * (j + 1)]
        keep = first if hh == 0 else jnp.logical_not(first)
        return jnp.where(keep, q, jnp.zeros_like(q))
    return q_of


def _mla_q(qn_ref, qr_ref):
    lane = _lane_iota()
    first = lane < MLA_NOPE

    def q_of(j, hh):
        h = 2 * j + hh
        qn = qn_ref[:, LANES * j:LANES * (j + 1)]
        keep = first if hh == 0 else jnp.logical_not(first)
        qn = jnp.where(keep, qn, jnp.zeros_like(qn))
        qr = qr_ref[:, LANES * (h // 4):LANES * (h // 4 + 1)]
        qr = jnp.where((lane // MLA_ROPE) == (h % 4), qr, jnp.zeros_like(qr))
        return jnp.concatenate([qn, qr], axis=1)
    return q_of


def _attn_ctx_kernel(layer, dq_ref, dk_ref, dv_ref, nq_ref, nk_ref, nv_ref,
                     mqn_ref, mqr_ref, kn_ref, kr4_ref, vn_ref, lam_ref, g_ref,
                     od_ref, on_ref, om_ref):
    _diff_heads(dq_ref, [dk_ref], [dv_ref], lam_ref, g_ref, od_ref, layer, jnp.exp)
    none = lambda j: [None]
    pair = lambda j: slice(LANES * j, LANES * (j + 1))
    _pair_heads(_na_q(nq_ref),
                lambda j: [nk_ref[:, pair(j)].astype(BF16)],
                lambda j: [nv_ref[:, pair(j)].astype(BF16)],
                none, on_ref, NA_HEADS // 2, jnp.exp, stack=True)
    kr4 = kr4_ref[...]
    _pair_heads(_mla_q(mqn_ref, mqr_ref),
                lambda j: [jnp.concatenate([kn_ref[:, pair(j)], kr4], axis=1)],
                lambda j: [vn_ref[:, pair(j)]],
                none, om_ref, MLA_HEADS // 2, jnp.exp, stack=True)


def _attn_ctx(p, lam, g, layer):
    t = p["dq"].shape[0]
    row = lambda w: pl.BlockSpec((SEQ, w), lambda b: (b, 0))
    names = ["dq", "dk", "dv", "nq", "nk", "nv", "mqn", "mqr", "kn", "kr4", "vn"]
    out = jax.ShapeDtypeStruct((t, 512), BF16)
    return pl.pallas_call(
        functools.partial(_attn_ctx_kernel, layer),
        out_shape=[out, out, out],
        grid=(t // SEQ,),
        in_specs=[row(p[n].shape[1]) for n in names] + [_full((4, DIFF_DH)), _full((1, LANES))],
        out_specs=[row(512)] * 3,
        compiler_params=_params("parallel"),
        name="attn_ctx",
    )(*[p[n] for n in names], lam, g)


def _cache_spec(width, layer):
    return pl.BlockSpec((None, None, PAST_LEN, width), lambda b, q: (b, layer, 0, 0))


def _batch_spec(width):
    return pl.BlockSpec((None, DEC_SEQ, width), lambda b, q: (b, 0, 0))


def _qtile_spec(width, tq):
    steps = DEC_SEQ // tq
    return pl.BlockSpec((tq, width), lambda b, q: (b * steps + q, 0))


def _attn_diff_lat_kernel(layer, q_ref, kc_ref, vc_ref, kn_ref, vn_ref, lam_ref, g_ref, o_ref):
    _diff_heads(q_ref, [kc_ref, kn_ref], [vc_ref, vn_ref], lam_ref, g_ref, o_ref, layer, jnp.exp2)


def _attn_diff_lat(p, cache_k, cache_v, lam, g, layer):
    t = p["dq"].shape[0]
    b3 = lambda a: a.reshape(DEC_BATCH, DEC_SEQ, a.shape[1])
    return pl.pallas_call(
        functools.partial(_attn_diff_lat_kernel, layer),
        out_shape=jax.ShapeDtypeStruct((t, DIFF_W), BF16),
        grid=(DEC_BATCH, DEC_SEQ // ATT_TQ),
        in_specs=[_qtile_spec(DIFF_W, ATT_TQ), _cache_spec(DIFF_W, layer), _cache_spec(DIFF_W, layer),
                  _batch_spec(DIFF_W), _batch_spec(DIFF_W),
                  pl.BlockSpec((4, DIFF_DH), lambda b, q: (0, 0)),
                  pl.BlockSpec((1, LANES), lambda b, q: (0, 0))],
        out_specs=_qtile_spec(DIFF_W, ATT_TQ),
        compiler_params=_params("parallel", "parallel"),
        name="attn_diff_lat",
    )(p["dq"], cache_k, cache_v, b3(p["dk"]), b3(p["dv"]), lam, g)


def _na_key_start(q):
    return jnp.clip(q * NA_Q_ROWS - NA_WIN_ROWS // 2, 0, GRID_ROWS - NA_KEY_ROWS)


def _attn_na_lat_kernel(q_ref, kc_ref, vc_ref, kn_ref, vn_ref, bias_ref, o_ref):
    start = pl.multiple_of(_na_key_start(pl.program_id(1)) * GRID_W, GRID_W)
    kw = kn_ref[pl.ds(start, NA_KEYS), :]
    vw = vn_ref[pl.ds(start, NA_KEYS), :]
    pair = lambda j: slice(LANES * j, LANES * (j + 1))
    _pair_heads(_na_q(q_ref),
                lambda j: [kc_ref[:, pair(j)].astype(BF16), kw[:, pair(j)]],
                lambda j: [vc_ref[:, pair(j)].astype(BF16), vw[:, pair(j)]],
                lambda j: [None, bias_ref[j]],
                o_ref, NA_HEADS // 2, jnp.exp2, stack=True)


def _na_bias_tables():
    n_blocks = DEC_SEQ // NA_TQ
    qr = np.arange(NA_Q_ROWS)
    kr = np.arange(NA_KEY_ROWS)
    row_sel = []
    for qb in range(n_blocks):
        ks = int(np.clip(qb * NA_Q_ROWS - NA_WIN_ROWS // 2, 0, GRID_ROWS - NA_KEY_ROWS))
        r = qb * NA_Q_ROWS + qr
        r0 = np.clip(r - NA_WIN_ROWS // 2, 0, GRID_ROWS - NA_WIN_ROWS)
        krow = ks + kr
        ok = (krow[None, :] >= r0[:, None]) & (krow[None, :] < r0[:, None] + NA_WIN_ROWS)
        off = np.where(ok, krow[None, :] - r[:, None] + NA_WIN_ROWS - 1, 2 * NA_WIN_ROWS - 1)
        sel = off[:, :, None] == np.arange(2 * NA_WIN_ROWS)
        row_sel.append(sel.astype(np.float32))
    kinds, kind_of_block = [], []
    for qb in range(n_blocks):
        for n, other in enumerate(kinds):
            if np.array_equal(row_sel[qb], row_sel[other]):
                kind_of_block.append(n)
                break
        else:
            kind_of_block.append(len(kinds))
            kinds.append(qb)
    c = np.arange(GRID_W)
    c0 = np.clip(c - NA_WIN_COLS // 2, 0, GRID_W - NA_WIN_COLS)
    col_ok = (c[None, :] >= c0[:, None]) & (c[None, :] < c0[:, None] + NA_WIN_COLS)
    coff = np.where(col_ok, c[None, :] - c[:, None] + NA_WIN_COLS - 1, 2 * NA_WIN_COLS - 1)
    col_sel = (coff[:, :, None] == np.arange(2 * NA_WIN_COLS)).astype(np.float32)
    rsel = np.stack([row_sel[qb] for qb in kinds])
    return rsel, col_sel, kind_of_block


_NA_ROW_SEL, _NA_COL_SEL, _NA_KIND_OF_BLOCK = _na_bias_tables()


def _na_bias(rpb):
    hp = lax.Precision.HIGHEST
    table = jnp.pad(rpb * LOG2E, ((0, 0), (0, 1), (0, 1)), constant_values=NEG_BIG)
    cols = jnp.einsum("hij,ckj->hick", table, _NA_COL_SEL, precision=hp)
    b = jnp.einsum("nqri,hick->nhqcrk", _NA_ROW_SEL, cols, precision=hp)
    return b.reshape(_NA_ROW_SEL.shape[0], NA_HEADS // 2, 2 * NA_TQ, NA_KEYS).astype(F32)


def _na_kind(q):
    return (q > 0).astype(jnp.int32) + (q == DEC_SEQ // NA_TQ - 1).astype(jnp.int32)


def _attn_na_lat(p, cache_k, cache_v, bias, layer):
    assert _NA_KIND_OF_BLOCK == [0] + [1] * (DEC_SEQ // NA_TQ - 2) + [2]
    t = p["nq"].shape[0]
    b3 = lambda a: a.reshape(DEC_BATCH, DEC_SEQ, a.shape[1])
    return pl.pallas_call(
        _attn_na_lat_kernel,
        out_shape=jax.ShapeDtypeStruct((t, NA_W), BF16),
        grid=(DEC_BATCH, DEC_SEQ // NA_TQ),
        in_specs=[_qtile_spec(NA_W, NA_TQ), _cache_spec(NA_W, layer), _cache_spec(NA_W, layer),
                  _batch_spec(NA_W), _batch_spec(NA_W),
                  pl.BlockSpec((None, NA_HEADS // 2, 2 * NA_TQ, NA_KEYS),
                               lambda b, q: (_na_kind(q), 0, 0, 0))],
        out_specs=_qtile_spec(NA_W, NA_TQ),
        compiler_params=_params("parallel", "parallel"),
        name="attn_na_lat",
    )(p["nq"], cache_k, cache_v, b3(p["nk"]), b3(p["nv"]), bias)


def _mla_cache_kernel(ckv_ref, kr_ref, wkvb_ref, rep_ref, kc_ref, vc_ref, krc_ref):
    kv = _dot(ckv_ref[...].astype(BF16), wkvb_ref[...])
    kc_ref[...] = kv[:, :512].astype(BF16)
    vc_ref[...] = kv[:, 512:].astype(BF16)
    krc_ref[...] = _dot(kr_ref[...].astype(BF16), rep_ref[...]).astype(BF16)


def _mla_cache(cache_ckv, cache_kr, wkvb, layer):
    rep = jnp.asarray(np.tile(np.eye(MLA_ROPE, dtype=np.float32), (1, LANES // MLA_ROPE)), BF16)
    spec_in = lambda w: pl.BlockSpec((None, None, PAST_LEN, w), lambda b: (b, layer, 0, 0))
    spec_out = lambda w: pl.BlockSpec((None, PAST_LEN, w), lambda b: (b, 0, 0))
    shp = lambda w: jax.ShapeDtypeStruct((DEC_BATCH, PAST_LEN, w), BF16)
    return pl.pallas_call(
        _mla_cache_kernel,
        out_shape=[shp(512), shp(512), shp(LANES)],
        grid=(DEC_BATCH,),
        in_specs=[spec_in(MLA_KV_LORA), spec_in(MLA_ROPE), _full(wkvb.shape), _full(rep.shape)],
        out_specs=[spec_out(512), spec_out(512), spec_out(LANES)],
        compiler_params=_params("parallel"),
        name="mla_cache",
    )(cache_ckv, cache_kr, wkvb, rep)


def _attn_mla_lat_kernel(qn_ref, qr_ref, kc_ref, krc_ref, vc_ref, kn_ref, krn_ref, vn_ref, o_ref):
    pair = lambda j: slice(LANES * j, LANES * (j + 1))
    krc = krc_ref[...]
    krn = krn_ref[...]
    _pair_heads(_mla_q(qn_ref, qr_ref),
                lambda j: [jnp.concatenate([kc_ref[:, pair(j)], krc], axis=1),
                           jnp.concatenate([kn_ref[:, pair(j)], krn], axis=1)],
                lambda j: [vc_ref[:, pair(j)], vn_ref[:, pair(j)]],
                lambda j: [None, None],
                o_ref, MLA_HEADS // 2, jnp.exp2)


def _attn_mla_lat(p, kc, vc, krc):
    t = p["mqn"].shape[0]
    b3 = lambda a: a.reshape(DEC_BATCH, DEC_SEQ, a.shape[1])
    cspec = lambda w: pl.BlockSpec((None, PAST_LEN, w), lambda b, q: (b, 0, 0))
    return pl.pallas_call(
        _attn_mla_lat_kernel,
        out_shape=jax.ShapeDtypeStruct((t, MLA_W), BF16),
        grid=(DEC_BATCH, DEC_SEQ // ATT_TQ),
        in_specs=[_qtile_spec(512, ATT_TQ), _qtile_spec(256, ATT_TQ), cspec(512), cspec(LANES), cspec(512),
                  _batch_spec(512), _batch_spec(LANES), _batch_spec(512)],
        out_specs=_qtile_spec(MLA_W, ATT_TQ),
        compiler_params=_params("parallel", "parallel"),
        name="attn_mla_lat",
    )(p["mqn"], p["mqr"], kc, krc, vc, b3(p["kn"]), b3(p["kr4"]), b3(p["vn"]))


def _merge_kernel(x_ref, mod_ref, od_ref, on_ref, om_ref, wg_ref, wbd_ref, wbn_ref, wbm_ref, wo_ref,
                  g_ref, b_ref, x1_ref, h2_ref):
    x = x_ref[...]
    m = mod_ref[0]
    h = (x * (1.0 + m[1:2]) + m[0:1]).astype(BF16)
    gates = _dot(h, wg_ref[...])
    mix = (_sigmoid(gates[:, 0:1024]) * _dot(od_ref[...], wbd_ref[...])
           + _sigmoid(gates[:, 1024:2048]) * _dot(on_ref[...], wbn_ref[...])
           + _sigmoid(gates[:, 2048:3072]) * _dot(om_ref[...], wbm_ref[...]))
    out = _dot(mix.astype(BF16), wo_ref[...])
    x1 = _layer_norm(DN_ALPHA * x + m[2:3] * out, g_ref[...], b_ref[...])
    x1_ref[...] = x1
    h2_ref[...] = _pack_bf16_pairs(x1 * (1.0 + m[4:5]) + m[3:4])


def _merge(x, mod, od, on, om, W, *, latent):
    t = x.shape[0]
    tm = PROJ_TM
    steps_per_batch = (DEC_SEQ if latent else t) // tm
    row = lambda w: pl.BlockSpec((tm, w), lambda i: (i, 0))
    return pl.pallas_call(
        _merge_kernel,
        out_shape=[jax.ShapeDtypeStruct((t, D_MODEL), F32), jax.ShapeDtypeStruct((t, PACKED_W), jnp.int32)],
        grid=(t // tm,),
        in_specs=[row(D_MODEL),
                  pl.BlockSpec((1, 6, D_MODEL), lambda i: (i // steps_per_batch, 0, 0)),
                  row(512), row(512), row(512),
                  _full(W["wg"].shape), _full(W["wbd"].shape), _full(W["wbn"].shape),
                  _full(W["wbm"].shape), _full(W["wo"].shape),
                  _full((1, D_MODEL)), _full((1, D_MODEL))],
        out_specs=[row(D_MODEL), row(PACKED_W)],
        compiler_params=_params("parallel"),
        name="merge",
    )(x, mod, od, on, om, W["wg"], W["wbd"], W["wbn"], W["wbm"], W["wo"], W["ln1_g"], W["ln1_b"])


def _first_index_of_max(vals, idx, sentinel):
    mx = functools.reduce(jnp.maximum, [jnp.max(v, axis=0, keepdims=True) for v in vals])
    cand = [jnp.min(jnp.where(v == mx, i, sentinel), axis=0, keepdims=True) for v, i in zip(vals, idx)]
    return mx, functools.reduce(jnp.minimum, cand)


def _router_kernel(h_ref, wr_ref, bias_ref, tri_ref, eid_ref, rank_ref, wtok_ref, cnt_ref, base_ref):
    tm = h_ref.shape[0]

    @pl.when(pl.program_id(0) == 0)
    def _():
        base_ref[...] = jnp.zeros_like(base_ref)

    logits = _dot_nt(wr_ref[...], _unpack_bf16_pairs(h_ref[...]).astype(BF16))
    scores = _sigmoid(logits)
    biased = scores + bias_ref[...]
    member = lax.broadcasted_iota(jnp.int32, (GROUP_SIZE, tm), 0)
    slabs = [biased[GROUP_SIZE * g:GROUP_SIZE * (g + 1)] for g in range(N_GROUPS)]
    gscore = []
    for s in slabs:
        m1, first = _first_index_of_max([s], [member], GROUP_SIZE)
        m2 = jnp.max(jnp.where(member == first, -jnp.inf, s), axis=0, keepdims=True)
        gscore.append(m1 + m2)
    gs = jnp.concatenate(gscore, axis=0)
    gidx = lax.broadcasted_iota(jnp.int32, (N_GROUPS, tm), 0)
    gsel = jnp.zeros((N_GROUPS, tm), F32)
    for _ in range(TOPK_GROUPS):
        _, first = _first_index_of_max([gs], [gidx], N_GROUPS)
        pick = gidx == first
        gsel = jnp.where(pick, 1.0, gsel)
        gs = jnp.where(pick, -jnp.inf, gs)
    cur = [jnp.where(gsel[g:g + 1] > 0.0, slabs[g], -jnp.inf) for g in range(N_GROUPS)]
    eidx = [member + GROUP_SIZE * g for g in range(N_GROUPS)]
    sel = [jnp.zeros((GROUP_SIZE, tm), F32) for _ in range(N_GROUPS)]
    picks = []
    for _ in range(TOP_K):
        _, first = _first_index_of_max(cur, eidx, N_EXPERTS)
        pick = [eidx[g] == first for g in range(N_GROUPS)]
        picks.append((first, pick))
        for g in range(N_GROUPS):
            sel[g] = jnp.where(pick[g], 1.0, sel[g])
            cur[g] = jnp.where(pick[g], -jnp.inf, cur[g])
    w = [jnp.where(sel[g] > 0.0, scores[GROUP_SIZE * g:GROUP_SIZE * (g + 1)], 0.0) for g in range(N_GROUPS)]
    total = functools.reduce(lambda u, v: u + v, [jnp.sum(x, axis=0, keepdims=True) for x in w])
    w = [x / total * ROUTED_SCALE for x in w]

    sel_all = jnp.concatenate(sel, axis=0)
    incl = _dot(sel_all.astype(BF16), tri_ref[...])
    base = base_ref[:, 0:1]
    rank_all = incl - sel_all + base
    rank = [rank_all[GROUP_SIZE * g:GROUP_SIZE * (g + 1)] for g in range(N_GROUPS)]
    cnt = base + jnp.sum(sel_all, axis=1, keepdims=True)
    base_ref[...] = jnp.broadcast_to(cnt, base_ref.shape)
    cnt_ref[...] = jnp.broadcast_to(cnt, cnt_ref.shape)

    def picked(vals, pick):
        parts = [jnp.sum(jnp.where(p, v, 0.0), axis=0, keepdims=True) for p, v in zip(pick, vals)]
        return functools.reduce(lambda u, v: u + v, parts)

    eid_ref[...] = jnp.concatenate([first for first, _ in picks], axis=0)
    rank_ref[...] = jnp.concatenate([picked(rank, pick) for _, pick in picks], axis=0).astype(jnp.int32)
    w_rows = [picked(w, pick) for _, pick in picks] + [jnp.zeros((LANES - TOP_K, tm), F32)]
    wtok_ref[...] = jnp.concatenate(w_rows, axis=0).T


def _router(h2, wr_t, bias_col):
    t = h2.shape[0]
    tm = PROJ_TM
    tri = jnp.asarray(np.triu(np.ones((tm, tm), np.float32)), BF16)
    slots = lambda dt: jax.ShapeDtypeStruct((TOP_K, t), dt)
    return pl.pallas_call(
        _router_kernel,
        out_shape=[slots(jnp.int32), slots(jnp.int32), jax.ShapeDtypeStruct((t, LANES), F32),
                   jax.ShapeDtypeStruct((N_EXPERTS, LANES), F32)],
        grid=(t // tm,),
        in_specs=[pl.BlockSpec((tm, PACKED_W), lambda i: (i, 0)),
                  _full((N_EXPERTS, D_MODEL)), _full((N_EXPERTS, 1)), _full((tm, tm))],
        out_specs=[pl.BlockSpec((TOP_K, tm), lambda i: (0, i)), pl.BlockSpec((TOP_K, tm), lambda i: (0, i)),
                   pl.BlockSpec((tm, LANES), lambda i: (i, 0)), _full((N_EXPERTS, LANES))],
        scratch_shapes=[pltpu.VMEM((N_EXPERTS, LANES), F32)],
        compiler_params=_params("arbitrary"),
        name="router",
    )(h2, wr_t, bias_col, tri)


def _moe_tile(t):
    mean_rows = t * TOP_K // N_EXPERTS
    return int(min(max(pl.next_power_of_2(mean_rows // 2), MOE_TILE_MIN), MOE_TILE_MAX))


def _moe_tiles(t):
    return (t * TOP_K) // _moe_tile(t) + N_EXPERTS


def _moe_plan(eid, rank, cnt, t):
    tile = _moe_tile(t)
    counts = cnt[:, 0].astype(jnp.int32)
    tiles = jnp.maximum((counts + tile - 1) // tile, 1)
    ends = jnp.cumsum(tiles)
    starts = ends - tiles
    experts = jnp.arange(N_EXPERTS, dtype=jnp.int32)
    pos = rank + jnp.sum(jnp.where(eid[:, :, None] == experts, starts * tile, 0), axis=-1)
    tile_ids = jnp.arange(_moe_tiles(t), dtype=jnp.int32)
    owner = tile_ids[:, None] >= ends[None, :]
    tile_expert = jnp.minimum(jnp.sum(owner, axis=-1), N_EXPERTS - 1)
    is_owner = tile_expert[:, None] == experts[None, :]
    start_of = jnp.sum(jnp.where(is_owner, starts, 0), axis=-1)
    count_of = jnp.sum(jnp.where(is_owner, counts, 0), axis=-1)
    real = jnp.clip(count_of - (tile_ids - start_of) * tile, 0, tile)
    real = jnp.where(tile_ids < ends[-1], real, 0)
    return pos.astype(jnp.int32), tile_expert.astype(jnp.int32), real.astype(jnp.int32)


def _sc_workers():
    info = plsc.get_sparse_core_info()
    return info.num_cores, info.num_subcores


def _sc_mesh():
    return plsc.VectorSubcoreMesh(core_axis_name="core", subcore_axis_name="subcore")


def _sc_worker_id(n_cores):
    return lax.axis_index("subcore") * n_cores + lax.axis_index("core")


def _dispatch(h2, pos):
    t = h2.shape[0]
    n_cores, n_sub = _sc_workers()
    ch = SC_ROWS
    per_worker = t // (n_cores * n_sub)
    n_chunks = per_worker // ch
    pos_chunks = pos.reshape(TOP_K, t // ch, ch).transpose(1, 0, 2)

    @functools.partial(
        pl.kernel, mesh=_sc_mesh(),
        out_type=jax.ShapeDtypeStruct((_moe_tiles(t) * _moe_tile(t), PACKED_W), jnp.int32),
        scratch_types=[pltpu.VMEM((TOP_K, ch), jnp.int32), pltpu.VMEM((ch, PACKED_W), jnp.int32),
                       pltpu.SemaphoreType.DMA],
        name="moe_dispatch",
    )
    def run(h_hbm, pos_hbm, xs_hbm, idx_ref, rows_ref, sem):
        first = _sc_worker_id(n_cores) * n_chunks

        @pl.loop(0, n_chunks)
        def _(j):
            c = first + j
            pltpu.sync_copy(pos_hbm.at[c], idx_ref)
            pltpu.sync_copy(h_hbm.at[pl.ds(pl.multiple_of(c * ch, ch), ch)], rows_ref)
            copies = [pltpu.async_copy(rows_ref, xs_hbm.at[idx_ref.at[k]], sem) for k in range(TOP_K)]
            for cp in copies:
                cp.wait()

    return run(h2, pos_chunks)


def _gather_rows(ys, idx):
    n = idx.shape[0]
    n_cores, n_sub = _sc_workers()
    ch = SC_ROWS
    per_worker = n // (n_cores * n_sub)
    n_chunks = per_worker // ch

    @functools.partial(
        pl.kernel, mesh=_sc_mesh(),
        out_type=jax.ShapeDtypeStruct((n, PACKED_W), jnp.int32),
        scratch_types=[pltpu.VMEM((ch,), jnp.int32), pltpu.VMEM((ch, PACKED_W), jnp.int32),
                       pltpu.SemaphoreType.DMA],
        name="moe_gather",
    )
    def run(ys_hbm, idx_hbm, out_hbm, idx_ref, rows_ref, sem):
        first = _sc_worker_id(n_cores) * per_worker

        @pl.loop(0, n_chunks)
        def _(j):
            off = pl.multiple_of(first + j * ch, ch)
            pltpu.sync_copy(idx_hbm.at[pl.ds(off, ch)], idx_ref)
            pltpu.async_copy(ys_hbm.at[idx_ref], rows_ref, sem).wait()
            pltpu.sync_copy(rows_ref, out_hbm.at[pl.ds(off, ch)])

    return run(ys, idx)


def _ffn_kernel(te_ref, real_ref, xs_ref, weg_ref, weu_ref, wed_ref, ys_ref):
    i = pl.program_id(0)
    real = real_ref[i]

    @pl.when(real > 0)
    def _():
        row = lax.broadcasted_iota(jnp.int32, (xs_ref.shape[0], 1), 0)
        x = _unpack_bf16_pairs(jnp.where(row < real, xs_ref[...], 0)).astype(BF16)
        act = _silu(_dot(x, weg_ref[0])) * _dot(x, weu_ref[0])
        ys_ref[...] = _pack_bf16_pairs(_dot(act.astype(BF16), wed_ref[0]))

    @pl.when(real == 0)
    def _():
        ys_ref[...] = jnp.zeros_like(ys_ref)


def _ffn(xs, tile_expert, tile_real, W):
    tile = xs.shape[0] // tile_expert.shape[0]
    wspec = lambda shape: pl.BlockSpec((1,) + shape, lambda i, te, tr: (te[i], 0, 0))
    return pl.pallas_call(
        _ffn_kernel,
        out_shape=jax.ShapeDtypeStruct(xs.shape, jnp.int32),
        grid_spec=pltpu.PrefetchScalarGridSpec(
            num_scalar_prefetch=2,
            grid=(tile_expert.shape[0],),
            in_specs=[pl.BlockSpec((tile, PACKED_W), lambda i, te, tr: (i, 0)),
                      wspec((D_MODEL, EXPERT_FF)), wspec((D_MODEL, EXPERT_FF)), wspec((EXPERT_FF, D_MODEL))],
            out_specs=pl.BlockSpec((tile, PACKED_W), lambda i, te, tr: (i, 0))),
        compiler_params=_params("parallel"),
        name="moe_ffn",
    )(tile_expert, tile_real, xs, W["weg"], W["weu"], W["wed"])


def _combine_kernel(y_ref, h_ref, wtok_ref, x1_ref, mod_ref, wsg_ref, wsu_ref, wsd_ref,
                    g_ref, b_ref, out_ref):
    wt = wtok_ref[...]
    routed = None
    for k in range(TOP_K):
        part = wt[:, k:k + 1] * _unpack_bf16_pairs(y_ref[k])
        routed = part if routed is None else routed + part
    h = _unpack_bf16_pairs(h_ref[...]).astype(BF16)
    shared = _dot((_silu(_dot(h, wsg_ref[...])) * _dot(h, wsu_ref[...])).astype(BF16), wsd_ref[...])
    m = mod_ref[0]
    y2 = DN_ALPHA * x1_ref[...] + m[5:6] * (routed + shared)
    out_ref[...] = _layer_norm(y2, g_ref[...], b_ref[...])


def _combine(ytok, h2, wtok, x1, mod, W, *, latent):
    t = h2.shape[0]
    ct = COMBINE_TM
    steps_per_batch = (DEC_SEQ if latent else t) // ct
    row = lambda w: pl.BlockSpec((ct, w), lambda i: (i, 0))
    return pl.pallas_call(
        _combine_kernel,
        out_shape=jax.ShapeDtypeStruct((t, D_MODEL), F32),
        grid=(t // ct,),
        in_specs=[pl.BlockSpec((TOP_K, ct, PACKED_W), lambda i: (0, i, 0)),
                  row(PACKED_W), row(LANES), row(D_MODEL),
                  pl.BlockSpec((1, 6, D_MODEL), lambda i: (i // steps_per_batch, 0, 0)),
                  _full((D_MODEL, SHARED_FF)), _full((D_MODEL, SHARED_FF)), _full((SHARED_FF, D_MODEL)),
                  _full((1, D_MODEL)), _full((1, D_MODEL))],
        out_specs=row(D_MODEL),
        compiler_params=_params("parallel"),
        name="moe_combine",
    )(ytok, h2, wtok, x1, mod, W["wsg"], W["wsu"], W["wsd"], W["ln2_g"], W["ln2_b"])


def _moe(h2, x1, mod, W, *, latent):
    t = h2.shape[0]
    eid, rank, wtok, cnt = _router(h2, W["wr_t"], W["rbias"])
    pos, tile_expert, tile_real = _moe_plan(eid, rank, cnt, t)
    xs = _dispatch(h2, pos)
    ys = _ffn(xs, tile_expert, tile_real, W)
    ytok = _gather_rows(ys, pos.reshape(TOP_K * t)).reshape(TOP_K, t, PACKED_W)
    return _combine(ytok, h2, wtok, x1, mod, W, latent=latent)


def _rope_partner(n_blocks, block):
    half = block // 2
    i = np.arange(n_blocks * block)
    return np.where((i % block) < half, i + half, i - half)


_QB_NOPE = np.concatenate([np.arange(MLA_NOPE) + (MLA_NOPE + MLA_ROPE) * h for h in range(MLA_HEADS)])
_QB_ROPE = np.concatenate([np.arange(MLA_ROPE) + (MLA_NOPE + MLA_ROPE) * h + MLA_NOPE for h in range(MLA_HEADS)])
_KVB_NOPE = np.concatenate([np.arange(MLA_NOPE) + (MLA_NOPE + MLA_V) * h for h in range(MLA_HEADS)])
_KVB_V = np.concatenate([np.arange(MLA_V) + (MLA_NOPE + MLA_V) * h + MLA_NOPE for h in range(MLA_HEADS)])
_DIFF_PARTNER = _rope_partner(2 * DIFF_W // 32, 32)
_MLA_PARTNER = _rope_partner(MLA_HEADS * MLA_ROPE // 16, 16)
_KR_PARTNER = _rope_partner(MLA_ROPE // 16, 16)


def _layer_weights(l, w_in, mla_qa_g, mla_wq_b, mla_kva_g, mla_wkv_b, w_branch_diff, w_branch_na,
                   w_branch_mla, w_out, ln1_g, ln1_b, ln2_g, ln2_b, w_router, router_bias,
                   w_exp_gate, w_exp_up, w_exp_down, w_sh_gate, w_sh_up, w_sh_down):
    win = w_in[l].astype(BF16)
    wa = win[:, :3072]
    qa_kva = win[:, 3072:3712]
    kr = win[:, 3712:3744]
    kr4 = jnp.tile(kr, (1, LANES // MLA_ROPE))
    krp4 = jnp.tile(kr[:, _KR_PARTNER], (1, LANES // MLA_ROPE))
    wqb = mla_wq_b[l].astype(BF16)
    wq_rope = wqb[:, _QB_ROPE]
    wkvb = mla_wkv_b[l].astype(BF16)
    row = lambda v: v[l].reshape(1, -1).astype(F32)
    return {
        "wa": wa,
        "wp": wa[:, :2 * DIFF_W][:, _DIFF_PARTNER],
        "wm_ctx": jnp.concatenate([qa_kva, kr4], axis=1),
        "wm_lat": jnp.concatenate([qa_kva, kr4, krp4], axis=1),
        "qag": row(mla_qa_g), "kvag": row(mla_kva_g),
        "wqb": jnp.concatenate([wqb[:, _QB_NOPE], wq_rope], axis=1),
        "wqp": wq_rope[:, _MLA_PARTNER],
        "wkvb": jnp.concatenate([wkvb[:, _KVB_NOPE], wkvb[:, _KVB_V]], axis=1),
        "wg": win[:, 3744:],
        "wbd": w_branch_diff[l].astype(BF16), "wbn": w_branch_na[l].astype(BF16),
        "wbm": w_branch_mla[l].astype(BF16), "wo": w_out[l].astype(BF16),
        "ln1_g": row(ln1_g), "ln1_b": row(ln1_b), "ln2_g": row(ln2_g), "ln2_b": row(ln2_b),
        "wr_t": w_router[l].T.astype(BF16), "rbias": router_bias[l].reshape(N_EXPERTS, 1).astype(F32),
        "weg": w_exp_gate[l].astype(BF16), "weu": w_exp_up[l].astype(BF16), "wed": w_exp_down[l].astype(BF16),
        "wsg": w_sh_gate[l].astype(BF16), "wsu": w_sh_up[l].astype(BF16), "wsd": w_sh_down[l].astype(BF16),
    }


def _rope_tables():
    t = jnp.arange(DEC_SEQ)
    pos = [(t // GRID_W).astype(F32), (t % GRID_W).astype(F32)]

    def table(block):
        half = block // 4
        inv = ROPE_BASE ** (-jnp.arange(half, dtype=F32) / half)
        cos, sin = [], []
        for p in pos:
            ang = p[:, None] * inv[None, :]
            cos += [jnp.cos(ang), jnp.cos(ang)]
            sin += [-jnp.sin(ang), jnp.sin(ang)]
        reps = LANES // block
        return (jnp.tile(jnp.concatenate(cos, axis=1), (1, reps)),
                jnp.tile(jnp.concatenate(sin, axis=1), (1, reps)))

    cd, sd = table(DIFF_DH)
    cm, sm = table(MLA_ROPE)
    return {"cd": cd, "sd": sd, "cm": cm, "sm": sm}


def kernel(x_prompt, x_sample, cache_diff_k, cache_diff_v, cache_na_k, cache_na_v, cache_mla_ckv, cache_mla_krope, c, c_ctx, w_mod, b_mod, w_in, diff_lambda, diff_subln_g, na_rpb, mla_qa_g, mla_wq_b, mla_kva_g, mla_wkv_b, w_branch_diff, w_branch_na, w_branch_mla, w_out, ln1_g, ln1_b, ln2_g, ln2_b, w_router, router_bias, w_exp_gate, w_exp_up, w_exp_down, w_sh_gate, w_sh_up, w_sh_down):
    t_ctx = BATCH * SEQ
    t_lat = DEC_BATCH * DEC_SEQ
    cvec = jnp.concatenate([c, c_ctx[None, :], jnp.zeros((MOD_ROWS - DEC_BATCH - 1, D_MODEL), F32)], axis=0)
    mods = _modulation(cvec, w_mod, b_mod).reshape(DEPTH, MOD_ROWS, 6, D_MODEL)
    rope = _rope_tables()
    ck_d = cache_diff_k.reshape(DEC_BATCH, DEPTH, PAST_LEN, DIFF_W)
    cv_d = cache_diff_v.reshape(DEC_BATCH, DEPTH, PAST_LEN, DIFF_W)
    ck_n = cache_na_k.reshape(DEC_BATCH, DEPTH, PAST_LEN, NA_W)
    cv_n = cache_na_v.reshape(DEC_BATCH, DEPTH, PAST_LEN, NA_W)

    xc = x_prompt.reshape(t_ctx, D_MODEL)
    xl = x_sample.reshape(t_lat, D_MODEL)
    states = []
    for l in range(DEPTH):
        W = _layer_weights(l, w_in, mla_qa_g, mla_wq_b, mla_kva_g, mla_wkv_b, w_branch_diff, w_branch_na,
                           w_branch_mla, w_out, ln1_g, ln1_b, ln2_g, ln2_b, w_router, router_bias,
                           w_exp_gate, w_exp_up, w_exp_down, w_sh_gate, w_sh_up, w_sh_down)
        lam = diff_lambda[l].astype(F32)
        subln = diff_subln_g[l].reshape(1, LANES).astype(F32)
        mod_c = mods[l, DEC_BATCH:DEC_BATCH + 1]
        mod_l = mods[l, :DEC_BATCH]

        Wc = dict(W, wm=W["wm_ctx"])
        names = ["dq", "nq", "mqn", "mqr", "kr4", "kn", "vn", "dk", "dv", "nk", "nv", "ckv", "kr"]
        pc = dict(zip(names, _project(xc, mod_c, Wc, None, latent=False)))
        od, on, om = _attn_ctx(pc, lam, subln, l)
        x1, h2 = _merge(xc, mod_c, od, on, om, W, latent=False)
        xc = _moe(h2, x1, mod_c, W, latent=False)
        states.append(pc)

        Wl = dict(W, wm=W["wm_lat"])
        names = ["dq", "dk", "dv", "nq", "nk", "nv", "mqn", "mqr", "kr4", "kn", "vn"]
        pll = dict(zip(names, _project(xl, mod_l, Wl, rope, latent=True)))
        od = _attn_diff_lat(pll, ck_d, cv_d, lam, subln, l)
        on = _attn_na_lat(pll, ck_n, cv_n, _na_bias(na_rpb[l].astype(F32)), l)
        kc, vc, krc = _mla_cache(cache_mla_ckv, cache_mla_krope, W["wkvb"], l)
        om = _attn_mla_lat(pll, kc, vc, krc)
        x1, h2 = _merge(xl, mod_l, od, on, om, W, latent=True)
        xl = _moe(h2, x1, mod_l, W, latent=True)

    def stack(name, shape):
        width = states[0][name].shape[1]
        both = jnp.stack([s[name].reshape(BATCH, SEQ, width) for s in states], axis=1)
        return both.reshape((BATCH, DEPTH, SEQ) + shape)

    return (xc.reshape(BATCH, SEQ, D_MODEL), xl.reshape(DEC_BATCH, DEC_SEQ, D_MODEL),
            stack("dk", (DIFF_HEADS, 2 * DIFF_DH)), stack("dv", (DIFF_HEADS, 2 * DIFF_DH)),
            stack("nk", (NA_HEADS, NA_DH)), stack("nv", (NA_HEADS, NA_DH)),
            stack("ckv", (MLA_KV_LORA,)), stack("kr", (MLA_ROPE,)))
```

```python
import functools

import numpy as np
import jax
import jax.numpy as jnp
from jax import lax
from jax.experimental import pallas as pl
from jax.experimental.pallas import tpu as pltpu
from jax.experimental.pallas import tpu_sc as plsc

D_MODEL = 1024
BATCH = 32
SEQ = 256
DEPTH = 2
DEC_BATCH = 8
DEC_SEQ = 2048
PAST_LEN = 512
GRID_W = 64
GRID_ROWS = DEC_SEQ // GRID_W
ROPE_BASE = 10000.0
DIFF_HEADS = 4
DIFF_DH = 64
DIFF_W = 512
NA_HEADS = 8
NA_DH = 64
NA_W = 512
NA_WIN_ROWS = 8
NA_WIN_COLS = 16
MLA_HEADS = 8
MLA_Q_LORA = 384
MLA_KV_LORA = 256
MLA_NOPE = 64
MLA_ROPE = 32
MLA_V = 64
MLA_W = 512
N_EXPERTS = 64
N_GROUPS = 8
GROUP_SIZE = N_EXPERTS // N_GROUPS
TOPK_GROUPS = 4
TOP_K = 8
EXPERT_FF = 256
SHARED_FF = 256
ROUTED_SCALE = 2.5
DN_ALPHA = (2 * DEPTH) ** 0.25
LN_EPS = 1e-5
RMS_EPS = 1e-6

F32 = jnp.float32
BF16 = jnp.bfloat16

LANES = 128
VMEM_LIMIT_BYTES = 56 * 1024 * 1024
NEG_BIG = -1e30
LOG2E = 1.4426950408889634

DIFF_SCALE = DIFF_DH ** -0.5
NA_SCALE = NA_DH ** -0.5
MLA_SCALE = (MLA_NOPE + MLA_ROPE) ** -0.5

PROJ_TM = 512
ATT_TQ = 512
NA_TQ = 256
NA_Q_ROWS = NA_TQ // GRID_W
NA_KEY_ROWS = NA_Q_ROWS + NA_WIN_ROWS
NA_KEYS = NA_KEY_ROWS * GRID_W
MOE_TILE_MIN = 128
MOE_TILE_MAX = 1024
SC_ROWS = 128
COMBINE_TM = 512


def _dot(a, b):
    return jnp.dot(a, b, preferred_element_type=F32)


def _dot_nt(a, b):
    return lax.dot_general(a, b, (((1,), (1,)), ((), ())), preferred_element_type=F32)


def _sigmoid(x):
    return 1.0 / (1.0 + jnp.exp(-x))


def _silu(x):
    return x * _sigmoid(x)


def _params(*sem):
    return pltpu.CompilerParams(dimension_semantics=sem, vmem_limit_bytes=VMEM_LIMIT_BYTES)


def _full(shape):
    n = len(shape)
    return pl.BlockSpec(shape, lambda *_: (0,) * n)


def _layer_norm(y, g, b):
    mu = jnp.mean(y, axis=-1, keepdims=True)
    yc = y - mu
    var = jnp.mean(yc * yc, axis=-1, keepdims=True)
    return yc * lax.rsqrt(var + LN_EPS) * g + b


def _rms(x, g):
    return x * lax.rsqrt(jnp.mean(x * x, axis=-1, keepdims=True) + RMS_EPS) * g


HIGH_HALF = -65536
PACKED_W = D_MODEL // 2


def _pack_bf16_pairs(x):
    n = x.shape[1] // 2
    bits = lax.bitcast_convert_type(x.astype(BF16).astype(F32), jnp.int32)
    return lax.shift_right_logical(bits[:, :n], 16) | (bits[:, n:] & HIGH_HALF)


def _unpack_bf16_pairs(w):
    lo = lax.bitcast_convert_type(lax.shift_left(w, 16), F32)
    hi = lax.bitcast_convert_type(w & HIGH_HALF, F32)
    return jnp.concatenate([lo, hi], axis=1)


MOD_ROWS = 16
MOD_TN = 1536


def _mod_kernel(c_ref, w_ref, b_ref, o_ref):
    s = _silu(c_ref[...]).astype(BF16)
    o_ref[0] = _dot(s, w_ref[0].astype(BF16)) + b_ref[0]


def _modulation(cvec, w_mod, b_mod):
    n = 6 * D_MODEL
    return pl.pallas_call(
        _mod_kernel,
        out_shape=jax.ShapeDtypeStruct((DEPTH, MOD_ROWS, n), F32),
        grid=(DEPTH, n // MOD_TN),
        in_specs=[
            pl.BlockSpec((MOD_ROWS, D_MODEL), lambda l, j: (0, 0)),
            pl.BlockSpec((1, D_MODEL, MOD_TN), lambda l, j: (l, 0, j)),
            pl.BlockSpec((1, 1, MOD_TN), lambda l, j: (l, 0, j)),
        ],
        out_specs=pl.BlockSpec((1, MOD_ROWS, MOD_TN), lambda l, j: (l, 0, j)),
        compiler_params=_params("parallel", "parallel"),
        name="modulation",
    )(cvec, w_mod, b_mod.reshape(DEPTH, 1, n))


def _proj_common(x_ref, mod_ref, wa_ref, wm_ref, qag_ref, kvag_ref, wqb_ref, wkvb_ref):
    m = mod_ref[0]
    h = (x_ref[...] * (1.0 + m[1:2]) + m[0:1]).astype(BF16)
    a = _dot(h, wa_ref[...])
    mm = _dot(h, wm_ref[...])
    qan = _rms(mm[:, :MLA_Q_LORA], qag_ref[...]).astype(BF16)
    mq = _dot(qan, wqb_ref[...])
    ckv = _rms(mm[:, MLA_Q_LORA:MLA_Q_LORA + MLA_KV_LORA], kvag_ref[...])
    kv = _dot(ckv.astype(BF16), wkvb_ref[...])
    return h, a, mm, qan, mq, ckv, kv


def _proj_ctx_kernel(x_ref, mod_ref, wa_ref, wm_ref, qag_ref, kvag_ref, wqb_ref, wkvb_ref,
                     dq_ref, nq_ref, mqn_ref, mqr_ref, kr4_ref, kn_ref, vn_ref,
                     dk_ref, dv_ref, nk_ref, nv_ref, ckv_ref, kr_ref):
    _, a, mm, _, mq, ckv, kv = _proj_common(x_ref, mod_ref, wa_ref, wm_ref, qag_ref, kvag_ref,
                                            wqb_ref, wkvb_ref)
    dq_ref[...] = (a[:, 0:512] * DIFF_SCALE).astype(BF16)
    dk_ref[...] = a[:, 512:1024]
    dv_ref[...] = a[:, 1024:1536]
    nq_ref[...] = (a[:, 1536:2048] * NA_SCALE).astype(BF16)
    nk_ref[...] = a[:, 2048:2560]
    nv_ref[...] = a[:, 2560:3072]
    mqn_ref[...] = (mq[:, :512] * MLA_SCALE).astype(BF16)
    mqr_ref[...] = (mq[:, 512:768] * MLA_SCALE).astype(BF16)
    kr4 = mm[:, 640:768]
    kr4_ref[...] = kr4.astype(BF16)
    kr_ref[...] = kr4[:, :MLA_ROPE]
    ckv_ref[...] = ckv
    kn_ref[...] = kv[:, :512].astype(BF16)
    vn_ref[...] = kv[:, 512:].astype(BF16)


def _proj_lat_kernel(x_ref, mod_ref, wa_ref, wm_ref, qag_ref, kvag_ref, wqb_ref, wkvb_ref,
                     wp_ref, wqp_ref, cd_ref, sd_ref, cm_ref, sm_ref,
                     dq_ref, dk_ref, dv_ref, nq_ref, nk_ref, nv_ref,
                     mqn_ref, mqr_ref, kr4_ref, kn_ref, vn_ref):
    h, a, mm, qan, mq, _, kv = _proj_common(x_ref, mod_ref, wa_ref, wm_ref, qag_ref, kvag_ref,
                                            wqb_ref, wkvb_ref)
    ap = _dot(h, wp_ref[...])
    mqp = _dot(qan, wqp_ref[...])
    cd = cd_ref[...]
    sd = sd_ref[...]
    cm = cm_ref[...]
    sm = sm_ref[...]
    for j in range(DIFF_W // LANES):
        lo, hi = LANES * j, LANES * (j + 1)
        dq_ref[:, lo:hi] = ((a[:, lo:hi] * cd + ap[:, lo:hi] * sd) * (DIFF_SCALE * LOG2E)).astype(BF16)
        dk_ref[:, lo:hi] = (a[:, 512 + lo:512 + hi] * cd + ap[:, 512 + lo:512 + hi] * sd).astype(BF16)
    dv_ref[...] = a[:, 1024:1536].astype(BF16)
    nq_ref[...] = (a[:, 1536:2048] * (NA_SCALE * LOG2E)).astype(BF16)
    nk_ref[...] = a[:, 2048:2560].astype(BF16)
    nv_ref[...] = a[:, 2560:3072].astype(BF16)
    mqn_ref[...] = (mq[:, :512] * (MLA_SCALE * LOG2E)).astype(BF16)
    for j in range(2):
        lo, hi = LANES * j, LANES * (j + 1)
        mqr_ref[:, lo:hi] = ((mq[:, 512 + lo:512 + hi] * cm + mqp[:, lo:hi] * sm)
                             * (MLA_SCALE * LOG2E)).astype(BF16)
    kr4_ref[...] = (mm[:, 640:768] * cm + mm[:, 768:896] * sm).astype(BF16)
    kn_ref[...] = kv[:, :512].astype(BF16)
    vn_ref[...] = kv[:, 512:].astype(BF16)


def _project(x, mod, W, rope, *, latent):
    t = x.shape[0]
    tm = PROJ_TM
    tokens_per_batch = DEC_SEQ if latent else t
    steps_per_batch = tokens_per_batch // tm
    row = lambda w: pl.BlockSpec((tm, w), lambda i: (i, 0))
    common_in = [
        row(D_MODEL),
        pl.BlockSpec((1, 6, D_MODEL), lambda i: (i // steps_per_batch, 0, 0)),
        _full(W["wa"].shape), _full(W["wm"].shape), _full((1, MLA_Q_LORA)), _full((1, MLA_KV_LORA)),
        _full(W["wqb"].shape), _full(W["wkvb"].shape),
    ]
    common_args = [x, mod, W["wa"], W["wm"], W["qag"], W["kvag"], W["wqb"], W["wkvb"]]
    bf = lambda w: jax.ShapeDtypeStruct((t, w), BF16)
    f32 = lambda w: jax.ShapeDtypeStruct((t, w), F32)
    if latent:
        tab = pl.BlockSpec((tm, LANES), lambda i: (i % steps_per_batch, 0))
        widths = [512, 512, 512, 512, 512, 512, 512, 256, 128, 512, 512]
        return pl.pallas_call(
            _proj_lat_kernel,
            out_shape=[bf(w) for w in widths],
            grid=(t // tm,),
            in_specs=common_in + [_full(W["wp"].shape), _full(W["wqp"].shape), tab, tab, tab, tab],
            out_specs=[row(w) for w in widths],
            compiler_params=_params("parallel"),
            name="proj_lat",
        )(*common_args, W["wp"], W["wqp"], rope["cd"], rope["sd"], rope["cm"], rope["sm"])
    bf_w = [512, 512, 512, 256, 128, 512, 512]
    f32_w = [512, 512, 512, 512, 256, 32]
    return pl.pallas_call(
        _proj_ctx_kernel,
        out_shape=[bf(w) for w in bf_w] + [f32(w) for w in f32_w],
        grid=(t // tm,),
        in_specs=common_in,
        out_specs=[row(w) for w in bf_w + f32_w],
        compiler_params=_params("parallel"),
        name="proj_ctx",
    )(*common_args)


def _lane_iota():
    return lax.broadcasted_iota(jnp.int32, (1, LANES), 1)


def _softmax_parts(parts, exp):
    m = functools.reduce(jnp.maximum, [jnp.max(s, axis=-1, keepdims=True) for s in parts])
    es = [exp(s - m) for s in parts]
    l = functools.reduce(lambda u, v: u + v, [jnp.sum(e, axis=-1, keepdims=True) for e in es])
    return es, l


def _diff_lambda(lam_ref, layer):
    lp = lam_ref[...]
    lam_init = 0.8 - 0.6 * float(np.exp(-0.3 * layer))
    s1 = jnp.sum(lp[0:1] * lp[1:2], axis=-1, keepdims=True)
    s2 = jnp.sum(lp[2:3] * lp[3:4], axis=-1, keepdims=True)
    return jnp.exp(s1) - jnp.exp(s2) + lam_init, lam_init


def _diff_heads(q_ref, ks, vs, lam_ref, g_ref, o_ref, layer, exp):
    lam, lam_init = _diff_lambda(lam_ref, layer)
    first_map = _lane_iota() < DIFF_DH
    g = g_ref[...]
    for h in range(DIFF_HEADS):
        hs = slice(LANES * h, LANES * (h + 1))
        q = q_ref[:, hs]
        q1 = jnp.where(first_map, q, jnp.zeros_like(q))
        q2 = jnp.where(first_map, jnp.zeros_like(q), q)
        kk = [k[:, hs].astype(BF16) for k in ks]
        e1, l1 = _softmax_parts([_dot_nt(q1, k) for k in kk], exp)
        e2, l2 = _softmax_parts([_dot_nt(q2, k) for k in kk], exp)
        c1 = 1.0 / l1
        c2 = lam / l2
        o = None
        for a1, a2, v in zip(e1, e2, vs):
            part = _dot((a1 * c1 - a2 * c2).astype(BF16), v[:, hs].astype(BF16))
            o = part if o is None else o + part
        o = _rms(o, g) * (1.0 - lam_init)
        o_ref[:, hs] = o.astype(BF16)


def _pair_heads(q_of, k_of, v_of, bias_of, o_ref, n_pairs, exp, stack=False):
    first = _lane_iota() < 64
    for j in range(n_pairs):
        ps = slice(LANES * j, LANES * (j + 1))
        ks = k_of(j)
        vs = v_of(j)
        qs = [q_of(j, 0), q_of(j, 1)]
        if stack:
            qs = [jnp.concatenate(qs, axis=0)]
        outs = []
        for q in qs:
            ss = [_dot_nt(q, k) for k in ks]
            ss = [s if b is None else s + b for s, b in zip(ss, bias_of(j))]
            es, l = _softmax_parts(ss, exp)
            o = None
            for e, v in zip(es, vs):
                part = _dot(e.astype(BF16), v)
                o = part if o is None else o + part
            outs.append(o / l)
        if stack:
            n = outs[0].shape[0] // 2
            outs = [outs[0][:n], outs[0][n:]]
        o_ref[:, ps] = jnp.where(first, outs[0], outs[1]).astype(BF16)


def _na_q(q_ref):
    first = _lane_iota() < NA_DH

    def q_of(j, hh):
        q = q_ref[:, LANES * j:LANES * (j + 1)]
        keep = first if hh == 0 else jnp.logical_not(first)
        return jnp.where(keep, q, jnp.zeros_like(q))
    return q_of


def _mla_q(qn_ref, qr_ref):
    lane = _lane_iota()
    first = lane < MLA_NOPE

    def q_of(j, hh):
        h = 2 * j + hh
        qn = qn_ref[:, LANES * j:LANES * (j + 1)]
        keep = first if hh == 0 else jnp.logical_not(first)
        qn = jnp.where(keep, qn, jnp.zeros_like(qn))
        qr = qr_ref[:, LANES * (h // 4):LANES * (h // 4 + 1)]
        qr = jnp.where((lane // MLA_ROPE) == (h % 4), qr, jnp.zeros_like(qr))
        return jnp.concatenate([qn, qr], axis=1)
    return q_of


def _attn_ctx_kernel(layer, dq_ref, dk_ref, dv_ref, nq_ref, nk_ref, nv_ref,
                     mqn_ref, mqr_ref, kn_ref, kr4_ref, vn_ref, lam_ref, g_ref,
                     od_ref, on_ref, om_ref):
    _diff_heads(dq_ref, [dk_ref], [dv_ref], lam_ref, g_ref, od_ref, layer, jnp.exp)
    none = lambda j: [None]
    pair = lambda j: slice(LANES * j, LANES * (j + 1))
    _pair_heads(_na_q(nq_ref),
                lambda j: [nk_ref[:, pair(j)].astype(BF16)],
                lambda j: [nv_ref[:, pair(j)].astype(BF16)],
                none, on_ref, NA_HEADS // 2, jnp.exp, stack=True)
    kr4 = kr4_ref[...]
    _pair_heads(_mla_q(mqn_ref, mqr_ref),
                lambda j: [jnp.concatenate([kn_ref[:, pair(j)], kr4], axis=1)],
                lambda j: [vn_ref[:, pair(j)]],
                none, om_ref, MLA_HEADS // 2, jnp.exp, stack=True)


def _attn_ctx(p, lam, g, layer):
    t = p["dq"].shape[0]
    row = lambda w: pl.BlockSpec((SEQ, w), lambda b: (b, 0))
    names = ["dq", "dk", "dv", "nq", "nk", "nv", "mqn", "mqr", "kn", "kr4", "vn"]
    out = jax.ShapeDtypeStruct((t, 512), BF16)
    return pl.pallas_call(
        functools.partial(_attn_ctx_kernel, layer),
        out_shape=[out, out, out],
        grid=(t // SEQ,),
        in_specs=[row(p[n].shape[1]) for n in names] + [_full((4, DIFF_DH)), _full((1, LANES))],
        out_specs=[row(512)] * 3,
        compiler_params=_params("parallel"),
        name="attn_ctx",
    )(*[p[n] for n in names], lam, g)


def _cache_spec(width, layer):
    return pl.BlockSpec((None, None, PAST_LEN, width), lambda b, q: (b, layer, 0, 0))


def _batch_spec(width):
    return pl.BlockSpec((None, DEC_SEQ, width), lambda b, q: (b, 0, 0))


def _qtile_spec(width, tq):
    steps = DEC_SEQ // tq
    return pl.BlockSpec((tq, width), lambda b, q: (b * steps + q, 0))


def _attn_diff_lat_kernel(layer, q_ref, kc_ref, vc_ref, kn_ref, vn_ref, lam_ref, g_ref, o_ref):
    _diff_heads(q_ref, [kc_ref, kn_ref], [vc_ref, vn_ref], lam_ref, g_ref, o_ref, layer, jnp.exp2)


def _attn_diff_lat(p, cache_k, cache_v, lam, g, layer):
    t = p["dq"].shape[0]
    b3 = lambda a: a.reshape(DEC_BATCH, DEC_SEQ, a.shape[1])
    return pl.pallas_call(
        functools.partial(_attn_diff_lat_kernel, layer),
        out_shape=jax.ShapeDtypeStruct((t, DIFF_W), BF16),
        grid=(DEC_BATCH, DEC_SEQ // ATT_TQ),
        in_specs=[_qtile_spec(DIFF_W, ATT_TQ), _cache_spec(DIFF_W, layer), _cache_spec(DIFF_W, layer),
                  _batch_spec(DIFF_W), _batch_spec(DIFF_W),
                  pl.BlockSpec((4, DIFF_DH), lambda b, q: (0, 0)),
                  pl.BlockSpec((1, LANES), lambda b, q: (0, 0))],
        out_specs=_qtile_spec(DIFF_W, ATT_TQ),
        compiler_params=_params("parallel", "parallel"),
        name="attn_diff_lat",
    )(p["dq"], cache_k, cache_v, b3(p["dk"]), b3(p["dv"]), lam, g)


def _na_key_start(q):
    return jnp.clip(q * NA_Q_ROWS - NA_WIN_ROWS // 2, 0, GRID_ROWS - NA_KEY_ROWS)


def _attn_na_lat_kernel(q_ref, kc_ref, vc_ref, kn_ref, vn_ref, bias_ref, o_ref):
    start = pl.multiple_of(_na_key_start(pl.program_id(1)) * GRID_W, GRID_W)
    kw = kn_ref[pl.ds(start, NA_KEYS), :]
    vw = vn_ref[pl.ds(start, NA_KEYS), :]
    pair = lambda j: slice(LANES * j, LANES * (j + 1))
    _pair_heads(_na_q(q_ref),
                lambda j: [kc_ref[:, pair(j)].astype(BF16), kw[:, pair(j)]],
                lambda j: [vc_ref[:, pair(j)].astype(BF16), vw[:, pair(j)]],
                lambda j: [None, bias_ref[j]],
                o_ref, NA_HEADS // 2, jnp.exp2, stack=True)


def _na_bias_tables():
    n_blocks = DEC_SEQ // NA_TQ
    qr = np.arange(NA_Q_ROWS)
    kr = np.arange(NA_KEY_ROWS)
    row_sel = []
    for qb in range(n_blocks):
        ks = int(np.clip(qb * NA_Q_ROWS - NA_WIN_ROWS // 2, 0, GRID_ROWS - NA_KEY_ROWS))
        r = qb * NA_Q_ROWS + qr
        r0 = np.clip(r - NA_WIN_ROWS // 2, 0, GRID_ROWS - NA_WIN_ROWS)
        krow = ks + kr
        ok = (krow[None, :] >= r0[:, None]) & (krow[None, :] < r0[:, None] + NA_WIN_ROWS)
        off = np.where(ok, krow[None, :] - r[:, None] + NA_WIN_ROWS - 1, 2 * NA_WIN_ROWS - 1)
        sel = off[:, :, None] == np.arange(2 * NA_WIN_ROWS)
        row_sel.append(sel.astype(np.float32))
    kinds, kind_of_block = [], []
    for qb in range(n_blocks):
        for n, other in enumerate(kinds):
            if np.array_equal(row_sel[qb], row_sel[other]):
                kind_of_block.append(n)
                break
        else:
            kind_of_block.append(len(kinds))
            kinds.append(qb)
    c = np.arange(GRID_W)
    c0 = np.clip(c - NA_WIN_COLS // 2, 0, GRID_W - NA_WIN_COLS)
    col_ok = (c[None, :] >= c0[:, None]) & (c[None, :] < c0[:, None] + NA_WIN_COLS)
    coff = np.where(col_ok, c[None, :] - c[:, None] + NA_WIN_COLS - 1, 2 * NA_WIN_COLS - 1)
    col_sel = (coff[:, :, None] == np.arange(2 * NA_WIN_COLS)).astype(np.float32)
    rsel = np.stack([row_sel[qb] for qb in kinds])
    return rsel, col_sel, kind_of_block


_NA_ROW_SEL, _NA_COL_SEL, _NA_KIND_OF_BLOCK = _na_bias_tables()


def _na_bias(rpb):
    hp = lax.Precision.HIGHEST
    table = jnp.pad(rpb * LOG2E, ((0, 0), (0, 1), (0, 1)), constant_values=NEG_BIG)
    cols = jnp.einsum("hij,ckj->hick", table, _NA_COL_SEL, precision=hp)
    b = jnp.einsum("nqri,hick->nhqcrk", _NA_ROW_SEL, cols, precision=hp)
    return b.reshape(_NA_ROW_SEL.shape[0], NA_HEADS // 2, 2 * NA_TQ, NA_KEYS).astype(F32)


def _na_kind(q):
    return (q > 0).astype(jnp.int32) + (q == DEC_SEQ // NA_TQ - 1).astype(jnp.int32)


def _attn_na_lat(p, cache_k, cache_v, bias, layer):
    assert _NA_KIND_OF_BLOCK == [0] + [1] * (DEC_SEQ // NA_TQ - 2) + [2]
    t = p["nq"].shape[0]
    b3 = lambda a: a.reshape(DEC_BATCH, DEC_SEQ, a.shape[1])
    return pl.pallas_call(
        _attn_na_lat_kernel,
        out_shape=jax.ShapeDtypeStruct((t, NA_W), BF16),
        grid=(DEC_BATCH, DEC_SEQ // NA_TQ),
        in_specs=[_qtile_spec(NA_W, NA_TQ), _cache_spec(NA_W, layer), _cache_spec(NA_W, layer),
                  _batch_spec(NA_W), _batch_spec(NA_W),
                  pl.BlockSpec((None, NA_HEADS // 2, 2 * NA_TQ, NA_KEYS),
                               lambda b, q: (_na_kind(q), 0, 0, 0))],
        out_specs=_qtile_spec(NA_W, NA_TQ),
        compiler_params=_params("parallel", "parallel"),
        name="attn_na_lat",
    )(p["nq"], cache_k, cache_v, b3(p["nk"]), b3(p["nv"]), bias)


def _mla_cache_kernel(ckv_ref, kr_ref, wkvb_ref, rep_ref, kc_ref, vc_ref, krc_ref):
    kv = _dot(ckv_ref[...].astype(BF16), wkvb_ref[...])
    kc_ref[...] = kv[:, :512].astype(BF16)
    vc_ref[...] = kv[:, 512:].astype(BF16)
    krc_ref[...] = _dot(kr_ref[...].astype(BF16), rep_ref[...]).astype(BF16)


def _mla_cache(cache_ckv, cache_kr, wkvb, layer):
    rep = jnp.asarray(np.tile(np.eye(MLA_ROPE, dtype=np.float32), (1, LANES // MLA_ROPE)), BF16)
    spec_in = lambda w: pl.BlockSpec((None, None, PAST_LEN, w), lambda b: (b, layer, 0, 0))
    spec_out = lambda w: pl.BlockSpec((None, PAST_LEN, w), lambda b: (b, 0, 0))
    shp = lambda w: jax.ShapeDtypeStruct((DEC_BATCH, PAST_LEN, w), BF16)
    return pl.pallas_call(
        _mla_cache_kernel,
        out_shape=[shp(512), shp(512), shp(LANES)],
        grid=(DEC_BATCH,),
        in_specs=[spec_in(MLA_KV_LORA), spec_in(MLA_ROPE), _full(wkvb.shape), _full(rep.shape)],
        out_specs=[spec_out(512), spec_out(512), spec_out(LANES)],
        compiler_params=_params("parallel"),
        name="mla_cache",
    )(cache_ckv, cache_kr, wkvb, rep)


def _attn_mla_lat_kernel(qn_ref, qr_ref, kc_ref, krc_ref, vc_ref, kn_ref, krn_ref, vn_ref, o_ref):
    pair = lambda j: slice(LANES * j, LANES * (j + 1))
    krc = krc_ref[...]
    krn = krn_ref[...]
    _pair_heads(_mla_q(qn_ref, qr_ref),
                lambda j: [jnp.concatenate([kc_ref[:, pair(j)], krc], axis=1),
                           jnp.concatenate([kn_ref[:, pair(j)], krn], axis=1)],
                lambda j: [vc_ref[:, pair(j)], vn_ref[:, pair(j)]],
                lambda j: [None, None],
                o_ref, MLA_HEADS // 2, jnp.exp2)


def _attn_mla_lat(p, kc, vc, krc):
    t = p["mqn"].shape[0]
    b3 = lambda a: a.reshape(DEC_BATCH, DEC_SEQ, a.shape[1])
    cspec = lambda w: pl.BlockSpec((None, PAST_LEN, w), lambda b, q: (b, 0, 0))
    return pl.pallas_call(
        _attn_mla_lat_kernel,
        out_shape=jax.ShapeDtypeStruct((t, MLA_W), BF16),
        grid=(DEC_BATCH, DEC_SEQ // ATT_TQ),
        in_specs=[_qtile_spec(512, ATT_TQ), _qtile_spec(256, ATT_TQ), cspec(512), cspec(LANES), cspec(512),
                  _batch_spec(512), _batch_spec(LANES), _batch_spec(512)],
        out_specs=_qtile_spec(MLA_W, ATT_TQ),
        compiler_params=_params("parallel", "parallel"),
        name="attn_mla_lat",
    )(p["mqn"], p["mqr"], kc, krc, vc, b3(p["kn"]), b3(p["kr4"]), b3(p["vn"]))


def _merge_kernel(x_ref, mod_ref, od_ref, on_ref, om_ref, wg_ref, wbd_ref, wbn_ref, wbm_ref, wo_ref,
                  g_ref, b_ref, x1_ref, h2_ref):
    x = x_ref[...]
    m = mod_ref[0]
    h = (x * (1.0 + m[1:2]) + m[0:1]).astype(BF16)
    gates = _dot(h, wg_ref[...])
    mix = (_sigmoid(gates[:, 0:1024]) * _dot(od_ref[...], wbd_ref[...])
           + _sigmoid(gates[:, 1024:2048]) * _dot(on_ref[...], wbn_ref[...])
           + _sigmoid(gates[:, 2048:3072]) * _dot(om_ref[...], wbm_ref[...]))
    out = _dot(mix.astype(BF16), wo_ref[...])
    x1 = _layer_norm(DN_ALPHA * x + m[2:3] * out, g_ref[...], b_ref[...])
    x1_ref[...] = x1
    h2_ref[...] = _pack_bf16_pairs(x1 * (1.0 + m[4:5]) + m[3:4])


def _merge(x, mod, od, on, om, W, *, latent):
    t = x.shape[0]
    tm = PROJ_TM
    steps_per_batch = (DEC_SEQ if latent else t) // tm
    row = lambda w: pl.BlockSpec((tm, w), lambda i: (i, 0))
    return pl.pallas_call(
        _merge_kernel,
        out_shape=[jax.ShapeDtypeStruct((t, D_MODEL), F32), jax.ShapeDtypeStruct((t, PACKED_W), jnp.int32)],
        grid=(t // tm,),
        in_specs=[row(D_MODEL),
                  pl.BlockSpec((1, 6, D_MODEL), lambda i: (i // steps_per_batch, 0, 0)),
                  row(512), row(512), row(512),
                  _full(W["wg"].shape), _full(W["wbd"].shape), _full(W["wbn"].shape),
                  _full(W["wbm"].shape), _full(W["wo"].shape),
                  _full((1, D_MODEL)), _full((1, D_MODEL))],
        out_specs=[row(D_MODEL), row(PACKED_W)],
        compiler_params=_params("parallel"),
        name="merge",
    )(x, mod, od, on, om, W["wg"], W["wbd"], W["wbn"], W["wbm"], W["wo"], W["ln1_g"], W["ln1_b"])


def _first_index_of_max(vals, idx, sentinel):
    mx = functools.reduce(jnp.maximum, [jnp.max(v, axis=0, keepdims=True) for v in vals])
    cand = [jnp.min(jnp.where(v == mx, i, sentinel), axis=0, keepdims=True) for v, i in zip(vals, idx)]
    return mx, functools.reduce(jnp.minimum, cand)


def _router_kernel(h_ref, wr_ref, bias_ref, tri_ref, eid_ref, rank_ref, wtok_ref, cnt_ref, base_ref):
    tm = h_ref.shape[0]

    @pl.when(pl.program_id(0) == 0)
    def _():
        base_ref[...] = jnp.zeros_like(base_ref)

    logits = _dot_nt(wr_ref[...], _unpack_bf16_pairs(h_ref[...]).astype(BF16))
    scores = _sigmoid(logits)
    biased = scores + bias_ref[...]
    member = lax.broadcasted_iota(jnp.int32, (GROUP_SIZE, tm), 0)
    slabs = [biased[GROUP_SIZE * g:GROUP_SIZE * (g + 1)] for g in range(N_GROUPS)]
    gscore = []
    for s in slabs:
        m1, first = _first_index_of_max([s], [member], GROUP_SIZE)
        m2 = jnp.max(jnp.where(member == first, -jnp.inf, s), axis=0, keepdims=True)
        gscore.append(m1 + m2)
    gs = jnp.concatenate(gscore, axis=0)
    gidx = lax.broadcasted_iota(jnp.int32, (N_GROUPS, tm), 0)
    gsel = jnp.zeros((N_GROUPS, tm), F32)
    for _ in range(TOPK_GROUPS):
        _, first = _first_index_of_max([gs], [gidx], N_GROUPS)
        pick = gidx == first
        gsel = jnp.where(pick, 1.0, gsel)
        gs = jnp.where(pick, -jnp.inf, gs)
    cur = [jnp.where(gsel[g:g + 1] > 0.0, slabs[g], -jnp.inf) for g in range(N_GROUPS)]
    eidx = [member + GROUP_SIZE * g for g in range(N_GROUPS)]
    sel = [jnp.zeros((GROUP_SIZE, tm), F32) for _ in range(N_GROUPS)]
    picks = []
    for _ in range(TOP_K):
        _, first = _first_index_of_max(cur, eidx, N_EXPERTS)
        pick = [eidx[g] == first for g in range(N_GROUPS)]
        picks.append((first, pick))
        for g in range(N_GROUPS):
            sel[g] = jnp.where(pick[g], 1.0, sel[g])
            cur[g] = jnp.where(pick[g], -jnp.inf, cur[g])
    w = [jnp.where(sel[g] > 0.0, scores[GROUP_SIZE * g:GROUP_SIZE * (g + 1)], 0.0) for g in range(N_GROUPS)]
    total = functools.reduce(lambda u, v: u + v, [jnp.sum(x, axis=0, keepdims=True) for x in w])
    w = [x / total * ROUTED_SCALE for x in w]

    sel_all = jnp.concatenate(sel, axis=0)
    incl = _dot(sel_all.astype(BF16), tri_ref[...])
    base = base_ref[:, 0:1]
    rank_all = incl - sel_all + base
    rank = [rank_all[GROUP_SIZE * g:GROUP_SIZE * (g + 1)] for g in range(N_GROUPS)]
    cnt = base + jnp.sum(sel_all, axis=1, keepdims=True)
    base_ref[...] = jnp.broadcast_to(cnt, base_ref.shape)
    cnt_ref[...] = jnp.broadcast_to(cnt, cnt_ref.shape)

    def picked(vals, pick):
        parts = [jnp.sum(jnp.where(p, v, 0.0), axis=0, keepdims=True) for p, v in zip(pick, vals)]
        return functools.reduce(lambda u, v: u + v, parts)

    eid_ref[...] = jnp.concatenate([first for first, _ in picks], axis=0)
    rank_ref[...] = jnp.concatenate([picked(rank, pick) for _, pick in picks], axis=0).astype(jnp.int32)
    w_rows = [picked(w, pick) for _, pick in picks] + [jnp.zeros((LANES - TOP_K, tm), F32)]
    wtok_ref[...] = jnp.concatenate(w_rows, axis=0).T


def _router(h2, wr_t, bias_col):
    t = h2.shape[0]
    tm = PROJ_TM
    tri = jnp.asarray(np.triu(np.ones((tm, tm), np.float32)), BF16)
    slots = lambda dt: jax.ShapeDtypeStruct((TOP_K, t), dt)
    return pl.pallas_call(
        _router_kernel,
        out_shape=[slots(jnp.int32), slots(jnp.int32), jax.ShapeDtypeStruct((t, LANES), F32),
                   jax.ShapeDtypeStruct((N_EXPERTS, LANES), F32)],
        grid=(t // tm,),
        in_specs=[pl.BlockSpec((tm, PACKED_W), lambda i: (i, 0)),
                  _full((N_EXPERTS, D_MODEL)), _full((N_EXPERTS, 1)), _full((tm, tm))],
        out_specs=[pl.BlockSpec((TOP_K, tm), lambda i: (0, i)), pl.BlockSpec((TOP_K, tm), lambda i: (0, i)),
                   pl.BlockSpec((tm, LANES), lambda i: (i, 0)), _full((N_EXPERTS, LANES))],
        scratch_shapes=[pltpu.VMEM((N_EXPERTS, LANES), F32)],
        compiler_params=_params("arbitrary"),
        name="router",
    )(h2, wr_t, bias_col, tri)


def _moe_tile(t):
    mean_rows = t * TOP_K // N_EXPERTS
    return int(min(max(pl.next_power_of_2(mean_rows // 2), MOE_TILE_MIN), MOE_TILE_MAX))


def _moe_tiles(t):
    return (t * TOP_K) // _moe_tile(t) + N_EXPERTS


def _moe_plan(eid, rank, cnt, t):
    tile = _moe_tile(t)
    counts = cnt[:, 0].astype(jnp.int32)
    tiles = jnp.maximum((counts + tile - 1) // tile, 1)
    ends = jnp.cumsum(tiles)
    starts = ends - tiles
    experts = jnp.arange(N_EXPERTS, dtype=jnp.int32)
    pos = rank + jnp.sum(jnp.where(eid[:, :, None] == experts, starts * tile, 0), axis=-1)
    tile_ids = jnp.arange(_moe_tiles(t), dtype=jnp.int32)
    owner = tile_ids[:, None] >= ends[None, :]
    tile_expert = jnp.minimum(jnp.sum(owner, axis=-1), N_EXPERTS - 1)
    is_owner = tile_expert[:, None] == experts[None, :]
    start_of = jnp.sum(jnp.where(is_owner, starts, 0), axis=-1)
    count_of = jnp.sum(jnp.where(is_owner, counts, 0), axis=-1)
    real = jnp.clip(count_of - (tile_ids - start_of) * tile, 0, tile)
    real = jnp.where(tile_ids < ends[-1], real, 0)
    return pos.astype(jnp.int32), tile_expert.astype(jnp.int32), real.astype(jnp.int32)


def _sc_workers():
    info = plsc.get_sparse_core_info()
    return info.num_cores, info.num_subcores


def _sc_mesh():
    return plsc.VectorSubcoreMesh(core_axis_name="core", subcore_axis_name="subcore")


def _sc_worker_id(n_cores):
    return lax.axis_index("subcore") * n_cores + lax.axis_index("core")


def _dispatch(h2, pos):
    t = h2.shape[0]
    n_cores, n_sub = _sc_workers()
    ch = SC_ROWS
    per_worker = t // (n_cores * n_sub)
    n_chunks = per_worker // ch
    pos_chunks = pos.reshape(TOP_K, t // ch, ch).transpose(1, 0, 2)

    @functools.partial(
        pl.kernel, mesh=_sc_mesh(),
        out_type=jax.ShapeDtypeStruct((_moe_tiles(t) * _moe_tile(t), PACKED_W), jnp.int32),
        scratch_types=[pltpu.VMEM((TOP_K, ch), jnp.int32), pltpu.VMEM((ch, PACKED_W), jnp.int32),
                       pltpu.SemaphoreType.DMA],
        name="moe_dispatch",
    )
    def run(h_hbm, pos_hbm, xs_hbm, idx_ref, rows_ref, sem):
        first = _sc_worker_id(n_cores) * n_chunks

        @pl.loop(0, n_chunks)
        def _(j):
            c = first + j
            pltpu.sync_copy(pos_hbm.at[c], idx_ref)
            pltpu.sync_copy(h_hbm.at[pl.ds(pl.multiple_of(c * ch, ch), ch)], rows_ref)
            copies = [pltpu.async_copy(rows_ref, xs_hbm.at[idx_ref.at[k]], sem) for k in range(TOP_K)]
            for cp in copies:
                cp.wait()

    return run(h2, pos_chunks)


def _gather_rows(ys, idx):
    n = idx.shape[0]
    n_cores, n_sub = _sc_workers()
    ch = SC_ROWS
    per_worker = n // (n_cores * n_sub)
    n_chunks = per_worker // ch

    @functools.partial(
        pl.kernel, mesh=_sc_mesh(),
        out_type=jax.ShapeDtypeStruct((n, PACKED_W), jnp.int32),
        scratch_types=[pltpu.VMEM((ch,), jnp.int32), pltpu.VMEM((ch, PACKED_W), jnp.int32),
                       pltpu.SemaphoreType.DMA],
        name="moe_gather",
    )
    def run(ys_hbm, idx_hbm, out_hbm, idx_ref, rows_ref, sem):
        first = _sc_worker_id(n_cores) * per_worker

        @pl.loop(0, n_chunks)
        def _(j):
            off = pl.multiple_of(first + j * ch, ch)
            pltpu.sync_copy(idx_hbm.at[pl.ds(off, ch)], idx_ref)
            pltpu.async_copy(ys_hbm.at[idx_ref], rows_ref, sem).wait()
            pltpu.sync_copy(rows_ref, out_hbm.at[pl.ds(off, ch)])

    return run(ys, idx)


def _ffn_kernel(te_ref, real_ref, xs_ref, weg_ref, weu_ref, wed_ref, ys_ref):
    i = pl.program_id(0)
    real = real_ref[i]

    @pl.when(real > 0)
    def _():
        row = lax.broadcasted_iota(jnp.int32, (xs_ref.shape[0], 1), 0)
        x = _unpack_bf16_pairs(jnp.where(row < real, xs_ref[...], 0)).astype(BF16)
        act = _silu(_dot(x, weg_ref[0])) * _dot(x, weu_ref[0])
        ys_ref[...] = _pack_bf16_pairs(_dot(act.astype(BF16), wed_ref[0]))

    @pl.when(real == 0)
    def _():
        ys_ref[...] = jnp.zeros_like(ys_ref)


def _ffn(xs, tile_expert, tile_real, W):
    tile = xs.shape[0] // tile_expert.shape[0]
    wspec = lambda shape: pl.BlockSpec((1,) + shape, lambda i, te, tr: (te[i], 0, 0))
    return pl.pallas_call(
        _ffn_kernel,
        out_shape=jax.ShapeDtypeStruct(xs.shape, jnp.int32),
        grid_spec=pltpu.PrefetchScalarGridSpec(
            num_scalar_prefetch=2,
            grid=(tile_expert.shape[0],),
            in_specs=[pl.BlockSpec((tile, PACKED_W), lambda i, te, tr: (i, 0)),
                      wspec((D_MODEL, EXPERT_FF)), wspec((D_MODEL, EXPERT_FF)), wspec((EXPERT_FF, D_MODEL))],
            out_specs=pl.BlockSpec((tile, PACKED_W), lambda i, te, tr: (i, 0))),
        compiler_params=_params("parallel"),
        name="moe_ffn",
    )(tile_expert, tile_real, xs, W["weg"], W["weu"], W["wed"])


def _combine_kernel(y_ref, h_ref, wtok_ref, x1_ref, mod_ref, wsg_ref, wsu_ref, wsd_ref,
                    g_ref, b_ref, out_ref):
    wt = wtok_ref[...]
    routed = None
    for k in range(TOP_K):
        part = wt[:, k:k + 1] * _unpack_bf16_pairs(y_ref[k])
        routed = part if routed is None else routed + part
    h = _unpack_bf16_pairs(h_ref[...]).astype(BF16)
    shared = _dot((_silu(_dot(h, wsg_ref[...])) * _dot(h, wsu_ref[...])).astype(BF16), wsd_ref[...])
    m = mod_ref[0]
    y2 = DN_ALPHA * x1_ref[...] + m[5:6] * (routed + shared)
    out_ref[...] = _layer_norm(y2, g_ref[...], b_ref[...])


def _combine(ytok, h2, wtok, x1, mod, W, *, latent):
    t = h2.shape[0]
    ct = COMBINE_TM
    steps_per_batch = (DEC_SEQ if latent else t) // ct
    row = lambda w: pl.BlockSpec((ct, w), lambda i: (i, 0))
    return pl.pallas_call(
        _combine_kernel,
        out_shape=jax.ShapeDtypeStruct((t, D_MODEL), F32),
        grid=(t // ct,),
        in_specs=[pl.BlockSpec((TOP_K, ct, PACKED_W), lambda i: (0, i, 0)),
                  row(PACKED_W), row(LANES), row(D_MODEL),
                  pl.BlockSpec((1, 6, D_MODEL), lambda i: (i // steps_per_batch, 0, 0)),
                  _full((D_MODEL, SHARED_FF)), _full((D_MODEL, SHARED_FF)), _full((SHARED_FF, D_MODEL)),
                  _full((1, D_MODEL)), _full((1, D_MODEL))],
        out_specs=row(D_MODEL),
        compiler_params=_params("parallel"),
        name="moe_combine",
    )(ytok, h2, wtok, x1, mod, W["wsg"], W["wsu"], W["wsd"], W["ln2_g"], W["ln2_b"])


def _moe(h2, x1, mod, W, *, latent):
    t = h2.shape[0]
    eid, rank, wtok, cnt = _router(h2, W["wr_t"], W["rbias"])
    pos, tile_expert, tile_real = _moe_plan(eid, rank, cnt, t)
    xs = _dispatch(h2, pos)
    ys = _ffn(xs, tile_expert, tile_real, W)
    ytok = _gather_rows(ys, pos.reshape(TOP_K * t)).reshape(TOP_K, t, PACKED_W)
    return _combine(ytok, h2, wtok, x1, mod, W, latent=latent)


def _rope_partner(n_blocks, block):
    half = block // 2
    i = np.arange(n_blocks * block)
    return np.where((i % block) < half, i + half, i - half)


_QB_NOPE = np.concatenate([np.arange(MLA_NOPE) + (MLA_NOPE + MLA_ROPE) * h for h in range(MLA_HEADS)])
_QB_ROPE = np.concatenate([np.arange(MLA_ROPE) + (MLA_NOPE + MLA_ROPE) * h + MLA_NOPE for h in range(MLA_HEADS)])
_KVB_NOPE = np.concatenate([np.arange(MLA_NOPE) + (MLA_NOPE + MLA_V) * h for h in range(MLA_HEADS)])
_KVB_V = np.concatenate([np.arange(MLA_V) + (MLA_NOPE + MLA_V) * h + MLA_NOPE for h in range(MLA_HEADS)])
_DIFF_PARTNER = _rope_partner(2 * DIFF_W // 32, 32)
_MLA_PARTNER = _rope_partner(MLA_HEADS * MLA_ROPE // 16, 16)
_KR_PARTNER = _rope_partner(MLA_ROPE // 16, 16)


def _layer_weights(l, w_in, mla_qa_g, mla_wq_b, mla_kva_g, mla_wkv_b, w_branch_diff, w_branch_na,
                   w_branch_mla, w_out, ln1_g, ln1_b, ln2_g, ln2_b, w_router, router_bias,
                   w_exp_gate, w_exp_up, w_exp_down, w_sh_gate, w_sh_up, w_sh_down):
    win = w_in[l].astype(BF16)
    wa = win[:, :3072]
    qa_kva = win[:, 3072:3712]
    kr = win[:, 3712:3744]
    kr4 = jnp.tile(kr, (1, LANES // MLA_ROPE))
    krp4 = jnp.tile(kr[:, _KR_PARTNER], (1, LANES // MLA_ROPE))
    wqb = mla_wq_b[l].astype(BF16)
    wq_rope = wqb[:, _QB_ROPE]
    wkvb = mla_wkv_b[l].astype(BF16)
    row = lambda v: v[l].reshape(1, -1).astype(F32)
    return {
        "wa": wa,
        "wp": wa[:, :2 * DIFF_W][:, _DIFF_PARTNER],
        "wm_ctx": jnp.concatenate([qa_kva, kr4], axis=1),
        "wm_lat": jnp.concatenate([qa_kva, kr4, krp4], axis=1),
        "qag": row(mla_qa_g), "kvag": row(mla_kva_g),
        "wqb": jnp.concatenate([wqb[:, _QB_NOPE], wq_rope], axis=1),
        "wqp": wq_rope[:, _MLA_PARTNER],
        "wkvb": jnp.concatenate([wkvb[:, _KVB_NOPE], wkvb[:, _KVB_V]], axis=1),
        "wg": win[:, 3744:],
        "wbd": w_branch_diff[l].astype(BF16), "wbn": w_branch_na[l].astype(BF16),
        "wbm": w_branch_mla[l].astype(BF16), "wo": w_out[l].astype(BF16),
        "ln1_g": row(ln1_g), "ln1_b": row(ln1_b), "ln2_g": row(ln2_g), "ln2_b": row(ln2_b),
        "wr_t": w_router[l].T.astype(BF16), "rbias": router_bias[l].reshape(N_EXPERTS, 1).astype(F32),
        "weg": w_exp_gate[l].astype(BF16), "weu": w_exp_up[l].astype(BF16), "wed": w_exp_down[l].astype(BF16),
        "wsg": w_sh_gate[l].astype(BF16), "wsu": w_sh_up[l].astype(BF16), "wsd": w_sh_down[l].astype(BF16),
    }


def _rope_tables():
    t = jnp.arange(DEC_SEQ)
    pos = [(t // GRID_W).astype(F32), (t % GRID_W).astype(F32)]

    def table(block):
        half = block // 4
        inv = ROPE_BASE ** (-jnp.arange(half, dtype=F32) / half)
        cos, sin = [], []
        for p in pos:
            ang = p[:, None] * inv[None, :]
            cos += [jnp.cos(ang), jnp.cos(ang)]
            sin += [-jnp.sin(ang), jnp.sin(ang)]
        reps = LANES // block
        return (jnp.tile(jnp.concatenate(cos, axis=1), (1, reps)),
                jnp.tile(jnp.concatenate(sin, axis=1), (1, reps)))

    cd, sd = table(DIFF_DH)
    cm, sm = table(MLA_ROPE)
    return {"cd": cd, "sd": sd, "cm": cm, "sm": sm}


def kernel(x_prompt, x_sample, cache_diff_k, cache_diff_v, cache_na_k, cache_na_v, cache_mla_ckv, cache_mla_krope, c, c_ctx, w_mod, b_mod, w_in, diff_lambda, diff_subln_g, na_rpb, mla_qa_g, mla_wq_b, mla_kva_g, mla_wkv_b, w_branch_diff, w_branch_na, w_branch_mla, w_out, ln1_g, ln1_b, ln2_g, ln2_b, w_router, router_bias, w_exp_gate, w_exp_up, w_exp_down, w_sh_gate, w_sh_up, w_sh_down):
    t_ctx = BATCH * SEQ
    t_lat = DEC_BATCH * DEC_SEQ
    cvec = jnp.concatenate([c, c_ctx[None, :], jnp.zeros((MOD_ROWS - DEC_BATCH - 1, D_MODEL), F32)], axis=0)
    mods = _modulation(cvec, w_mod, b_mod).reshape(DEPTH, MOD_ROWS, 6, D_MODEL)
    rope = _rope_tables()
    ck_d = cache_diff_k.reshape(DEC_BATCH, DEPTH, PAST_LEN, DIFF_W)
    cv_d = cache_diff_v.reshape(DEC_BATCH, DEPTH, PAST_LEN, DIFF_W)
    ck_n = cache_na_k.reshape(DEC_BATCH, DEPTH, PAST_LEN, NA_W)
    cv_n = cache_na_v.reshape(DEC_BATCH, DEPTH, PAST_LEN, NA_W)

    xc = x_prompt.reshape(t_ctx, D_MODEL)
    xl = x_sample.reshape(t_lat, D_MODEL)
    states = []
    for l in range(DEPTH):
        W = _layer_weights(l, w_in, mla_qa_g, mla_wq_b, mla_kva_g, mla_wkv_b, w_branch_diff, w_branch_na,
                           w_branch_mla, w_out, ln1_g, ln1_b, ln2_g, ln2_b, w_router, router_bias,
                           w_exp_gate, w_exp_up, w_exp_down, w_sh_gate, w_sh_up, w_sh_down)
        lam = diff_lambda[l].astype(F32)
        subln = diff_subln_g[l].reshape(1, LANES).astype(F32)
        mod_c = mods[l, DEC_BATCH:DEC_BATCH + 1]
        mod_l = mods[l, :DEC_BATCH]

        Wc = dict(W, wm=W["wm_ctx"])
        names = ["dq", "nq", "mqn", "mqr", "kr4", "kn", "vn", "dk", "dv", "nk", "nv", "ckv", "kr"]
        pc = dict(zip(names, _project(xc, mod_c, Wc, None, latent=False)))
        od, on, om = _attn_ctx(pc, lam, subln, l)
        x1, h2 = _merge(xc, mod_c, od, on, om, W, latent=False)
        xc = _moe(h2, x1, mod_c, W, latent=False)
        states.append(pc)

        Wl = dict(W, wm=W["wm_lat"])
        names = ["dq", "dk", "dv", "nq", "nk", "nv", "mqn", "mqr", "kr4", "kn", "vn"]
        pll = dict(zip(names, _project(xl, mod_l, Wl, rope, latent=True)))
        od = _attn_diff_lat(pll, ck_d, cv_d, lam, subln, l)
        on = _attn_na_lat(pll, ck_n, cv_n, _na_bias(na_rpb[l].astype(F32)), l)
        kc, vc, krc = _mla_cache(cache_mla_ckv, cache_mla_krope, W["wkvb"], l)
        om = _attn_mla_lat(pll, kc, vc, krc)
        x1, h2 = _merge(xl, mod_l, od, on, om, W, latent=True)
        xl = _moe(h2, x1, mod_l, W, latent=True)

    def stack(name, shape):
        width = states[0][name].shape[1]
        both = jnp.stack([s[name].reshape(BATCH, SEQ, width) for s in states], axis=1)
        return both.reshape((BATCH, DEPTH, SEQ) + shape)

    return (xc.reshape(BATCH, SEQ, D_MODEL), xl.reshape(DEC_BATCH, DEC_SEQ, D_MODEL),
            stack("dk", (DIFF_HEADS, 2 * DIFF_DH)), stack("dv", (DIFF_HEADS, 2 * DIFF_DH)),
            stack("nk", (NA_HEADS, NA_DH)), stack("nv", (NA_HEADS, NA_DH)),
            stack("ckv", (MLA_KV_LORA,)), stack("kr", (MLA_ROPE,)))
```

```python
import functools

import numpy as np
import jax
import jax.numpy as jnp
from jax import lax
from jax.experimental import pallas as pl
from jax.experimental.pallas import tpu as pltpu
from jax.experimental.pallas import tpu_sc as plsc

D_MODEL = 1024
BATCH = 32
SEQ = 256
DEPTH = 2
DEC_BATCH = 8
DEC_SEQ = 2048
PAST_LEN = 512
GRID_W = 64
GRID_ROWS = DEC_SEQ // GRID_W
ROPE_BASE = 10000.0
DIFF_HEADS = 4
DIFF_DH = 64
DIFF_W = 512
NA_HEADS = 8
NA_DH = 64
NA_W = 512
NA_WIN_ROWS = 8
NA_WIN_COLS = 16
MLA_HEADS = 8
MLA_Q_LORA = 384
MLA_KV_LORA = 256
MLA_NOPE = 64
MLA_ROPE = 32
MLA_V = 64
MLA_W = 512
N_EXPERTS = 64
N_GROUPS = 8
GROUP_SIZE = N_EXPERTS // N_GROUPS
TOPK_GROUPS = 4
TOP_K = 8
EXPERT_FF = 256
SHARED_FF = 256
ROUTED_SCALE = 2.5
DN_ALPHA = (2 * DEPTH) ** 0.25
LN_EPS = 1e-5
RMS_EPS = 1e-6

F32 = jnp.float32
BF16 = jnp.bfloat16

LANES = 128
VMEM_LIMIT_BYTES = 56 * 1024 * 1024
NEG_BIG = -1e30
LOG2E = 1.4426950408889634

DIFF_SCALE = DIFF_DH ** -0.5
NA_SCALE = NA_DH ** -0.5
MLA_SCALE = (MLA_NOPE + MLA_ROPE) ** -0.5

PROJ_TM = 512
ATT_TQ = 512
NA_TQ = 256
NA_Q_ROWS = NA_TQ // GRID_W
NA_KEY_ROWS = NA_Q_ROWS + NA_WIN_ROWS
NA_KEYS = NA_KEY_ROWS * GRID_W
MOE_TILE_MIN = 128
MOE_TILE_MAX = 1024
SC_ROWS = 128
COMBINE_TM = 512


def _dot(a, b):
    return jnp.dot(a, b, preferred_element_type=F32)


def _dot_nt(a, b):
    return lax.dot_general(a, b, (((1,), (1,)), ((), ())), preferred_element_type=F32)


def _sigmoid(x):
    return 1.0 / (1.0 + jnp.exp(-x))


def _silu(x):
    return x * _sigmoid(x)


def _params(*sem):
    return pltpu.CompilerParams(dimension_semantics=sem, vmem_limit_bytes=VMEM_LIMIT_BYTES)


def _full(shape):
    n = len(shape)
    return pl.BlockSpec(shape, lambda *_: (0,) * n)


def _layer_norm(y, g, b):
    mu = jnp.mean(y, axis=-1, keepdims=True)
    yc = y - mu
    var = jnp.mean(yc * yc, axis=-1, keepdims=True)
    return yc * lax.rsqrt(var + LN_EPS) * g + b


def _rms(x, g):
    return x * lax.rsqrt(jnp.mean(x * x, axis=-1, keepdims=True) + RMS_EPS) * g


HIGH_HALF = -65536
PACKED_W = D_MODEL // 2


def _pack_bf16_pairs(x):
    n = x.shape[1] // 2
    bits = lax.bitcast_convert_type(x.astype(BF16).astype(F32), jnp.int32)
    return lax.shift_right_logical(bits[:, :n], 16) | (bits[:, n:] & HIGH_HALF)


def _unpack_bf16_pairs(w):
    lo = lax.bitcast_convert_type(lax.shift_left(w, 16), F32)
    hi = lax.bitcast_convert_type(w & HIGH_HALF, F32)
    return jnp.concatenate([lo, hi], axis=1)


MOD_ROWS = 16
MOD_TN = 1536


def _mod_kernel(c_ref, w_ref, b_ref, o_ref):
    s = _silu(c_ref[...]).astype(BF16)
    o_ref[0] = _dot(s, w_ref[0].astype(BF16)) + b_ref[0]


def _modulation(cvec, w_mod, b_mod):
    n = 6 * D_MODEL
    return pl.pallas_call(
        _mod_kernel,
        out_shape=jax.ShapeDtypeStruct((DEPTH, MOD_ROWS, n), F32),
        grid=(DEPTH, n // MOD_TN),
        in_specs=[
            pl.BlockSpec((MOD_ROWS, D_MODEL), lambda l, j: (0, 0)),
            pl.BlockSpec((1, D_MODEL, MOD_TN), lambda l, j: (l, 0, j)),
            pl.BlockSpec((1, 1, MOD_TN), lambda l, j: (l, 0, j)),
        ],
        out_specs=pl.BlockSpec((1, MOD_ROWS, MOD_TN), lambda l, j: (l, 0, j)),
        compiler_params=_params("parallel", "parallel"),
        name="modulation",
    )(cvec, w_mod, b_mod.reshape(DEPTH, 1, n))


def _proj_common(x_ref, mod_ref, wa_ref, wm_ref, qag_ref, kvag_ref, wqb_ref, wkvb_ref):
    m = mod_ref[0]
    h = (x_ref[...] * (1.0 + m[1:2]) + m[0:1]).astype(BF16)
    a = _dot(h, wa_ref[...])
    mm = _dot(h, wm_ref[...])
    qan = _rms(mm[:, :MLA_Q_LORA], qag_ref[...]).astype(BF16)
    mq = _dot(qan, wqb_ref[...])
    ckv = _rms(mm[:, MLA_Q_LORA:MLA_Q_LORA + MLA_KV_LORA], kvag_ref[...])
    kv = _dot(ckv.astype(BF16), wkvb_ref[...])
    return h, a, mm, qan, mq, ckv, kv


def _proj_ctx_kernel(x_ref, mod_ref, wa_ref, wm_ref, qag_ref, kvag_ref, wqb_ref, wkvb_ref,
                     dq_ref, nq_ref, mqn_ref, mqr_ref, kr4_ref, kn_ref, vn_ref,
                     dk_ref, dv_ref, nk_ref, nv_ref, ckv_ref, kr_ref):
    _, a, mm, _, mq, ckv, kv = _proj_common(x_ref, mod_ref, wa_ref, wm_ref, qag_ref, kvag_ref,
                                            wqb_ref, wkvb_ref)
    dq_ref[...] = (a[:, 0:512] * DIFF_SCALE).astype(BF16)
    dk_ref[...] = a[:, 512:1024]
    dv_ref[...] = a[:, 1024:1536]
    nq_ref[...] = (a[:, 1536:2048] * NA_SCALE).astype(BF16)
    nk_ref[...] = a[:, 2048:2560]
    nv_ref[...] = a[:, 2560:3072]
    mqn_ref[...] = (mq[:, :512] * MLA_SCALE).astype(BF16)
    mqr_ref[...] = (mq[:, 512:768] * MLA_SCALE).astype(BF16)
    kr4 = mm[:, 640:768]
    kr4_ref[...] = kr4.astype(BF16)
    kr_ref[...] = kr4[:, :MLA_ROPE]
    ckv_ref[...] = ckv
    kn_ref[...] = kv[:, :512].astype(BF16)
    vn_ref[...] = kv[:, 512:].astype(BF16)


def _proj_lat_kernel(x_ref, mod_ref, wa_ref, wm_ref, qag_ref, kvag_ref, wqb_ref, wkvb_ref,
                     wp_ref, wqp_ref, cd_ref, sd_ref, cm_ref, sm_ref,
                     dq_ref, dk_ref, dv_ref, nq_ref, nk_ref, nv_ref,
                     mqn_ref, mqr_ref, kr4_ref, kn_ref, vn_ref):
    h, a, mm, qan, mq, _, kv = _proj_common(x_ref, mod_ref, wa_ref, wm_ref, qag_ref, kvag_ref,
                                            wqb_ref, wkvb_ref)
    ap = _dot(h, wp_ref[...])
    mqp = _dot(qan, wqp_ref[...])
    cd = cd_ref[...]
    sd = sd_ref[...]
    cm = cm_ref[...]
    sm = sm_ref[...]
    for j in range(DIFF_W // LANES):
        lo, hi = LANES * j, LANES * (j + 1)
        dq_ref[:, lo:hi] = ((a[:, lo:hi] * cd + ap[:, lo:hi] * sd) * (DIFF_SCALE * LOG2E)).astype(BF16)
        dk_ref[:, lo:hi] = (a[:, 512 + lo:512 + hi] * cd + ap[:, 512 + lo:512 + hi] * sd).astype(BF16)
    dv_ref[...] = a[:, 1024:1536].astype(BF16)
    nq_ref[...] = (a[:, 1536:2048] * (NA_SCALE * LOG2E)).astype(BF16)
    nk_ref[...] = a[:, 2048:2560].astype(BF16)
    nv_ref[...] = a[:, 2560:3072].astype(BF16)
    mqn_ref[...] = (mq[:, :512] * (MLA_SCALE * LOG2E)).astype(BF16)
    for j in range(2):
        lo, hi = LANES * j, LANES * (j + 1)
        mqr_ref[:, lo:hi] = ((mq[:, 512 + lo:512 + hi] * cm + mqp[:, lo:hi] * sm)
                             * (MLA_SCALE * LOG2E)).astype(BF16)
    kr4_ref[...] = (mm[:, 640:768] * cm + mm[:, 768:896] * sm).astype(BF16)
    kn_ref[...] = kv[:, :512].astype(BF16)
    vn_ref[...] = kv[:, 512:].astype(BF16)


def _project(x, mod, W, rope, *, latent):
    t = x.shape[0]
    tm = PROJ_TM
    tokens_per_batch = DEC_SEQ if latent else t
    steps_per_batch = tokens_per_batch // tm
    row = lambda w: pl.BlockSpec((tm, w), lambda i: (i, 0))
    common_in = [
        row(D_MODEL),
        pl.BlockSpec((1, 6, D_MODEL), lambda i: (i // steps_per_batch, 0, 0)),
        _full(W["wa"].shape), _full(W["wm"].shape), _full((1, MLA_Q_LORA)), _full((1, MLA_KV_LORA)),
        _full(W["wqb"].shape), _full(W["wkvb"].shape),
    ]
    common_args = [x, mod, W["wa"], W["wm"], W["qag"], W["kvag"], W["wqb"], W["wkvb"]]
    bf = lambda w: jax.ShapeDtypeStruct((t, w), BF16)
    f32 = lambda w: jax.ShapeDtypeStruct((t, w), F32)
    if latent:
        tab = pl.BlockSpec((tm, LANES), lambda i: (i % steps_per_batch, 0))
        widths = [512, 512, 512, 512, 512, 512, 512, 256, 128, 512, 512]
        return pl.pallas_call(
            _proj_lat_kernel,
            out_shape=[bf(w) for w in widths],
            grid=(t // tm,),
            in_specs=common_in + [_full(W["wp"].shape), _full(W["wqp"].shape), tab, tab, tab, tab],
            out_specs=[row(w) for w in widths],
            compiler_params=_params("parallel"),
            name="proj_lat",
        )(*common_args, W["wp"], W["wqp"], rope["cd"], rope["sd"], rope["cm"], rope["sm"])
    bf_w = [512, 512, 512, 256, 128, 512, 512]
    f32_w = [512, 512, 512, 512, 256, 32]
    return pl.pallas_call(
        _proj_ctx_kernel,
        out_shape=[bf(w) for w in bf_w] + [f32(w) for w in f32_w],
        grid=(t // tm,),
        in_specs=common_in,
        out_specs=[row(w) for w in bf_w + f32_w],
        compiler_params=_params("parallel"),
        name="proj_ctx",
    )(*common_args)


def _lane_iota():
    return lax.broadcasted_iota(jnp.int32, (1, LANES), 1)


def _softmax_parts(parts, exp):
    m = functools.reduce(jnp.maximum, [jnp.max(s, axis=-1, keepdims=True) for s in parts])
    es = [exp(s - m) for s in parts]
    l = functools.reduce(lambda u, v: u + v, [jnp.sum(e, axis=-1, keepdims=True) for e in es])
    return es, l


def _diff_lambda(lam_ref, layer):
    lp = lam_ref[...]
    lam_init = 0.8 - 0.6 * float(np.exp(-0.3 * layer))
    s1 = jnp.sum(lp[0:1] * lp[1:2], axis=-1, keepdims=True)
    s2 = jnp.sum(lp[2:3] * lp[3:4], axis=-1, keepdims=True)
    return jnp.exp(s1) - jnp.exp(s2) + lam_init, lam_init


def _diff_heads(q_ref, ks, vs, lam_ref, g_ref, o_ref, layer, exp):
    lam, lam_init = _diff_lambda(lam_ref, layer)
    first_map = _lane_iota() < DIFF_DH
    g = g_ref[...]
    for h in range(DIFF_HEADS):
        hs = slice(LANES * h, LANES * (h + 1))
        q = q_ref[:, hs]
        q1 = jnp.where(first_map, q, jnp.zeros_like(q))
        q2 = jnp.where(first_map, jnp.zeros_like(q), q)
        kk = [k[:, hs].astype(BF16) for k in ks]
        e1, l1 = _softmax_parts([_dot_nt(q1, k) for k in kk], exp)
        e2, l2 = _softmax_parts([_dot_nt(q2, k) for k in kk], exp)
        r = lam * l1 / l2
        o = None
        for a1, a2, v in zip(e1, e2, vs):
            part = _dot((a1 - a2 * r).astype(BF16), v[:, hs].astype(BF16))
            o = part if o is None else o + part
        o = _rms(o / l1, g) * (1.0 - lam_init)
        o_ref[:, hs] = o.astype(BF16)


def _pair_heads(q_of, k_of, v_of, bias_of, o_ref, n_pairs, exp, stack=False):
    first = _lane_iota() < 64
    for j in range(n_pairs):
        ps = slice(LANES * j, LANES * (j + 1))
        ks = k_of(j)
        vs = v_of(j)
        qs = [q_of(j, 0), q_of(j, 1)]
        if stack:
            qs = [jnp.concatenate(qs, axis=0)]
        outs = []
        for q in qs:
            ss = [_dot_nt(q, k) for k in ks]
            ss = [s if b is None else s + b for s, b in zip(ss, bias_of(j))]
            es, l = _softmax_parts(ss, exp)
            o = None
            for e, v in zip(es, vs):
                part = _dot(e.astype(BF16), v)
                o = part if o is None else o + part
            outs.append(o / l)
        if stack:
            n = outs[0].shape[0] // 2
            outs = [outs[0][:n], outs[0][n:]]
        o_ref[:, ps] = jnp.where(first, outs[0], outs[1]).astype(BF16)


def _na_q(q_ref):
    first = _lane_iota() < NA_DH

    def q_of(j, hh):
        q = q_ref[:, LANES * j:LANES * (j + 1)]
        keep = first if hh == 0 else jnp.logical_not(first)
        return jnp.where(keep, q, jnp.zeros_like(q))
    return q_of


def _mla_q(qn_ref, qr_ref):
    lane = _lane_iota()
    first = lane < MLA_NOPE

    def q_of(j, hh):
        h = 2 * j + hh
        qn = qn_ref[:, LANES * j:LANES * (j + 1)]
        keep = first if hh == 0 else jnp.logical_not(first)
        qn = jnp.where(keep, qn, jnp.zeros_like(qn))
        qr = qr_ref[:, LANES * (h // 4):LANES * (h // 4 + 1)]
        qr = jnp.where((lane // MLA_ROPE) == (h % 4), qr, jnp.zeros_like(qr))
        return jnp.concatenate([qn, qr], axis=1)
    return q_of


def _attn_ctx_kernel(layer, dq_ref, dk_ref, dv_ref, nq_ref, nk_ref, nv_ref,
                     mqn_ref, mqr_ref, kn_ref, kr4_ref, vn_ref, lam_ref, g_ref,
                     od_ref, on_ref, om_ref):
    _diff_heads(dq_ref, [dk_ref], [dv_ref], lam_ref, g_ref, od_ref, layer, jnp.exp)
    none = lambda j: [None]
    pair = lambda j: slice(LANES * j, LANES * (j + 1))
    _pair_heads(_na_q(nq_ref),
                lambda j: [nk_ref[:, pair(j)].astype(BF16)],
                lambda j: [nv_ref[:, pair(j)].astype(BF16)],
                none, on_ref, NA_HEADS // 2, jnp.exp, stack=True)
    kr4 = kr4_ref[...]
    _pair_heads(_mla_q(mqn_ref, mqr_ref),
                lambda j: [jnp.concatenate([kn_ref[:, pair(j)], kr4], axis=1)],
                lambda j: [vn_ref[:, pair(j)]],
                none, om_ref, MLA_HEADS // 2, jnp.exp, stack=True)


def _attn_ctx(p, lam, g, layer):
    t = p["dq"].shape[0]
    row = lambda w: pl.BlockSpec((SEQ, w), lambda b: (b, 0))
    names = ["dq", "dk", "dv", "nq", "nk", "nv", "mqn", "mqr", "kn", "kr4", "vn"]
    out = jax.ShapeDtypeStruct((t, 512), BF16)
    return pl.pallas_call(
        functools.partial(_attn_ctx_kernel, layer),
        out_shape=[out, out, out],
        grid=(t // SEQ,),
        in_specs=[row(p[n].shape[1]) for n in names] + [_full((4, DIFF_DH)), _full((1, LANES))],
        out_specs=[row(512)] * 3,
        compiler_params=_params("parallel"),
        name="attn_ctx",
    )(*[p[n] for n in names], lam, g)


def _cache_spec(width, layer):
    return pl.BlockSpec((None, None, PAST_LEN, width), lambda b, q: (b, layer, 0, 0))


def _batch_spec(width):
    return pl.BlockSpec((None, DEC_SEQ, width), lambda b, q: (b, 0, 0))


def _qtile_spec(width, tq):
    steps = DEC_SEQ // tq
    return pl.BlockSpec((tq, width), lambda b, q: (b * steps + q, 0))


def _attn_diff_lat_kernel(layer, q_ref, kc_ref, vc_ref, kn_ref, vn_ref, lam_ref, g_ref, o_ref):
    _diff_heads(q_ref, [kc_ref, kn_ref], [vc_ref, vn_ref], lam_ref, g_ref, o_ref, layer, jnp.exp2)


def _attn_diff_lat(p, cache_k, cache_v, lam, g, layer):
    t = p["dq"].shape[0]
    b3 = lambda a: a.reshape(DEC_BATCH, DEC_SEQ, a.shape[1])
    return pl.pallas_call(
        functools.partial(_attn_diff_lat_kernel, layer),
        out_shape=jax.ShapeDtypeStruct((t, DIFF_W), BF16),
        grid=(DEC_BATCH, DEC_SEQ // ATT_TQ),
        in_specs=[_qtile_spec(DIFF_W, ATT_TQ), _cache_spec(DIFF_W, layer), _cache_spec(DIFF_W, layer),
                  _batch_spec(DIFF_W), _batch_spec(DIFF_W),
                  pl.BlockSpec((4, DIFF_DH), lambda b, q: (0, 0)),
                  pl.BlockSpec((1, LANES), lambda b, q: (0, 0))],
        out_specs=_qtile_spec(DIFF_W, ATT_TQ),
        compiler_params=_params("parallel", "parallel"),
        name="attn_diff_lat",
    )(p["dq"], cache_k, cache_v, b3(p["dk"]), b3(p["dv"]), lam, g)


def _na_key_start(q):
    return jnp.clip(q * NA_Q_ROWS - NA_WIN_ROWS // 2, 0, GRID_ROWS - NA_KEY_ROWS)


def _attn_na_lat_kernel(q_ref, kc_ref, vc_ref, kn_ref, vn_ref, bias_ref, o_ref):
    start = pl.multiple_of(_na_key_start(pl.program_id(1)) * GRID_W, GRID_W)
    kw = kn_ref[pl.ds(start, NA_KEYS), :]
    vw = vn_ref[pl.ds(start, NA_KEYS), :]
    pair = lambda j: slice(LANES * j, LANES * (j + 1))
    _pair_heads(_na_q(q_ref),
                lambda j: [kc_ref[:, pair(j)].astype(BF16), kw[:, pair(j)]],
                lambda j: [vc_ref[:, pair(j)].astype(BF16), vw[:, pair(j)]],
                lambda j: [None, bias_ref[j]],
                o_ref, NA_HEADS // 2, jnp.exp2, stack=True)


def _na_bias_tables():
    n_blocks = DEC_SEQ // NA_TQ
    qr = np.arange(NA_Q_ROWS)
    kr = np.arange(NA_KEY_ROWS)
    row_sel = []
    for qb in range(n_blocks):
        ks = int(np.clip(qb * NA_Q_ROWS - NA_WIN_ROWS // 2, 0, GRID_ROWS - NA_KEY_ROWS))
        r = qb * NA_Q_ROWS + qr
        r0 = np.clip(r - NA_WIN_ROWS // 2, 0, GRID_ROWS - NA_WIN_ROWS)
        krow = ks + kr
        ok = (krow[None, :] >= r0[:, None]) & (krow[None, :] < r0[:, None] + NA_WIN_ROWS)
        off = np.where(ok, krow[None, :] - r[:, None] + NA_WIN_ROWS - 1, 2 * NA_WIN_ROWS - 1)
        sel = off[:, :, None] == np.arange(2 * NA_WIN_ROWS)
        row_sel.append(sel.astype(np.float32))
    kinds, kind_of_block = [], []
    for qb in range(n_blocks):
        for n, other in enumerate(kinds):
            if np.array_equal(row_sel[qb], row_sel[other]):
                kind_of_block.append(n)
                break
        else:
            kind_of_block.append(len(kinds))
            kinds.append(qb)
    c = np.arange(GRID_W)
    c0 = np.clip(c - NA_WIN_COLS // 2, 0, GRID_W - NA_WIN_COLS)
    col_ok = (c[None, :] >= c0[:, None]) & (c[None, :] < c0[:, None] + NA_WIN_COLS)
    coff = np.where(col_ok, c[None, :] - c[:, None] + NA_WIN_COLS - 1, 2 * NA_WIN_COLS - 1)
    col_sel = (coff[:, :, None] == np.arange(2 * NA_WIN_COLS)).astype(np.float32)
    rsel = np.stack([row_sel[qb] for qb in kinds])
    return rsel, col_sel, kind_of_block


_NA_ROW_SEL, _NA_COL_SEL, _NA_KIND_OF_BLOCK = _na_bias_tables()


def _na_bias(rpb):
    hp = lax.Precision.HIGHEST
    table = jnp.pad(rpb * LOG2E, ((0, 0), (0, 1), (0, 1)), constant_values=NEG_BIG)
    cols = jnp.einsum("hij,ckj->hick", table, _NA_COL_SEL, precision=hp)
    b = jnp.einsum("nqri,hick->nhqcrk", _NA_ROW_SEL, cols, precision=hp)
    return b.reshape(_NA_ROW_SEL.shape[0], NA_HEADS // 2, 2 * NA_TQ, NA_KEYS).astype(F32)


def _na_kind(q):
    return (q > 0).astype(jnp.int32) + (q == DEC_SEQ // NA_TQ - 1).astype(jnp.int32)


def _attn_na_lat(p, cache_k, cache_v, bias, layer):
    assert _NA_KIND_OF_BLOCK == [0] + [1] * (DEC_SEQ // NA_TQ - 2) + [2]
    t = p["nq"].shape[0]
    b3 = lambda a: a.reshape(DEC_BATCH, DEC_SEQ, a.shape[1])
    return pl.pallas_call(
        _attn_na_lat_kernel,
        out_shape=jax.ShapeDtypeStruct((t, NA_W), BF16),
        grid=(DEC_BATCH, DEC_SEQ // NA_TQ),
        in_specs=[_qtile_spec(NA_W, NA_TQ), _cache_spec(NA_W, layer), _cache_spec(NA_W, layer),
                  _batch_spec(NA_W), _batch_spec(NA_W),
                  pl.BlockSpec((None, NA_HEADS // 2, 2 * NA_TQ, NA_KEYS),
                               lambda b, q: (_na_kind(q), 0, 0, 0))],
        out_specs=_qtile_spec(NA_W, NA_TQ),
        compiler_params=_params("parallel", "parallel"),
        name="attn_na_lat",
    )(p["nq"], cache_k, cache_v, b3(p["nk"]), b3(p["nv"]), bias)


def _mla_cache_kernel(ckv_ref, kr_ref, wkvb_ref, rep_ref, kc_ref, vc_ref, krc_ref):
    kv = _dot(ckv_ref[...].astype(BF16), wkvb_ref[...])
    kc_ref[...] = kv[:, :512].astype(BF16)
    vc_ref[...] = kv[:, 512:].astype(BF16)
    krc_ref[...] = _dot(kr_ref[...].astype(BF16), rep_ref[...]).astype(BF16)


def _mla_cache(cache_ckv, cache_kr, wkvb, layer):
    rep = jnp.asarray(np.tile(np.eye(MLA_ROPE, dtype=np.float32), (1, LANES // MLA_ROPE)), BF16)
    spec_in = lambda w: pl.BlockSpec((None, None, PAST_LEN, w), lambda b: (b, layer, 0, 0))
    spec_out = lambda w: pl.BlockSpec((None, PAST_LEN, w), lambda b: (b, 0, 0))
    shp = lambda w: jax.ShapeDtypeStruct((DEC_BATCH, PAST_LEN, w), BF16)
    return pl.pallas_call(
        _mla_cache_kernel,
        out_shape=[shp(512), shp(512), shp(LANES)],
        grid=(DEC_BATCH,),
        in_specs=[spec_in(MLA_KV_LORA), spec_in(MLA_ROPE), _full(wkvb.shape), _full(rep.shape)],
        out_specs=[spec_out(512), spec_out(512), spec_out(LANES)],
        compiler_params=_params("parallel"),
        name="mla_cache",
    )(cache_ckv, cache_kr, wkvb, rep)


def _attn_mla_lat_kernel(qn_ref, qr_ref, kc_ref, krc_ref, vc_ref, kn_ref, krn_ref, vn_ref, o_ref):
    pair = lambda j: slice(LANES * j, LANES * (j + 1))
    krc = krc_ref[...]
    krn = krn_ref[...]
    _pair_heads(_mla_q(qn_ref, qr_ref),
                lambda j: [jnp.concatenate([kc_ref[:, pair(j)], krc], axis=1),
                           jnp.concatenate([kn_ref[:, pair(j)], krn], axis=1)],
                lambda j: [vc_ref[:, pair(j)], vn_ref[:, pair(j)]],
                lambda j: [None, None],
                o_ref, MLA_HEADS // 2, jnp.exp2)


def _attn_mla_lat(p, kc, vc, krc):
    t = p["mqn"].shape[0]
    b3 = lambda a: a.reshape(DEC_BATCH, DEC_SEQ, a.shape[1])
    cspec = lambda w: pl.BlockSpec((None, PAST_LEN, w), lambda b, q: (b, 0, 0))
    return pl.pallas_call(
        _attn_mla_lat_kernel,
        out_shape=jax.ShapeDtypeStruct((t, MLA_W), BF16),
        grid=(DEC_BATCH, DEC_SEQ // ATT_TQ),
        in_specs=[_qtile_spec(512, ATT_TQ), _qtile_spec(256, ATT_TQ), cspec(512), cspec(LANES), cspec(512),
                  _batch_spec(512), _batch_spec(LANES), _batch_spec(512)],
        out_specs=_qtile_spec(MLA_W, ATT_TQ),
        compiler_params=_params("parallel", "parallel"),
        name="attn_mla_lat",
    )(p["mqn"], p["mqr"], kc, krc, vc, b3(p["kn"]), b3(p["kr4"]), b3(p["vn"]))


def _merge_kernel(x_ref, mod_ref, od_ref, on_ref, om_ref, wg_ref, wbd_ref, wbn_ref, wbm_ref, wo_ref,
                  g_ref, b_ref, x1_ref, h2_ref):
    x = x_ref[...]
    m = mod_ref[0]
    h = (x * (1.0 + m[1:2]) + m[0:1]).astype(BF16)
    gates = _dot(h, wg_ref[...])
    mix = (_sigmoid(gates[:, 0:1024]) * _dot(od_ref[...], wbd_ref[...])
           + _sigmoid(gates[:, 1024:2048]) * _dot(on_ref[...], wbn_ref[...])
           + _sigmoid(gates[:, 2048:3072]) * _dot(om_ref[...], wbm_ref[...]))
    out = _dot(mix.astype(BF16), wo_ref[...])
    x1 = _layer_norm(DN_ALPHA * x + m[2:3] * out, g_ref[...], b_ref[...])
    x1_ref[...] = x1
    h2_ref[...] = _pack_bf16_pairs(x1 * (1.0 + m[4:5]) + m[3:4])


def _merge(x, mod, od, on, om, W, *, latent):
    t = x.shape[0]
    tm = PROJ_TM
    steps_per_batch = (DEC_SEQ if latent else t) // tm
    row = lambda w: pl.BlockSpec((tm, w), lambda i: (i, 0))
    return pl.pallas_call(
        _merge_kernel,
        out_shape=[jax.ShapeDtypeStruct((t, D_MODEL), F32), jax.ShapeDtypeStruct((t, PACKED_W), jnp.int32)],
        grid=(t // tm,),
        in_specs=[row(D_MODEL),
                  pl.BlockSpec((1, 6, D_MODEL), lambda i: (i // steps_per_batch, 0, 0)),
                  row(512), row(512), row(512),
                  _full(W["wg"].shape), _full(W["wbd"].shape), _full(W["wbn"].shape),
                  _full(W["wbm"].shape), _full(W["wo"].shape),
                  _full((1, D_MODEL)), _full((1, D_MODEL))],
        out_specs=[row(D_MODEL), row(PACKED_W)],
        compiler_params=_params("parallel"),
        name="merge",
    )(x, mod, od, on, om, W["wg"], W["wbd"], W["wbn"], W["wbm"], W["wo"], W["ln1_g"], W["ln1_b"])


def _first_index_of_max(vals, idx, sentinel):
    mx = functools.reduce(jnp.maximum, [jnp.max(v, axis=0, keepdims=True) for v in vals])
    cand = [jnp.min(jnp.where(v == mx, i, sentinel), axis=0, keepdims=True) for v, i in zip(vals, idx)]
    return mx, functools.reduce(jnp.minimum, cand)


def _router_kernel(h_ref, wr_ref, bias_ref, tri_ref, eid_ref, rank_ref, wtok_ref, cnt_ref, base_ref):
    tm = h_ref.shape[0]

    @pl.when(pl.program_id(0) == 0)
    def _():
        base_ref[...] = jnp.zeros_like(base_ref)

    logits = _dot_nt(wr_ref[...], _unpack_bf16_pairs(h_ref[...]).astype(BF16))
    scores = _sigmoid(logits)
    biased = scores + bias_ref[...]
    member = lax.broadcasted_iota(jnp.int32, (GROUP_SIZE, tm), 0)
    slabs = [biased[GROUP_SIZE * g:GROUP_SIZE * (g + 1)] for g in range(N_GROUPS)]
    gscore = []
    for s in slabs:
        m1, first = _first_index_of_max([s], [member], GROUP_SIZE)
        m2 = jnp.max(jnp.where(member == first, -jnp.inf, s), axis=0, keepdims=True)
        gscore.append(m1 + m2)
    gs = jnp.concatenate(gscore, axis=0)
    gidx = lax.broadcasted_iota(jnp.int32, (N_GROUPS, tm), 0)
    gsel = jnp.zeros((N_GROUPS, tm), F32)
    for _ in range(TOPK_GROUPS):
        _, first = _first_index_of_max([gs], [gidx], N_GROUPS)
        pick = gidx == first
        gsel = jnp.where(pick, 1.0, gsel)
        gs = jnp.where(pick, -jnp.inf, gs)
    cur = [jnp.where(gsel[g:g + 1] > 0.0, slabs[g], -jnp.inf) for g in range(N_GROUPS)]
    eidx = [member + GROUP_SIZE * g for g in range(N_GROUPS)]
    sel = [jnp.zeros((GROUP_SIZE, tm), F32) for _ in range(N_GROUPS)]
    picks = []
    for _ in range(TOP_K):
        _, first = _first_index_of_max(cur, eidx, N_EXPERTS)
        pick = [eidx[g] == first for g in range(N_GROUPS)]
        picks.append((first, pick))
        for g in range(N_GROUPS):
            sel[g] = jnp.where(pick[g], 1.0, sel[g])
            cur[g] = jnp.where(pick[g], -jnp.inf, cur[g])
    w = [jnp.where(sel[g] > 0.0, scores[GROUP_SIZE * g:GROUP_SIZE * (g + 1)], 0.0) for g in range(N_GROUPS)]
    total = functools.reduce(lambda u, v: u + v, [jnp.sum(x, axis=0, keepdims=True) for x in w])
    w = [x / total * ROUTED_SCALE for x in w]

    sel_all = jnp.concatenate(sel, axis=0)
    incl = _dot(sel_all.astype(BF16), tri_ref[...])
    base = base_ref[:, 0:1]
    rank_all = incl - sel_all + base
    rank = [rank_all[GROUP_SIZE * g:GROUP_SIZE * (g + 1)] for g in range(N_GROUPS)]
    cnt = base + jnp.sum(sel_all, axis=1, keepdims=True)
    base_ref[...] = jnp.broadcast_to(cnt, base_ref.shape)
    cnt_ref[...] = jnp.broadcast_to(cnt, cnt_ref.shape)

    def picked(vals, pick):
        parts = [jnp.sum(jnp.where(p, v, 0.0), axis=0, keepdims=True) for p, v in zip(pick, vals)]
        return functools.reduce(lambda u, v: u + v, parts)

    eid_ref[...] = jnp.concatenate([first for first, _ in picks], axis=0)
    rank_ref[...] = jnp.concatenate([picked(rank, pick) for _, pick in picks], axis=0).astype(jnp.int32)
    w_rows = [picked(w, pick) for _, pick in picks] + [jnp.zeros((LANES - TOP_K, tm), F32)]
    wtok_ref[...] = jnp.concatenate(w_rows, axis=0).T


def _router(h2, wr_t, bias_col):
    t = h2.shape[0]
    tm = PROJ_TM
    tri = jnp.asarray(np.triu(np.ones((tm, tm), np.float32)), BF16)
    slots = lambda dt: jax.ShapeDtypeStruct((TOP_K, t), dt)
    return pl.pallas_call(
        _router_kernel,
        out_shape=[slots(jnp.int32), slots(jnp.int32), jax.ShapeDtypeStruct((t, LANES), F32),
                   jax.ShapeDtypeStruct((N_EXPERTS, LANES), F32)],
        grid=(t // tm,),
        in_specs=[pl.BlockSpec((tm, PACKED_W), lambda i: (i, 0)),
                  _full((N_EXPERTS, D_MODEL)), _full((N_EXPERTS, 1)), _full((tm, tm))],
        out_specs=[pl.BlockSpec((TOP_K, tm), lambda i: (0, i)), pl.BlockSpec((TOP_K, tm), lambda i: (0, i)),
                   pl.BlockSpec((tm, LANES), lambda i: (i, 0)), _full((N_EXPERTS, LANES))],
        scratch_shapes=[pltpu.VMEM((N_EXPERTS, LANES), F32)],
        compiler_params=_params("arbitrary"),
        name="router",
    )(h2, wr_t, bias_col, tri)


def _moe_tile(t):
    mean_rows = t * TOP_K // N_EXPERTS
    return int(min(max(pl.next_power_of_2(mean_rows // 2), MOE_TILE_MIN), MOE_TILE_MAX))


def _moe_tiles(t):
    return (t * TOP_K) // _moe_tile(t) + N_EXPERTS


def _moe_plan(eid, rank, cnt, t):
    tile = _moe_tile(t)
    counts = cnt[:, 0].astype(jnp.int32)
    tiles = jnp.maximum((counts + tile - 1) // tile, 1)
    ends = jnp.cumsum(tiles)
    starts = ends - tiles
    experts = jnp.arange(N_EXPERTS, dtype=jnp.int32)
    pos = rank + jnp.sum(jnp.where(eid[:, :, None] == experts, starts * tile, 0), axis=-1)
    tile_ids = jnp.arange(_moe_tiles(t), dtype=jnp.int32)
    owner = tile_ids[:, None] >= ends[None, :]
    tile_expert = jnp.minimum(jnp.sum(owner, axis=-1), N_EXPERTS - 1)
    is_owner = tile_expert[:, None] == experts[None, :]
    start_of = jnp.sum(jnp.where(is_owner, starts, 0), axis=-1)
    count_of = jnp.sum(jnp.where(is_owner, counts, 0), axis=-1)
    real = jnp.clip(count_of - (tile_ids - start_of) * tile, 0, tile)
    real = jnp.where(tile_ids < ends[-1], real, 0)
    return pos.astype(jnp.int32), tile_expert.astype(jnp.int32), real.astype(jnp.int32)


def _sc_workers():
    info = plsc.get_sparse_core_info()
    return info.num_cores, info.num_subcores


def _sc_mesh():
    return plsc.VectorSubcoreMesh(core_axis_name="core", subcore_axis_name="subcore")


def _sc_worker_id(n_cores):
    return lax.axis_index("subcore") * n_cores + lax.axis_index("core")


def _dispatch(h2, pos):
    t = h2.shape[0]
    n_cores, n_sub = _sc_workers()
    ch = SC_ROWS
    per_worker = t // (n_cores * n_sub)
    n_chunks = per_worker // ch
    pos_chunks = pos.reshape(TOP_K, t // ch, ch).transpose(1, 0, 2)

    @functools.partial(
        pl.kernel, mesh=_sc_mesh(),
        out_type=jax.ShapeDtypeStruct((_moe_tiles(t) * _moe_tile(t), PACKED_W), jnp.int32),
        scratch_types=[pltpu.VMEM((TOP_K, ch), jnp.int32), pltpu.VMEM((ch, PACKED_W), jnp.int32),
                       pltpu.SemaphoreType.DMA],
        name="moe_dispatch",
    )
    def run(h_hbm, pos_hbm, xs_hbm, idx_ref, rows_ref, sem):
        first = _sc_worker_id(n_cores) * n_chunks

        @pl.loop(0, n_chunks)
        def _(j):
            c = first + j
            pltpu.sync_copy(pos_hbm.at[c], idx_ref)
            pltpu.sync_copy(h_hbm.at[pl.ds(pl.multiple_of(c * ch, ch), ch)], rows_ref)
            copies = [pltpu.async_copy(rows_ref, xs_hbm.at[idx_ref.at[k]], sem) for k in range(TOP_K)]
            for cp in copies:
                cp.wait()

    return run(h2, pos_chunks)


def _gather_rows(ys, idx):
    n = idx.shape[0]
    n_cores, n_sub = _sc_workers()
    ch = SC_ROWS
    per_worker = n // (n_cores * n_sub)
    n_chunks = per_worker // ch

    @functools.partial(
        pl.kernel, mesh=_sc_mesh(),
        out_type=jax.ShapeDtypeStruct((n, PACKED_W), jnp.int32),
        scratch_types=[pltpu.VMEM((ch,), jnp.int32), pltpu.VMEM((ch, PACKED_W), jnp.int32),
                       pltpu.SemaphoreType.DMA],
        name="moe_gather",
    )
    def run(ys_hbm, idx_hbm, out_hbm, idx_ref, rows_ref, sem):
        first = _sc_worker_id(n_cores) * per_worker

        @pl.loop(0, n_chunks)
        def _(j):
            off = pl.multiple_of(first + j * ch, ch)
            pltpu.sync_copy(idx_hbm.at[pl.ds(off, ch)], idx_ref)
            pltpu.async_copy(ys_hbm.at[idx_ref], rows_ref, sem).wait()
            pltpu.sync_copy(rows_ref, out_hbm.at[pl.ds(off, ch)])

    return run(ys, idx)


def _ffn_kernel(te_ref, real_ref, xs_ref, weg_ref, weu_ref, wed_ref, ys_ref):
    i = pl.program_id(0)
    real = real_ref[i]

    @pl.when(real > 0)
    def _():
        row = lax.broadcasted_iota(jnp.int32, (xs_ref.shape[0], 1), 0)
        x = _unpack_bf16_pairs(jnp.where(row < real, xs_ref[...], 0)).astype(BF16)
        act = _silu(_dot(x, weg_ref[0])) * _dot(x, weu_ref[0])
        ys_ref[...] = _pack_bf16_pairs(_dot(act.astype(BF16), wed_ref[0]))

    @pl.when(real == 0)
    def _():
        ys_ref[...] = jnp.zeros_like(ys_ref)


def _ffn(xs, tile_expert, tile_real, W):
    tile = xs.shape[0] // tile_expert.shape[0]
    wspec = lambda shape: pl.BlockSpec((1,) + shape, lambda i, te, tr: (te[i], 0, 0))
    return pl.pallas_call(
        _ffn_kernel,
        out_shape=jax.ShapeDtypeStruct(xs.shape, jnp.int32),
        grid_spec=pltpu.PrefetchScalarGridSpec(
            num_scalar_prefetch=2,
            grid=(tile_expert.shape[0],),
            in_specs=[pl.BlockSpec((tile, PACKED_W), lambda i, te, tr: (i, 0)),
                      wspec((D_MODEL, EXPERT_FF)), wspec((D_MODEL, EXPERT_FF)), wspec((EXPERT_FF, D_MODEL))],
            out_specs=pl.BlockSpec((tile, PACKED_W), lambda i, te, tr: (i, 0))),
        compiler_params=_params("parallel"),
        name="moe_ffn",
    )(tile_expert, tile_real, xs, W["weg"], W["weu"], W["wed"])


def _combine_kernel(y_ref, h_ref, wtok_ref, x1_ref, mod_ref, wsg_ref, wsu_ref, wsd_ref,
                    g_ref, b_ref, out_ref):
    wt = wtok_ref[...]
    routed = None
    for k in range(TOP_K):
        part = wt[:, k:k + 1] * _unpack_bf16_pairs(y_ref[k])
        routed = part if routed is None else routed + part
    h = _unpack_bf16_pairs(h_ref[...]).astype(BF16)
    shared = _dot((_silu(_dot(h, wsg_ref[...])) * _dot(h, wsu_ref[...])).astype(BF16), wsd_ref[...])
    m = mod_ref[0]
    y2 = DN_ALPHA * x1_ref[...] + m[5:6] * (routed + shared)
    out_ref[...] = _layer_norm(y2, g_ref[...], b_ref[...])


def _combine(ytok, h2, wtok, x1, mod, W, *, latent):
    t = h2.shape[0]
    ct = COMBINE_TM
    steps_per_batch = (DEC_SEQ if latent else t) // ct
    row = lambda w: pl.BlockSpec((ct, w), lambda i: (i, 0))
    return pl.pallas_call(
        _combine_kernel,
        out_shape=jax.ShapeDtypeStruct((t, D_MODEL), F32),
        grid=(t // ct,),
        in_specs=[pl.BlockSpec((TOP_K, ct, PACKED_W), lambda i: (0, i, 0)),
                  row(PACKED_W), row(LANES), row(D_MODEL),
                  pl.BlockSpec((1, 6, D_MODEL), lambda i: (i // steps_per_batch, 0, 0)),
                  _full((D_MODEL, SHARED_FF)), _full((D_MODEL, SHARED_FF)), _full((SHARED_FF, D_MODEL)),
                  _full((1, D_MODEL)), _full((1, D_MODEL))],
        out_specs=row(D_MODEL),
        compiler_params=_params("parallel"),
        name="moe_combine",
    )(ytok, h2, wtok, x1, mod, W["wsg"], W["wsu"], W["wsd"], W["ln2_g"], W["ln2_b"])


def _moe(h2, x1, mod, W, *, latent):
    t = h2.shape[0]
    eid, rank, wtok, cnt = _router(h2, W["wr_t"], W["rbias"])
    pos, tile_expert, tile_real = _moe_plan(eid, rank, cnt, t)
    xs = _dispatch(h2, pos)
    ys = _ffn(xs, tile_expert, tile_real, W)
    ytok = _gather_rows(ys, pos.reshape(TOP_K * t)).reshape(TOP_K, t, PACKED_W)
    return _combine(ytok, h2, wtok, x1, mod, W, latent=latent)


def _rope_partner(n_blocks, block):
    half = block // 2
    i = np.arange(n_blocks * block)
    return np.where((i % block) < half, i + half, i - half)


_QB_NOPE = np.concatenate([np.arange(MLA_NOPE) + (MLA_NOPE + MLA_ROPE) * h for h in range(MLA_HEADS)])
_QB_ROPE = np.concatenate([np.arange(MLA_ROPE) + (MLA_NOPE + MLA_ROPE) * h + MLA_NOPE for h in range(MLA_HEADS)])
_KVB_NOPE = np.concatenate([np.arange(MLA_NOPE) + (MLA_NOPE + MLA_V) * h for h in range(MLA_HEADS)])
_KVB_V = np.concatenate([np.arange(MLA_V) + (MLA_NOPE + MLA_V) * h + MLA_NOPE for h in range(MLA_HEADS)])
_DIFF_PARTNER = _rope_partner(2 * DIFF_W // 32, 32)
_MLA_PARTNER = _rope_partner(MLA_HEADS * MLA_ROPE // 16, 16)
_KR_PARTNER = _rope_partner(MLA_ROPE // 16, 16)


def _layer_weights(l, w_in, mla_qa_g, mla_wq_b, mla_kva_g, mla_wkv_b, w_branch_diff, w_branch_na,
                   w_branch_mla, w_out, ln1_g, ln1_b, ln2_g, ln2_b, w_router, router_bias,
                   w_exp_gate, w_exp_up, w_exp_down, w_sh_gate, w_sh_up, w_sh_down):
    win = w_in[l].astype(BF16)
    wa = win[:, :3072]
    qa_kva = win[:, 3072:3712]
    kr = win[:, 3712:3744]
    kr4 = jnp.tile(kr, (1, LANES // MLA_ROPE))
    krp4 = jnp.tile(kr[:, _KR_PARTNER], (1, LANES // MLA_ROPE))
    wqb = mla_wq_b[l].astype(BF16)
    wq_rope = wqb[:, _QB_ROPE]
    wkvb = mla_wkv_b[l].astype(BF16)
    row = lambda v: v[l].reshape(1, -1).astype(F32)
    return {
        "wa": wa,
        "wp": wa[:, :2 * DIFF_W][:, _DIFF_PARTNER],
        "wm_ctx": jnp.concatenate([qa_kva, kr4], axis=1),
        "wm_lat": jnp.concatenate([qa_kva, kr4, krp4], axis=1),
        "qag": row(mla_qa_g), "kvag": row(mla_kva_g),
        "wqb": jnp.concatenate([wqb[:, _QB_NOPE], wq_rope], axis=1),
        "wqp": wq_rope[:, _MLA_PARTNER],
        "wkvb": jnp.concatenate([wkvb[:, _KVB_NOPE], wkvb[:, _KVB_V]], axis=1),
        "wg": win[:, 3744:],
        "wbd": w_branch_diff[l].astype(BF16), "wbn": w_branch_na[l].astype(BF16),
        "wbm": w_branch_mla[l].astype(BF16), "wo": w_out[l].astype(BF16),
        "ln1_g": row(ln1_g), "ln1_b": row(ln1_b), "ln2_g": row(ln2_g), "ln2_b": row(ln2_b),
        "wr_t": w_router[l].T.astype(BF16), "rbias": router_bias[l].reshape(N_EXPERTS, 1).astype(F32),
        "weg": w_exp_gate[l].astype(BF16), "weu": w_exp_up[l].astype(BF16), "wed": w_exp_down[l].astype(BF16),
        "wsg": w_sh_gate[l].astype(BF16), "wsu": w_sh_up[l].astype(BF16), "wsd": w_sh_down[l].astype(BF16),
    }


def _rope_tables():
    t = jnp.arange(DEC_SEQ)
    pos = [(t // GRID_W).astype(F32), (t % GRID_W).astype(F32)]

    def table(block):
        half = block // 4
        inv = ROPE_BASE ** (-jnp.arange(half, dtype=F32) / half)
        cos, sin = [], []
        for p in pos:
            ang = p[:, None] * inv[None, :]
            cos += [jnp.cos(ang), jnp.cos(ang)]
            sin += [-jnp.sin(ang), jnp.sin(ang)]
        reps = LANES // block
        return (jnp.tile(jnp.concatenate(cos, axis=1), (1, reps)),
                jnp.tile(jnp.concatenate(sin, axis=1), (1, reps)))

    cd, sd = table(DIFF_DH)
    cm, sm = table(MLA_ROPE)
    return {"cd": cd, "sd": sd, "cm": cm, "sm": sm}


def kernel(x_prompt, x_sample, cache_diff_k, cache_diff_v, cache_na_k, cache_na_v, cache_mla_ckv, cache_mla_krope, c, c_ctx, w_mod, b_mod, w_in, diff_lambda, diff_subln_g, na_rpb, mla_qa_g, mla_wq_b, mla_kva_g, mla_wkv_b, w_branch_diff, w_branch_na, w_branch_mla, w_out, ln1_g, ln1_b, ln2_g, ln2_b, w_router, router_bias, w_exp_gate, w_exp_up, w_exp_down, w_sh_gate, w_sh_up, w_sh_down):
    t_ctx = BATCH * SEQ
    t_lat = DEC_BATCH * DEC_SEQ
    cvec = jnp.concatenate([c, c_ctx[None, :], jnp.zeros((MOD_ROWS - DEC_BATCH - 1, D_MODEL), F32)], axis=0)
    mods = _modulation(cvec, w_mod, b_mod).reshape(DEPTH, MOD_ROWS, 6, D_MODEL)
    rope = _rope_tables()
    ck_d = cache_diff_k.reshape(DEC_BATCH, DEPTH, PAST_LEN, DIFF_W)
    cv_d = cache_diff_v.reshape(DEC_BATCH, DEPTH, PAST_LEN, DIFF_W)
    ck_n = cache_na_k.reshape(DEC_BATCH, DEPTH, PAST_LEN, NA_W)
    cv_n = cache_na_v.reshape(DEC_BATCH, DEPTH, PAST_LEN, NA_W)

    xc = x_prompt.reshape(t_ctx, D_MODEL)
    xl = x_sample.reshape(t_lat, D_MODEL)
    states = []
    for l in range(DEPTH):
        W = _layer_weights(l, w_in, mla_qa_g, mla_wq_b, mla_kva_g, mla_wkv_b, w_branch_diff, w_branch_na,
                           w_branch_mla, w_out, ln1_g, ln1_b, ln2_g, ln2_b, w_router, router_bias,
                           w_exp_gate, w_exp_up, w_exp_down, w_sh_gate, w_sh_up, w_sh_down)
        lam = diff_lambda[l].astype(F32)
        subln = diff_subln_g[l].reshape(1, LANES).astype(F32)
        mod_c = mods[l, DEC_BATCH:DEC_BATCH + 1]
        mod_l = mods[l, :DEC_BATCH]

        Wc = dict(W, wm=W["wm_ctx"])
        names = ["dq", "nq", "mqn", "mqr", "kr4", "kn", "vn", "dk", "dv", "nk", "nv", "ckv", "kr"]
        pc = dict(zip(names, _project(xc, mod_c, Wc, None, latent=False)))
        od, on, om = _attn_ctx(pc, lam, subln, l)
        x1, h2 = _merge(xc, mod_c, od, on, om, W, latent=False)
        xc = _moe(h2, x1, mod_c, W, latent=False)
        states.append(pc)

        Wl = dict(W, wm=W["wm_lat"])
        names = ["dq", "dk", "dv", "nq", "nk", "nv", "mqn", "mqr", "kr4", "kn", "vn"]
        pll = dict(zip(names, _project(xl, mod_l, Wl, rope, latent=True)))
        od = _attn_diff_lat(pll, ck_d, cv_d, lam, subln, l)
        on = _attn_na_lat(pll, ck_n, cv_n, _na_bias(na_rpb[l].astype(F32)), l)
        kc, vc, krc = _mla_cache(cache_mla_ckv, cache_mla_krope, W["wkvb"], l)
        om = _attn_mla_lat(pll, kc, vc, krc)
        x1, h2 = _merge(xl, mod_l, od, on, om, W, latent=True)
        xl = _moe(h2, x1, mod_l, W, latent=True)

    def stack(name, shape):
        width = states[0][name].shape[1]
        both = jnp.stack([s[name].reshape(BATCH, SEQ, width) for s in states], axis=1)
        return both.reshape((BATCH, DEPTH, SEQ) + shape)

    return (xc.reshape(BATCH, SEQ, D_MODEL), xl.reshape(DEC_BATCH, DEC_SEQ, D_MODEL),
            stack("dk", (DIFF_HEADS, 2 * DIFF_DH)), stack("dv", (DIFF_HEADS, 2 * DIFF_DH)),
            stack("nk", (NA_HEADS, NA_DH)), stack("nv", (NA_HEADS, NA_DH)),
            stack("ckv", (MLA_KV_LORA,)), stack("kr", (MLA_ROPE,)))
```
